```python
import jax, jax.numpy as jnp
from jax import lax
import numpy as np

D_MODEL = 2048
BATCH = 8
SEQ = 4096
DEPTH = 1

N_HEADS_MLA = 8
Q_LORA_RANK = 512
KV_LORA_RANK = 512
QK_NOPE_DIM = 128
QK_ROPE_DIM = 64
QK_HEAD_DIM = QK_NOPE_DIM + QK_ROPE_DIM
V_HEAD_DIM = 128
MLA_WIDTH = N_HEADS_MLA * V_HEAD_DIM
N_HEADS_SB = 8
SB_HEAD_DIM = 128
SB_WIDTH = N_HEADS_SB * SB_HEAD_DIM
D_FF = -(-8 * D_MODEL // (3 * 256)) * 256
D_IN = Q_LORA_RANK + KV_LORA_RANK + QK_ROPE_DIM + 3 * SB_WIDTH + 2 * D_MODEL
Q_BLOCK = 128
ROPE_THETA = 10000.0
EPS = 1e-6

kernel_name = "hybrid_mla_stickbreaking_gated_block"


def _rms(x, g):
    xf = x.astype(jnp.float32)
    y = xf * lax.rsqrt(jnp.mean(xf * xf, axis=-1, keepdims=True) + EPS)
    return (y * g.astype(jnp.float32)).astype(x.dtype)


def _rope(x, pos):
    half = x.shape[-1] // 2
    freqs = ROPE_THETA ** (-jnp.arange(half, dtype=jnp.float32) / half)
    ang = pos.astype(jnp.float32)[..., None] * freqs
    cos = jnp.cos(ang)[:, :, None, :]
    sin = jnp.sin(ang)[:, :, None, :]
    xf = x.astype(jnp.float32)
    x1, x2 = xf[..., :half], xf[..., half:]
    return jnp.concatenate([x1 * cos - x2 * sin, x1 * sin + x2 * cos], axis=-1).astype(x.dtype)


def _mla_attention(q, k, v):
    S = q.shape[2]
    scale = QK_HEAD_DIM ** -0.5
    outs = []
    for i in range(S // Q_BLOCK):
        end = (i + 1) * Q_BLOCK
        qb = q[:, :, i * Q_BLOCK:end]
        s = jnp.einsum('bhqd,bhkd->bhqk', qb, k[:, :, :end]).astype(jnp.float32) * scale
        qi = i * Q_BLOCK + jnp.arange(Q_BLOCK)
        ki = jnp.arange(end)
        s = jnp.where(ki[None, :] <= qi[:, None], s, -jnp.inf)
        p = jax.nn.softmax(s, axis=-1).astype(v.dtype)
        outs.append(jnp.einsum('bhqk,bhkd->bhqd', p, v[:, :, :end]))
    return jnp.concatenate(outs, axis=2)


def _stick_breaking(q, k, v):
    S = q.shape[2]
    scale = SB_HEAD_DIM ** -0.5
    outs = []
    for i in range(S // Q_BLOCK):
        end = (i + 1) * Q_BLOCK
        qb = q[:, :, i * Q_BLOCK:end]
        z = jnp.einsum('bhqd,bhkd->bhqk', qb, k[:, :, :end]).astype(jnp.float32) * scale
        qi = i * Q_BLOCK + jnp.arange(Q_BLOCK)
        ki = jnp.arange(end)
        mask = ki[None, :] < qi[:, None]
        log_beta = jax.nn.log_sigmoid(z)
        log_one_minus = jnp.where(mask, jax.nn.log_sigmoid(-z), 0.0)
        tail = lax.cumsum(log_one_minus, axis=3, reverse=True) - log_one_minus
        a = jnp.where(mask, jnp.exp(log_beta + tail), 0.0).astype(v.dtype)
        outs.append(jnp.einsum('bhqk,bhkd->bhqd', a, v[:, :, :end]))
    return jnp.concatenate(outs, axis=2)


def _layer(x, c_act, pos, w_ada, b_ada, g_norm1, g_norm2, w_in, g_q_latent, g_kv_latent,
           w_uq, w_ukv, g_q_head, g_k_head, w_proj_mla, w_proj_sb, w_out, w_ffn_in, w_ffn_out):
    B, S, _ = x.shape
    ada = (c_act @ w_ada + b_ada)[:, None, :]
    sh1, sc1, gt1, sh2, sc2, gt2 = jnp.split(ada, 6, axis=-1)

    h = _rms(x, g_norm1) * (1 + sc1) + sh1
    proj = h @ w_in
    offs = np.cumsum([Q_LORA_RANK, KV_LORA_RANK, QK_ROPE_DIM, SB_WIDTH, SB_WIDTH, SB_WIDTH, D_MODEL])
    c_q, c_kv, k_pe, q_sb, k_sb, v_sb, gl_a, gl_b = jnp.split(proj, [int(o) for o in offs], axis=-1)

    q = (_rms(c_q, g_q_latent) @ w_uq).reshape(B, S, N_HEADS_MLA, QK_HEAD_DIM)
    kv = (_rms(c_kv, g_kv_latent) @ w_ukv).reshape(B, S, N_HEADS_MLA, QK_NOPE_DIM + V_HEAD_DIM)
    k_nope, v = kv[..., :QK_NOPE_DIM], kv[..., QK_NOPE_DIM:]
    k_pe_h = jnp.broadcast_to(k_pe[:, :, None, :], (B, S, N_HEADS_MLA, QK_ROPE_DIM))
    k = jnp.concatenate([k_nope, k_pe_h], axis=-1)
    q = _rms(q, g_q_head)
    k = _rms(k, g_k_head)
    q = jnp.concatenate([q[..., :QK_NOPE_DIM], _rope(q[..., QK_NOPE_DIM:], pos)], axis=-1)
    k = jnp.concatenate([k[..., :QK_NOPE_DIM], _rope(k[..., QK_NOPE_DIM:], pos)], axis=-1)
    y_a = _mla_attention(q.transpose(0, 2, 1, 3), k.transpose(0, 2, 1, 3), v.transpose(0, 2, 1, 3))
    y_a = y_a.transpose(0, 2, 1, 3).reshape(B, S, MLA_WIDTH)

    to_heads = lambda t: t.reshape(B, S, N_HEADS_SB, SB_HEAD_DIM).transpose(0, 2, 1, 3)
    y_b = _stick_breaking(to_heads(q_sb), to_heads(k_sb), to_heads(v_sb))
    y_b = y_b.transpose(0, 2, 1, 3).reshape(B, S, SB_WIDTH)

    merged = jax.nn.sigmoid(gl_a) * (y_a @ w_proj_mla) + jax.nn.sigmoid(gl_b) * (y_b @ w_proj_sb)
    x = x + gt1 * (merged @ w_out)

    h2 = _rms(x, g_norm2) * (1 + sc2) + sh2
    gate, up = jnp.split(h2 @ w_ffn_in, 2, axis=-1)
    x = x + gt2 * ((jax.nn.silu(gate) * up) @ w_ffn_out)
    return x


def _fwd_setup_inputs(seed: int = 0) -> dict:
    key = jax.random.key(seed)
    ks = jax.random.split(key, 24)
    f32 = jnp.float32

    def nrm(k, shape, fan_in):
        return jax.random.normal(k, shape, f32) * (fan_in ** -0.5)

    def gain(k, n):
        return 1.0 + 0.02 * jax.random.normal(k, (DEPTH, n), f32)

    x = jax.random.normal(ks[0], (BATCH, SEQ, D_MODEL), f32)
    c = jax.random.normal(ks[1], (BATCH, D_MODEL), f32)
    offset = jax.random.randint(ks[2], (BATCH, 1), 0, 1024, dtype=jnp.int32)
    positions = offset + jnp.arange(SEQ, dtype=jnp.int32)[None, :]
    return {
        "x": x,
        "c": c,
        "positions": positions,
        "w_ada": nrm(ks[3], (DEPTH, D_MODEL, 6 * D_MODEL), D_MODEL),
        "b_ada": 0.02 * jax.random.normal(ks[4], (DEPTH, 6 * D_MODEL), f32),
        "g_norm1": gain(ks[5], D_MODEL),
        "g_norm2": gain(ks[6], D_MODEL),
        "w_in": nrm(ks[7], (DEPTH, D_MODEL, D_IN), D_MODEL),
        "g_q_latent": gain(ks[8], Q_LORA_RANK),
        "g_kv_latent": gain(ks[9], KV_LORA_RANK),
        "w_uq": nrm(ks[10], (DEPTH, Q_LORA_RANK, N_HEADS_MLA * QK_HEAD_DIM), Q_LORA_RANK),
        "w_ukv": nrm(ks[11], (DEPTH, KV_LORA_RANK, N_HEADS_MLA * (QK_NOPE_DIM + V_HEAD_DIM)), KV_LORA_RANK),
        "g_q_head": gain(ks[12], QK_HEAD_DIM),
        "g_k_head": gain(ks[13], QK_HEAD_DIM),
        "w_proj_mla": nrm(ks[14], (DEPTH, MLA_WIDTH, D_MODEL), MLA_WIDTH),
        "w_proj_sb": nrm(ks[15], (DEPTH, SB_WIDTH, D_MODEL), SB_WIDTH),
        "w_out": nrm(ks[16], (DEPTH, D_MODEL, D_MODEL), D_MODEL),
        "w_ffn_in": nrm(ks[17], (DEPTH, D_MODEL, 2 * D_FF), D_MODEL),
        "w_ffn_out": nrm(ks[18], (DEPTH, D_FF, D_MODEL), D_FF),
    }


def _fwd_reference(x, c, positions, w_ada, b_ada, g_norm1, g_norm2, w_in, g_q_latent, g_kv_latent,
              w_uq, w_ukv, g_q_head, g_k_head, w_proj_mla, w_proj_sb, w_out, w_ffn_in, w_ffn_out):
    c_act = jax.nn.silu(c)
    for l in range(DEPTH):
        x = _layer(x, c_act, positions, w_ada[l], b_ada[l], g_norm1[l], g_norm2[l], w_in[l],
                   g_q_latent[l], g_kv_latent[l], w_uq[l], w_ukv[l], g_q_head[l], g_k_head[l],
                   w_proj_mla[l], w_proj_sb[l], w_out[l], w_ffn_in[l], w_ffn_out[l])
    return x


import jax as _jax
import jax.numpy as _jnp

TWIN_FORMAT = 'train_step'
FWD_PARAMS = ['x', 'c', 'positions', 'w_ada', 'b_ada', 'g_norm1', 'g_norm2', 'w_in', 'g_q_latent', 'g_kv_latent', 'w_uq', 'w_ukv', 'g_q_head', 'g_k_head', 'w_proj_mla', 'w_proj_sb', 'w_out', 'w_ffn_in', 'w_ffn_out']
TWIN_WEIGHTS = ['w_ada', 'b_ada', 'g_norm1', 'g_norm2', 'w_in', 'g_q_latent', 'g_kv_latent', 'w_uq', 'w_ukv', 'g_q_head', 'g_k_head', 'w_proj_mla', 'w_proj_sb', 'w_out', 'w_ffn_in', 'w_ffn_out']
TWIN_DIFF_INPUT = 'x'
TWIN_INPUTS = ['x', 'c', 'positions', 'w_ada', 'b_ada', 'g_norm1', 'g_norm2', 'w_in', 'g_q_latent', 'g_kv_latent', 'w_uq', 'w_ukv', 'g_q_head', 'g_k_head', 'w_proj_mla', 'w_proj_sb', 'w_out', 'w_ffn_in', 'w_ffn_out', 'loss_target', 'm_w_ada', 'm_b_ada', 'm_g_norm1', 'm_g_norm2', 'm_w_in', 'm_g_q_latent', 'm_g_kv_latent', 'm_w_uq', 'm_w_ukv', 'm_g_q_head', 'm_g_k_head', 'm_w_proj_mla', 'm_w_proj_sb', 'm_w_out', 'm_w_ffn_in', 'm_w_ffn_out', 'v_w_ada', 'v_b_ada', 'v_g_norm1', 'v_g_norm2', 'v_w_in', 'v_g_q_latent', 'v_g_kv_latent', 'v_w_uq', 'v_w_ukv', 'v_g_q_head', 'v_g_k_head', 'v_w_proj_mla', 'v_w_proj_sb', 'v_w_out', 'v_w_ffn_in', 'v_w_ffn_out']
TWIN_OUTPUTS = ['loss', 'grad_x', 'grad_w_ada', 'grad_b_ada', 'grad_g_norm1', 'grad_g_norm2', 'grad_w_in', 'grad_g_q_latent', 'grad_g_kv_latent', 'grad_w_uq', 'grad_w_ukv', 'grad_g_q_head', 'grad_g_k_head', 'grad_w_proj_mla', 'grad_w_proj_sb', 'grad_w_out', 'grad_w_ffn_in', 'grad_w_ffn_out', 'delta_w_ada', 'delta_b_ada', 'delta_g_norm1', 'delta_g_norm2', 'delta_w_in', 'delta_g_q_latent', 'delta_g_kv_latent', 'delta_w_uq', 'delta_w_ukv', 'delta_g_q_head', 'delta_g_k_head', 'delta_w_proj_mla', 'delta_w_proj_sb', 'delta_w_out', 'delta_w_ffn_in', 'delta_w_ffn_out', 'new_m_w_ada', 'new_m_b_ada', 'new_m_g_norm1', 'new_m_g_norm2', 'new_m_w_in', 'new_m_g_q_latent', 'new_m_g_kv_latent', 'new_m_w_uq', 'new_m_w_ukv', 'new_m_g_q_head', 'new_m_g_k_head', 'new_m_w_proj_mla', 'new_m_w_proj_sb', 'new_m_w_out', 'new_m_w_ffn_in', 'new_m_w_ffn_out', 'new_v_w_ada', 'new_v_b_ada', 'new_v_g_norm1', 'new_v_g_norm2', 'new_v_w_in', 'new_v_g_q_latent', 'new_v_g_kv_latent', 'new_v_w_uq', 'new_v_w_ukv', 'new_v_g_q_head', 'new_v_g_k_head', 'new_v_w_proj_mla', 'new_v_w_proj_sb', 'new_v_w_out', 'new_v_w_ffn_in', 'new_v_w_ffn_out']
TWIN_LEAF_KINDS = {'loss': 'loss', 'grad_x': 'grad_x', 'grad_w_ada': 'grad_w', 'grad_b_ada': 'grad_w', 'grad_g_norm1': 'grad_w', 'grad_g_norm2': 'grad_w', 'grad_w_in': 'grad_w', 'grad_g_q_latent': 'grad_w', 'grad_g_kv_latent': 'grad_w', 'grad_w_uq': 'grad_w', 'grad_w_ukv': 'grad_w', 'grad_g_q_head': 'grad_w', 'grad_g_k_head': 'grad_w', 'grad_w_proj_mla': 'grad_w', 'grad_w_proj_sb': 'grad_w', 'grad_w_out': 'grad_w', 'grad_w_ffn_in': 'grad_w', 'grad_w_ffn_out': 'grad_w', 'delta_w_ada': 'delta_w', 'delta_b_ada': 'delta_w', 'delta_g_norm1': 'delta_w', 'delta_g_norm2': 'delta_w', 'delta_w_in': 'delta_w', 'delta_g_q_latent': 'delta_w', 'delta_g_kv_latent': 'delta_w', 'delta_w_uq': 'delta_w', 'delta_w_ukv': 'delta_w', 'delta_g_q_head': 'delta_w', 'delta_g_k_head': 'delta_w', 'delta_w_proj_mla': 'delta_w', 'delta_w_proj_sb': 'delta_w', 'delta_w_out': 'delta_w', 'delta_w_ffn_in': 'delta_w', 'delta_w_ffn_out': 'delta_w', 'new_m_w_ada': 'new_m', 'new_m_b_ada': 'new_m', 'new_m_g_norm1': 'new_m', 'new_m_g_norm2': 'new_m', 'new_m_w_in': 'new_m', 'new_m_g_q_latent': 'new_m', 'new_m_g_kv_latent': 'new_m', 'new_m_w_uq': 'new_m', 'new_m_w_ukv': 'new_m', 'new_m_g_q_head': 'new_m', 'new_m_g_k_head': 'new_m', 'new_m_w_proj_mla': 'new_m', 'new_m_w_proj_sb': 'new_m', 'new_m_w_out': 'new_m', 'new_m_w_ffn_in': 'new_m', 'new_m_w_ffn_out': 'new_m', 'new_v_w_ada': 'new_v', 'new_v_b_ada': 'new_v', 'new_v_g_norm1': 'new_v', 'new_v_g_norm2': 'new_v', 'new_v_w_in': 'new_v', 'new_v_g_q_latent': 'new_v', 'new_v_g_kv_latent': 'new_v', 'new_v_w_uq': 'new_v', 'new_v_w_ukv': 'new_v', 'new_v_g_q_head': 'new_v', 'new_v_g_k_head': 'new_v', 'new_v_w_proj_mla': 'new_v', 'new_v_w_proj_sb': 'new_v', 'new_v_w_out': 'new_v', 'new_v_w_ffn_in': 'new_v', 'new_v_w_ffn_out': 'new_v'}


def _forward(args):
    return _fwd_reference(*[args[k] for k in FWD_PARAMS])


def _output_shape():
    def fwd():
        inp = _fwd_setup_inputs(0)
        return _fwd_reference(*[inp[k] for k in FWD_PARAMS])
    out = _jax.eval_shape(fwd)
    return out.shape, out.dtype

N_MICROBATCH = 1
ADAM_LR = 0.001
ADAM_B1 = 0.9
ADAM_B2 = 0.999
ADAM_EPS = 1e-08
ADAM_WD = 0.01
ADAM_STEP = 10
PER_EXAMPLE_BATCH_AXIS = {'x': 0, 'c': 0, 'positions': 0, 'loss_target': 0}
SHARED_INPUTS = []
_WEIGHT_DTYPES = {'w_ada': _jnp.float32, 'b_ada': _jnp.float32, 'g_norm1': _jnp.float32, 'g_norm2': _jnp.float32, 'w_in': _jnp.float32, 'g_q_latent': _jnp.float32, 'g_kv_latent': _jnp.float32, 'w_uq': _jnp.float32, 'w_ukv': _jnp.float32, 'g_q_head': _jnp.float32, 'g_k_head': _jnp.float32, 'w_proj_mla': _jnp.float32, 'w_proj_sb': _jnp.float32, 'w_out': _jnp.float32, 'w_ffn_in': _jnp.float32, 'w_ffn_out': _jnp.float32}
MOMENT_SCALE = {'w_ada': 1.826848e+00, 'b_ada': 4.812662e+00, 'g_norm1': 1.548094e+00, 'g_norm2': 1.291537e+01, 'w_in': 5.968249e-01, 'g_q_latent': 3.293415e-02, 'g_kv_latent': 2.341585e+00, 'w_uq': 1.803132e-02, 'w_ukv': 6.382865e-01, 'g_q_head': 9.299202e-02, 'g_k_head': 9.462631e-02, 'w_proj_mla': 6.090649e-01, 'w_proj_sb': 7.760147e-01, 'w_out': 9.236784e-01, 'w_ffn_in': 4.761113e-01, 'w_ffn_out': 3.941187e-01}


def _to_microbatches(a, axis):
    t = _jnp.moveaxis(a, axis, 0)
    t = t.reshape((N_MICROBATCH, t.shape[0] // N_MICROBATCH) + t.shape[1:])
    return _jnp.moveaxis(t, 1, axis + 1)


def setup_inputs(seed: int = 0) -> dict:
    inp = _fwd_setup_inputs(seed)
    key = _jax.random.fold_in(_jax.random.key(seed), 7919)
    shape, _ = _output_shape()
    out = dict(inp)
    out["loss_target"] = _jax.random.normal(_jax.random.fold_in(key, 0), shape, _jnp.float32)
    for i, name in enumerate(TWIN_WEIGHTS):
        w = inp[name].astype(_jnp.float32)
        if MOMENT_SCALE is None:
            s = _jnp.sqrt(_jnp.mean(_jnp.square(w)) + 1e-30)
        else:
            s = MOMENT_SCALE[name]
        km, kv = _jax.random.split(_jax.random.fold_in(key, i + 1))
        out[name] = w
        out["m_" + name] = s * _jax.random.normal(km, w.shape, _jnp.float32)
        out["v_" + name] = (s * s) * _jax.random.uniform(kv, w.shape, _jnp.float32, 0.5, 1.5)
    if N_MICROBATCH > 1:
        for name, axis in PER_EXAMPLE_BATCH_AXIS.items():
            out[name] = _to_microbatches(out[name], axis)
    return {'x': out['x'], 'c': out['c'], 'positions': out['positions'], 'w_ada': out['w_ada'], 'b_ada': out['b_ada'], 'g_norm1': out['g_norm1'], 'g_norm2': out['g_norm2'], 'w_in': out['w_in'], 'g_q_latent': out['g_q_latent'], 'g_kv_latent': out['g_kv_latent'], 'w_uq': out['w_uq'], 'w_ukv': out['w_ukv'], 'g_q_head': out['g_q_head'], 'g_k_head': out['g_k_head'], 'w_proj_mla': out['w_proj_mla'], 'w_proj_sb': out['w_proj_sb'], 'w_out': out['w_out'], 'w_ffn_in': out['w_ffn_in'], 'w_ffn_out': out['w_ffn_out'], 'loss_target': out['loss_target'], 'm_w_ada': out['m_w_ada'], 'm_b_ada': out['m_b_ada'], 'm_g_norm1': out['m_g_norm1'], 'm_g_norm2': out['m_g_norm2'], 'm_w_in': out['m_w_in'], 'm_g_q_latent': out['m_g_q_latent'], 'm_g_kv_latent': out['m_g_kv_latent'], 'm_w_uq': out['m_w_uq'], 'm_w_ukv': out['m_w_ukv'], 'm_g_q_head': out['m_g_q_head'], 'm_g_k_head': out['m_g_k_head'], 'm_w_proj_mla': out['m_w_proj_mla'], 'm_w_proj_sb': out['m_w_proj_sb'], 'm_w_out': out['m_w_out'], 'm_w_ffn_in': out['m_w_ffn_in'], 'm_w_ffn_out': out['m_w_ffn_out'], 'v_w_ada': out['v_w_ada'], 'v_b_ada': out['v_b_ada'], 'v_g_norm1': out['v_g_norm1'], 'v_g_norm2': out['v_g_norm2'], 'v_w_in': out['v_w_in'], 'v_g_q_latent': out['v_g_q_latent'], 'v_g_kv_latent': out['v_g_kv_latent'], 'v_w_uq': out['v_w_uq'], 'v_w_ukv': out['v_w_ukv'], 'v_g_q_head': out['v_g_q_head'], 'v_g_k_head': out['v_g_k_head'], 'v_w_proj_mla': out['v_w_proj_mla'], 'v_w_proj_sb': out['v_w_proj_sb'], 'v_w_out': out['v_w_out'], 'v_w_ffn_in': out['v_w_ffn_in'], 'v_w_ffn_out': out['v_w_ffn_out']}


def _loss(weights, diff, rest, loss_target):
    with _jax.named_scope("forward"):
        args = {**rest, TWIN_DIFF_INPUT: diff, **{k: w.astype(_WEIGHT_DTYPES[k]) for k, w in weights.items()}}
        y = _forward(args)
    with _jax.named_scope("loss_head"):
        err = _jnp.square(y.astype(_jnp.float32) - loss_target)
        return 0.5 * _jnp.sum(_jnp.mean(err, axis=-1)) if err.ndim else 0.5 * err


def _adamw(w, g, m, v):
    m = ADAM_B1 * m + (1.0 - ADAM_B1) * g
    v = ADAM_B2 * v + (1.0 - ADAM_B2) * _jnp.square(g)
    m_hat = m / (1.0 - ADAM_B1 ** ADAM_STEP)
    v_hat = v / (1.0 - ADAM_B2 ** ADAM_STEP)
    delta = -ADAM_LR * (m_hat / (_jnp.sqrt(v_hat) + ADAM_EPS) + ADAM_WD * w)
    return delta, m, v


def reference(x, c, positions, w_ada, b_ada, g_norm1, g_norm2, w_in, g_q_latent, g_kv_latent, w_uq, w_ukv, g_q_head, g_k_head, w_proj_mla, w_proj_sb, w_out, w_ffn_in, w_ffn_out, loss_target, m_w_ada, m_b_ada, m_g_norm1, m_g_norm2, m_w_in, m_g_q_latent, m_g_kv_latent, m_w_uq, m_w_ukv, m_g_q_head, m_g_k_head, m_w_proj_mla, m_w_proj_sb, m_w_out, m_w_ffn_in, m_w_ffn_out, v_w_ada, v_b_ada, v_g_norm1, v_g_norm2, v_w_in, v_g_q_latent, v_g_kv_latent, v_w_uq, v_w_ukv, v_g_q_head, v_g_k_head, v_w_proj_mla, v_w_proj_sb, v_w_out, v_w_ffn_in, v_w_ffn_out):
    given = dict(x=x, c=c, positions=positions, w_ada=w_ada, b_ada=b_ada, g_norm1=g_norm1, g_norm2=g_norm2, w_in=w_in, g_q_latent=g_q_latent, g_kv_latent=g_kv_latent, w_uq=w_uq, w_ukv=w_ukv, g_q_head=g_q_head, g_k_head=g_k_head, w_proj_mla=w_proj_mla, w_proj_sb=w_proj_sb, w_out=w_out, w_ffn_in=w_ffn_in, w_ffn_out=w_ffn_out, loss_target=loss_target, m_w_ada=m_w_ada, m_b_ada=m_b_ada, m_g_norm1=m_g_norm1, m_g_norm2=m_g_norm2, m_w_in=m_w_in, m_g_q_latent=m_g_q_latent, m_g_kv_latent=m_g_kv_latent, m_w_uq=m_w_uq, m_w_ukv=m_w_ukv, m_g_q_head=m_g_q_head, m_g_k_head=m_g_k_head, m_w_proj_mla=m_w_proj_mla, m_w_proj_sb=m_w_proj_sb, m_w_out=m_w_out, m_w_ffn_in=m_w_ffn_in, m_w_ffn_out=m_w_ffn_out, v_w_ada=v_w_ada, v_b_ada=v_b_ada, v_g_norm1=v_g_norm1, v_g_norm2=v_g_norm2, v_w_in=v_w_in, v_g_q_latent=v_g_q_latent, v_g_kv_latent=v_g_kv_latent, v_w_uq=v_w_uq, v_w_ukv=v_w_ukv, v_g_q_head=v_g_q_head, v_g_k_head=v_g_k_head, v_w_proj_mla=v_w_proj_mla, v_w_proj_sb=v_w_proj_sb, v_w_out=v_w_out, v_w_ffn_in=v_w_ffn_in, v_w_ffn_out=v_w_ffn_out)
    weights = {n: given[n] for n in TWIN_WEIGHTS}
    shared = {n: given[n] for n in SHARED_INPUTS}
    per_example = {n: given[n] for n in ['x', 'c', 'positions']}
    grad_fn = _jax.value_and_grad(_loss, argnums=(0, 1))

    def one_microbatch(ex, loss_target):
        ex = dict(ex)
        diff = ex.pop(TWIN_DIFF_INPUT)
        return grad_fn(weights, diff, {**shared, **ex}, loss_target)

    if N_MICROBATCH == 1:
        loss, (grad_w, grad_x) = one_microbatch(per_example, given["loss_target"])
    else:
        def body(carry, xs):
            loss_sum, grad_sum = carry
            l_k, (gw_k, gx_k) = one_microbatch(xs[0], xs[1])
            with _jax.named_scope("update"):
                return (loss_sum + l_k, _jax.tree.map(_jnp.add, grad_sum, gw_k)), gx_k

        init = (_jnp.zeros((), _jnp.float32), _jax.tree.map(_jnp.zeros_like, weights))
        (loss, grad_w), grad_x = _jax.lax.scan(body, init, (per_example, given["loss_target"]))
    with _jax.named_scope("update"):
        delta_w, new_m, new_v = {}, {}, {}
        for n in TWIN_WEIGHTS:
            delta_w[n], new_m[n], new_v[n] = _adamw(weights[n], grad_w[n], given["m_" + n], given["v_" + n])
    return (loss, grad_x, *[grad_w[n] for n in TWIN_WEIGHTS], *[delta_w[n] for n in TWIN_WEIGHTS],
            *[new_m[n] for n in TWIN_WEIGHTS], *[new_v[n] for n in TWIN_WEIGHTS])
```

```python
import functools

import jax
import jax.numpy as jnp
from jax import lax
from jax.experimental import pallas as pl
from jax.experimental.pallas import tpu as pltpu

F32 = jnp.float32
BF16 = jnp.bfloat16

N_DEV = 8
LANES = 128
PACK_W = 1024
VMEM_LIMIT = 48 * 1024 * 1024

EPS = 1e-6
ROPE_THETA = 10000.0
NOPE = 128
ROPE = 64
QK_DIM = NOPE + ROPE
QK_PAD = 2 * LANES
V_DIM = 128
SB_DIM = 128
ATT_BLK = 256

ADAM_LR = 0.001
ADAM_B1 = 0.9
ADAM_B2 = 0.999
ADAM_EPS = 1e-08
ADAM_WD = 0.01
ADAM_STEP = 10

WEIGHT_NAMES = ("w_ada", "b_ada", "g_norm1", "g_norm2", "w_in", "g_q_latent", "g_kv_latent", "w_uq", "w_ukv",
                "g_q_head", "g_k_head", "w_proj_mla", "w_proj_sb", "w_out", "w_ffn_in", "w_ffn_out")
BIG = (("w_in", True), ("w_uq", True), ("w_ukv", True), ("w_proj_mla", True), ("w_proj_sb", True),
       ("w_out", False), ("w_ffn_in", True), ("w_ffn_out", False))
SMALL = ("b_ada", "g_norm1", "g_norm2", "g_q_latent", "g_kv_latent", "g_q_head", "g_k_head")


def _tile(n, pref):
    if n <= pref:
        return n
    for step in (LANES, 8):
        for t in range(pref - pref % step, 0, -step):
            if n % t == 0:
                return t
    return n


def _params():
    return pltpu.CompilerParams(vmem_limit_bytes=VMEM_LIMIT)


def _sigmoid(x):
    return 1.0 / (1.0 + jnp.exp(-x))


def _dot(a, b):
    return lax.dot_general(a, b, (((1,), (0,)), ((), ())), preferred_element_type=F32)


def _dot_nt(a, b):
    return lax.dot_general(a, b, (((1,), (1,)), ((), ())), preferred_element_type=F32)


def _dot_tn(a, b):
    return lax.dot_general(a, b, (((0,), (0,)), ((), ())), preferred_element_type=F32)


def _split_dot(x, tri):
    hi = x.astype(BF16)
    lo = (x - hi.astype(F32)).astype(BF16)
    return _dot(hi, tri) + _dot(lo, tri)


def _exchange(src, *, gather, name):
    slot_shape = tuple(src.shape) if gather else tuple(src.shape[1:])

    def body(src_ref, dst_ref, send_sems, recv_sems, local_sem):
        x, y, c = lax.axis_index("x"), lax.axis_index("y"), lax.axis_index("c")
        me = 4 * x + 2 * y + c

        def slot_for(idx):
            return src_ref if gather else src_ref.at[idx]

        own = pltpu.make_async_copy(slot_for(me), dst_ref.at[me], local_sem)
        own.start()
        copies = []
        for k in range(1, N_DEV):
            peer = ((1 - x) if (k >> 2) & 1 else x, (1 - y) if (k >> 1) & 1 else y, (1 - c) if k & 1 else c)
            peer_idx = 4 * peer[0] + 2 * peer[1] + peer[2]
            cp = pltpu.make_async_remote_copy(
                src_ref=slot_for(peer_idx), dst_ref=dst_ref.at[me],
                send_sem=send_sems.at[k - 1], recv_sem=recv_sems.at[k - 1],
                device_id=peer, device_id_type=pl.DeviceIdType.MESH)
            cp.start()
            copies.append(cp)
        for cp in copies:
            cp.wait()
        own.wait()

    return pl.pallas_call(
        body, name=name,
        out_shape=jax.ShapeDtypeStruct((N_DEV,) + slot_shape, src.dtype),
        in_specs=[pl.BlockSpec(memory_space=pltpu.HBM)],
        out_specs=pl.BlockSpec(memory_space=pltpu.HBM),
        scratch_shapes=[pltpu.SemaphoreType.DMA((N_DEV - 1,)), pltpu.SemaphoreType.DMA((N_DEV - 1,)),
                        pltpu.SemaphoreType.DMA(())],
    )(src)


def _gather_rows(v, name):
    out = _exchange(jnp.broadcast_to(v, (8, v.shape[1])), gather=True, name=name)
    return out[:, 0, :]


def _mm(a, b, *, ta=False, tb=False, out_dtype=F32, name):
    kdim, m = a.shape if ta else a.shape[::-1]
    n, kdim_b = b.shape if tb else b.shape[::-1]
    assert kdim == kdim_b, (a.shape, b.shape, ta, tb)
    tm, tn, tk = _tile(m, 1024), _tile(n, 1024), _tile(kdim, 512)
    nk = kdim // tk
    dims = (((0 if ta else 1,), (1 if tb else 0,)), ((), ()))

    def body(a_ref, b_ref, o_ref, acc_ref):
        k = pl.program_id(2)

        @pl.when(k == 0)
        def _():
            acc_ref[...] = jnp.zeros_like(acc_ref)

        acc_ref[...] += lax.dot_general(a_ref[...].astype(BF16), b_ref[...].astype(BF16), dims,
                                        preferred_element_type=F32)

        @pl.when(k == nk - 1)
        def _():
            o_ref[...] = acc_ref[...].astype(o_ref.dtype)

    a_spec = (pl.BlockSpec((tk, tm), lambda i, j, k: (k, i)) if ta else pl.BlockSpec((tm, tk), lambda i, j, k: (i, k)))
    b_spec = (pl.BlockSpec((tn, tk), lambda i, j, k: (j, k)) if tb else pl.BlockSpec((tk, tn), lambda i, j, k: (k, j)))
    return pl.pallas_call(
        body, name=name, grid=(m // tm, n // tn, nk),
        out_shape=jax.ShapeDtypeStruct((m, n), out_dtype),
        in_specs=[a_spec, b_spec],
        out_specs=pl.BlockSpec((tm, tn), lambda i, j, k: (i, j)),
        scratch_shapes=[pltpu.VMEM((tm, tn), F32)],
        compiler_params=_params(),
    )(a, b)


def _ada_fwd(c_all, w_shard):
    d, n = w_shard.shape
    tn = _tile(n, 512)

    def body(c_ref, w_ref, o_ref):
        cv = c_ref[...]
        o_ref[...] = jnp.dot(cv * _sigmoid(cv), w_ref[...], precision=lax.Precision.HIGHEST,
                             preferred_element_type=F32)

    return pl.pallas_call(
        body, name="ada_fwd", grid=(n // tn,),
        out_shape=jax.ShapeDtypeStruct((N_DEV, n), F32),
        in_specs=[pl.BlockSpec((N_DEV, d), lambda j: (0, 0)), pl.BlockSpec((d, tn), lambda j: (0, j))],
        out_specs=pl.BlockSpec((N_DEV, tn), lambda j: (0, j)),
        compiler_params=_params(),
    )(c_all, w_shard)


def _ada_grad(c_all_t, d_rows):
    d, n = c_all_t.shape[0], d_rows.shape[1]
    tn = _tile(n, 512)

    def body(ct_ref, d_ref, o_ref):
        cv = ct_ref[...]
        s = cv * _sigmoid(cv)
        dv = d_ref[...]
        acc = s[:, 0:1] * dv[0:1, :]
        for b in range(1, N_DEV):
            acc = acc + s[:, b:b + 1] * dv[b:b + 1, :]
        o_ref[...] = acc

    return pl.pallas_call(
        body, name="ada_grad", grid=(n // tn,),
        out_shape=jax.ShapeDtypeStruct((d, n), F32),
        in_specs=[pl.BlockSpec((d, N_DEV), lambda j: (0, 0)), pl.BlockSpec((N_DEV, tn), lambda j: (0, j))],
        out_specs=pl.BlockSpec((d, tn), lambda j: (0, j)),
        compiler_params=_params(),
    )(c_all_t, d_rows)


def _row(ts, w, col=0):
    return pl.BlockSpec((ts, w), lambda i, col=col: (i, col))


def _vec(w):
    return pl.BlockSpec((1, w), lambda i: (0, 0))


def _norm_mod(x, g, sc, sh):
    s, d = x.shape
    ts = _tile(s, 512)

    def body(x_ref, g_ref, sc_ref, sh_ref, h_ref):
        xv = x_ref[...]
        r = lax.rsqrt(jnp.mean(xv * xv, axis=-1, keepdims=True) + EPS)
        h_ref[...] = ((xv * r) * g_ref[...] * (1.0 + sc_ref[...]) + sh_ref[...]).astype(BF16)

    return pl.pallas_call(
        body, name="norm_mod", grid=(s // ts,),
        out_shape=jax.ShapeDtypeStruct((s, d), BF16),
        in_specs=[_row(ts, d), _vec(d), _vec(d), _vec(d)],
        out_specs=_row(ts, d), compiler_params=_params(),
    )(x, g, sc, sh)


def _latent_norm(proj, off_q, off_kv, lat, g_q, g_kv):
    s = proj.shape[0]
    ts = _tile(s, 512)

    def body(cq_ref, ckv_ref, gq_ref, gkv_ref, oq_ref, okv_ref):
        for c_ref, g_ref, o_ref in ((cq_ref, gq_ref, oq_ref), (ckv_ref, gkv_ref, okv_ref)):
            v = c_ref[...]
            r = lax.rsqrt(jnp.mean(v * v, axis=-1, keepdims=True) + EPS)
            o_ref[...] = ((v * r) * g_ref[...]).astype(BF16)

    return pl.pallas_call(
        body, name="latent_norm", grid=(s // ts,),
        out_shape=(jax.ShapeDtypeStruct((s, lat), BF16), jax.ShapeDtypeStruct((s, lat), BF16)),
        in_specs=[_row(ts, lat, off_q // lat), _row(ts, lat, off_kv // lat), _vec(lat), _vec(lat)],
        out_specs=(_row(ts, lat), _row(ts, lat)), compiler_params=_params(),
    )(proj, proj, g_q, g_kv)


def _latent_norm_bwd(proj, off_q, off_kv, lat, g_q, g_kv, d_cqn, d_ckvn):
    s = proj.shape[0]
    ts = _tile(s, 512)

    def body(cq_ref, ckv_ref, gq_ref, gkv_ref, dq_ref, dkv_ref, oq_ref, okv_ref, sums_ref):
        @pl.when(pl.program_id(0) == 0)
        def _():
            sums_ref[...] = jnp.zeros_like(sums_ref)

        for row, (c_ref, g_ref, d_ref, o_ref) in enumerate(((cq_ref, gq_ref, dq_ref, oq_ref),
                                                             (ckv_ref, gkv_ref, dkv_ref, okv_ref))):
            v = c_ref[...]
            r = lax.rsqrt(jnp.mean(v * v, axis=-1, keepdims=True) + EPS)
            vn = v * r
            dn = d_ref[...]
            sums_ref[row:row + 1, :] += jnp.sum(dn * vn, axis=0, keepdims=True)
            dvn = dn * g_ref[...]
            o_ref[...] = (r * (dvn - vn * jnp.mean(dvn * vn, axis=-1, keepdims=True))).astype(BF16)

    return pl.pallas_call(
        body, name="latent_norm_bwd", grid=(s // ts,),
        out_shape=(jax.ShapeDtypeStruct((s, lat), BF16), jax.ShapeDtypeStruct((s, lat), BF16),
                   jax.ShapeDtypeStruct((8, lat), F32)),
        in_specs=[_row(ts, lat, off_q // lat), _row(ts, lat, off_kv // lat), _vec(lat), _vec(lat),
                  _row(ts, lat), _row(ts, lat)],
        out_specs=(_row(ts, lat), _row(ts, lat), pl.BlockSpec((8, lat), lambda i: (0, 0))),
        compiler_params=_params(),
    )(proj, proj, g_q, g_kv, d_cqn, d_ckvn)


def _merge(proj, off_a, off_b, ya_p, yb_p):
    s, d = ya_p.shape
    ts = _tile(s, 256)

    def body(ga_ref, gb_ref, ya_ref, yb_ref, o_ref):
        o_ref[...] = (_sigmoid(ga_ref[...]) * ya_ref[...] + _sigmoid(gb_ref[...]) * yb_ref[...]).astype(BF16)

    return pl.pallas_call(
        body, name="merge", grid=(s // ts,),
        out_shape=jax.ShapeDtypeStruct((s, d), BF16),
        in_specs=[_row(ts, d, off_a // d), _row(ts, d, off_b // d), _row(ts, d), _row(ts, d)],
        out_specs=_row(ts, d), compiler_params=_params(),
    )(proj, proj, ya_p, yb_p)


def _merge_bwd(proj, off_a, off_b, ya_p, yb_p, d_merged):
    s, d = ya_p.shape
    ts = _tile(s, 256)

    def body(ga_ref, gb_ref, ya_ref, yb_ref, dm_ref, dya_ref, dyb_ref, dga_ref, dgb_ref):
        dm = dm_ref[...]
        for g_ref, y_ref, dy_ref, dg_ref in ((ga_ref, ya_ref, dya_ref, dga_ref), (gb_ref, yb_ref, dyb_ref, dgb_ref)):
            sg = _sigmoid(g_ref[...])
            dy_ref[...] = (dm * sg).astype(BF16)
            dg_ref[...] = (dm * y_ref[...] * sg * (1.0 - sg)).astype(BF16)

    sd = jax.ShapeDtypeStruct((s, d), BF16)
    return pl.pallas_call(
        body, name="merge_bwd", grid=(s // ts,),
        out_shape=(sd, sd, sd, sd),
        in_specs=[_row(ts, d, off_a // d), _row(ts, d, off_b // d), _row(ts, d), _row(ts, d), _row(ts, d)],
        out_specs=(_row(ts, d),) * 4, compiler_params=_params(),
    )(proj, proj, ya_p, yb_p, d_merged)


def _resid_norm_mod(x, o, gt, g, sc, sh):
    s, d = x.shape
    ts = _tile(s, 256)

    def body(x_ref, o_ref, gt_ref, g_ref, sc_ref, sh_ref, x1_ref, h_ref):
        x1 = x_ref[...] + gt_ref[...] * o_ref[...]
        x1_ref[...] = x1
        r = lax.rsqrt(jnp.mean(x1 * x1, axis=-1, keepdims=True) + EPS)
        h_ref[...] = ((x1 * r) * g_ref[...] * (1.0 + sc_ref[...]) + sh_ref[...]).astype(BF16)

    return pl.pallas_call(
        body, name="resid_norm_mod", grid=(s // ts,),
        out_shape=(jax.ShapeDtypeStruct((s, d), F32), jax.ShapeDtypeStruct((s, d), BF16)),
        in_specs=[_row(ts, d), _row(ts, d), _vec(d), _vec(d), _vec(d), _vec(d)],
        out_specs=(_row(ts, d), _row(ts, d)), compiler_params=_params(),
    )(x, o, gt, g, sc, sh)


def _loss_head(x1, o2, gt2, target):
    s, d = x1.shape
    ts = _tile(s, 256)

    def body(x1_ref, o2_ref, gt_ref, t_ref, dy_ref, do_ref, sums_ref):
        @pl.when(pl.program_id(0) == 0)
        def _():
            sums_ref[...] = jnp.zeros_like(sums_ref)

        o2 = o2_ref[...]
        e = x1_ref[...] + gt_ref[...] * o2 - t_ref[...]
        dy = e / d
        dy_ref[...] = dy
        do_ref[...] = (dy * gt_ref[...]).astype(BF16)
        sums_ref[0:1, :] += jnp.sum(dy * o2, axis=0, keepdims=True)
        sums_ref[1:2, :] += jnp.sum(e * e, axis=0, keepdims=True)

    return pl.pallas_call(
        body, name="loss_head", grid=(s // ts,),
        out_shape=(jax.ShapeDtypeStruct((s, d), F32), jax.ShapeDtypeStruct((s, d), BF16),
                   jax.ShapeDtypeStruct((8, d), F32)),
        in_specs=[_row(ts, d), _row(ts, d), _vec(d), _row(ts, d)],
        out_specs=(_row(ts, d), _row(ts, d), pl.BlockSpec((8, d), lambda i: (0, 0))),
        compiler_params=_params(),
    )(x1, o2, gt2, target)


def _norm_mod_bwd(dh, xin, g, sc, dres, gate=None):
    s, d = xin.shape
    ts = _tile(s, 256)
    gated = gate is not None

    def body(*refs):
        if gated:
            dh_ref, x_ref, g_ref, sc_ref, dr_ref, o_ref, gt_ref, dx_ref, do_ref, sums_ref = refs
        else:
            dh_ref, x_ref, g_ref, sc_ref, dr_ref, dx_ref, sums_ref = refs

        @pl.when(pl.program_id(0) == 0)
        def _():
            sums_ref[...] = jnp.zeros_like(sums_ref)

        xv, dhv = x_ref[...], dh_ref[...]
        r = lax.rsqrt(jnp.mean(xv * xv, axis=-1, keepdims=True) + EPS)
        xn = xv * r
        one_sc = 1.0 + sc_ref[...]
        sums_ref[0:1, :] += jnp.sum(dhv, axis=0, keepdims=True)
        sums_ref[1:2, :] += jnp.sum(dhv * (xn * g_ref[...]), axis=0, keepdims=True)
        sums_ref[2:3, :] += jnp.sum(dhv * one_sc * xn, axis=0, keepdims=True)
        dxn = dhv * one_sc * g_ref[...]
        dx = dr_ref[...] + r * (dxn - xn * jnp.mean(dxn * xn, axis=-1, keepdims=True))
        dx_ref[...] = dx
        if gated:
            sums_ref[3:4, :] += jnp.sum(dx * o_ref[...], axis=0, keepdims=True)
            do_ref[...] = (dx * gt_ref[...]).astype(BF16)

    in_specs = [_row(ts, d), _row(ts, d), _vec(d), _vec(d), _row(ts, d)]
    args = [dh, xin, g, sc, dres]
    out_shape = [jax.ShapeDtypeStruct((s, d), F32)]
    out_specs = [_row(ts, d)]
    if gated:
        in_specs += [_row(ts, d), _vec(d)]
        args += list(gate)
        out_shape.append(jax.ShapeDtypeStruct((s, d), BF16))
        out_specs.append(_row(ts, d))
    out_shape.append(jax.ShapeDtypeStruct((8, d), F32))
    out_specs.append(pl.BlockSpec((8, d), lambda i: (0, 0)))
    return pl.pallas_call(
        body, name="norm_mod_bwd_gated" if gated else "norm_mod_bwd", grid=(s // ts,),
        out_shape=tuple(out_shape), in_specs=in_specs, out_specs=tuple(out_specs), compiler_params=_params(),
    )(*args)


def _swiglu(gu):
    s, f2 = gu.shape
    f = f2 // 2
    ts, tc = _tile(s, 512), _tile(f, 512)
    nf = f // tc

    def body(g_ref, u_ref, o_ref):
        gv = g_ref[...]
        o_ref[...] = (gv * _sigmoid(gv) * u_ref[...]).astype(BF16)

    return pl.pallas_call(
        body, name="swiglu", grid=(s // ts, nf),
        out_shape=jax.ShapeDtypeStruct((s, f), BF16),
        in_specs=[pl.BlockSpec((ts, tc), lambda i, j: (i, j)), pl.BlockSpec((ts, tc), lambda i, j: (i, nf + j))],
        out_specs=pl.BlockSpec((ts, tc), lambda i, j: (i, j)), compiler_params=_params(),
    )(gu, gu)


def _swiglu_bwd(gu, d_act):
    s, f2 = gu.shape
    f = f2 // 2
    ts, tc = _tile(s, 512), _tile(f, 512)
    nf = f // tc

    def body(g_ref, u_ref, da_ref, o_ref):
        gv, da = g_ref[...], da_ref[...]
        sg = _sigmoid(gv)
        d_gate = da * u_ref[...] * (sg * (1.0 + gv * (1.0 - sg)))
        d_up = da * (gv * sg)
        o_ref[...] = jnp.where(pl.program_id(1) < nf, d_gate, d_up).astype(BF16)

    return pl.pallas_call(
        body, name="swiglu_bwd", grid=(s // ts, 2 * nf),
        out_shape=jax.ShapeDtypeStruct((s, f2), BF16),
        in_specs=[pl.BlockSpec((ts, tc), lambda i, j: (i, j % nf)),
                  pl.BlockSpec((ts, tc), lambda i, j: (i, nf + j % nf)),
                  pl.BlockSpec((ts, tc), lambda i, j: (i, j % nf))],
        out_specs=pl.BlockSpec((ts, tc), lambda i, j: (i, j)), compiler_params=_params(),
    )(gu, gu, d_act)


def _swap_halves(t):
    return pltpu.roll(t, ROPE // 2, 1) + pltpu.roll(t, LANES - ROPE // 2, 1)


def _head_norm(raw, g):
    r = lax.rsqrt(jnp.sum(raw * raw, axis=-1, keepdims=True) / QK_DIM + EPS)
    return raw * r, r


def _rope_fwd(v, cos, sin):
    rope_tile = v[:, NOPE:]
    return jnp.concatenate([v[:, :NOPE], rope_tile * cos + _swap_halves(rope_tile) * sin], axis=1)


def _rope_bwd(d, cos, sin, lane_ok):
    d_tile = d[:, NOPE:]
    return jnp.concatenate([d[:, :NOPE], d_tile * cos + _swap_halves(d_tile * sin) * lane_ok], axis=1)


def _qk_prep(q_raw, kv, proj, off_pe, cos, sin, g_q, g_k):
    s, hw = q_raw.shape
    heads = hw // QK_PAD
    ts = _tile(s, 512)

    def body(q_ref, kv_ref, pe_ref, cos_ref, sin_ref, gq_ref, gk_ref, qo_ref, ko_ref, vo_ref):
        cos_v, sin_v = cos_ref[...], sin_ref[...]
        qn, _ = _head_norm(q_ref[...], None)
        qo_ref[...] = _rope_fwd(qn * gq_ref[...], cos_v, sin_v).astype(BF16)
        kvv = kv_ref[...]
        kn, _ = _head_norm(jnp.concatenate([kvv[:, :NOPE], pe_ref[...]], axis=1), None)
        ko_ref[...] = _rope_fwd(kn * gk_ref[...], cos_v, sin_v).astype(BF16)
        vo_ref[...] = kvv[:, NOPE:].astype(BF16)

    blk = lambda w: pl.BlockSpec((ts, w), lambda i, h: (i, h))
    fixed = lambda w, col=0: pl.BlockSpec((ts, w), lambda i, h, col=col: (i, col))
    vec = pl.BlockSpec((1, QK_PAD), lambda i, h: (0, 0))
    return pl.pallas_call(
        body, name="qk_prep", grid=(s // ts, heads),
        out_shape=(jax.ShapeDtypeStruct((s, hw), BF16), jax.ShapeDtypeStruct((s, hw), BF16),
                   jax.ShapeDtypeStruct((s, heads * V_DIM), BF16)),
        in_specs=[blk(QK_PAD), blk(QK_PAD), fixed(LANES, off_pe // LANES), fixed(LANES), fixed(LANES), vec, vec],
        out_specs=(blk(QK_PAD), blk(QK_PAD), blk(V_DIM)), compiler_params=_params(),
    )(q_raw, kv, proj, cos, sin, g_q, g_k)


def _qk_prep_bwd(q_raw, kv, proj, off_pe, cos, sin, g_q, g_k, dq, dk, dv):
    s, hw = q_raw.shape
    heads = hw // QK_PAD
    ts = _tile(s, 512)

    def body(q_ref, kv_ref, pe_ref, cos_ref, sin_ref, gq_ref, gk_ref, dq_ref, dk_ref, dv_ref,
             dqr_ref, dkv_ref, dpe_ref, sums_ref):
        i, h = pl.program_id(0), pl.program_id(1)

        @pl.when((i == 0) & (h == 0))
        def _():
            sums_ref[...] = jnp.zeros_like(sums_ref)

        cos_v, sin_v = cos_ref[...], sin_ref[...]
        lane_ok = (lax.broadcasted_iota(jnp.int32, (ts, LANES), 1) < ROPE).astype(F32)

        def one(raw, g, d_post, row):
            vn, r = _head_norm(raw, None)
            d_pre = _rope_bwd(d_post, cos_v, sin_v, lane_ok)
            sums_ref[row:row + 1, :] += jnp.sum(d_pre * vn, axis=0, keepdims=True)
            dvn = d_pre * g
            return r * (dvn - vn * (jnp.sum(dvn * vn, axis=-1, keepdims=True) / QK_DIM))

        dqr_ref[...] = one(q_ref[...], gq_ref[...], dq_ref[...], 0).astype(BF16)
        kvv = kv_ref[...]
        d_kraw = one(jnp.concatenate([kvv[:, :NOPE], pe_ref[...]], axis=1), gk_ref[...], dk_ref[...], 1)
        dkv_ref[...] = jnp.concatenate([d_kraw[:, :NOPE], dv_ref[...]], axis=1).astype(BF16)

        @pl.when(h == 0)
        def _():
            dpe_ref[...] = jnp.zeros_like(dpe_ref)

        dpe_ref[...] += d_kraw[:, NOPE:]

    blk = lambda w: pl.BlockSpec((ts, w), lambda i, h: (i, h))
    fixed = lambda w, col=0: pl.BlockSpec((ts, w), lambda i, h, col=col: (i, col))
    vec = pl.BlockSpec((1, QK_PAD), lambda i, h: (0, 0))
    return pl.pallas_call(
        body, name="qk_prep_bwd", grid=(s // ts, heads),
        out_shape=(jax.ShapeDtypeStruct((s, hw), BF16), jax.ShapeDtypeStruct((s, hw), BF16),
                   jax.ShapeDtypeStruct((s, LANES), F32), jax.ShapeDtypeStruct((8, QK_PAD), F32)),
        in_specs=[blk(QK_PAD), blk(QK_PAD), fixed(LANES, off_pe // LANES), fixed(LANES), fixed(LANES), vec, vec,
                  blk(QK_PAD), blk(QK_PAD), blk(V_DIM)],
        out_specs=(blk(QK_PAD), blk(QK_PAD), fixed(LANES), pl.BlockSpec((8, QK_PAD), lambda i, h: (0, 0))),
        compiler_params=_params(),
    )(q_raw, kv, proj, cos, sin, g_q, g_k, dq, dk, dv)


def _block_ids(qi, kb, t):
    row = qi * t + lax.broadcasted_iota(jnp.int32, (t, t), 0)
    col = kb * t + lax.broadcasted_iota(jnp.int32, (t, t), 1)
    return row, col


def _mla_fwd(q, k, v):
    s = q.shape[0]
    heads = q.shape[1] // QK_PAD
    t = _tile(s, ATT_BLK)
    scale = QK_DIM ** -0.5

    def body(q_ref, k_ref, v_ref, o_ref, lse_ref):
        qi = pl.program_id(1)
        qv = q_ref[...]

        def step(kb, carry):
            m, l, acc = carry
            off = pl.multiple_of(kb * t, t)
            sc = _dot_nt(qv, k_ref[pl.ds(off, t), :]) * scale
            row, col = _block_ids(qi, kb, t)
            sc = jnp.where(col <= row, sc, -1e30)
            m_new = jnp.maximum(m, jnp.max(sc, axis=-1, keepdims=True))
            alpha = jnp.exp(m - m_new)
            p = jnp.exp(sc - m_new)
            l = alpha * l + jnp.sum(p, axis=-1, keepdims=True)
            acc = alpha * acc + _dot(p.astype(BF16), v_ref[pl.ds(off, t), :])
            return m_new, l, acc

        m, l, acc = lax.fori_loop(0, qi + 1, step, (jnp.full((t, 1), -1e30, F32), jnp.zeros((t, 1), F32),
                                                    jnp.zeros((t, V_DIM), F32)))
        o_ref[...] = (acc / l).astype(BF16)
        lse_ref[0] = m + jnp.log(l)

    return pl.pallas_call(
        body, name="mla_fwd", grid=(heads, s // t),
        out_shape=(jax.ShapeDtypeStruct((s, heads * V_DIM), BF16), jax.ShapeDtypeStruct((heads, s, 1), F32)),
        in_specs=[pl.BlockSpec((t, QK_PAD), lambda h, i: (i, h)), pl.BlockSpec((s, QK_PAD), lambda h, i: (0, h)),
                  pl.BlockSpec((s, V_DIM), lambda h, i: (0, h))],
        out_specs=(pl.BlockSpec((t, V_DIM), lambda h, i: (i, h)), pl.BlockSpec((1, t, 1), lambda h, i: (h, i, 0))),
        compiler_params=_params(),
    )(q, k, v)


def _mla_bwd(q, k, v, o, lse, do):
    s = q.shape[0]
    heads = q.shape[1] // QK_PAD
    t = _tile(s, ATT_BLK)
    scale = QK_DIM ** -0.5

    def body(q_ref, k_ref, v_ref, o_ref, lse_ref, do_ref, dq_ref, dk_ref, dv_ref):
        qi = pl.program_id(1)

        @pl.when(qi == 0)
        def _():
            dk_ref[...] = jnp.zeros_like(dk_ref)
            dv_ref[...] = jnp.zeros_like(dv_ref)

        qv = q_ref[...]
        dov = do_ref[...]
        do_b = dov.astype(BF16)
        delta = jnp.sum(dov * o_ref[...].astype(F32), axis=-1, keepdims=True)
        lse_v = lse_ref[0]

        def step(kb, dq):
            off = pl.multiple_of(kb * t, t)
            kv_ = k_ref[pl.ds(off, t), :]
            sc = _dot_nt(qv, kv_) * scale
            row, col = _block_ids(qi, kb, t)
            p = jnp.where(col <= row, jnp.exp(sc - lse_v), 0.0)
            dp = _dot_nt(do_b, v_ref[pl.ds(off, t), :])
            ds = (p * (dp - delta) * scale).astype(BF16)
            dk_ref[pl.ds(off, t), :] += _dot_tn(ds, qv)
            dv_ref[pl.ds(off, t), :] += _dot_tn(p.astype(BF16), do_b)
            return dq + _dot(ds, kv_)

        dq_ref[...] = lax.fori_loop(0, qi + 1, step, jnp.zeros((t, QK_PAD), F32))

    return pl.pallas_call(
        body, name="mla_bwd", grid=(heads, s // t),
        out_shape=(jax.ShapeDtypeStruct(q.shape, F32), jax.ShapeDtypeStruct(k.shape, F32),
                   jax.ShapeDtypeStruct(v.shape, F32)),
        in_specs=[pl.BlockSpec((t, QK_PAD), lambda h, i: (i, h)), pl.BlockSpec((s, QK_PAD), lambda h, i: (0, h)),
                  pl.BlockSpec((s, V_DIM), lambda h, i: (0, h)), pl.BlockSpec((t, V_DIM), lambda h, i: (i, h)),
                  pl.BlockSpec((1, t, 1), lambda h, i: (h, i, 0)), pl.BlockSpec((t, V_DIM), lambda h, i: (i, h))],
        out_specs=(pl.BlockSpec((t, QK_PAD), lambda h, i: (i, h)), pl.BlockSpec((s, QK_PAD), lambda h, i: (0, h)),
                   pl.BlockSpec((s, V_DIM), lambda h, i: (0, h))),
        compiler_params=_params(),
    )(q, k, v, o, lse, do)


def _sb_terms(qv, k_blk, qi, kb, t, scale):
    z = _dot_nt(qv, k_blk) * scale
    row, col = _block_ids(qi, kb, t)
    mask = col < row
    log_beta = jnp.minimum(z, 0.0) - jnp.log(1.0 + jnp.exp(-jnp.abs(z)))
    log_rest = jnp.where(mask, log_beta - z, 0.0)
    return log_beta, log_rest, mask


def _sb_fwd(proj, off_q, off_k, off_v, heads):
    s = proj.shape[0]
    t = _tile(s, ATT_BLK)
    scale = SB_DIM ** -0.5

    def body(q_ref, k_ref, v_ref, o_ref):
        qi = pl.program_id(1)
        qv = q_ref[...].astype(BF16)
        after = (lax.broadcasted_iota(jnp.int32, (t, t), 0) > lax.broadcasted_iota(jnp.int32, (t, t), 1)).astype(BF16)

        def step(i, carry):
            tail_right, acc = carry
            kb = qi - i
            off = pl.multiple_of(kb * t, t)
            log_beta, log_rest, mask = _sb_terms(qv, k_ref[pl.ds(off, t), :].astype(BF16), qi, kb, t, scale)
            tail = _split_dot(log_rest, after) + tail_right
            a = jnp.where(mask, jnp.exp(log_beta + tail), 0.0)
            acc = acc + _dot(a.astype(BF16), v_ref[pl.ds(off, t), :].astype(BF16))
            return tail_right + jnp.sum(log_rest, axis=-1, keepdims=True), acc

        _, acc = lax.fori_loop(0, qi + 1, step, (jnp.zeros((t, 1), F32), jnp.zeros((t, SB_DIM), F32)))
        o_ref[...] = acc.astype(BF16)

    cb = lambda off: off // SB_DIM
    return pl.pallas_call(
        body, name="sb_fwd", grid=(heads, s // t),
        out_shape=jax.ShapeDtypeStruct((s, heads * SB_DIM), BF16),
        in_specs=[pl.BlockSpec((t, SB_DIM), lambda h, i: (i, cb(off_q) + h)),
                  pl.BlockSpec((s, SB_DIM), lambda h, i: (0, cb(off_k) + h)),
                  pl.BlockSpec((s, SB_DIM), lambda h, i: (0, cb(off_v) + h))],
        out_specs=pl.BlockSpec((t, SB_DIM), lambda h, i: (i, h)),
        compiler_params=_params(),
    )(proj, proj, proj)


def _sb_bwd(proj, off_q, off_k, off_v, heads, dy):
    s = proj.shape[0]
    t = _tile(s, ATT_BLK)
    nq = s // t
    scale = SB_DIM ** -0.5

    def body(q_ref, k_ref, v_ref, dy_ref, dq_ref, dk_ref, dv_ref, g_s, beta_s):
        qi = pl.program_id(1)

        @pl.when(qi == 0)
        def _():
            dk_ref[...] = jnp.zeros_like(dk_ref)
            dv_ref[...] = jnp.zeros_like(dv_ref)

        qv = q_ref[...].astype(BF16)
        dy_b = dy_ref[...].astype(BF16)
        rows = lax.broadcasted_iota(jnp.int32, (t, t), 0)
        cols = lax.broadcasted_iota(jnp.int32, (t, t), 1)
        after = (rows > cols).astype(BF16)
        before = (rows < cols).astype(BF16)

        def pass1(i, tail_right):
            kb = qi - i
            off = pl.multiple_of(kb * t, t)
            log_beta, log_rest, mask = _sb_terms(qv, k_ref[pl.ds(off, t), :].astype(BF16), qi, kb, t, scale)
            tail = _split_dot(log_rest, after) + tail_right
            a = jnp.where(mask, jnp.exp(log_beta + tail), 0.0)
            da = _dot_nt(dy_b, v_ref[pl.ds(off, t), :].astype(BF16))
            dv_ref[pl.ds(off, t), :] += _dot_tn(a.astype(BF16), dy_b)
            g_s[kb] = a * da
            beta_s[kb] = jnp.where(mask, jnp.exp(log_beta), 0.0)
            return tail_right + jnp.sum(log_rest, axis=-1, keepdims=True)

        lax.fori_loop(0, qi + 1, pass1, jnp.zeros((t, 1), F32))

        def pass2(kb, carry):
            g_left, dq = carry
            off = pl.multiple_of(kb * t, t)
            g, beta = g_s[kb], beta_s[kb]
            g_before = _split_dot(g, before) + g_left
            dz = ((g * (1.0 - beta) - g_before * beta) * scale).astype(BF16)
            dk_ref[pl.ds(off, t), :] += _dot_tn(dz, qv)
            dq = dq + _dot(dz, k_ref[pl.ds(off, t), :].astype(BF16))
            return g_left + jnp.sum(g, axis=-1, keepdims=True), dq

        _, dq = lax.fori_loop(0, qi + 1, pass2, (jnp.zeros((t, 1), F32), jnp.zeros((t, SB_DIM), F32)))
        dq_ref[...] = dq

    cb = lambda off: off // SB_DIM
    sd = jax.ShapeDtypeStruct((s, heads * SB_DIM), F32)
    return pl.pallas_call(
        body, name="sb_bwd", grid=(heads, nq),
        out_shape=(sd, sd, sd),
        in_specs=[pl.BlockSpec((t, SB_DIM), lambda h, i: (i, cb(off_q) + h)),
                  pl.BlockSpec((s, SB_DIM), lambda h, i: (0, cb(off_k) + h)),
                  pl.BlockSpec((s, SB_DIM), lambda h, i: (0, cb(off_v) + h)),
                  pl.BlockSpec((t, SB_DIM), lambda h, i: (i, h))],
        out_specs=(pl.BlockSpec((t, SB_DIM), lambda h, i: (i, h)), pl.BlockSpec((s, SB_DIM), lambda h, i: (0, h)),
                   pl.BlockSpec((s, SB_DIM), lambda h, i: (0, h))),
        scratch_shapes=[pltpu.VMEM((nq, t, t), F32), pltpu.VMEM((nq, t, t), F32)],
        compiler_params=_params(),
    )(proj, proj, proj, dy)


def _slot_sum(recv):
    _, r, w = recv.shape
    tr = _tile(r, 256)

    def body(r_ref, o_ref):
        acc = r_ref[0].astype(F32)
        for d in range(1, N_DEV):
            acc = acc + r_ref[d].astype(F32)
        o_ref[...] = acc

    return pl.pallas_call(
        body, name="slot_sum", grid=(r // tr,),
        out_shape=jax.ShapeDtypeStruct((r, w), F32),
        in_specs=[pl.BlockSpec((N_DEV, tr, w), lambda i: (0, i, 0))],
        out_specs=pl.BlockSpec((tr, w), lambda i: (i, 0)), compiler_params=_params(),
    )(recv)


def _small_sum(rows, loss_lo, loss_hi, d_model):
    n = rows.shape[1]

    def body(r_ref, o_ref, loss_ref):
        rv = r_ref[...]
        acc = rv[0:1, :]
        for d in range(1, N_DEV):
            acc = acc + rv[d:d + 1, :]
        o_ref[...] = acc
        total = jnp.sum(acc[:, loss_lo:loss_hi], axis=-1, keepdims=True) * (0.5 / d_model)
        loss_ref[...] = jnp.broadcast_to(total, (1, LANES))

    return pl.pallas_call(
        body, name="small_sum",
        out_shape=(jax.ShapeDtypeStruct((1, n), F32), jax.ShapeDtypeStruct((1, LANES), F32)),
        compiler_params=_params(),
    )(rows)


def _adamw(w, g, m, v, name):
    r, c = w.shape
    tr = _tile(r, max(8, (1 << 18) // c // 8 * 8))

    def body(w_ref, g_ref, m_ref, v_ref, d_ref, mo_ref, vo_ref):
        gv = g_ref[...]
        m_new = ADAM_B1 * m_ref[...] + (1.0 - ADAM_B1) * gv
        v_new = ADAM_B2 * v_ref[...] + (1.0 - ADAM_B2) * (gv * gv)
        m_hat = m_new / (1.0 - ADAM_B1 ** ADAM_STEP)
        v_hat = v_new / (1.0 - ADAM_B2 ** ADAM_STEP)
        d_ref[...] = -ADAM_LR * (m_hat / (jnp.sqrt(v_hat) + ADAM_EPS) + ADAM_WD * w_ref[...])
        mo_ref[...] = m_new
        vo_ref[...] = v_new

    spec = pl.BlockSpec((tr, c), lambda i: (i, 0))
    sd = jax.ShapeDtypeStruct((r, c), F32)
    return pl.pallas_call(
        body, name=name, grid=(r // tr,), out_shape=(sd, sd, sd),
        in_specs=[spec] * 4, out_specs=(spec,) * 3, compiler_params=_params(),
    )(w, g, m, v)


def _pack_rows(a):
    return a.reshape(-1, PACK_W)


def _unshard(slots, shape, col_sharded):
    r, c = shape
    if col_sharded:
        return slots.reshape(N_DEV, r, c).transpose(1, 0, 2).reshape(r, N_DEV * c)
    return slots.reshape(N_DEV * r, c)


def _to_shards(full, col_sharded):
    r, c = full.shape
    if col_sharded:
        return full.reshape(r, N_DEV, c // N_DEV).transpose(1, 0, 2).reshape(N_DEV, -1, PACK_W)
    return full.reshape(N_DEV, -1, PACK_W)


def kernel(x, c, positions, w_ada, b_ada, g_norm1, g_norm2, w_in, g_q_latent, g_kv_latent, w_uq, w_ukv, g_q_head, g_k_head, w_proj_mla, w_proj_sb, w_out, w_ffn_in, w_ffn_out, loss_target, m_w_ada, m_b_ada, m_g_norm1, m_g_norm2, m_w_in, m_g_q_latent, m_g_kv_latent, m_w_uq, m_w_ukv, m_g_q_head, m_g_k_head, m_w_proj_mla, m_w_proj_sb, m_w_out, m_w_ffn_in, m_w_ffn_out, v_w_ada, v_b_ada, v_g_norm1, v_g_norm2, v_w_in, v_g_q_latent, v_g_kv_latent, v_w_uq, v_w_ukv, v_g_q_head, v_g_k_head, v_w_proj_mla, v_w_proj_sb, v_w_out, v_w_ffn_in, v_w_ffn_out):
    env = dict(locals())
    drop = lambda a: a[0] if a.ndim == 3 else a
    wts = {n: drop(env[n]) for n in WEIGHT_NAMES}
    mom = {n: drop(env["m_" + n]) for n in WEIGHT_NAMES}
    var = {n: drop(env["v_" + n]) for n in WEIGHT_NAMES}
    xs, tgt, pos = x[0], loss_target[0], positions[0]
    s, d = xs.shape
    lat = wts["w_uq"].shape[0]
    assert wts["w_ukv"].shape[0] == lat
    h_mla = wts["w_uq"].shape[1] * N_DEV // QK_DIM
    sb_w = wts["w_proj_sb"].shape[0]
    h_sb = sb_w // SB_DIM
    d_ff = wts["w_ffn_out"].shape[0] * N_DEV
    me = 4 * lax.axis_index("x") + 2 * lax.axis_index("y") + lax.axis_index("c")

    ref_w = (("c_q", lat), ("c_kv", lat), ("k_pe", ROPE), ("q_sb", sb_w), ("k_sb", sb_w), ("v_sb", sb_w),
             ("gl_a", d), ("gl_b", d))
    ref_off, o = {}, 0
    for n_, w_ in ref_w:
        ref_off[n_] = (o, w_)
        o += w_
    order = ("gl_a", "gl_b", "q_sb", "k_sb", "v_sb", "c_q", "c_kv", "k_pe")
    off, o = {}, 0
    for n_ in order:
        w_ = LANES if n_ == "k_pe" else ref_off[n_][1]
        assert o % w_ == 0
        off[n_] = o
        o += w_

    packed = jnp.concatenate([_pack_rows(wts[n].astype(BF16)) for n, _ in BIG], axis=0)
    slots = _exchange(packed, gather=True, name="gather_weights")
    full, r0 = {}, 0
    for n, col in BIG:
        rows = wts[n].size // PACK_W
        full[n] = _unshard(slots[:, r0:r0 + rows], wts[n].shape, col)
        r0 += rows
    seg = lambda a, n_: a[:, ref_off[n_][0]:ref_off[n_][0] + ref_off[n_][1]]
    w_in_k = jnp.concatenate([seg(full["w_in"], n_) for n_ in order] + [jnp.zeros((d, LANES - ROPE), BF16)], axis=1)
    w_uq_k = jnp.pad(full["w_uq"].reshape(lat, h_mla, QK_DIM), ((0, 0), (0, 0), (0, QK_PAD - QK_DIM))
                     ).reshape(lat, h_mla * QK_PAD)
    pad_gain = lambda g: jnp.pad(g, ((0, 0), (0, QK_PAD - QK_DIM)))
    g_qh, g_kh = pad_gain(wts["g_q_head"]), pad_gain(wts["g_k_head"])

    c_all = _gather_rows(c, "gather_c")
    ada_cols = _exchange(_ada_fwd(c_all, wts["w_ada"]), gather=True, name="gather_ada")
    ada = lax.dynamic_index_in_dim(ada_cols, me, axis=1, keepdims=False).reshape(1, 6 * d) + wts["b_ada"]
    sh1, sc1, gt1, sh2, sc2, gt2 = [ada[:, i * d:(i + 1) * d] for i in range(6)]

    half = ROPE // 2
    ang = pos.astype(F32)[:, None] * (ROPE_THETA ** (-jnp.arange(half, dtype=F32) / half))
    zeros = jnp.zeros((s, LANES - ROPE), F32)
    cos_t = jnp.concatenate([jnp.cos(ang), jnp.cos(ang), zeros], axis=1)
    sin_t = jnp.concatenate([-jnp.sin(ang), jnp.sin(ang), zeros], axis=1)

    h1 = _norm_mod(xs, wts["g_norm1"], sc1, sh1)
    proj = _mm(h1, w_in_k, name="mm_in")
    cqn, ckvn = _latent_norm(proj, off["c_q"], off["c_kv"], lat, wts["g_q_latent"], wts["g_kv_latent"])
    q_raw = _mm(cqn, w_uq_k, name="mm_uq")
    kv = _mm(ckvn, full["w_ukv"], name="mm_ukv")
    q, k, v = _qk_prep(q_raw, kv, proj, off["k_pe"], cos_t, sin_t, g_qh, g_kh)
    y_a, lse = _mla_fwd(q, k, v)
    y_b = _sb_fwd(proj, off["q_sb"], off["k_sb"], off["v_sb"], h_sb)
    ya_p = _mm(y_a, full["w_proj_mla"], name="mm_proj_mla")
    yb_p = _mm(y_b, full["w_proj_sb"], name="mm_proj_sb")
    merged = _merge(proj, off["gl_a"], off["gl_b"], ya_p, yb_p)
    o1 = _mm(merged, full["w_out"], name="mm_out")
    x1, h2 = _resid_norm_mod(xs, o1, gt1, wts["g_norm2"], sc2, sh2)
    gu = _mm(h2, full["w_ffn_in"], name="mm_ffn_in")
    act = _swiglu(gu)
    o2 = _mm(act, full["w_ffn_out"], name="mm_ffn_out")
    dy, d_o2, sums_l = _loss_head(x1, o2, gt2, tgt)

    grads = {}
    d_act = _mm(d_o2, full["w_ffn_out"], tb=True, name="mm_d_act")
    grads["w_ffn_out"] = _mm(act, d_o2, ta=True, name="mm_g_ffn_out")
    d_gu = _swiglu_bwd(gu, d_act)
    grads["w_ffn_in"] = _mm(h2, d_gu, ta=True, name="mm_g_ffn_in")
    d_h2 = _mm(d_gu, full["w_ffn_in"], tb=True, name="mm_d_h2")
    d_x1, d_o1, sums_2 = _norm_mod_bwd(d_h2, x1, wts["g_norm2"], sc2, dy, gate=(o1, gt1))
    grads["w_out"] = _mm(merged, d_o1, ta=True, name="mm_g_out")
    d_merged = _mm(d_o1, full["w_out"], tb=True, name="mm_d_merged")
    d_yap, d_ybp, d_gla, d_glb = _merge_bwd(proj, off["gl_a"], off["gl_b"], ya_p, yb_p, d_merged)
    grads["w_proj_mla"] = _mm(y_a, d_yap, ta=True, name="mm_g_proj_mla")
    grads["w_proj_sb"] = _mm(y_b, d_ybp, ta=True, name="mm_g_proj_sb")
    d_ya = _mm(d_yap, full["w_proj_mla"], tb=True, name="mm_d_ya")
    d_yb = _mm(d_ybp, full["w_proj_sb"], tb=True, name="mm_d_yb")
    dq_sb, dk_sb, dv_sb = _sb_bwd(proj, off["q_sb"], off["k_sb"], off["v_sb"], h_sb, d_yb)
    dq, dk, dv = _mla_bwd(q, k, v, y_a, lse, d_ya)
    d_qraw, d_kv, d_kpe, sums_h = _qk_prep_bwd(q_raw, kv, proj, off["k_pe"], cos_t, sin_t, g_qh, g_kh, dq, dk, dv)
    g_uq_k = _mm(cqn, d_qraw, ta=True, name="mm_g_uq")
    grads["w_uq"] = g_uq_k.reshape(lat, h_mla, QK_PAD)[:, :, :QK_DIM].reshape(lat, h_mla * QK_DIM)
    grads["w_ukv"] = _mm(ckvn, d_kv, ta=True, name="mm_g_ukv")
    d_cqn = _mm(d_qraw, w_uq_k, tb=True, name="mm_d_cqn")
    d_ckvn = _mm(d_kv, full["w_ukv"], tb=True, name="mm_d_ckvn")
    d_cq, d_ckv, sums_lat = _latent_norm_bwd(proj, off["c_q"], off["c_kv"], lat, wts["g_q_latent"],
                                             wts["g_kv_latent"], d_cqn, d_ckvn)
    d_parts = {"gl_a": d_gla, "gl_b": d_glb, "q_sb": dq_sb, "k_sb": dk_sb, "v_sb": dv_sb, "c_q": d_cq, "c_kv": d_ckv,
               "k_pe": d_kpe}
    d_proj = jnp.concatenate([d_parts[n_].astype(BF16) for n_ in order], axis=1)
    g_in_k = _mm(h1, d_proj, ta=True, name="mm_g_in")
    grads["w_in"] = jnp.concatenate([g_in_k[:, off[n_]:off[n_] + w_] for n_, w_ in ref_w], axis=1)
    d_h1 = _mm(d_proj, w_in_k, tb=True, name="mm_d_h1")
    grad_x, sums_1 = _norm_mod_bwd(d_h1, xs, wts["g_norm1"], sc1, d_x1)

    parts = [sums_1[0:1], sums_1[1:2], sums_2[3:4], sums_2[0:1], sums_2[1:2], sums_l[0:1],
             sums_1[2:3], sums_2[2:3], sums_lat[0:1], sums_lat[1:2], sums_h[0:1], sums_h[1:2], sums_l[1:2]]
    part_off, o = [], 0
    for p in parts:
        part_off.append(o)
        o += p.shape[1]
    all_rows = _gather_rows(jnp.concatenate(parts, axis=1), "gather_small")
    summed, loss_row = _small_sum(all_rows, part_off[12], part_off[12] + d, d)
    take = lambda i, w: summed[:, part_off[i]:part_off[i] + w]
    grads["b_ada"] = summed[:, :6 * d]
    grads["g_norm1"], grads["g_norm2"] = take(6, d), take(7, d)
    grads["g_q_latent"], grads["g_kv_latent"] = take(8, lat), take(9, lat)
    grads["g_q_head"], grads["g_k_head"] = take(10, QK_DIM), take(11, QK_DIM)
    n_ada = 6 * d // N_DEV
    d_ada_mine = lax.dynamic_slice_in_dim(all_rows[:, :6 * d], me * n_ada, n_ada, axis=1)
    grads["w_ada"] = _ada_grad(c_all.T, d_ada_mine)

    send = jnp.concatenate([_to_shards(grads[n].astype(BF16), col) for n, col in BIG], axis=1)
    g_packed = _slot_sum(_exchange(send, gather=False, name="scatter_grads"))
    r0 = 0
    for n, _ in BIG:
        rows = wts[n].size // PACK_W
        grads[n] = g_packed[r0:r0 + rows].reshape(wts[n].shape)
        r0 += rows

    delta, new_m, new_v = {}, {}, {}
    for n in ("w_ada",) + tuple(n for n, _ in BIG):
        delta[n], new_m[n], new_v[n] = _adamw(wts[n], grads[n], mom[n], var[n], "adamw_" + n)
    cat = lambda t: jnp.concatenate([t[n] for n in SMALL], axis=1)
    d_s, m_s, v_s = _adamw(cat(wts), cat(grads), cat(mom), cat(var), "adamw_small")
    o = 0
    for n in SMALL:
        w_ = wts[n].shape[1]
        delta[n], new_m[n], new_v[n] = d_s[:, o:o + w_], m_s[:, o:o + w_], v_s[:, o:o + w_]
        o += w_

    lead = lambda t: [t[n].reshape(env[n].shape) for n in WEIGHT_NAMES]
    return (loss_row[0, 0], grad_x[None], *lead(grads), *lead(delta), *lead(new_m), *lead(new_v))
```

```python
import jax
import jax.numpy as jnp
from jax import lax
from jax.experimental import pallas as pl
from jax.experimental.pallas import tpu as pltpu

F32 = jnp.float32
BF16 = jnp.bfloat16

N_DEV = 8
LANES = 128
PACK_W = 1024
VMEM_LIMIT = 48 * 1024 * 1024

EPS = 1e-6
ROPE_THETA = 10000.0
NOPE = 128
ROPE = 64
QK_DIM = NOPE + ROPE
QK_PAD = 2 * LANES
V_DIM = 128
SB_DIM = 128
ATT_Q = 512
ATT_K = 256
MM_TK = 2816

ADAM_LR = 0.001
ADAM_B1 = 0.9
ADAM_B2 = 0.999
ADAM_EPS = 1e-08
ADAM_WD = 0.01
ADAM_STEP = 10

WEIGHT_NAMES = ("w_ada", "b_ada", "g_norm1", "g_norm2", "w_in", "g_q_latent", "g_kv_latent", "w_uq", "w_ukv",
                "g_q_head", "g_k_head", "w_proj_mla", "w_proj_sb", "w_out", "w_ffn_in", "w_ffn_out")
BIG = (("w_in", True), ("w_uq", True), ("w_ukv", True), ("w_proj_mla", True), ("w_proj_sb", True),
       ("w_out", False), ("w_ffn_in", True), ("w_ffn_out", False))
SMALL = ("b_ada", "g_norm1", "g_norm2", "g_q_latent", "g_kv_latent", "g_q_head", "g_k_head")


def _tile(n, pref):
    if n <= pref:
        return n
    for step in (LANES, 8):
        for t in range(pref - pref % step, 0, -step):
            if n % t == 0:
                return t
    return n


def _params():
    return pltpu.CompilerParams(vmem_limit_bytes=VMEM_LIMIT)


def _sigmoid(x):
    return 1.0 / (1.0 + jnp.exp(-x))


def _dot(a, b):
    return lax.dot_general(a, b, (((1,), (0,)), ((), ())), preferred_element_type=F32)


def _dot_nt(a, b):
    return lax.dot_general(a, b, (((1,), (1,)), ((), ())), preferred_element_type=F32)


def _dot_tn(a, b):
    return lax.dot_general(a, b, (((0,), (0,)), ((), ())), preferred_element_type=F32)


def _split_dot(x, tri):
    hi = x.astype(BF16)
    lo = (x - hi.astype(F32)).astype(BF16)
    return _dot(hi, tri) + _dot(lo, tri)


_HBM = pl.BlockSpec(memory_space=pltpu.HBM)


def _carry_parts(carry):
    src, gather = carry
    slot_shape = tuple(src.shape) if gather else tuple(src.shape[1:])
    return (src, _HBM, jax.ShapeDtypeStruct((N_DEV,) + slot_shape, src.dtype), _HBM,
            [pltpu.SemaphoreType.DMA((N_DEV - 1,)), pltpu.SemaphoreType.DMA((N_DEV - 1,)), pltpu.SemaphoreType.DMA(())])


def _exchange_copies(src_ref, dst_ref, send_sems, recv_sems, local_sem, gather):
    x, y, c = lax.axis_index("x"), lax.axis_index("y"), lax.axis_index("c")
    me = 4 * x + 2 * y + c

    def slot_for(idx):
        return src_ref if gather else src_ref.at[idx]

    copies = [pltpu.make_async_copy(slot_for(me), dst_ref.at[me], local_sem)]
    for k in range(1, N_DEV):
        peer = ((1 - x) if (k >> 2) & 1 else x, (1 - y) if (k >> 1) & 1 else y, (1 - c) if k & 1 else c)
        peer_idx = 4 * peer[0] + 2 * peer[1] + peer[2]
        copies.append(pltpu.make_async_remote_copy(
            src_ref=slot_for(peer_idx), dst_ref=dst_ref.at[me],
            send_sem=send_sems.at[k - 1], recv_sem=recv_sems.at[k - 1],
            device_id=peer, device_id_type=pl.DeviceIdType.MESH))
    return copies


def _carried_exchange(refs, gather, first, last):
    @pl.when(first)
    def _():
        for cp in _exchange_copies(*refs, gather):
            cp.start()

    @pl.when(last)
    def _():
        for cp in _exchange_copies(*refs, gather):
            cp.wait()


def _exchange(src, *, gather, name):
    operand, in_spec, out_shape, out_spec, scratch = _carry_parts((src, gather))

    def body(src_ref, dst_ref, send_sems, recv_sems, local_sem):
        copies = _exchange_copies(src_ref, dst_ref, send_sems, recv_sems, local_sem, gather)
        for cp in copies:
            cp.start()
        for cp in copies:
            cp.wait()

    return pl.pallas_call(body, name=name, out_shape=out_shape, in_specs=[in_spec], out_specs=out_spec,
                          scratch_shapes=scratch)(operand)


def _gather_rows(v, name):
    out = _exchange(jnp.broadcast_to(v, (8, v.shape[1])), gather=True, name=name)
    return out[:, 0, :]


def _split_refs(refs, n_in, n_out, carry):
    if carry is None:
        return refs[:n_in], refs[n_in:n_in + n_out], refs[n_in + n_out:], None
    ins, src_ref = refs[:n_in], refs[n_in]
    outs, dst_ref = refs[n_in + 1:n_in + 1 + n_out], refs[n_in + 1 + n_out]
    rest = refs[n_in + n_out + 2:]
    return ins, outs, rest[:-3], (src_ref, dst_ref) + tuple(rest[-3:])


def _with_carry(carry, args, in_specs, out_shape, out_specs, scratch):
    if carry is not None:
        operand, c_in, c_shape, c_out, c_scratch = _carry_parts(carry)
        args.append(operand)
        in_specs.append(c_in)
        out_shape.append(c_shape)
        out_specs.append(c_out)
        scratch.extend(c_scratch)


def _mm(a, b, *, ta=False, tb=False, out_dtype=F32, name, carry=None):
    kdim, m = a.shape if ta else a.shape[::-1]
    n, kdim_b = b.shape if tb else b.shape[::-1]
    assert kdim == kdim_b, (a.shape, b.shape, ta, tb)
    tm, tn, tk = _tile(m, 1024), _tile(n, 1024), _tile(kdim, MM_TK)
    grid = (m // tm, n // tn, kdim // tk)
    nk = grid[2]
    dims = (((0 if ta else 1,), (1 if tb else 0,)), ((), ()))

    def body(*refs):
        (a_ref, b_ref), (o_ref,), scratch_refs, xrefs = _split_refs(refs, 2, 1, carry)
        i, j, k = pl.program_id(0), pl.program_id(1), pl.program_id(2)
        if carry is not None:
            _carried_exchange(xrefs, carry[1], (i == 0) & (j == 0) & (k == 0),
                              (i == grid[0] - 1) & (j == grid[1] - 1) & (k == nk - 1))
        prod = lax.dot_general(a_ref[...].astype(BF16), b_ref[...].astype(BF16), dims, preferred_element_type=F32)
        if nk == 1:
            o_ref[...] = prod.astype(o_ref.dtype)
        else:
            acc_ref = scratch_refs[0]

            @pl.when(k == 0)
            def _():
                acc_ref[...] = prod

            @pl.when(k > 0)
            def _():
                acc_ref[...] += prod

            @pl.when(k == nk - 1)
            def _():
                o_ref[...] = acc_ref[...].astype(o_ref.dtype)

    a_spec = (pl.BlockSpec((tk, tm), lambda i, j, k: (k, i)) if ta else pl.BlockSpec((tm, tk), lambda i, j, k: (i, k)))
    b_spec = (pl.BlockSpec((tn, tk), lambda i, j, k: (j, k)) if tb else pl.BlockSpec((tk, tn), lambda i, j, k: (k, j)))
    args, in_specs = [a, b], [a_spec, b_spec]
    out_shape, out_specs = [jax.ShapeDtypeStruct((m, n), out_dtype)], [pl.BlockSpec((tm, tn), lambda i, j, k: (i, j))]
    scratch = [] if nk == 1 else [pltpu.VMEM((tm, tn), F32)]
    _with_carry(carry, args, in_specs, out_shape, out_specs, scratch)
    out = pl.pallas_call(
        body, name=name, grid=grid, out_shape=tuple(out_shape), in_specs=in_specs, out_specs=tuple(out_specs),
        scratch_shapes=scratch, compiler_params=_params(),
    )(*args)
    return out if carry is not None else out[0]


def _ada_fwd(c_all, w_shard):
    d, n = w_shard.shape
    tn = _tile(n, 512)

    def body(c_ref, w_ref, o_ref):
        cv = c_ref[...]
        o_ref[...] = jnp.dot(cv * _sigmoid(cv), w_ref[...], precision=lax.Precision.HIGHEST,
                             preferred_element_type=F32)

    return pl.pallas_call(
        body, name="ada_fwd", grid=(n // tn,),
        out_shape=jax.ShapeDtypeStruct((N_DEV, n), F32),
        in_specs=[pl.BlockSpec((N_DEV, d), lambda j: (0, 0)), pl.BlockSpec((d, tn), lambda j: (0, j))],
        out_specs=pl.BlockSpec((N_DEV, tn), lambda j: (0, j)),
        compiler_params=_params(),
    )(c_all, w_shard)


def _ada_grad(c_all_t, d_rows):
    d, n = c_all_t.shape[0], d_rows.shape[1]
    tn = _tile(n, 512)

    def body(ct_ref, d_ref, o_ref):
        cv = ct_ref[...]
        s = cv * _sigmoid(cv)
        dv = d_ref[...]
        acc = s[:, 0:1] * dv[0:1, :]
        for b in range(1, N_DEV):
            acc = acc + s[:, b:b + 1] * dv[b:b + 1, :]
        o_ref[...] = acc

    return pl.pallas_call(
        body, name="ada_grad", grid=(n // tn,),
        out_shape=jax.ShapeDtypeStruct((d, n), F32),
        in_specs=[pl.BlockSpec((d, N_DEV), lambda j: (0, 0)), pl.BlockSpec((N_DEV, tn), lambda j: (0, j))],
        out_specs=pl.BlockSpec((d, tn), lambda j: (0, j)),
        compiler_params=_params(),
    )(c_all_t, d_rows)


def _row(ts, w, col=0):
    return pl.BlockSpec((ts, w), lambda i, col=col: (i, col))


def _vec(w):
    return pl.BlockSpec((1, w), lambda i: (0, 0))


def _norm_mod(x, g, sc, sh):
    s, d = x.shape
    ts = _tile(s, 512)

    def body(x_ref, g_ref, sc_ref, sh_ref, h_ref):
        xv = x_ref[...]
        r = lax.rsqrt(jnp.mean(xv * xv, axis=-1, keepdims=True) + EPS)
        h_ref[...] = ((xv * r) * g_ref[...] * (1.0 + sc_ref[...]) + sh_ref[...]).astype(BF16)

    return pl.pallas_call(
        body, name="norm_mod", grid=(s // ts,),
        out_shape=jax.ShapeDtypeStruct((s, d), BF16),
        in_specs=[_row(ts, d), _vec(d), _vec(d), _vec(d)],
        out_specs=_row(ts, d), compiler_params=_params(),
    )(x, g, sc, sh)


def _latent_norm(proj, off_q, off_kv, lat, g_q, g_kv):
    s = proj.shape[0]
    ts = _tile(s, 512)

    def body(cq_ref, ckv_ref, gq_ref, gkv_ref, oq_ref, okv_ref):
        for c_ref, g_ref, o_ref in ((cq_ref, gq_ref, oq_ref), (ckv_ref, gkv_ref, okv_ref)):
            v = c_ref[...]
            r = lax.rsqrt(jnp.mean(v * v, axis=-1, keepdims=True) + EPS)
            o_ref[...] = ((v * r) * g_ref[...]).astype(BF16)

    return pl.pallas_call(
        body, name="latent_norm", grid=(s // ts,),
        out_shape=(jax.ShapeDtypeStruct((s, lat), BF16), jax.ShapeDtypeStruct((s, lat), BF16)),
        in_specs=[_row(ts, lat, off_q // lat), _row(ts, lat, off_kv // lat), _vec(lat), _vec(lat)],
        out_specs=(_row(ts, lat), _row(ts, lat)), compiler_params=_params(),
    )(proj, proj, g_q, g_kv)


def _latent_norm_bwd(proj, off_q, off_kv, lat, g_q, g_kv, d_cqn, d_ckvn):
    s = proj.shape[0]
    ts = _tile(s, 512)

    def body(cq_ref, ckv_ref, gq_ref, gkv_ref, dq_ref, dkv_ref, oq_ref, okv_ref, sums_ref):
        @pl.when(pl.program_id(0) == 0)
        def _():
            sums_ref[...] = jnp.zeros_like(sums_ref)

        for row, (c_ref, g_ref, d_ref, o_ref) in enumerate(((cq_ref, gq_ref, dq_ref, oq_ref),
                                                             (ckv_ref, gkv_ref, dkv_ref, okv_ref))):
            v = c_ref[...]
            r = lax.rsqrt(jnp.mean(v * v, axis=-1, keepdims=True) + EPS)
            vn = v * r
            dn = d_ref[...]
            sums_ref[row:row + 1, :] += jnp.sum(dn * vn, axis=0, keepdims=True)
            dvn = dn * g_ref[...]
            o_ref[...] = (r * (dvn - vn * jnp.mean(dvn * vn, axis=-1, keepdims=True))).astype(BF16)

    return pl.pallas_call(
        body, name="latent_norm_bwd", grid=(s // ts,),
        out_shape=(jax.ShapeDtypeStruct((s, lat), BF16), jax.ShapeDtypeStruct((s, lat), BF16),
                   jax.ShapeDtypeStruct((8, lat), F32)),
        in_specs=[_row(ts, lat, off_q // lat), _row(ts, lat, off_kv // lat), _vec(lat), _vec(lat),
                  _row(ts, lat), _row(ts, lat)],
        out_specs=(_row(ts, lat), _row(ts, lat), pl.BlockSpec((8, lat), lambda i: (0, 0))),
        compiler_params=_params(),
    )(proj, proj, g_q, g_kv, d_cqn, d_ckvn)


def _merge(proj, off_a, off_b, ya_p, yb_p):
    s, d = ya_p.shape
    ts = _tile(s, 256)

    def body(ga_ref, gb_ref, ya_ref, yb_ref, o_ref):
        o_ref[...] = (_sigmoid(ga_ref[...]) * ya_ref[...] + _sigmoid(gb_ref[...]) * yb_ref[...]).astype(BF16)

    return pl.pallas_call(
        body, name="merge", grid=(s // ts,),
        out_shape=jax.ShapeDtypeStruct((s, d), BF16),
        in_specs=[_row(ts, d, off_a // d), _row(ts, d, off_b // d), _row(ts, d), _row(ts, d)],
        out_specs=_row(ts, d), compiler_params=_params(),
    )(proj, proj, ya_p, yb_p)


def _merge_bwd(proj, off_a, off_b, ya_p, yb_p, d_merged):
    s, d = ya_p.shape
    ts = _tile(s, 256)

    def body(ga_ref, gb_ref, ya_ref, yb_ref, dm_ref, dya_ref, dyb_ref, dga_ref, dgb_ref):
        dm = dm_ref[...]
        for g_ref, y_ref, dy_ref, dg_ref in ((ga_ref, ya_ref, dya_ref, dga_ref), (gb_ref, yb_ref, dyb_ref, dgb_ref)):
            sg = _sigmoid(g_ref[...])
            dy_ref[...] = (dm * sg).astype(BF16)
            dg_ref[...] = (dm * y_ref[...] * sg * (1.0 - sg)).astype(BF16)

    sd = jax.ShapeDtypeStruct((s, d), BF16)
    return pl.pallas_call(
        body, name="merge_bwd", grid=(s // ts,),
        out_shape=(sd, sd, sd, sd),
        in_specs=[_row(ts, d, off_a // d), _row(ts, d, off_b // d), _row(ts, d), _row(ts, d), _row(ts, d)],
        out_specs=(_row(ts, d),) * 4, compiler_params=_params(),
    )(proj, proj, ya_p, yb_p, d_merged)


def _resid_norm_mod(x, o, gt, g, sc, sh):
    s, d = x.shape
    ts = _tile(s, 256)

    def body(x_ref, o_ref, gt_ref, g_ref, sc_ref, sh_ref, x1_ref, h_ref):
        x1 = x_ref[...] + gt_ref[...] * o_ref[...]
        x1_ref[...] = x1
        r = lax.rsqrt(jnp.mean(x1 * x1, axis=-1, keepdims=True) + EPS)
        h_ref[...] = ((x1 * r) * g_ref[...] * (1.0 + sc_ref[...]) + sh_ref[...]).astype(BF16)

    return pl.pallas_call(
        body, name="resid_norm_mod", grid=(s // ts,),
        out_shape=(jax.ShapeDtypeStruct((s, d), F32), jax.ShapeDtypeStruct((s, d), BF16)),
        in_specs=[_row(ts, d), _row(ts, d), _vec(d), _vec(d), _vec(d), _vec(d)],
        out_specs=(_row(ts, d), _row(ts, d)), compiler_params=_params(),
    )(x, o, gt, g, sc, sh)


def _loss_head(x1, o2, gt2, target):
    s, d = x1.shape
    ts = _tile(s, 256)

    def body(x1_ref, o2_ref, gt_ref, t_ref, dy_ref, do_ref, sums_ref):
        @pl.when(pl.program_id(0) == 0)
        def _():
            sums_ref[...] = jnp.zeros_like(sums_ref)

        o2 = o2_ref[...]
        e = x1_ref[...] + gt_ref[...] * o2 - t_ref[...]
        dy = e / d
        dy_ref[...] = dy
        do_ref[...] = (dy * gt_ref[...]).astype(BF16)
        sums_ref[0:1, :] += jnp.sum(dy * o2, axis=0, keepdims=True)
        sums_ref[1:2, :] += jnp.sum(e * e, axis=0, keepdims=True)

    return pl.pallas_call(
        body, name="loss_head", grid=(s // ts,),
        out_shape=(jax.ShapeDtypeStruct((s, d), F32), jax.ShapeDtypeStruct((s, d), BF16),
                   jax.ShapeDtypeStruct((8, d), F32)),
        in_specs=[_row(ts, d), _row(ts, d), _vec(d), _row(ts, d)],
        out_specs=(_row(ts, d), _row(ts, d), pl.BlockSpec((8, d), lambda i: (0, 0))),
        compiler_params=_params(),
    )(x1, o2, gt2, target)


def _norm_mod_bwd(dh, xin, g, sc, dres, gate=None):
    s, d = xin.shape
    ts = _tile(s, 256)
    gated = gate is not None

    def body(*refs):
        if gated:
            dh_ref, x_ref, g_ref, sc_ref, dr_ref, o_ref, gt_ref, dx_ref, do_ref, sums_ref = refs
        else:
            dh_ref, x_ref, g_ref, sc_ref, dr_ref, dx_ref, sums_ref = refs

        @pl.when(pl.program_id(0) == 0)
        def _():
            sums_ref[...] = jnp.zeros_like(sums_ref)

        xv, dhv = x_ref[...], dh_ref[...]
        r = lax.rsqrt(jnp.mean(xv * xv, axis=-1, keepdims=True) + EPS)
        xn = xv * r
        one_sc = 1.0 + sc_ref[...]
        sums_ref[0:1, :] += jnp.sum(dhv, axis=0, keepdims=True)
        sums_ref[1:2, :] += jnp.sum(dhv * (xn * g_ref[...]), axis=0, keepdims=True)
        sums_ref[2:3, :] += jnp.sum(dhv * one_sc * xn, axis=0, keepdims=True)
        dxn = dhv * one_sc * g_ref[...]
        dx = dr_ref[...] + r * (dxn - xn * jnp.mean(dxn * xn, axis=-1, keepdims=True))
        dx_ref[...] = dx
        if gated:
            sums_ref[3:4, :] += jnp.sum(dx * o_ref[...], axis=0, keepdims=True)
            do_ref[...] = (dx * gt_ref[...]).astype(BF16)

    in_specs = [_row(ts, d), _row(ts, d), _vec(d), _vec(d), _row(ts, d)]
    args = [dh, xin, g, sc, dres]
    out_shape = [jax.ShapeDtypeStruct((s, d), F32)]
    out_specs = [_row(ts, d)]
    if gated:
        in_specs += [_row(ts, d), _vec(d)]
        args += list(gate)
        out_shape.append(jax.ShapeDtypeStruct((s, d), BF16))
        out_specs.append(_row(ts, d))
    out_shape.append(jax.ShapeDtypeStruct((8, d), F32))
    out_specs.append(pl.BlockSpec((8, d), lambda i: (0, 0)))
    return pl.pallas_call(
        body, name="norm_mod_bwd_gated" if gated else "norm_mod_bwd", grid=(s // ts,),
        out_shape=tuple(out_shape), in_specs=in_specs, out_specs=tuple(out_specs), compiler_params=_params(),
    )(*args)


def _swiglu(gu):
    s, f2 = gu.shape
    f = f2 // 2
    ts, tc = _tile(s, 512), _tile(f, 512)
    nf = f // tc

    def body(g_ref, u_ref, o_ref):
        gv = g_ref[...]
        o_ref[...] = (gv * _sigmoid(gv) * u_ref[...]).astype(BF16)

    return pl.pallas_call(
        body, name="swiglu", grid=(s // ts, nf),
        out_shape=jax.ShapeDtypeStruct((s, f), BF16),
        in_specs=[pl.BlockSpec((ts, tc), lambda i, j: (i, j)), pl.BlockSpec((ts, tc), lambda i, j: (i, nf + j))],
        out_specs=pl.BlockSpec((ts, tc), lambda i, j: (i, j)), compiler_params=_params(),
    )(gu, gu)


def _swiglu_bwd(gu, d_act):
    s, f2 = gu.shape
    f = f2 // 2
    ts, tc = _tile(s, 512), _tile(f, 512)
    nf = f // tc

    def body(g_ref, u_ref, da_ref, o_ref):
        gv, da = g_ref[...], da_ref[...]
        sg = _sigmoid(gv)
        d_gate = da * u_ref[...] * (sg * (1.0 + gv * (1.0 - sg)))
        d_up = da * (gv * sg)
        o_ref[...] = jnp.where(pl.program_id(1) < nf, d_gate, d_up).astype(BF16)

    return pl.pallas_call(
        body, name="swiglu_bwd", grid=(s // ts, 2 * nf),
        out_shape=jax.ShapeDtypeStruct((s, f2), BF16),
        in_specs=[pl.BlockSpec((ts, tc), lambda i, j: (i, j % nf)),
                  pl.BlockSpec((ts, tc), lambda i, j: (i, nf + j % nf)),
                  pl.BlockSpec((ts, tc), lambda i, j: (i, j % nf))],
        out_specs=pl.BlockSpec((ts, tc), lambda i, j: (i, j)), compiler_params=_params(),
    )(gu, gu, d_act)


def _swap_halves(t):
    return pltpu.roll(t, ROPE // 2, 1) + pltpu.roll(t, LANES - ROPE // 2, 1)


def _head_norm(raw):
    r = lax.rsqrt(jnp.sum(raw * raw, axis=-1, keepdims=True) / QK_DIM + EPS)
    return raw * r, r


def _rope_fwd(v, cos, sin):
    rope_tile = v[:, NOPE:]
    return jnp.concatenate([v[:, :NOPE], rope_tile * cos + _swap_halves(rope_tile) * sin], axis=1)


def _rope_bwd(d, cos, sin, lane_ok):
    d_tile = d[:, NOPE:]
    return jnp.concatenate([d[:, :NOPE], d_tile * cos + _swap_halves(d_tile * sin) * lane_ok], axis=1)


def _qk_prep(q_raw, kv, proj, off_pe, cos, sin, g_q, g_k):
    s, hw = q_raw.shape
    heads = hw // QK_PAD
    ts = _tile(s, 512)

    def body(q_ref, kv_ref, pe_ref, cos_ref, sin_ref, gq_ref, gk_ref, qo_ref, ko_ref, vo_ref):
        cos_v, sin_v = cos_ref[...], sin_ref[...]
        qn, _ = _head_norm(q_ref[...])
        qo_ref[...] = _rope_fwd(qn * gq_ref[...], cos_v, sin_v).astype(BF16)
        kvv = kv_ref[...]
        kn, _ = _head_norm(jnp.concatenate([kvv[:, :NOPE], pe_ref[...]], axis=1))
        ko_ref[...] = _rope_fwd(kn * gk_ref[...], cos_v, sin_v).astype(BF16)
        vo_ref[...] = kvv[:, NOPE:].astype(BF16)

    blk = lambda w: pl.BlockSpec((ts, w), lambda i, h: (i, h))
    fixed = lambda w, col=0: pl.BlockSpec((ts, w), lambda i, h, col=col: (i, col))
    vec = pl.BlockSpec((1, QK_PAD), lambda i, h: (0, 0))
    return pl.pallas_call(
        body, name="qk_prep", grid=(s // ts, heads),
        out_shape=(jax.ShapeDtypeStruct((s, hw), BF16), jax.ShapeDtypeStruct((s, hw), BF16),
                   jax.ShapeDtypeStruct((s, heads * V_DIM), BF16)),
        in_specs=[blk(QK_PAD), blk(QK_PAD), fixed(LANES, off_pe // LANES), fixed(LANES), fixed(LANES), vec, vec],
        out_specs=(blk(QK_PAD), blk(QK_PAD), blk(V_DIM)), compiler_params=_params(),
    )(q_raw, kv, proj, cos, sin, g_q, g_k)


def _qk_prep_bwd(q_raw, kv, proj, off_pe, cos, sin, g_q, g_k, dq, dk, dv):
    s, hw = q_raw.shape
    heads = hw // QK_PAD
    ts = _tile(s, 512)

    def body(q_ref, kv_ref, pe_ref, cos_ref, sin_ref, gq_ref, gk_ref, dq_ref, dk_ref, dv_ref,
             dqr_ref, dkv_ref, dpe_ref, sums_ref):
        i, h = pl.program_id(0), pl.program_id(1)

        @pl.when((i == 0) & (h == 0))
        def _():
            sums_ref[...] = jnp.zeros_like(sums_ref)

        cos_v, sin_v = cos_ref[...], sin_ref[...]
        lane_ok = (lax.broadcasted_iota(jnp.int32, (ts, LANES), 1) < ROPE).astype(F32)

        def one(raw, g, d_post, row):
            vn, r = _head_norm(raw)
            d_pre = _rope_bwd(d_post, cos_v, sin_v, lane_ok)
            sums_ref[row:row + 1, :] += jnp.sum(d_pre * vn, axis=0, keepdims=True)
            dvn = d_pre * g
            return r * (dvn - vn * (jnp.sum(dvn * vn, axis=-1, keepdims=True) / QK_DIM))

        dqr_ref[...] = one(q_ref[...], gq_ref[...], dq_ref[...], 0).astype(BF16)
        kvv = kv_ref[...]
        d_kraw = one(jnp.concatenate([kvv[:, :NOPE], pe_ref[...]], axis=1), gk_ref[...], dk_ref[...], 1)
        dkv_ref[...] = jnp.concatenate([d_kraw[:, :NOPE], dv_ref[...]], axis=1).astype(BF16)

        @pl.when(h == 0)
        def _():
            dpe_ref[...] = jnp.zeros_like(dpe_ref)

        dpe_ref[...] += d_kraw[:, NOPE:]

    blk = lambda w: pl.BlockSpec((ts, w), lambda i, h: (i, h))
    fixed = lambda w, col=0: pl.BlockSpec((ts, w), lambda i, h, col=col: (i, col))
    vec = pl.BlockSpec((1, QK_PAD), lambda i, h: (0, 0))
    return pl.pallas_call(
        body, name="qk_prep_bwd", grid=(s // ts, heads),
        out_shape=(jax.ShapeDtypeStruct((s, hw), BF16), jax.ShapeDtypeStruct((s, hw), BF16),
                   jax.ShapeDtypeStruct((s, LANES), F32), jax.ShapeDtypeStruct((8, QK_PAD), F32)),
        in_specs=[blk(QK_PAD), blk(QK_PAD), fixed(LANES, off_pe // LANES), fixed(LANES), fixed(LANES), vec, vec,
                  blk(QK_PAD), blk(QK_PAD), blk(V_DIM)],
        out_specs=(blk(QK_PAD), blk(QK_PAD), fixed(LANES), pl.BlockSpec((8, QK_PAD), lambda i, h: (0, 0))),
        compiler_params=_params(),
    )(q_raw, kv, proj, cos, sin, g_q, g_k, dq, dk, dv)


def _att_blocks(s):
    tq = _tile(s, ATT_Q)
    tk = _tile(tq, ATT_K)
    return tq, tk, tq // tk


def _visible(qi, kb, tq, tk, strict):
    row = qi * tq + lax.broadcasted_iota(jnp.int32, (tq, tk), 0)
    col = kb * tk + lax.broadcasted_iota(jnp.int32, (tq, tk), 1)
    return (col < row) if strict else (col <= row)


def _first_last(heads, nq):
    h, qi = pl.program_id(0), pl.program_id(1)
    return (h == 0) & (qi == 0), (h == heads - 1) & (qi == nq - 1)


def _mla_fwd(q, k, v, carry=None):
    s = q.shape[0]
    heads = q.shape[1] // QK_PAD
    tq, tk, ratio = _att_blocks(s)
    nq = s // tq
    scale = QK_DIM ** -0.5

    def body(*refs):
        (q_ref, k_ref, v_ref), (o_ref, lse_ref), _, xrefs = _split_refs(refs, 3, 2, carry)
        if carry is not None:
            _carried_exchange(xrefs, carry[1], *_first_last(heads, nq))
        qi = pl.program_id(1)
        qv = q_ref[...]

        def step(kb, state, masked):
            m, l, acc = state
            off = pl.multiple_of(kb * tk, tk)
            sc = _dot_nt(qv, k_ref[pl.ds(off, tk), :]) * scale
            if masked:
                sc = jnp.where(_visible(qi, kb, tq, tk, False), sc, -1e30)
            m_new = jnp.maximum(m, jnp.max(sc, axis=-1, keepdims=True))
            alpha = jnp.exp(m - m_new)
            p = jnp.exp(sc - m_new)
            l = alpha * l + jnp.sum(p, axis=-1, keepdims=True)
            acc = alpha * acc + _dot(p.astype(BF16), v_ref[pl.ds(off, tk), :])
            return m_new, l, acc

        state = (jnp.full((tq, 1), -1e30, F32), jnp.zeros((tq, 1), F32), jnp.zeros((tq, V_DIM), F32))
        state = lax.fori_loop(0, qi * ratio, lambda kb, st: step(kb, st, False), state)
        for i in range(ratio):
            state = step(qi * ratio + i, state, True)
        m, l, acc = state
        o_ref[...] = (acc / l).astype(BF16)
        lse_ref[0] = m + jnp.log(l)

    args = [q, k, v]
    in_specs = [pl.BlockSpec((tq, QK_PAD), lambda h, i: (i, h)), pl.BlockSpec((s, QK_PAD), lambda h, i: (0, h)),
                pl.BlockSpec((s, V_DIM), lambda h, i: (0, h))]
    out_shape = [jax.ShapeDtypeStruct((s, heads * V_DIM), BF16), jax.ShapeDtypeStruct((heads, s, 1), F32)]
    out_specs = [pl.BlockSpec((tq, V_DIM), lambda h, i: (i, h)), pl.BlockSpec((1, tq, 1), lambda h, i: (h, i, 0))]
    scratch = []
    _with_carry(carry, args, in_specs, out_shape, out_specs, scratch)
    return pl.pallas_call(
        body, name="mla_fwd", grid=(heads, nq), out_shape=tuple(out_shape), in_specs=in_specs,
        out_specs=tuple(out_specs), scratch_shapes=scratch, compiler_params=_params(),
    )(*args)


def _mla_bwd(q, k, v, o, lse, do, carry=None):
    s = q.shape[0]
    heads = q.shape[1] // QK_PAD
    tq, tk, ratio = _att_blocks(s)
    nq = s // tq
    scale = QK_DIM ** -0.5

    def body(*refs):
        (q_ref, k_ref, v_ref, o_ref, lse_ref, do_ref), (dq_ref, dk_ref, dv_ref), _, xrefs = _split_refs(refs, 6, 3, carry)
        if carry is not None:
            _carried_exchange(xrefs, carry[1], *_first_last(heads, nq))
        qi = pl.program_id(1)

        @pl.when(qi == 0)
        def _():
            dk_ref[...] = jnp.zeros_like(dk_ref)
            dv_ref[...] = jnp.zeros_like(dv_ref)

        qv = q_ref[...]
        dov = do_ref[...]
        do_b = dov.astype(BF16)
        delta = jnp.sum(dov * o_ref[...].astype(F32), axis=-1, keepdims=True)
        lse_v = lse_ref[0]

        def step(kb, dq, masked):
            off = pl.multiple_of(kb * tk, tk)
            kv_ = k_ref[pl.ds(off, tk), :]
            p = jnp.exp(_dot_nt(qv, kv_) * scale - lse_v)
            if masked:
                p = jnp.where(_visible(qi, kb, tq, tk, False), p, 0.0)
            dp = _dot_nt(do_b, v_ref[pl.ds(off, tk), :])
            ds = (p * (dp - delta) * scale).astype(BF16)
            dk_ref[pl.ds(off, tk), :] += _dot_tn(ds, qv)
            dv_ref[pl.ds(off, tk), :] += _dot_tn(p.astype(BF16), do_b)
            return dq + _dot(ds, kv_)

        dq = lax.fori_loop(0, qi * ratio, lambda kb, acc: step(kb, acc, False), jnp.zeros((tq, QK_PAD), F32))
        for i in range(ratio):
            dq = step(qi * ratio + i, dq, True)
        dq_ref[...] = dq

    args = [q, k, v, o, lse, do]
    in_specs = [pl.BlockSpec((tq, QK_PAD), lambda h, i: (i, h)), pl.BlockSpec((s, QK_PAD), lambda h, i: (0, h)),
                pl.BlockSpec((s, V_DIM), lambda h, i: (0, h)), pl.BlockSpec((tq, V_DIM), lambda h, i: (i, h)),
                pl.BlockSpec((1, tq, 1), lambda h, i: (h, i, 0)), pl.BlockSpec((tq, V_DIM), lambda h, i: (i, h))]
    out_shape = [jax.ShapeDtypeStruct(q.shape, F32), jax.ShapeDtypeStruct(k.shape, F32),
                 jax.ShapeDtypeStruct(v.shape, F32)]
    out_specs = [pl.BlockSpec((tq, QK_PAD), lambda h, i: (i, h)), pl.BlockSpec((s, QK_PAD), lambda h, i: (0, h)),
                 pl.BlockSpec((s, V_DIM), lambda h, i: (0, h))]
    scratch = []
    _with_carry(carry, args, in_specs, out_shape, out_specs, scratch)
    return pl.pallas_call(
        body, name="mla_bwd", grid=(heads, nq), out_shape=tuple(out_shape), in_specs=in_specs,
        out_specs=tuple(out_specs), scratch_shapes=scratch, compiler_params=_params(),
    )(*args)


def _sb_terms(qv, k_blk, scale, mask):
    z = _dot_nt(qv, k_blk) * scale
    log_beta = jnp.minimum(z, 0.0) - jnp.log(1.0 + jnp.exp(-jnp.abs(z)))
    log_rest = log_beta - z
    if mask is not None:
        log_rest = jnp.where(mask, log_rest, 0.0)
    return log_beta, log_rest


def _sb_specs(s, tq, off_q, off_k, off_v):
    cb = lambda off: off // SB_DIM
    return [pl.BlockSpec((tq, SB_DIM), lambda h, i: (i, cb(off_q) + h)),
            pl.BlockSpec((s, SB_DIM), lambda h, i: (0, cb(off_k) + h)),
            pl.BlockSpec((s, SB_DIM), lambda h, i: (0, cb(off_v) + h))]


def _sb_fwd(proj, off_q, off_k, off_v, heads, carry=None):
    s = proj.shape[0]
    tq, tk, ratio = _att_blocks(s)
    nq = s // tq
    scale = SB_DIM ** -0.5

    def body(*refs):
        (q_ref, k_ref, v_ref), (o_ref,), _, xrefs = _split_refs(refs, 3, 1, carry)
        if carry is not None:
            _carried_exchange(xrefs, carry[1], *_first_last(heads, nq))
        qi = pl.program_id(1)
        qv = q_ref[...].astype(BF16)
        after = (lax.broadcasted_iota(jnp.int32, (tk, tk), 0) > lax.broadcasted_iota(jnp.int32, (tk, tk), 1)).astype(BF16)

        def step(kb, state, masked):
            tail_right, acc = state
            off = pl.multiple_of(kb * tk, tk)
            mask = _visible(qi, kb, tq, tk, True) if masked else None
            log_beta, log_rest = _sb_terms(qv, k_ref[pl.ds(off, tk), :].astype(BF16), scale, mask)
            tail = _split_dot(log_rest, after) + tail_right
            a = jnp.exp(log_beta + tail)
            if masked:
                a = jnp.where(mask, a, 0.0)
            acc = acc + _dot(a.astype(BF16), v_ref[pl.ds(off, tk), :].astype(BF16))
            return tail_right + jnp.sum(log_rest, axis=-1, keepdims=True), acc

        state = (jnp.zeros((tq, 1), F32), jnp.zeros((tq, SB_DIM), F32))
        for i in range(ratio):
            state = step((qi + 1) * ratio - 1 - i, state, True)
        _, acc = lax.fori_loop(0, qi * ratio, lambda i, st: step(qi * ratio - 1 - i, st, False), state)
        o_ref[...] = acc.astype(BF16)

    args, in_specs = [proj, proj, proj], _sb_specs(s, tq, off_q, off_k, off_v)
    out_shape = [jax.ShapeDtypeStruct((s, heads * SB_DIM), BF16)]
    out_specs = [pl.BlockSpec((tq, SB_DIM), lambda h, i: (i, h))]
    scratch = []
    _with_carry(carry, args, in_specs, out_shape, out_specs, scratch)
    out = pl.pallas_call(
        body, name="sb_fwd", grid=(heads, nq), out_shape=tuple(out_shape), in_specs=in_specs,
        out_specs=tuple(out_specs), scratch_shapes=scratch, compiler_params=_params(),
    )(*args)
    return out if carry is not None else out[0]


def _sb_bwd(proj, off_q, off_k, off_v, heads, dy, carry=None):
    s = proj.shape[0]
    tq, tk, ratio = _att_blocks(s)
    nq = s // tq
    scale = SB_DIM ** -0.5

    def body(*refs):
        (q_ref, k_ref, v_ref, dy_ref), (dq_ref, dk_ref, dv_ref), (g_s, beta_s), xrefs = _split_refs(refs, 4, 3, carry)
        if carry is not None:
            _carried_exchange(xrefs, carry[1], *_first_last(heads, nq))
        qi = pl.program_id(1)

        @pl.when(qi == 0)
        def _():
            dk_ref[...] = jnp.zeros_like(dk_ref)
            dv_ref[...] = jnp.zeros_like(dv_ref)

        qv = q_ref[...].astype(BF16)
        dy_b = dy_ref[...].astype(BF16)
        rows = lax.broadcasted_iota(jnp.int32, (tk, tk), 0)
        cols = lax.broadcasted_iota(jnp.int32, (tk, tk), 1)
        after = (rows > cols).astype(BF16)
        before = (rows < cols).astype(BF16)

        def pass1(kb, tail_right, masked):
            off = pl.multiple_of(kb * tk, tk)
            mask = _visible(qi, kb, tq, tk, True) if masked else None
            log_beta, log_rest = _sb_terms(qv, k_ref[pl.ds(off, tk), :].astype(BF16), scale, mask)
            tail = _split_dot(log_rest, after) + tail_right
            a = jnp.exp(log_beta + tail)
            beta = jnp.exp(log_beta)
            if masked:
                a = jnp.where(mask, a, 0.0)
                beta = jnp.where(mask, beta, 0.0)
            da = _dot_nt(dy_b, v_ref[pl.ds(off, tk), :].astype(BF16))
            dv_ref[pl.ds(off, tk), :] += _dot_tn(a.astype(BF16), dy_b)
            g_s[kb] = a * da
            beta_s[kb] = beta
            return tail_right + jnp.sum(log_rest, axis=-1, keepdims=True)

        tail_right = jnp.zeros((tq, 1), F32)
        for i in range(ratio):
            tail_right = pass1((qi + 1) * ratio - 1 - i, tail_right, True)
        lax.fori_loop(0, qi * ratio, lambda i, t_: pass1(qi * ratio - 1 - i, t_, False), tail_right)

        def pass2(kb, state):
            g_left, dq = state
            off = pl.multiple_of(kb * tk, tk)
            g, beta = g_s[kb], beta_s[kb]
            g_before = _split_dot(g, before) + g_left
            dz = ((g * (1.0 - beta) - g_before * beta) * scale).astype(BF16)
            dk_ref[pl.ds(off, tk), :] += _dot_tn(dz, qv)
            dq = dq + _dot(dz, k_ref[pl.ds(off, tk), :].astype(BF16))
            return g_left + jnp.sum(g, axis=-1, keepdims=True), dq

        _, dq = lax.fori_loop(0, (qi + 1) * ratio, pass2, (jnp.zeros((tq, 1), F32), jnp.zeros((tq, SB_DIM), F32)))
        dq_ref[...] = dq

    args = [proj, proj, proj, dy]
    in_specs = _sb_specs(s, tq, off_q, off_k, off_v) + [pl.BlockSpec((tq, SB_DIM), lambda h, i: (i, h))]
    out_shape = [jax.ShapeDtypeStruct((s, heads * SB_DIM), F32)] * 3
    out_specs = [pl.BlockSpec((tq, SB_DIM), lambda h, i: (i, h)), pl.BlockSpec((s, SB_DIM), lambda h, i: (0, h)),
                 pl.BlockSpec((s, SB_DIM), lambda h, i: (0, h))]
    scratch = [pltpu.VMEM((s // tk, tq, tk), F32), pltpu.VMEM((s // tk, tq, tk), F32)]
    _with_carry(carry, args, in_specs, out_shape, out_specs, scratch)
    return pl.pallas_call(
        body, name="sb_bwd", grid=(heads, nq), out_shape=tuple(out_shape), in_specs=in_specs,
        out_specs=tuple(out_specs), scratch_shapes=scratch, compiler_params=_params(),
    )(*args)


def _slot_sum(recv, name):
    _, r, w = recv.shape
    tr = _tile(r, 256)

    def body(r_ref, o_ref):
        acc = r_ref[0].astype(F32)
        for d in range(1, N_DEV):
            acc = acc + r_ref[d].astype(F32)
        o_ref[...] = acc

    return pl.pallas_call(
        body, name=name, grid=(r // tr,),
        out_shape=jax.ShapeDtypeStruct((r, w), F32),
        in_specs=[pl.BlockSpec((N_DEV, tr, w), lambda i: (0, i, 0))],
        out_specs=pl.BlockSpec((tr, w), lambda i: (i, 0)), compiler_params=_params(),
    )(recv)


def _small_sum(rows, loss_lo, loss_hi, d_model):
    n = rows.shape[1]

    def body(r_ref, o_ref, loss_ref):
        rv = r_ref[...]
        acc = rv[0:1, :]
        for d in range(1, N_DEV):
            acc = acc + rv[d:d + 1, :]
        o_ref[...] = acc
        total = jnp.sum(acc[:, loss_lo:loss_hi], axis=-1, keepdims=True) * (0.5 / d_model)
        loss_ref[...] = jnp.broadcast_to(total, (1, LANES))

    return pl.pallas_call(
        body, name="small_sum",
        out_shape=(jax.ShapeDtypeStruct((1, n), F32), jax.ShapeDtypeStruct((1, LANES), F32)),
        compiler_params=_params(),
    )(rows)


def _adamw(w, g, m, v, name):
    r, c = w.shape
    tr = _tile(r, max(8, (1 << 18) // c // 8 * 8))

    def body(w_ref, g_ref, m_ref, v_ref, d_ref, mo_ref, vo_ref):
        gv = g_ref[...]
        m_new = ADAM_B1 * m_ref[...] + (1.0 - ADAM_B1) * gv
        v_new = ADAM_B2 * v_ref[...] + (1.0 - ADAM_B2) * (gv * gv)
        m_hat = m_new / (1.0 - ADAM_B1 ** ADAM_STEP)
        v_hat = v_new / (1.0 - ADAM_B2 ** ADAM_STEP)
        d_ref[...] = -ADAM_LR * (m_hat / (jnp.sqrt(v_hat) + ADAM_EPS) + ADAM_WD * w_ref[...])
        mo_ref[...] = m_new
        vo_ref[...] = v_new

    spec = pl.BlockSpec((tr, c), lambda i: (i, 0))
    sd = jax.ShapeDtypeStruct((r, c), F32)
    return pl.pallas_call(
        body, name=name, grid=(r // tr,), out_shape=(sd, sd, sd),
        in_specs=[spec] * 4, out_specs=(spec,) * 3, compiler_params=_params(),
    )(w, g, m, v)


def _pack_rows(a):
    return a.reshape(-1, PACK_W)


def _unshard(slots, shape, col_sharded):
    r, c = shape
    if col_sharded:
        return slots.reshape(N_DEV, r, c).transpose(1, 0, 2).reshape(r, N_DEV * c)
    return slots.reshape(N_DEV * r, c)


def _to_shards(full, col_sharded):
    r, c = full.shape
    if col_sharded:
        return full.reshape(r, N_DEV, c // N_DEV).transpose(1, 0, 2).reshape(N_DEV, -1, PACK_W)
    return full.reshape(N_DEV, -1, PACK_W)


def kernel(x, c, positions, w_ada, b_ada, g_norm1, g_norm2, w_in, g_q_latent, g_kv_latent, w_uq, w_ukv, g_q_head, g_k_head, w_proj_mla, w_proj_sb, w_out, w_ffn_in, w_ffn_out, loss_target, m_w_ada, m_b_ada, m_g_norm1, m_g_norm2, m_w_in, m_g_q_latent, m_g_kv_latent, m_w_uq, m_w_ukv, m_g_q_head, m_g_k_head, m_w_proj_mla, m_w_proj_sb, m_w_out, m_w_ffn_in, m_w_ffn_out, v_w_ada, v_b_ada, v_g_norm1, v_g_norm2, v_w_in, v_g_q_latent, v_g_kv_latent, v_w_uq, v_w_ukv, v_g_q_head, v_g_k_head, v_w_proj_mla, v_w_proj_sb, v_w_out, v_w_ffn_in, v_w_ffn_out):
    env = dict(locals())
    drop = lambda a: a[0] if a.ndim == 3 else a
    wts = {n: drop(env[n]) for n in WEIGHT_NAMES}
    mom = {n: drop(env["m_" + n]) for n in WEIGHT_NAMES}
    var = {n: drop(env["v_" + n]) for n in WEIGHT_NAMES}
    xs, tgt, pos = x[0], loss_target[0], positions[0]
    s, d = xs.shape
    lat = wts["w_uq"].shape[0]
    assert wts["w_ukv"].shape[0] == lat
    h_mla = wts["w_uq"].shape[1] * N_DEV // QK_DIM
    sb_w = wts["w_proj_sb"].shape[0]
    h_sb = sb_w // SB_DIM
    d_ff = wts["w_ffn_out"].shape[0] * N_DEV
    me = 4 * lax.axis_index("x") + 2 * lax.axis_index("y") + lax.axis_index("c")

    ref_w = (("c_q", lat), ("c_kv", lat), ("k_pe", ROPE), ("q_sb", sb_w), ("k_sb", sb_w), ("v_sb", sb_w),
             ("gl_a", d), ("gl_b", d))
    ref_off, o = {}, 0
    for n_, w_ in ref_w:
        ref_off[n_] = (o, w_)
        o += w_
    order = ("gl_a", "gl_b", "q_sb", "k_sb", "v_sb", "c_q", "c_kv", "k_pe")
    off, o = {}, 0
    for n_ in order:
        w_ = LANES if n_ == "k_pe" else ref_off[n_][1]
        assert o % w_ == 0
        off[n_] = o
        o += w_

    used_w = o
    proj_w = -(-used_w // (2 * LANES)) * (2 * LANES)

    col_sharded = dict(BIG)
    rows_of = {n: wts[n].size // PACK_W for n, _ in BIG}
    full, grads = {}, {}

    def pack_weights(names):
        return jnp.concatenate([_pack_rows(wts[n].astype(BF16)) for n in names], axis=0)

    def unpack_weights(slots, names):
        r0 = 0
        for n in names:
            full[n] = _unshard(slots[:, r0:r0 + rows_of[n]], wts[n].shape, col_sharded[n])
            r0 += rows_of[n]

    def pack_grads(names):
        return jnp.concatenate([_to_shards(grads[n].astype(BF16), col_sharded[n]) for n in names], axis=1)

    def unpack_grads(recv, names):
        summed_rows, r0 = _slot_sum(recv, "slot_sum_" + names[0]), 0
        for n in names:
            grads[n] = summed_rows[r0:r0 + rows_of[n]].reshape(wts[n].shape)
            r0 += rows_of[n]

    unpack_weights(_exchange(pack_weights(("w_in",)), gather=True, name="gather_w_in"), ("w_in",))
    seg = lambda a, n_: a[:, ref_off[n_][0]:ref_off[n_][0] + ref_off[n_][1]]
    w_in_k = jnp.concatenate([seg(full["w_in"], n_) for n_ in order]
                             + [jnp.zeros((d, proj_w - used_w + LANES - ROPE), BF16)], axis=1)
    pad_gain = lambda g: jnp.pad(g, ((0, 0), (0, QK_PAD - QK_DIM)))
    g_qh, g_kh = pad_gain(wts["g_q_head"]), pad_gain(wts["g_k_head"])

    c_all = _gather_rows(c, "gather_c")
    ada_cols = _exchange(_ada_fwd(c_all, wts["w_ada"]), gather=True, name="gather_ada")
    ada = lax.dynamic_index_in_dim(ada_cols, me, axis=1, keepdims=False).reshape(1, 6 * d) + wts["b_ada"]
    sh1, sc1, gt1, sh2, sc2, gt2 = [ada[:, i * d:(i + 1) * d] for i in range(6)]

    half = ROPE // 2
    ang = pos.astype(F32)[:, None] * (ROPE_THETA ** (-jnp.arange(half, dtype=F32) / half))
    zeros = jnp.zeros((s, LANES - ROPE), F32)
    cos_t = jnp.concatenate([jnp.cos(ang), jnp.cos(ang), zeros], axis=1)
    sin_t = jnp.concatenate([-jnp.sin(ang), jnp.sin(ang), zeros], axis=1)

    h1 = _norm_mod(xs, wts["g_norm1"], sc1, sh1)
    mixer_w = ("w_uq", "w_ukv", "w_proj_mla", "w_proj_sb", "w_out")
    proj, slots = _mm(h1, w_in_k, name="mm_in", carry=(pack_weights(mixer_w), True))
    unpack_weights(slots, mixer_w)
    w_uq_k = jnp.pad(full["w_uq"].reshape(lat, h_mla, QK_DIM), ((0, 0), (0, 0), (0, QK_PAD - QK_DIM))
                     ).reshape(lat, h_mla * QK_PAD)
    cqn, ckvn = _latent_norm(proj, off["c_q"], off["c_kv"], lat, wts["g_q_latent"], wts["g_kv_latent"])
    q_raw = _mm(cqn, w_uq_k, name="mm_uq")
    kv = _mm(ckvn, full["w_ukv"], name="mm_ukv")
    q, k, v = _qk_prep(q_raw, kv, proj, off["k_pe"], cos_t, sin_t, g_qh, g_kh)
    y_a, lse, slots = _mla_fwd(q, k, v, carry=(pack_weights(("w_ffn_out",)), True))
    unpack_weights(slots, ("w_ffn_out",))
    y_b, slots = _sb_fwd(proj, off["q_sb"], off["k_sb"], off["v_sb"], h_sb, carry=(pack_weights(("w_ffn_in",)), True))
    unpack_weights(slots, ("w_ffn_in",))
    ya_p = _mm(y_a, full["w_proj_mla"], name="mm_proj_mla")
    yb_p = _mm(y_b, full["w_proj_sb"], name="mm_proj_sb")
    merged = _merge(proj, off["gl_a"], off["gl_b"], ya_p, yb_p)
    o1 = _mm(merged, full["w_out"], name="mm_out")
    x1, h2 = _resid_norm_mod(xs, o1, gt1, wts["g_norm2"], sc2, sh2)
    gu = _mm(h2, full["w_ffn_in"], name="mm_ffn_in")
    act = _swiglu(gu)
    o2 = _mm(act, full["w_ffn_out"], name="mm_ffn_out")
    dy, d_o2, sums_l = _loss_head(x1, o2, gt2, tgt)

    recv = {}
    d_act = _mm(d_o2, full["w_ffn_out"], tb=True, name="mm_d_act")
    grads["w_ffn_out"] = _mm(act, d_o2, ta=True, name="mm_g_ffn_out")
    d_gu = _swiglu_bwd(gu, d_act)
    grads["w_ffn_in"] = _mm(h2, d_gu, ta=True, name="mm_g_ffn_in")
    d_h2, recv["w_ffn_out",] = _mm(d_gu, full["w_ffn_in"], tb=True, name="mm_d_h2",
                                   carry=(pack_grads(("w_ffn_out",)), False))
    d_x1, d_o1, sums_2 = _norm_mod_bwd(d_h2, x1, wts["g_norm2"], sc2, dy, gate=(o1, gt1))
    grads["w_out"] = _mm(merged, d_o1, ta=True, name="mm_g_out")
    d_merged = _mm(d_o1, full["w_out"], tb=True, name="mm_d_merged")
    d_yap, d_ybp, d_gla, d_glb = _merge_bwd(proj, off["gl_a"], off["gl_b"], ya_p, yb_p, d_merged)
    grads["w_proj_mla"] = _mm(y_a, d_yap, ta=True, name="mm_g_proj_mla")
    grads["w_proj_sb"] = _mm(y_b, d_ybp, ta=True, name="mm_g_proj_sb")
    d_ya = _mm(d_yap, full["w_proj_mla"], tb=True, name="mm_d_ya")
    d_yb = _mm(d_ybp, full["w_proj_sb"], tb=True, name="mm_d_yb")
    dq_sb, dk_sb, dv_sb, recv["w_ffn_in",] = _sb_bwd(proj, off["q_sb"], off["k_sb"], off["v_sb"], h_sb, d_yb,
                                                     carry=(pack_grads(("w_ffn_in",)), False))
    merge_w = ("w_out", "w_proj_mla", "w_proj_sb")
    dq, dk, dv, recv[merge_w] = _mla_bwd(q, k, v, y_a, lse, d_ya, carry=(pack_grads(merge_w), False))
    d_qraw, d_kv, d_kpe, sums_h = _qk_prep_bwd(q_raw, kv, proj, off["k_pe"], cos_t, sin_t, g_qh, g_kh, dq, dk, dv)
    g_uq_k = _mm(cqn, d_qraw, ta=True, name="mm_g_uq")
    grads["w_uq"] = g_uq_k.reshape(lat, h_mla, QK_PAD)[:, :, :QK_DIM].reshape(lat, h_mla * QK_DIM)
    grads["w_ukv"] = _mm(ckvn, d_kv, ta=True, name="mm_g_ukv")
    d_cqn = _mm(d_qraw, w_uq_k, tb=True, name="mm_d_cqn")
    d_ckvn = _mm(d_kv, full["w_ukv"], tb=True, name="mm_d_ckvn")
    d_cq, d_ckv, sums_lat = _latent_norm_bwd(proj, off["c_q"], off["c_kv"], lat, wts["g_q_latent"],
                                             wts["g_kv_latent"], d_cqn, d_ckvn)
    d_parts = {"gl_a": d_gla, "gl_b": d_glb, "q_sb": dq_sb, "k_sb": dk_sb, "v_sb": dv_sb, "c_q": d_cq, "c_kv": d_ckv,
               "k_pe": d_kpe}
    d_proj = jnp.concatenate([d_parts[n_].astype(BF16) for n_ in order]
                             + ([jnp.zeros((s, proj_w - used_w), BF16)] if proj_w > used_w else []), axis=1)
    g_in_k = _mm(h1, d_proj, ta=True, name="mm_g_in")
    grads["w_in"] = jnp.concatenate([g_in_k[:, off[n_]:off[n_] + w_] for n_, w_ in ref_w], axis=1)
    latent_w = ("w_uq", "w_ukv", "w_in")
    d_h1, recv[latent_w] = _mm(d_proj, w_in_k, tb=True, name="mm_d_h1", carry=(pack_grads(latent_w), False))
    grad_x, sums_1 = _norm_mod_bwd(d_h1, xs, wts["g_norm1"], sc1, d_x1)

    parts = [sums_1[0:1], sums_1[1:2], sums_2[3:4], sums_2[0:1], sums_2[1:2], sums_l[0:1],
             sums_1[2:3], sums_2[2:3], sums_lat[0:1], sums_lat[1:2], sums_h[0:1], sums_h[1:2], sums_l[1:2]]
    part_off, o = [], 0
    for p in parts:
        part_off.append(o)
        o += p.shape[1]
    all_rows = _gather_rows(jnp.concatenate(parts, axis=1), "gather_small")
    summed, loss_row = _small_sum(all_rows, part_off[12], part_off[12] + d, d)
    take = lambda i, w: summed[:, part_off[i]:part_off[i] + w]
    grads["b_ada"] = summed[:, :6 * d]
    grads["g_norm1"], grads["g_norm2"] = take(6, d), take(7, d)
    grads["g_q_latent"], grads["g_kv_latent"] = take(8, lat), take(9, lat)
    grads["g_q_head"], grads["g_k_head"] = take(10, QK_DIM), take(11, QK_DIM)
    n_ada = 6 * d // N_DEV
    d_ada_mine = lax.dynamic_slice_in_dim(all_rows[:, :6 * d], me * n_ada, n_ada, axis=1)
    grads["w_ada"] = _ada_grad(c_all.T, d_ada_mine)

    for names, slots in recv.items():
        unpack_grads(slots, names)

    delta, new_m, new_v = {}, {}, {}
    for n in ("w_ada",) + tuple(n for n, _ in BIG):
        delta[n], new_m[n], new_v[n] = _adamw(wts[n], grads[n], mom[n], var[n], "adamw_" + n)
    cat = lambda t: jnp.concatenate([t[n] for n in SMALL], axis=1)
    d_s, m_s, v_s = _adamw(cat(wts), cat(grads), cat(mom), cat(var), "adamw_small")
    o = 0
    for n in SMALL:
        w_ = wts[n].shape[1]
        delta[n], new_m[n], new_v[n] = d_s[:, o:o + w_], m_s[:, o:o + w_], v_s[:, o:o + w_]
        o += w_

    lead = lambda t: [t[n].reshape(env[n].shape) for n in WEIGHT_NAMES]
    return (loss_row[0, 0], grad_x[None], *lead(grads), *lead(delta), *lead(new_m), *lead(new_v))
```

```python
import jax
import jax.numpy as jnp
from jax import lax
from jax.experimental import pallas as pl
from jax.experimental.pallas import tpu as pltpu

F32 = jnp.float32
BF16 = jnp.bfloat16

N_DEV = 8
LANES = 128
PACK_W = 1024
VMEM_LIMIT = 48 * 1024 * 1024

EPS = 1e-6
ROPE_THETA = 10000.0
NOPE = 128
ROPE = 64
QK_DIM = NOPE + ROPE
QK_PAD = 2 * LANES
V_DIM = 128
SB_DIM = 128
ATT_Q = 512
ATT_K = 256
MM_TK = 2816

ADAM_LR = 0.001
ADAM_B1 = 0.9
ADAM_B2 = 0.999
ADAM_EPS = 1e-08
ADAM_WD = 0.01
ADAM_STEP = 10

WEIGHT_NAMES = ("w_ada", "b_ada", "g_norm1", "g_norm2", "w_in", "g_q_latent", "g_kv_latent", "w_uq", "w_ukv",
                "g_q_head", "g_k_head", "w_proj_mla", "w_proj_sb", "w_out", "w_ffn_in", "w_ffn_out")
BIG = (("w_in", True), ("w_uq", True), ("w_ukv", True), ("w_proj_mla", True), ("w_proj_sb", True),
       ("w_out", False), ("w_ffn_in", True), ("w_ffn_out", False))
SMALL = ("b_ada", "g_norm1", "g_norm2", "g_q_latent", "g_kv_latent", "g_q_head", "g_k_head")


def _tile(n, pref):
    if n <= pref:
        return n
    for step in (LANES, 8):
        for t in range(pref - pref % step, 0, -step):
            if n % t == 0:
                return t
    return n


def _params():
    return pltpu.CompilerParams(vmem_limit_bytes=VMEM_LIMIT)


def _sigmoid(x):
    return 1.0 / (1.0 + jnp.exp(-x))


def _dot(a, b):
    return lax.dot_general(a, b, (((1,), (0,)), ((), ())), preferred_element_type=F32)


def _dot_nt(a, b):
    return lax.dot_general(a, b, (((1,), (1,)), ((), ())), preferred_element_type=F32)


def _dot_tn(a, b):
    return lax.dot_general(a, b, (((0,), (0,)), ((), ())), preferred_element_type=F32)


def _split_dot(x, tri):
    hi = x.astype(BF16)
    lo = (x - hi.astype(F32)).astype(BF16)
    return _dot(hi, tri) + _dot(lo, tri)


_HBM = pl.BlockSpec(memory_space=pltpu.HBM)


def _carry_parts(carry):
    src, gather = carry
    slot_shape = tuple(src.shape) if gather else tuple(src.shape[1:])
    return (src, _HBM, jax.ShapeDtypeStruct((N_DEV,) + slot_shape, src.dtype), _HBM,
            [pltpu.SemaphoreType.DMA((N_DEV - 1,)), pltpu.SemaphoreType.DMA((N_DEV - 1,)), pltpu.SemaphoreType.DMA(())])


def _exchange_copies(src_ref, dst_ref, send_sems, recv_sems, local_sem, gather):
    x, y, c = lax.axis_index("x"), lax.axis_index("y"), lax.axis_index("c")
    me = 4 * x + 2 * y + c

    def slot_for(idx):
        return src_ref if gather else src_ref.at[idx]

    copies = [pltpu.make_async_copy(slot_for(me), dst_ref.at[me], local_sem)]
    for k in range(1, N_DEV):
        peer = ((1 - x) if (k >> 2) & 1 else x, (1 - y) if (k >> 1) & 1 else y, (1 - c) if k & 1 else c)
        peer_idx = 4 * peer[0] + 2 * peer[1] + peer[2]
        copies.append(pltpu.make_async_remote_copy(
            src_ref=slot_for(peer_idx), dst_ref=dst_ref.at[me],
            send_sem=send_sems.at[k - 1], recv_sem=recv_sems.at[k - 1],
            device_id=peer, device_id_type=pl.DeviceIdType.MESH))
    return copies


def _carried_exchange(refs, gather, first, last):
    @pl.when(first)
    def _():
        for cp in _exchange_copies(*refs, gather):
            cp.start()

    @pl.when(last)
    def _():
        for cp in _exchange_copies(*refs, gather):
            cp.wait()


def _exchange(src, *, gather, name):
    operand, in_spec, out_shape, out_spec, scratch = _carry_parts((src, gather))

    def body(src_ref, dst_ref, send_sems, recv_sems, local_sem):
        copies = _exchange_copies(src_ref, dst_ref, send_sems, recv_sems, local_sem, gather)
        for cp in copies:
            cp.start()
        for cp in copies:
            cp.wait()

    return pl.pallas_call(body, name=name, out_shape=out_shape, in_specs=[in_spec], out_specs=out_spec,
                          scratch_shapes=scratch)(operand)


def _gather_two_level(src, name):
    operand, in_spec, out_shape, out_spec, scratch = _carry_parts((src, True))

    def body(src_ref, dst_ref, send_sems, recv_sems, local_sem):
        x, y, c = lax.axis_index("x"), lax.axis_index("y"), lax.axis_index("c")
        me, sibling = (x, y, c), (x, y, 1 - c)
        chips = [(1 - x, y), (x, 1 - y), (1 - x, 1 - y)]

        def slot(px, py, pc):
            return dst_ref.at[4 * px + 2 * py + pc]

        def copy(k, block, to, own=False):
            return pltpu.make_async_remote_copy(
                src_ref=src_ref if own else slot(*block), dst_ref=slot(*block),
                send_sem=send_sems.at[k], recv_sem=recv_sems.at[k],
                device_id=to, device_id_type=pl.DeviceIdType.MESH)

        mine = pltpu.make_async_copy(src_ref, slot(*me), local_sem)
        mine.start()
        first = [copy(0, me, sibling, own=True)] + [copy(1 + j, me, (*chip, c), own=True) for j, chip in enumerate(chips)]
        for cp in first:
            cp.start()
        passed = [copy(4 + j, (*chip, c), sibling) for j, chip in enumerate(chips)]
        for j, chip in enumerate(chips):
            copy(1 + j, (*chip, c), me).wait_recv()
            passed[j].start()
        copy(0, sibling, me).wait_recv()
        for j, chip in enumerate(chips):
            copy(4 + j, (*chip, 1 - c), me).wait_recv()
        for cp in first + passed:
            cp.wait_send()
        mine.wait()

    return pl.pallas_call(body, name=name, out_shape=out_shape, in_specs=[in_spec], out_specs=out_spec,
                          scratch_shapes=scratch)(operand)


def _gather_rows(v, name):
    out = _exchange(jnp.broadcast_to(v, (8, v.shape[1])), gather=True, name=name)
    return out[:, 0, :]


def _split_refs(refs, n_in, n_out, carry):
    if carry is None:
        return refs[:n_in], refs[n_in:n_in + n_out], refs[n_in + n_out:], None
    ins, src_ref = refs[:n_in], refs[n_in]
    outs, dst_ref = refs[n_in + 1:n_in + 1 + n_out], refs[n_in + 1 + n_out]
    rest = refs[n_in + n_out + 2:]
    return ins, outs, rest[:-3], (src_ref, dst_ref) + tuple(rest[-3:])


def _with_carry(carry, args, in_specs, out_shape, out_specs, scratch):
    if carry is not None:
        operand, c_in, c_shape, c_out, c_scratch = _carry_parts(carry)
        args.append(operand)
        in_specs.append(c_in)
        out_shape.append(c_shape)
        out_specs.append(c_out)
        scratch.extend(c_scratch)


def _mm(a, b, *, ta=False, tb=False, out_dtype=F32, name, carry=None):
    kdim, m = a.shape if ta else a.shape[::-1]
    n, kdim_b = b.shape if tb else b.shape[::-1]
    assert kdim == kdim_b, (a.shape, b.shape, ta, tb)
    tm, tn, tk = _tile(m, 1024), _tile(n, 1024), _tile(kdim, MM_TK)
    grid = (m // tm, n // tn, kdim // tk)
    nk = grid[2]
    dims = (((0 if ta else 1,), (1 if tb else 0,)), ((), ()))

    def body(*refs):
        (a_ref, b_ref), (o_ref,), scratch_refs, xrefs = _split_refs(refs, 2, 1, carry)
        i, j, k = pl.program_id(0), pl.program_id(1), pl.program_id(2)
        if carry is not None:
            _carried_exchange(xrefs, carry[1], (i == 0) & (j == 0) & (k == 0),
                              (i == grid[0] - 1) & (j == grid[1] - 1) & (k == nk - 1))
        prod = lax.dot_general(a_ref[...].astype(BF16), b_ref[...].astype(BF16), dims, preferred_element_type=F32)
        if nk == 1:
            o_ref[...] = prod.astype(o_ref.dtype)
        else:
            acc_ref = scratch_refs[0]

            @pl.when(k == 0)
            def _():
                acc_ref[...] = prod

            @pl.when(k > 0)
            def _():
                acc_ref[...] += prod

            @pl.when(k == nk - 1)
            def _():
                o_ref[...] = acc_ref[...].astype(o_ref.dtype)

    a_spec = (pl.BlockSpec((tk, tm), lambda i, j, k: (k, i)) if ta else pl.BlockSpec((tm, tk), lambda i, j, k: (i, k)))
    b_spec = (pl.BlockSpec((tn, tk), lambda i, j, k: (j, k)) if tb else pl.BlockSpec((tk, tn), lambda i, j, k: (k, j)))
    args, in_specs = [a, b], [a_spec, b_spec]
    out_shape, out_specs = [jax.ShapeDtypeStruct((m, n), out_dtype)], [pl.BlockSpec((tm, tn), lambda i, j, k: (i, j))]
    scratch = [] if nk == 1 else [pltpu.VMEM((tm, tn), F32)]
    _with_carry(carry, args, in_specs, out_shape, out_specs, scratch)
    out = pl.pallas_call(
        body, name=name, grid=grid, out_shape=tuple(out_shape), in_specs=in_specs, out_specs=tuple(out_specs),
        scratch_shapes=scratch, compiler_params=_params(),
    )(*args)
    return out if carry is not None else out[0]


def _ada_fwd(c_all, w_shard):
    d, n = w_shard.shape
    tn = _tile(n, 512)

    def body(c_ref, w_ref, o_ref):
        cv = c_ref[...]
        o_ref[...] = jnp.dot(cv * _sigmoid(cv), w_ref[...], precision=lax.Precision.HIGHEST,
                             preferred_element_type=F32)

    return pl.pallas_call(
        body, name="ada_fwd", grid=(n // tn,),
        out_shape=jax.ShapeDtypeStruct((N_DEV, n), F32),
        in_specs=[pl.BlockSpec((N_DEV, d), lambda j: (0, 0)), pl.BlockSpec((d, tn), lambda j: (0, j))],
        out_specs=pl.BlockSpec((N_DEV, tn), lambda j: (0, j)),
        compiler_params=_params(),
    )(c_all, w_shard)


def _ada_grad(c_all_t, d_rows):
    d, n = c_all_t.shape[0], d_rows.shape[1]
    tn = _tile(n, 512)

    def body(ct_ref, d_ref, o_ref):
        cv = ct_ref[...]
        s = cv * _sigmoid(cv)
        dv = d_ref[...]
        acc = s[:, 0:1] * dv[0:1, :]
        for b in range(1, N_DEV):
            acc = acc + s[:, b:b + 1] * dv[b:b + 1, :]
        o_ref[...] = acc

    return pl.pallas_call(
        body, name="ada_grad", grid=(n // tn,),
        out_shape=jax.ShapeDtypeStruct((d, n), F32),
        in_specs=[pl.BlockSpec((d, N_DEV), lambda j: (0, 0)), pl.BlockSpec((N_DEV, tn), lambda j: (0, j))],
        out_specs=pl.BlockSpec((d, tn), lambda j: (0, j)),
        compiler_params=_params(),
    )(c_all_t, d_rows)


def _row(ts, w, col=0):
    return pl.BlockSpec((ts, w), lambda i, col=col: (i, col))


def _vec(w):
    return pl.BlockSpec((1, w), lambda i: (0, 0))


def _norm_mod(x, g, sc, sh):
    s, d = x.shape
    ts = _tile(s, 512)

    def body(x_ref, g_ref, sc_ref, sh_ref, h_ref):
        xv = x_ref[...]
        r = lax.rsqrt(jnp.mean(xv * xv, axis=-1, keepdims=True) + EPS)
        h_ref[...] = ((xv * r) * g_ref[...] * (1.0 + sc_ref[...]) + sh_ref[...]).astype(BF16)

    return pl.pallas_call(
        body, name="norm_mod", grid=(s // ts,),
        out_shape=jax.ShapeDtypeStruct((s, d), BF16),
        in_specs=[_row(ts, d), _vec(d), _vec(d), _vec(d)],
        out_specs=_row(ts, d), compiler_params=_params(),
    )(x, g, sc, sh)


def _latent_norm(proj, off_q, off_kv, lat, g_q, g_kv):
    s = proj.shape[0]
    ts = _tile(s, 512)

    def body(cq_ref, ckv_ref, gq_ref, gkv_ref, oq_ref, okv_ref):
        for c_ref, g_ref, o_ref in ((cq_ref, gq_ref, oq_ref), (ckv_ref, gkv_ref, okv_ref)):
            v = c_ref[...]
            r = lax.rsqrt(jnp.mean(v * v, axis=-1, keepdims=True) + EPS)
            o_ref[...] = ((v * r) * g_ref[...]).astype(BF16)

    return pl.pallas_call(
        body, name="latent_norm", grid=(s // ts,),
        out_shape=(jax.ShapeDtypeStruct((s, lat), BF16), jax.ShapeDtypeStruct((s, lat), BF16)),
        in_specs=[_row(ts, lat, off_q // lat), _row(ts, lat, off_kv // lat), _vec(lat), _vec(lat)],
        out_specs=(_row(ts, lat), _row(ts, lat)), compiler_params=_params(),
    )(proj, proj, g_q, g_kv)


def _latent_norm_bwd(proj, off_q, off_kv, lat, g_q, g_kv, d_cqn, d_ckvn):
    s = proj.shape[0]
    ts = _tile(s, 512)

    def body(cq_ref, ckv_ref, gq_ref, gkv_ref, dq_ref, dkv_ref, oq_ref, okv_ref, sums_ref):
        @pl.when(pl.program_id(0) == 0)
        def _():
            sums_ref[...] = jnp.zeros_like(sums_ref)

        for row, (c_ref, g_ref, d_ref, o_ref) in enumerate(((cq_ref, gq_ref, dq_ref, oq_ref),
                                                             (ckv_ref, gkv_ref, dkv_ref, okv_ref))):
            v = c_ref[...]
            r = lax.rsqrt(jnp.mean(v * v, axis=-1, keepdims=True) + EPS)
            vn = v * r
            dn = d_ref[...]
            sums_ref[row:row + 1, :] += jnp.sum(dn * vn, axis=0, keepdims=True)
            dvn = dn * g_ref[...]
            o_ref[...] = (r * (dvn - vn * jnp.mean(dvn * vn, axis=-1, keepdims=True))).astype(BF16)

    return pl.pallas_call(
        body, name="latent_norm_bwd", grid=(s // ts,),
        out_shape=(jax.ShapeDtypeStruct((s, lat), BF16), jax.ShapeDtypeStruct((s, lat), BF16),
                   jax.ShapeDtypeStruct((8, lat), F32)),
        in_specs=[_row(ts, lat, off_q // lat), _row(ts, lat, off_kv // lat), _vec(lat), _vec(lat),
                  _row(ts, lat), _row(ts, lat)],
        out_specs=(_row(ts, lat), _row(ts, lat), pl.BlockSpec((8, lat), lambda i: (0, 0))),
        compiler_params=_params(),
    )(proj, proj, g_q, g_kv, d_cqn, d_ckvn)


def _merge(proj, off_a, off_b, ya_p, yb_p):
    s, d = ya_p.shape
    ts = _tile(s, 256)

    def body(ga_ref, gb_ref, ya_ref, yb_ref, o_ref):
        o_ref[...] = (_sigmoid(ga_ref[...]) * ya_ref[...] + _sigmoid(gb_ref[...]) * yb_ref[...]).astype(BF16)

    return pl.pallas_call(
        body, name="merge", grid=(s // ts,),
        out_shape=jax.ShapeDtypeStruct((s, d), BF16),
        in_specs=[_row(ts, d, off_a // d), _row(ts, d, off_b // d), _row(ts, d), _row(ts, d)],
        out_specs=_row(ts, d), compiler_params=_params(),
    )(proj, proj, ya_p, yb_p)


def _merge_bwd(proj, off_a, off_b, ya_p, yb_p, d_merged):
    s, d = ya_p.shape
    ts = _tile(s, 256)

    def body(ga_ref, gb_ref, ya_ref, yb_ref, dm_ref, dya_ref, dyb_ref, dga_ref, dgb_ref):
        dm = dm_ref[...]
        for g_ref, y_ref, dy_ref, dg_ref in ((ga_ref, ya_ref, dya_ref, dga_ref), (gb_ref, yb_ref, dyb_ref, dgb_ref)):
            sg = _sigmoid(g_ref[...])
            dy_ref[...] = (dm * sg).astype(BF16)
            dg_ref[...] = (dm * y_ref[...] * sg * (1.0 - sg)).astype(BF16)

    sd = jax.ShapeDtypeStruct((s, d), BF16)
    return pl.pallas_call(
        body, name="merge_bwd", grid=(s // ts,),
        out_shape=(sd, sd, sd, sd),
        in_specs=[_row(ts, d, off_a // d), _row(ts, d, off_b // d), _row(ts, d), _row(ts, d), _row(ts, d)],
        out_specs=(_row(ts, d),) * 4, compiler_params=_params(),
    )(proj, proj, ya_p, yb_p, d_merged)


def _resid_norm_mod(x, o, gt, g, sc, sh):
    s, d = x.shape
    ts = _tile(s, 256)

    def body(x_ref, o_ref, gt_ref, g_ref, sc_ref, sh_ref, x1_ref, h_ref):
        x1 = x_ref[...] + gt_ref[...] * o_ref[...]
        x1_ref[...] = x1
        r = lax.rsqrt(jnp.mean(x1 * x1, axis=-1, keepdims=True) + EPS)
        h_ref[...] = ((x1 * r) * g_ref[...] * (1.0 + sc_ref[...]) + sh_ref[...]).astype(BF16)

    return pl.pallas_call(
        body, name="resid_norm_mod", grid=(s // ts,),
        out_shape=(jax.ShapeDtypeStruct((s, d), F32), jax.ShapeDtypeStruct((s, d), BF16)),
        in_specs=[_row(ts, d), _row(ts, d), _vec(d), _vec(d), _vec(d), _vec(d)],
        out_specs=(_row(ts, d), _row(ts, d)), compiler_params=_params(),
    )(x, o, gt, g, sc, sh)


def _loss_head(x1, o2, gt2, target):
    s, d = x1.shape
    ts = _tile(s, 256)

    def body(x1_ref, o2_ref, gt_ref, t_ref, dy_ref, do_ref, sums_ref):
        @pl.when(pl.program_id(0) == 0)
        def _():
            sums_ref[...] = jnp.zeros_like(sums_ref)

        o2 = o2_ref[...]
        e = x1_ref[...] + gt_ref[...] * o2 - t_ref[...]
        dy = e / d
        dy_ref[...] = dy
        do_ref[...] = (dy * gt_ref[...]).astype(BF16)
        sums_ref[0:1, :] += jnp.sum(dy * o2, axis=0, keepdims=True)
        sums_ref[1:2, :] += jnp.sum(e * e, axis=0, keepdims=True)

    return pl.pallas_call(
        body, name="loss_head", grid=(s // ts,),
        out_shape=(jax.ShapeDtypeStruct((s, d), F32), jax.ShapeDtypeStruct((s, d), BF16),
                   jax.ShapeDtypeStruct((8, d), F32)),
        in_specs=[_row(ts, d), _row(ts, d), _vec(d), _row(ts, d)],
        out_specs=(_row(ts, d), _row(ts, d), pl.BlockSpec((8, d), lambda i: (0, 0))),
        compiler_params=_params(),
    )(x1, o2, gt2, target)


def _norm_mod_bwd(dh, xin, g, sc, dres, gate=None):
    s, d = xin.shape
    ts = _tile(s, 256)
    gated = gate is not None

    def body(*refs):
        if gated:
            dh_ref, x_ref, g_ref, sc_ref, dr_ref, o_ref, gt_ref, dx_ref, do_ref, sums_ref = refs
        else:
            dh_ref, x_ref, g_ref, sc_ref, dr_ref, dx_ref, sums_ref = refs

        @pl.when(pl.program_id(0) == 0)
        def _():
            sums_ref[...] = jnp.zeros_like(sums_ref)

        xv, dhv = x_ref[...], dh_ref[...]
        r = lax.rsqrt(jnp.mean(xv * xv, axis=-1, keepdims=True) + EPS)
        xn = xv * r
        one_sc = 1.0 + sc_ref[...]
        sums_ref[0:1, :] += jnp.sum(dhv, axis=0, keepdims=True)
        sums_ref[1:2, :] += jnp.sum(dhv * (xn * g_ref[...]), axis=0, keepdims=True)
        sums_ref[2:3, :] += jnp.sum(dhv * one_sc * xn, axis=0, keepdims=True)
        dxn = dhv * one_sc * g_ref[...]
        dx = dr_ref[...] + r * (dxn - xn * jnp.mean(dxn * xn, axis=-1, keepdims=True))
        dx_ref[...] = dx
        if gated:
            sums_ref[3:4, :] += jnp.sum(dx * o_ref[...], axis=0, keepdims=True)
            do_ref[...] = (dx * gt_ref[...]).astype(BF16)

    in_specs = [_row(ts, d), _row(ts, d), _vec(d), _vec(d), _row(ts, d)]
    args = [dh, xin, g, sc, dres]
    out_shape = [jax.ShapeDtypeStruct((s, d), F32)]
    out_specs = [_row(ts, d)]
    if gated:
        in_specs += [_row(ts, d), _vec(d)]
        args += list(gate)
        out_shape.append(jax.ShapeDtypeStruct((s, d), BF16))
        out_specs.append(_row(ts, d))
    out_shape.append(jax.ShapeDtypeStruct((8, d), F32))
    out_specs.append(pl.BlockSpec((8, d), lambda i: (0, 0)))
    return pl.pallas_call(
        body, name="norm_mod_bwd_gated" if gated else "norm_mod_bwd", grid=(s // ts,),
        out_shape=tuple(out_shape), in_specs=in_specs, out_specs=tuple(out_specs), compiler_params=_params(),
    )(*args)


def _ffn_tile(f):
    return _tile(f, 512)


def _interleave(w, f):
    tc = _ffn_tile(f)
    return w.reshape(w.shape[0], 2, f // tc, tc).transpose(0, 2, 1, 3).reshape(w.shape[0], 2 * f)


def _deinterleave(w, f):
    tc = _ffn_tile(f)
    return w.reshape(w.shape[0], f // tc, 2, tc).transpose(0, 2, 1, 3).reshape(w.shape[0], 2 * f)


def _swiglu(gu):
    s, f2 = gu.shape
    f = f2 // 2
    ts, tc = _tile(s, 512), _ffn_tile(f)

    def body(gu_ref, o_ref):
        gv, uv = gu_ref[:, :tc].astype(F32), gu_ref[:, tc:].astype(F32)
        o_ref[...] = (gv * _sigmoid(gv) * uv).astype(BF16)

    return pl.pallas_call(
        body, name="swiglu", grid=(s // ts, f // tc),
        out_shape=jax.ShapeDtypeStruct((s, f), BF16),
        in_specs=[pl.BlockSpec((ts, 2 * tc), lambda i, j: (i, j))],
        out_specs=pl.BlockSpec((ts, tc), lambda i, j: (i, j)), compiler_params=_params(),
    )(gu)


def _swiglu_bwd(gu, d_act):
    s, f2 = gu.shape
    f = f2 // 2
    ts, tc = _tile(s, 512), _ffn_tile(f)

    def body(gu_ref, da_ref, o_ref):
        gv, uv, da = gu_ref[:, :tc].astype(F32), gu_ref[:, tc:].astype(F32), da_ref[...]
        sg = _sigmoid(gv)
        o_ref[:, :tc] = (da * uv * (sg * (1.0 + gv * (1.0 - sg)))).astype(BF16)
        o_ref[:, tc:] = (da * (gv * sg)).astype(BF16)

    return pl.pallas_call(
        body, name="swiglu_bwd", grid=(s // ts, f // tc),
        out_shape=jax.ShapeDtypeStruct((s, f2), BF16),
        in_specs=[pl.BlockSpec((ts, 2 * tc), lambda i, j: (i, j)), pl.BlockSpec((ts, tc), lambda i, j: (i, j))],
        out_specs=pl.BlockSpec((ts, 2 * tc), lambda i, j: (i, j)), compiler_params=_params(),
    )(gu, d_act)


def _swap_halves(t):
    return pltpu.roll(t, ROPE // 2, 1) + pltpu.roll(t, LANES - ROPE // 2, 1)


def _head_norm(raw):
    r = lax.rsqrt(jnp.sum(raw * raw, axis=-1, keepdims=True) / QK_DIM + EPS)
    return raw * r, r


def _rope_fwd(v, cos, sin):
    rope_tile = v[:, NOPE:]
    return jnp.concatenate([v[:, :NOPE], rope_tile * cos + _swap_halves(rope_tile) * sin], axis=1)


def _rope_bwd(d, cos, sin, lane_ok):
    d_tile = d[:, NOPE:]
    return jnp.concatenate([d[:, :NOPE], d_tile * cos + _swap_halves(d_tile * sin) * lane_ok], axis=1)


def _qk_prep(q_raw, kv, proj, off_pe, cos, sin, g_q, g_k):
    s, hw = q_raw.shape
    heads = hw // QK_PAD
    ts = _tile(s, 512)

    def body(q_ref, kv_ref, pe_ref, cos_ref, sin_ref, gq_ref, gk_ref, qo_ref, ko_ref, vo_ref):
        cos_v, sin_v = cos_ref[...], sin_ref[...]
        qn, _ = _head_norm(q_ref[...])
        qo_ref[...] = _rope_fwd(qn * gq_ref[...], cos_v, sin_v).astype(BF16)
        kvv = kv_ref[...]
        kn, _ = _head_norm(jnp.concatenate([kvv[:, :NOPE], pe_ref[...]], axis=1))
        ko_ref[...] = _rope_fwd(kn * gk_ref[...], cos_v, sin_v).astype(BF16)
        vo_ref[...] = kvv[:, NOPE:].astype(BF16)

    blk = lambda w: pl.BlockSpec((ts, w), lambda i, h: (i, h))
    fixed = lambda w, col=0: pl.BlockSpec((ts, w), lambda i, h, col=col: (i, col))
    vec = pl.BlockSpec((1, QK_PAD), lambda i, h: (0, 0))
    return pl.pallas_call(
        body, name="qk_prep", grid=(s // ts, heads),
        out_shape=(jax.ShapeDtypeStruct((s, hw), BF16), jax.ShapeDtypeStruct((s, hw), BF16),
                   jax.ShapeDtypeStruct((s, heads * V_DIM), BF16)),
        in_specs=[blk(QK_PAD), blk(QK_PAD), fixed(LANES, off_pe // LANES), fixed(LANES), fixed(LANES), vec, vec],
        out_specs=(blk(QK_PAD), blk(QK_PAD), blk(V_DIM)), compiler_params=_params(),
    )(q_raw, kv, proj, cos, sin, g_q, g_k)


def _qk_prep_bwd(q_raw, kv, proj, off_pe, cos, sin, g_q, g_k, dq, dk, dv):
    s, hw = q_raw.shape
    heads = hw // QK_PAD
    ts = _tile(s, 512)

    def body(q_ref, kv_ref, pe_ref, cos_ref, sin_ref, gq_ref, gk_ref, dq_ref, dk_ref, dv_ref,
             dqr_ref, dkv_ref, dpe_ref, sums_ref):
        i, h = pl.program_id(0), pl.program_id(1)

        @pl.when((i == 0) & (h == 0))
        def _():
            sums_ref[...] = jnp.zeros_like(sums_ref)

        cos_v, sin_v = cos_ref[...], sin_ref[...]
        lane_ok = (lax.broadcasted_iota(jnp.int32, (ts, LANES), 1) < ROPE).astype(F32)

        def one(raw, g, d_post, row):
            vn, r = _head_norm(raw)
            d_pre = _rope_bwd(d_post, cos_v, sin_v, lane_ok)
            sums_ref[row:row + 1, :] += jnp.sum(d_pre * vn, axis=0, keepdims=True)
            dvn = d_pre * g
            return r * (dvn - vn * (jnp.sum(dvn * vn, axis=-1, keepdims=True) / QK_DIM))

        dqr_ref[...] = one(q_ref[...], gq_ref[...], dq_ref[...], 0).astype(BF16)
        kvv = kv_ref[...]
        d_kraw = one(jnp.concatenate([kvv[:, :NOPE], pe_ref[...]], axis=1), gk_ref[...], dk_ref[...], 1)
        dkv_ref[...] = jnp.concatenate([d_kraw[:, :NOPE], dv_ref[...]], axis=1).astype(BF16)

        @pl.when(h == 0)
        def _():
            dpe_ref[...] = jnp.zeros_like(dpe_ref)

        dpe_ref[...] += d_kraw[:, NOPE:]

    blk = lambda w: pl.BlockSpec((ts, w), lambda i, h: (i, h))
    fixed = lambda w, col=0: pl.BlockSpec((ts, w), lambda i, h, col=col: (i, col))
    vec = pl.BlockSpec((1, QK_PAD), lambda i, h: (0, 0))
    return pl.pallas_call(
        body, name="qk_prep_bwd", grid=(s // ts, heads),
        out_shape=(jax.ShapeDtypeStruct((s, hw), BF16), jax.ShapeDtypeStruct((s, hw), BF16),
                   jax.ShapeDtypeStruct((s, LANES), F32), jax.ShapeDtypeStruct((8, QK_PAD), F32)),
        in_specs=[blk(QK_PAD), blk(QK_PAD), fixed(LANES, off_pe // LANES), fixed(LANES), fixed(LANES), vec, vec,
                  blk(QK_PAD), blk(QK_PAD), blk(V_DIM)],
        out_specs=(blk(QK_PAD), blk(QK_PAD), fixed(LANES), pl.BlockSpec((8, QK_PAD), lambda i, h: (0, 0))),
        compiler_params=_params(),
    )(q_raw, kv, proj, cos, sin, g_q, g_k, dq, dk, dv)


def _att_blocks(s):
    tq = _tile(s, ATT_Q)
    tk = _tile(tq, ATT_K)
    return tq, tk, tq // tk


def _visible(qi, kb, tq, tk, strict):
    row = qi * tq + lax.broadcasted_iota(jnp.int32, (tq, tk), 0)
    col = kb * tk + lax.broadcasted_iota(jnp.int32, (tq, tk), 1)
    return (col < row) if strict else (col <= row)


def _first_last(heads, nq):
    h, qi = pl.program_id(0), pl.program_id(1)
    return (h == 0) & (qi == 0), (h == heads - 1) & (qi == nq - 1)


def _mla_fwd(q, k, v, carry=None):
    s = q.shape[0]
    heads = q.shape[1] // QK_PAD
    tq, tk, ratio = _att_blocks(s)
    nq = s // tq
    scale = QK_DIM ** -0.5

    def body(*refs):
        (q_ref, k_ref, v_ref), (o_ref, lse_ref), (acc_ref,), xrefs = _split_refs(refs, 3, 2, carry)
        if carry is not None:
            _carried_exchange(xrefs, carry[1], *_first_last(heads, nq))
        qi = pl.program_id(1)
        qv = q_ref[...]
        acc_ref[...] = jnp.zeros_like(acc_ref)

        def step(kb, state, masked):
            m, l = state
            off = pl.multiple_of(kb * tk, tk)
            sc = _dot_nt(qv, k_ref[pl.ds(off, tk), :]) * scale
            if masked:
                sc = jnp.where(_visible(qi, kb, tq, tk, False), sc, -1e30)
            m_new = jnp.maximum(m, jnp.max(sc, axis=-1, keepdims=True))
            alpha = jnp.exp(m - m_new)
            p = jnp.exp(sc - m_new)
            acc_ref[...] = alpha * acc_ref[...] + _dot(p.astype(BF16), v_ref[pl.ds(off, tk), :])
            return m_new, alpha * l + jnp.sum(p, axis=-1, keepdims=True)

        state = (jnp.full((tq, 1), -1e30, F32), jnp.zeros((tq, 1), F32))
        state = lax.fori_loop(0, qi * ratio, lambda kb, st: step(kb, st, False), state)
        for i in range(ratio):
            state = step(qi * ratio + i, state, True)
        m, l = state
        o_ref[...] = (acc_ref[...] / l).astype(BF16)
        lse_ref[0] = m + jnp.log(l)

    args = [q, k, v]
    in_specs = [pl.BlockSpec((tq, QK_PAD), lambda h, i: (i, h)), pl.BlockSpec((s, QK_PAD), lambda h, i: (0, h)),
                pl.BlockSpec((s, V_DIM), lambda h, i: (0, h))]
    out_shape = [jax.ShapeDtypeStruct((s, heads * V_DIM), BF16), jax.ShapeDtypeStruct((heads, s, 1), F32)]
    out_specs = [pl.BlockSpec((tq, V_DIM), lambda h, i: (i, h)), pl.BlockSpec((1, tq, 1), lambda h, i: (h, i, 0))]
    scratch = [pltpu.VMEM((tq, V_DIM), F32)]
    _with_carry(carry, args, in_specs, out_shape, out_specs, scratch)
    return pl.pallas_call(
        body, name="mla_fwd", grid=(heads, nq), out_shape=tuple(out_shape), in_specs=in_specs,
        out_specs=tuple(out_specs), scratch_shapes=scratch, compiler_params=_params(),
    )(*args)


def _mla_bwd(q, k, v, o, lse, do, carry=None):
    s = q.shape[0]
    heads = q.shape[1] // QK_PAD
    tq, tk, ratio = _att_blocks(s)
    nq = s // tq
    scale = QK_DIM ** -0.5

    def body(*refs):
        (q_ref, k_ref, v_ref, o_ref, lse_ref, do_ref), (dq_ref, dk_ref, dv_ref), _, xrefs = _split_refs(refs, 6, 3, carry)
        if carry is not None:
            _carried_exchange(xrefs, carry[1], *_first_last(heads, nq))
        qi = pl.program_id(1)

        @pl.when(qi == 0)
        def _():
            dk_ref[...] = jnp.zeros_like(dk_ref)
            dv_ref[...] = jnp.zeros_like(dv_ref)

        qv = q_ref[...]
        dov = do_ref[...]
        do_b = dov.astype(BF16)
        delta = jnp.sum(dov * o_ref[...].astype(F32), axis=-1, keepdims=True)
        lse_v = lse_ref[0]

        dq_ref[...] = jnp.zeros_like(dq_ref)

        def step(kb, masked):
            off = pl.multiple_of(kb * tk, tk)
            kv_ = k_ref[pl.ds(off, tk), :]
            p = jnp.exp(_dot_nt(qv, kv_) * scale - lse_v)
            if masked:
                p = jnp.where(_visible(qi, kb, tq, tk, False), p, 0.0)
            dp = _dot_nt(do_b, v_ref[pl.ds(off, tk), :])
            ds = (p * (dp - delta) * scale).astype(BF16)
            dk_ref[pl.ds(off, tk), :] += _dot_tn(ds, qv)
            dv_ref[pl.ds(off, tk), :] += _dot_tn(p.astype(BF16), do_b)
            dq_ref[...] += _dot(ds, kv_)
            return 0

        lax.fori_loop(0, qi * ratio, lambda kb, _: step(kb, False), 0)
        for i in range(ratio):
            step(qi * ratio + i, True)

    args = [q, k, v, o, lse, do]
    in_specs = [pl.BlockSpec((tq, QK_PAD), lambda h, i: (i, h)), pl.BlockSpec((s, QK_PAD), lambda h, i: (0, h)),
                pl.BlockSpec((s, V_DIM), lambda h, i: (0, h)), pl.BlockSpec((tq, V_DIM), lambda h, i: (i, h)),
                pl.BlockSpec((1, tq, 1), lambda h, i: (h, i, 0)), pl.BlockSpec((tq, V_DIM), lambda h, i: (i, h))]
    out_shape = [jax.ShapeDtypeStruct(q.shape, F32), jax.ShapeDtypeStruct(k.shape, F32),
                 jax.ShapeDtypeStruct(v.shape, F32)]
    out_specs = [pl.BlockSpec((tq, QK_PAD), lambda h, i: (i, h)), pl.BlockSpec((s, QK_PAD), lambda h, i: (0, h)),
                 pl.BlockSpec((s, V_DIM), lambda h, i: (0, h))]
    scratch = []
    _with_carry(carry, args, in_specs, out_shape, out_specs, scratch)
    return pl.pallas_call(
        body, name="mla_bwd", grid=(heads, nq), out_shape=tuple(out_shape), in_specs=in_specs,
        out_specs=tuple(out_specs), scratch_shapes=scratch, compiler_params=_params(),
    )(*args)


def _sb_terms(qv, k_blk, scale, mask):
    z = _dot_nt(qv, k_blk) * scale
    log_beta = jnp.minimum(z, 0.0) - jnp.log(1.0 + jnp.exp(-jnp.abs(z)))
    log_rest = log_beta - z
    if mask is not None:
        log_rest = jnp.where(mask, log_rest, 0.0)
    return log_beta, log_rest


def _sb_specs(s, tq, off_q, off_k, off_v):
    cb = lambda off: off // SB_DIM
    return [pl.BlockSpec((tq, SB_DIM), lambda h, i: (i, cb(off_q) + h)),
            pl.BlockSpec((s, SB_DIM), lambda h, i: (0, cb(off_k) + h)),
            pl.BlockSpec((s, SB_DIM), lambda h, i: (0, cb(off_v) + h))]


def _sb_fwd(proj, off_q, off_k, off_v, heads, carry=None):
    s = proj.shape[0]
    tq, tk, ratio = _att_blocks(s)
    nq = s // tq
    scale = SB_DIM ** -0.5

    def body(*refs):
        (q_ref, k_ref, v_ref), (o_ref,), (acc_ref,), xrefs = _split_refs(refs, 3, 1, carry)
        if carry is not None:
            _carried_exchange(xrefs, carry[1], *_first_last(heads, nq))
        qi = pl.program_id(1)
        qv = q_ref[...].astype(BF16)
        after = (lax.broadcasted_iota(jnp.int32, (tk, tk), 0) > lax.broadcasted_iota(jnp.int32, (tk, tk), 1)).astype(BF16)
        acc_ref[...] = jnp.zeros_like(acc_ref)

        def step(kb, tail_right, masked):
            off = pl.multiple_of(kb * tk, tk)
            mask = _visible(qi, kb, tq, tk, True) if masked else None
            log_beta, log_rest = _sb_terms(qv, k_ref[pl.ds(off, tk), :].astype(BF16), scale, mask)
            tail = _split_dot(log_rest, after) + tail_right
            a = jnp.exp(log_beta + tail)
            if masked:
                a = jnp.where(mask, a, 0.0)
            acc_ref[...] += _dot(a.astype(BF16), v_ref[pl.ds(off, tk), :].astype(BF16))
            return tail_right + jnp.sum(log_rest, axis=-1, keepdims=True)

        tail_right = jnp.zeros((tq, 1), F32)
        for i in range(ratio):
            tail_right = step((qi + 1) * ratio - 1 - i, tail_right, True)
        lax.fori_loop(0, qi * ratio, lambda i, t_: step(qi * ratio - 1 - i, t_, False), tail_right)
        o_ref[...] = acc_ref[...].astype(BF16)

    args, in_specs = [proj, proj, proj], _sb_specs(s, tq, off_q, off_k, off_v)
    out_shape = [jax.ShapeDtypeStruct((s, heads * SB_DIM), BF16)]
    out_specs = [pl.BlockSpec((tq, SB_DIM), lambda h, i: (i, h))]
    scratch = [pltpu.VMEM((tq, SB_DIM), F32)]
    _with_carry(carry, args, in_specs, out_shape, out_specs, scratch)
    out = pl.pallas_call(
        body, name="sb_fwd", grid=(heads, nq), out_shape=tuple(out_shape), in_specs=in_specs,
        out_specs=tuple(out_specs), scratch_shapes=scratch, compiler_params=_params(),
    )(*args)
    return out if carry is not None else out[0]


def _sb_bwd(proj, off_q, off_k, off_v, heads, dy, carry=None):
    s = proj.shape[0]
    tq, tk, ratio = _att_blocks(s)
    nq = s // tq
    scale = SB_DIM ** -0.5

    def body(*refs):
        (q_ref, k_ref, v_ref, dy_ref), (dq_ref, dk_ref, dv_ref), (g_s, beta_s), xrefs = _split_refs(refs, 4, 3, carry)
        if carry is not None:
            _carried_exchange(xrefs, carry[1], *_first_last(heads, nq))
        qi = pl.program_id(1)

        @pl.when(qi == 0)
        def _():
            dk_ref[...] = jnp.zeros_like(dk_ref)
            dv_ref[...] = jnp.zeros_like(dv_ref)

        qv = q_ref[...].astype(BF16)
        dy_b = dy_ref[...].astype(BF16)
        rows = lax.broadcasted_iota(jnp.int32, (tk, tk), 0)
        cols = lax.broadcasted_iota(jnp.int32, (tk, tk), 1)
        after = (rows > cols).astype(BF16)
        before = (rows < cols).astype(BF16)

        def pass1(kb, tail_right, masked):
            off = pl.multiple_of(kb * tk, tk)
            mask = _visible(qi, kb, tq, tk, True) if masked else None
            log_beta, log_rest = _sb_terms(qv, k_ref[pl.ds(off, tk), :].astype(BF16), scale, mask)
            tail = _split_dot(log_rest, after) + tail_right
            a = jnp.exp(log_beta + tail)
            beta = jnp.exp(log_beta)
            if masked:
                a = jnp.where(mask, a, 0.0)
                beta = jnp.where(mask, beta, 0.0)
            da = _dot_nt(dy_b, v_ref[pl.ds(off, tk), :].astype(BF16))
            dv_ref[pl.ds(off, tk), :] += _dot_tn(a.astype(BF16), dy_b)
            g_s[kb] = a * da
            beta_s[kb] = beta
            return tail_right + jnp.sum(log_rest, axis=-1, keepdims=True)

        tail_right = jnp.zeros((tq, 1), F32)
        for i in range(ratio):
            tail_right = pass1((qi + 1) * ratio - 1 - i, tail_right, True)
        lax.fori_loop(0, qi * ratio, lambda i, t_: pass1(qi * ratio - 1 - i, t_, False), tail_right)

        dq_ref[...] = jnp.zeros_like(dq_ref)

        def pass2(kb, g_left):
            off = pl.multiple_of(kb * tk, tk)
            g, beta = g_s[kb], beta_s[kb]
            g_before = _split_dot(g, before) + g_left
            dz = ((g * (1.0 - beta) - g_before * beta) * scale).astype(BF16)
            dk_ref[pl.ds(off, tk), :] += _dot_tn(dz, qv)
            dq_ref[...] += _dot(dz, k_ref[pl.ds(off, tk), :].astype(BF16))
            return g_left + jnp.sum(g, axis=-1, keepdims=True)

        lax.fori_loop(0, (qi + 1) * ratio, pass2, jnp.zeros((tq, 1), F32))

    args = [proj, proj, proj, dy]
    in_specs = _sb_specs(s, tq, off_q, off_k, off_v) + [pl.BlockSpec((tq, SB_DIM), lambda h, i: (i, h))]
    out_shape = [jax.ShapeDtypeStruct((s, heads * SB_DIM), F32)] * 3
    out_specs = [pl.BlockSpec((tq, SB_DIM), lambda h, i: (i, h)), pl.BlockSpec((s, SB_DIM), lambda h, i: (0, h)),
                 pl.BlockSpec((s, SB_DIM), lambda h, i: (0, h))]
    scratch = [pltpu.VMEM((s // tk, tq, tk), F32), pltpu.VMEM((s // tk, tq, tk), F32)]
    _with_carry(carry, args, in_specs, out_shape, out_specs, scratch)
    return pl.pallas_call(
        body, name="sb_bwd", grid=(heads, nq), out_shape=tuple(out_shape), in_specs=in_specs,
        out_specs=tuple(out_specs), scratch_shapes=scratch, compiler_params=_params(),
    )(*args)


def _slot_sum(recv, name):
    _, r, w = recv.shape
    tr = _tile(r, 256)

    def body(r_ref, o_ref):
        acc = r_ref[0].astype(F32)
        for d in range(1, N_DEV):
            acc = acc + r_ref[d].astype(F32)
        o_ref[...] = acc

    return pl.pallas_call(
        body, name=name, grid=(r // tr,),
        out_shape=jax.ShapeDtypeStruct((r, w), F32),
        in_specs=[pl.BlockSpec((N_DEV, tr, w), lambda i: (0, i, 0))],
        out_specs=pl.BlockSpec((tr, w), lambda i: (i, 0)), compiler_params=_params(),
    )(recv)


def _small_sum(rows, loss_lo, loss_hi, d_model):
    n = rows.shape[1]

    def body(r_ref, o_ref, loss_ref):
        rv = r_ref[...]
        acc = rv[0:1, :]
        for d in range(1, N_DEV):
            acc = acc + rv[d:d + 1, :]
        o_ref[...] = acc
        total = jnp.sum(acc[:, loss_lo:loss_hi], axis=-1, keepdims=True) * (0.5 / d_model)
        loss_ref[...] = jnp.broadcast_to(total, (1, LANES))

    return pl.pallas_call(
        body, name="small_sum",
        out_shape=(jax.ShapeDtypeStruct((1, n), F32), jax.ShapeDtypeStruct((1, LANES), F32)),
        compiler_params=_params(),
    )(rows)


def _adamw(w, g, m, v, name):
    r, c = w.shape
    tr = _tile(r, max(8, (1 << 18) // c // 8 * 8))

    def body(w_ref, g_ref, m_ref, v_ref, d_ref, mo_ref, vo_ref):
        gv = g_ref[...]
        m_new = ADAM_B1 * m_ref[...] + (1.0 - ADAM_B1) * gv
        v_new = ADAM_B2 * v_ref[...] + (1.0 - ADAM_B2) * (gv * gv)
        m_hat = m_new / (1.0 - ADAM_B1 ** ADAM_STEP)
        v_hat = v_new / (1.0 - ADAM_B2 ** ADAM_STEP)
        d_ref[...] = -ADAM_LR * (m_hat / (jnp.sqrt(v_hat) + ADAM_EPS) + ADAM_WD * w_ref[...])
        mo_ref[...] = m_new
        vo_ref[...] = v_new

    spec = pl.BlockSpec((tr, c), lambda i: (i, 0))
    sd = jax.ShapeDtypeStruct((r, c), F32)
    return pl.pallas_call(
        body, name=name, grid=(r // tr,), out_shape=(sd, sd, sd),
        in_specs=[spec] * 4, out_specs=(spec,) * 3, compiler_params=_params(),
    )(w, g, m, v)


def _pack_rows(a):
    return a.reshape(-1, PACK_W)


def _unshard(slots, shape, col_sharded):
    r, c = shape
    if col_sharded:
        return slots.reshape(N_DEV, r, c).transpose(1, 0, 2).reshape(r, N_DEV * c)
    return slots.reshape(N_DEV * r, c)


def _to_shards(full, col_sharded, packed=True):
    r, c = full.shape
    shards = (full.reshape(r, N_DEV, c // N_DEV).transpose(1, 0, 2) if col_sharded
              else full.reshape(N_DEV, r // N_DEV, c))
    return shards.reshape(N_DEV, -1, PACK_W) if packed else shards


def kernel(x, c, positions, w_ada, b_ada, g_norm1, g_norm2, w_in, g_q_latent, g_kv_latent, w_uq, w_ukv, g_q_head, g_k_head, w_proj_mla, w_proj_sb, w_out, w_ffn_in, w_ffn_out, loss_target, m_w_ada, m_b_ada, m_g_norm1, m_g_norm2, m_w_in, m_g_q_latent, m_g_kv_latent, m_w_uq, m_w_ukv, m_g_q_head, m_g_k_head, m_w_proj_mla, m_w_proj_sb, m_w_out, m_w_ffn_in, m_w_ffn_out, v_w_ada, v_b_ada, v_g_norm1, v_g_norm2, v_w_in, v_g_q_latent, v_g_kv_latent, v_w_uq, v_w_ukv, v_g_q_head, v_g_k_head, v_w_proj_mla, v_w_proj_sb, v_w_out, v_w_ffn_in, v_w_ffn_out):
    env = dict(locals())
    drop = lambda a: a[0] if a.ndim == 3 else a
    wts = {n: drop(env[n]) for n in WEIGHT_NAMES}
    mom = {n: drop(env["m_" + n]) for n in WEIGHT_NAMES}
    var = {n: drop(env["v_" + n]) for n in WEIGHT_NAMES}
    xs, tgt, pos = x[0], loss_target[0], positions[0]
    s, d = xs.shape
    lat = wts["w_uq"].shape[0]
    assert wts["w_ukv"].shape[0] == lat
    h_mla = wts["w_uq"].shape[1] * N_DEV // QK_DIM
    sb_w = wts["w_proj_sb"].shape[0]
    h_sb = sb_w // SB_DIM
    d_ff = wts["w_ffn_out"].shape[0] * N_DEV
    me = 4 * lax.axis_index("x") + 2 * lax.axis_index("y") + lax.axis_index("c")

    ref_w = (("c_q", lat), ("c_kv", lat), ("k_pe", ROPE), ("q_sb", sb_w), ("k_sb", sb_w), ("v_sb", sb_w),
             ("gl_a", d), ("gl_b", d))
    ref_off, o = {}, 0
    for n_, w_ in ref_w:
        ref_off[n_] = (o, w_)
        o += w_
    order = ("gl_a", "gl_b", "q_sb", "k_sb", "v_sb", "c_q", "c_kv", "k_pe")
    off, o = {}, 0
    for n_ in order:
        w_ = LANES if n_ == "k_pe" else ref_off[n_][1]
        assert o % w_ == 0
        off[n_] = o
        o += w_

    used_w = o
    proj_w = -(-used_w // (2 * LANES)) * (2 * LANES)

    col_sharded = dict(BIG)
    rows_of = {n: wts[n].size // PACK_W for n, _ in BIG}
    full, grads = {}, {}

    def own_shape(names):
        return len(names) == 1 and wts[names[0]].shape[1] % LANES == 0

    def pack_weights(names):
        if own_shape(names):
            return wts[names[0]].astype(BF16)
        return jnp.concatenate([_pack_rows(wts[n].astype(BF16)) for n in names], axis=0)

    def unpack_weights(slots, names):
        r0 = 0
        for n in names:
            part = slots if own_shape(names) else slots[:, r0:r0 + rows_of[n]]
            full[n] = _unshard(part, wts[n].shape, col_sharded[n])
            r0 += rows_of[n]

    def pack_grads(names):
        if own_shape(names):
            return _to_shards(grads[names[0]].astype(BF16), col_sharded[names[0]], packed=False)
        return jnp.concatenate([_to_shards(grads[n].astype(BF16), col_sharded[n]) for n in names], axis=1)

    def unpack_grads(recv, names):
        summed_rows, r0 = _slot_sum(recv, "slot_sum_" + names[0]), 0
        for n in names:
            part = summed_rows if own_shape(names) else summed_rows[r0:r0 + rows_of[n]]
            grads[n] = part.reshape(wts[n].shape)
            r0 += rows_of[n]

    unpack_weights(_gather_two_level(pack_weights(("w_in",)), "gather_w_in"), ("w_in",))
    seg = lambda a, n_: a[:, ref_off[n_][0]:ref_off[n_][0] + ref_off[n_][1]]
    w_in_k = jnp.concatenate([seg(full["w_in"], n_) for n_ in order]
                             + [jnp.zeros((d, proj_w - used_w + LANES - ROPE), BF16)], axis=1)
    pad_gain = lambda g: jnp.pad(g, ((0, 0), (0, QK_PAD - QK_DIM)))
    g_qh, g_kh = pad_gain(wts["g_q_head"]), pad_gain(wts["g_k_head"])

    c_all = _gather_rows(c, "gather_c")
    ada_cols = _exchange(_ada_fwd(c_all, wts["w_ada"]), gather=True, name="gather_ada")
    ada = lax.dynamic_index_in_dim(ada_cols, me, axis=1, keepdims=False).reshape(1, 6 * d) + wts["b_ada"]
    sh1, sc1, gt1, sh2, sc2, gt2 = [ada[:, i * d:(i + 1) * d] for i in range(6)]

    half = ROPE // 2
    ang = pos.astype(F32)[:, None] * (ROPE_THETA ** (-jnp.arange(half, dtype=F32) / half))
    zeros = jnp.zeros((s, LANES - ROPE), F32)
    cos_t = jnp.concatenate([jnp.cos(ang), jnp.cos(ang), zeros], axis=1)
    sin_t = jnp.concatenate([-jnp.sin(ang), jnp.sin(ang), zeros], axis=1)

    h1 = _norm_mod(xs, wts["g_norm1"], sc1, sh1)
    mixer_w = ("w_uq", "w_ukv", "w_proj_mla", "w_proj_sb", "w_out")
    proj, slots = _mm(h1, w_in_k, name="mm_in", carry=(pack_weights(mixer_w), True))
    unpack_weights(slots, mixer_w)
    w_uq_k = jnp.pad(full["w_uq"].reshape(lat, h_mla, QK_DIM), ((0, 0), (0, 0), (0, QK_PAD - QK_DIM))
                     ).reshape(lat, h_mla * QK_PAD)
    cqn, ckvn = _latent_norm(proj, off["c_q"], off["c_kv"], lat, wts["g_q_latent"], wts["g_kv_latent"])
    q_raw = _mm(cqn, w_uq_k, name="mm_uq")
    kv = _mm(ckvn, full["w_ukv"], name="mm_ukv")
    q, k, v = _qk_prep(q_raw, kv, proj, off["k_pe"], cos_t, sin_t, g_qh, g_kh)
    y_a, lse, slots = _mla_fwd(q, k, v, carry=(pack_weights(("w_ffn_out",)), True))
    unpack_weights(slots, ("w_ffn_out",))
    y_b, slots = _sb_fwd(proj, off["q_sb"], off["k_sb"], off["v_sb"], h_sb, carry=(pack_weights(("w_ffn_in",)), True))
    unpack_weights(slots, ("w_ffn_in",))
    ya_p = _mm(y_a, full["w_proj_mla"], name="mm_proj_mla")
    yb_p = _mm(y_b, full["w_proj_sb"], name="mm_proj_sb")
    merged = _merge(proj, off["gl_a"], off["gl_b"], ya_p, yb_p)
    o1 = _mm(merged, full["w_out"], name="mm_out")
    x1, h2 = _resid_norm_mod(xs, o1, gt1, wts["g_norm2"], sc2, sh2)
    w_fi_k = _interleave(full["w_ffn_in"], d_ff)
    gu = _mm(h2, w_fi_k, out_dtype=BF16, name="mm_ffn_in")
    act = _swiglu(gu)
    o2 = _mm(act, full["w_ffn_out"], name="mm_ffn_out")
    dy, d_o2, sums_l = _loss_head(x1, o2, gt2, tgt)

    recv = {}
    d_act = _mm(d_o2, full["w_ffn_out"], tb=True, name="mm_d_act")
    grads["w_ffn_out"] = _mm(act, d_o2, ta=True, name="mm_g_ffn_out")
    d_gu = _swiglu_bwd(gu, d_act)
    grads["w_ffn_in"] = _deinterleave(_mm(h2, d_gu, ta=True, name="mm_g_ffn_in"), d_ff)
    d_h2, recv["w_ffn_out",] = _mm(d_gu, w_fi_k, tb=True, name="mm_d_h2",
                                   carry=(pack_grads(("w_ffn_out",)), False))
    d_x1, d_o1, sums_2 = _norm_mod_bwd(d_h2, x1, wts["g_norm2"], sc2, dy, gate=(o1, gt1))
    grads["w_out"] = _mm(merged, d_o1, ta=True, name="mm_g_out")
    d_merged = _mm(d_o1, full["w_out"], tb=True, name="mm_d_merged")
    d_yap, d_ybp, d_gla, d_glb = _merge_bwd(proj, off["gl_a"], off["gl_b"], ya_p, yb_p, d_merged)
    grads["w_proj_mla"] = _mm(y_a, d_yap, ta=True, name="mm_g_proj_mla")
    grads["w_proj_sb"] = _mm(y_b, d_ybp, ta=True, name="mm_g_proj_sb")
    d_ya = _mm(d_yap, full["w_proj_mla"], tb=True, name="mm_d_ya")
    d_yb = _mm(d_ybp, full["w_proj_sb"], tb=True, name="mm_d_yb")
    dq_sb, dk_sb, dv_sb, recv["w_ffn_in",] = _sb_bwd(proj, off["q_sb"], off["k_sb"], off["v_sb"], h_sb, d_yb,
                                                     carry=(pack_grads(("w_ffn_in",)), False))
    merge_w = ("w_out", "w_proj_mla", "w_proj_sb")
    dq, dk, dv, recv[merge_w] = _mla_bwd(q, k, v, y_a, lse, d_ya, carry=(pack_grads(merge_w), False))
    d_qraw, d_kv, d_kpe, sums_h = _qk_prep_bwd(q_raw, kv, proj, off["k_pe"], cos_t, sin_t, g_qh, g_kh, dq, dk, dv)
    g_uq_k = _mm(cqn, d_qraw, ta=True, name="mm_g_uq")
    grads["w_uq"] = g_uq_k.reshape(lat, h_mla, QK_PAD)[:, :, :QK_DIM].reshape(lat, h_mla * QK_DIM)
    grads["w_ukv"] = _mm(ckvn, d_kv, ta=True, name="mm_g_ukv")
    d_cqn = _mm(d_qraw, w_uq_k, tb=True, name="mm_d_cqn")
    d_ckvn = _mm(d_kv, full["w_ukv"], tb=True, name="mm_d_ckvn")
    d_cq, d_ckv, sums_lat = _latent_norm_bwd(proj, off["c_q"], off["c_kv"], lat, wts["g_q_latent"],
                                             wts["g_kv_latent"], d_cqn, d_ckvn)
    d_parts = {"gl_a": d_gla, "gl_b": d_glb, "q_sb": dq_sb, "k_sb": dk_sb, "v_sb": dv_sb, "c_q": d_cq, "c_kv": d_ckv,
               "k_pe": d_kpe}
    d_proj = jnp.concatenate([d_parts[n_].astype(BF16) for n_ in order]
                             + ([jnp.zeros((s, proj_w - used_w), BF16)] if proj_w > used_w else []), axis=1)
    g_in_k = _mm(h1, d_proj, ta=True, name="mm_g_in")
    grads["w_in"] = jnp.concatenate([g_in_k[:, off[n_]:off[n_] + w_] for n_, w_ in ref_w], axis=1)
    latent_w = ("w_uq", "w_ukv", "w_in")
    d_h1, recv[latent_w] = _mm(d_proj, w_in_k, tb=True, name="mm_d_h1", carry=(pack_grads(latent_w), False))
    grad_x, sums_1 = _norm_mod_bwd(d_h1, xs, wts["g_norm1"], sc1, d_x1)

    parts = [sums_1[0:1], sums_1[1:2], sums_2[3:4], sums_2[0:1], sums_2[1:2], sums_l[0:1],
             sums_1[2:3], sums_2[2:3], sums_lat[0:1], sums_lat[1:2], sums_h[0:1], sums_h[1:2], sums_l[1:2]]
    part_off, o = [], 0
    for p in parts:
        part_off.append(o)
        o += p.shape[1]
    all_rows = _gather_rows(jnp.concatenate(parts, axis=1), "gather_small")
    summed, loss_row = _small_sum(all_rows, part_off[12], part_off[12] + d, d)
    take = lambda i, w: summed[:, part_off[i]:part_off[i] + w]
    grads["b_ada"] = summed[:, :6 * d]
    grads["g_norm1"], grads["g_norm2"] = take(6, d), take(7, d)
    grads["g_q_latent"], grads["g_kv_latent"] = take(8, lat), take(9, lat)
    grads["g_q_head"], grads["g_k_head"] = take(10, QK_DIM), take(11, QK_DIM)
    n_ada = 6 * d // N_DEV
    d_ada_mine = lax.dynamic_slice_in_dim(all_rows[:, :6 * d], me * n_ada, n_ada, axis=1)
    grads["w_ada"] = _ada_grad(c_all.T, d_ada_mine)

    for names, slots in recv.items():
        unpack_grads(slots, names)

    delta, new_m, new_v = {}, {}, {}
    for n in ("w_ada",) + tuple(n for n, _ in BIG):
        delta[n], new_m[n], new_v[n] = _adamw(wts[n], grads[n], mom[n], var[n], "adamw_" + n)
    cat = lambda t: jnp.concatenate([t[n] for n in SMALL], axis=1)
    d_s, m_s, v_s = _adamw(cat(wts), cat(grads), cat(mom), cat(var), "adamw_small")
    o = 0
    for n in SMALL:
        w_ = wts[n].shape[1]
        delta[n], new_m[n], new_v[n] = d_s[:, o:o + w_], m_s[:, o:o + w_], v_s[:, o:o + w_]
        o += w_

    lead = lambda t: [t[n].reshape(env[n].shape) for n in WEIGHT_NAMES]
    return (loss_row[0, 0], grad_x[None], *lead(grads), *lead(delta), *lead(new_m), *lead(new_v))
```

```python
import jax
import jax.numpy as jnp
from jax import lax
from jax.experimental import pallas as pl
from jax.experimental.pallas import tpu as pltpu

F32 = jnp.float32
BF16 = jnp.bfloat16

N_DEV = 8
LANES = 128
PACK_W = 1024
VMEM_LIMIT = 48 * 1024 * 1024

EPS = 1e-6
ROPE_THETA = 10000.0
NOPE = 128
ROPE = 64
QK_DIM = NOPE + ROPE
QK_PAD = 2 * LANES
V_DIM = 128
SB_DIM = 128
ATT_Q = 512
ATT_K = 256
MM_TK = 2816

ADAM_LR = 0.001
ADAM_B1 = 0.9
ADAM_B2 = 0.999
ADAM_EPS = 1e-08
ADAM_WD = 0.01
ADAM_STEP = 10

WEIGHT_NAMES = ("w_ada", "b_ada", "g_norm1", "g_norm2", "w_in", "g_q_latent", "g_kv_latent", "w_uq", "w_ukv",
                "g_q_head", "g_k_head", "w_proj_mla", "w_proj_sb", "w_out", "w_ffn_in", "w_ffn_out")
BIG = (("w_in", True), ("w_uq", True), ("w_ukv", True), ("w_proj_mla", True), ("w_proj_sb", True),
       ("w_out", False), ("w_ffn_in", True), ("w_ffn_out", False))
SMALL = ("b_ada", "g_norm1", "g_norm2", "g_q_latent", "g_kv_latent", "g_q_head", "g_k_head")


def _tile(n, pref):
    if n <= pref:
        return n
    for step in (LANES, 8):
        for t in range(pref - pref % step, 0, -step):
            if n % t == 0:
                return t
    return n


def _params():
    return pltpu.CompilerParams(vmem_limit_bytes=VMEM_LIMIT)


def _sigmoid(x):
    return 1.0 / (1.0 + jnp.exp(-x))


def _dot(a, b):
    return lax.dot_general(a, b, (((1,), (0,)), ((), ())), preferred_element_type=F32)


def _dot_nt(a, b):
    return lax.dot_general(a, b, (((1,), (1,)), ((), ())), preferred_element_type=F32)


def _dot_tn(a, b):
    return lax.dot_general(a, b, (((0,), (0,)), ((), ())), preferred_element_type=F32)


def _split_dot(x, tri):
    hi = x.astype(BF16)
    lo = (x - hi.astype(F32)).astype(BF16)
    return _dot(hi, tri) + _dot(lo, tri)


_HBM = pl.BlockSpec(memory_space=pltpu.HBM)


def _carry_parts(carry):
    src, gather = carry
    slot_shape = tuple(src.shape) if gather else tuple(src.shape[1:])
    return (src, _HBM, jax.ShapeDtypeStruct((N_DEV,) + slot_shape, src.dtype), _HBM,
            [pltpu.SemaphoreType.DMA((N_DEV - 1,)), pltpu.SemaphoreType.DMA((N_DEV - 1,)), pltpu.SemaphoreType.DMA(())])


def _exchange_copies(src_ref, dst_ref, send_sems, recv_sems, local_sem, gather):
    x, y, c = lax.axis_index("x"), lax.axis_index("y"), lax.axis_index("c")
    me = 4 * x + 2 * y + c

    def slot_for(idx):
        return src_ref if gather else src_ref.at[idx]

    copies = [pltpu.make_async_copy(slot_for(me), dst_ref.at[me], local_sem)]
    for k in range(1, N_DEV):
        peer = ((1 - x) if (k >> 2) & 1 else x, (1 - y) if (k >> 1) & 1 else y, (1 - c) if k & 1 else c)
        peer_idx = 4 * peer[0] + 2 * peer[1] + peer[2]
        copies.append(pltpu.make_async_remote_copy(
            src_ref=slot_for(peer_idx), dst_ref=dst_ref.at[me],
            send_sem=send_sems.at[k - 1], recv_sem=recv_sems.at[k - 1],
            device_id=peer, device_id_type=pl.DeviceIdType.MESH))
    return copies


def _carried_exchange(refs, gather, first, last):
    @pl.when(first)
    def _():
        for cp in _exchange_copies(*refs, gather):
            cp.start()

    @pl.when(last)
    def _():
        for cp in _exchange_copies(*refs, gather):
            cp.wait()


def _exchange(src, *, gather, name):
    operand, in_spec, out_shape, out_spec, scratch = _carry_parts((src, gather))

    def body(src_ref, dst_ref, send_sems, recv_sems, local_sem):
        copies = _exchange_copies(src_ref, dst_ref, send_sems, recv_sems, local_sem, gather)
        for cp in copies:
            cp.start()
        for cp in copies:
            cp.wait()

    return pl.pallas_call(body, name=name, out_shape=out_shape, in_specs=[in_spec], out_specs=out_spec,
                          scratch_shapes=scratch)(operand)


def _gather_two_level(src, name):
    operand, in_spec, out_shape, out_spec, scratch = _carry_parts((src, True))

    def body(src_ref, dst_ref, send_sems, recv_sems, local_sem):
        x, y, c = lax.axis_index("x"), lax.axis_index("y"), lax.axis_index("c")
        me, sibling = (x, y, c), (x, y, 1 - c)
        chips = [(1 - x, y), (x, 1 - y), (1 - x, 1 - y)]

        def slot(px, py, pc):
            return dst_ref.at[4 * px + 2 * py + pc]

        def copy(k, block, to, own=False):
            return pltpu.make_async_remote_copy(
                src_ref=src_ref if own else slot(*block), dst_ref=slot(*block),
                send_sem=send_sems.at[k], recv_sem=recv_sems.at[k],
                device_id=to, device_id_type=pl.DeviceIdType.MESH)

        mine = pltpu.make_async_copy(src_ref, slot(*me), local_sem)
        mine.start()
        first = [copy(0, me, sibling, own=True)] + [copy(1 + j, me, (*chip, c), own=True) for j, chip in enumerate(chips)]
        for cp in first:
            cp.start()
        passed = [copy(4 + j, (*chip, c), sibling) for j, chip in enumerate(chips)]
        for j, chip in enumerate(chips):
            copy(1 + j, (*chip, c), me).wait_recv()
            passed[j].start()
        copy(0, sibling, me).wait_recv()
        for j, chip in enumerate(chips):
            copy(4 + j, (*chip, 1 - c), me).wait_recv()
        for cp in first + passed:
            cp.wait_send()
        mine.wait()

    return pl.pallas_call(body, name=name, out_shape=out_shape, in_specs=[in_spec], out_specs=out_spec,
                          scratch_shapes=scratch)(operand)


def _gather_rows(v, name):
    n = v.shape[1]
    padded = -(-n // (8 * LANES)) * (8 * LANES)
    tiles = jnp.pad(v, ((0, 0), (0, padded - n))).reshape(padded // LANES, LANES)
    return _exchange(tiles, gather=True, name=name).reshape(N_DEV, padded)[:, :n]


def _split_refs(refs, n_in, n_out, carry):
    if carry is None:
        return refs[:n_in], refs[n_in:n_in + n_out], refs[n_in + n_out:], None
    ins, src_ref = refs[:n_in], refs[n_in]
    outs, dst_ref = refs[n_in + 1:n_in + 1 + n_out], refs[n_in + 1 + n_out]
    rest = refs[n_in + n_out + 2:]
    return ins, outs, rest[:-3], (src_ref, dst_ref) + tuple(rest[-3:])


def _with_carry(carry, args, in_specs, out_shape, out_specs, scratch):
    if carry is not None:
        operand, c_in, c_shape, c_out, c_scratch = _carry_parts(carry)
        args.append(operand)
        in_specs.append(c_in)
        out_shape.append(c_shape)
        out_specs.append(c_out)
        scratch.extend(c_scratch)


def _slot_of_chunk(j):
    return j // 2 + (N_DEV // 2) * (j % 2)


def _mm(a, b, *, ta=False, tb=False, out_dtype=F32, name, carry=None, shards=None):
    kdim, m = a.shape if ta else a.shape[::-1]
    if shards == "b":
        chunk = b.shape[2]
        n, kdim_b = (b.shape[1], N_DEV * chunk) if tb else (N_DEV * chunk, b.shape[1])
    else:
        n, kdim_b = b.shape if tb else b.shape[::-1]
    assert kdim == kdim_b, (a.shape, b.shape, ta, tb)
    tm, tn, tk = _tile(m, 1024), _tile(n, 1024), _tile(kdim, MM_TK)
    if shards == "b":
        tn, tk = (tn, chunk) if tb else (chunk, tk)
    elif shards == "out":
        chunk = tn = n // N_DEV
    grid = (m // tm, n // tn, kdim // tk)
    nk = grid[2]
    dims = (((0 if ta else 1,), (1 if tb else 0,)), ((), ()))

    def body(*refs):
        (a_ref, b_ref), (o_ref,), scratch_refs, xrefs = _split_refs(refs, 2, 1, carry)
        i, j, k = pl.program_id(0), pl.program_id(1), pl.program_id(2)
        if carry is not None:
            _carried_exchange(xrefs, carry[1], (i == 0) & (j == 0) & (k == 0),
                              (i == grid[0] - 1) & (j == grid[1] - 1) & (k == nk - 1))
        prod = lax.dot_general(a_ref[...].astype(BF16), b_ref[...].astype(BF16), dims, preferred_element_type=F32)
        if nk == 1:
            o_ref[...] = prod.astype(o_ref.dtype)
        else:
            acc_ref = scratch_refs[0]

            @pl.when(k == 0)
            def _():
                acc_ref[...] = prod

            @pl.when(k > 0)
            def _():
                acc_ref[...] += prod

            @pl.when(k == nk - 1)
            def _():
                o_ref[...] = acc_ref[...].astype(o_ref.dtype)

    a_spec = (pl.BlockSpec((tk, tm), lambda i, j, k: (k, i)) if ta else pl.BlockSpec((tm, tk), lambda i, j, k: (i, k)))
    if shards == "b":
        b_spec = (pl.BlockSpec((None, tn, tk), lambda i, j, k: (_slot_of_chunk(k), j, 0)) if tb
                  else pl.BlockSpec((None, tk, tn), lambda i, j, k: (_slot_of_chunk(j), k, 0)))
    else:
        b_spec = (pl.BlockSpec((tn, tk), lambda i, j, k: (j, k)) if tb else pl.BlockSpec((tk, tn), lambda i, j, k: (k, j)))
    args, in_specs = [a, b], [a_spec, b_spec]
    if shards == "out":
        out_shape = [jax.ShapeDtypeStruct((N_DEV, m, tn), out_dtype)]
        out_specs = [pl.BlockSpec((None, tm, tn), lambda i, j, k: (_slot_of_chunk(j), i, 0))]
    else:
        out_shape, out_specs = [jax.ShapeDtypeStruct((m, n), out_dtype)], [pl.BlockSpec((tm, tn), lambda i, j, k: (i, j))]
    scratch = [] if nk == 1 else [pltpu.VMEM((tm, tn), F32)]
    _with_carry(carry, args, in_specs, out_shape, out_specs, scratch)
    out = pl.pallas_call(
        body, name=name, grid=grid, out_shape=tuple(out_shape), in_specs=in_specs, out_specs=tuple(out_specs),
        scratch_shapes=scratch, compiler_params=_params(),
    )(*args)
    return out if carry is not None else out[0]


def _ada_fwd(c_all, w_shard):
    d, n = w_shard.shape
    tn = _tile(n, 512)

    def body(c_ref, w_ref, o_ref):
        cv = c_ref[...]
        o_ref[...] = jnp.dot(cv * _sigmoid(cv), w_ref[...], precision=lax.Precision.HIGHEST,
                             preferred_element_type=F32)

    return pl.pallas_call(
        body, name="ada_fwd", grid=(n // tn,),
        out_shape=jax.ShapeDtypeStruct((N_DEV, n), F32),
        in_specs=[pl.BlockSpec((N_DEV, d), lambda j: (0, 0)), pl.BlockSpec((d, tn), lambda j: (0, j))],
        out_specs=pl.BlockSpec((N_DEV, tn), lambda j: (0, j)),
        compiler_params=_params(),
    )(c_all, w_shard)


def _ada_grad(c_all_t, d_rows):
    d, n = c_all_t.shape[0], d_rows.shape[1]
    tn = _tile(n, 512)

    def body(ct_ref, d_ref, o_ref):
        cv = ct_ref[...]
        s = cv * _sigmoid(cv)
        dv = d_ref[...]
        acc = s[:, 0:1] * dv[0:1, :]
        for b in range(1, N_DEV):
            acc = acc + s[:, b:b + 1] * dv[b:b + 1, :]
        o_ref[...] = acc

    return pl.pallas_call(
        body, name="ada_grad", grid=(n // tn,),
        out_shape=jax.ShapeDtypeStruct((d, n), F32),
        in_specs=[pl.BlockSpec((d, N_DEV), lambda j: (0, 0)), pl.BlockSpec((N_DEV, tn), lambda j: (0, j))],
        out_specs=pl.BlockSpec((d, tn), lambda j: (0, j)),
        compiler_params=_params(),
    )(c_all_t, d_rows)


def _row(ts, w, col=0):
    return pl.BlockSpec((ts, w), lambda i, col=col: (i, col))


def _vec(w):
    return pl.BlockSpec((1, w), lambda i: (0, 0))


def _norm_mod(x, g, sc, sh):
    s, d = x.shape
    ts = _tile(s, 512)

    def body(x_ref, g_ref, sc_ref, sh_ref, h_ref):
        xv = x_ref[...]
        r = lax.rsqrt(jnp.mean(xv * xv, axis=-1, keepdims=True) + EPS)
        h_ref[...] = ((xv * r) * g_ref[...] * (1.0 + sc_ref[...]) + sh_ref[...]).astype(BF16)

    return pl.pallas_call(
        body, name="norm_mod", grid=(s // ts,),
        out_shape=jax.ShapeDtypeStruct((s, d), BF16),
        in_specs=[_row(ts, d), _vec(d), _vec(d), _vec(d)],
        out_specs=_row(ts, d), compiler_params=_params(),
    )(x, g, sc, sh)


def _latent_norm(proj, off_q, off_kv, lat, g_q, g_kv):
    s = proj.shape[0]
    ts = _tile(s, 512)

    def body(cq_ref, ckv_ref, gq_ref, gkv_ref, oq_ref, okv_ref):
        for c_ref, g_ref, o_ref in ((cq_ref, gq_ref, oq_ref), (ckv_ref, gkv_ref, okv_ref)):
            v = c_ref[...]
            r = lax.rsqrt(jnp.mean(v * v, axis=-1, keepdims=True) + EPS)
            o_ref[...] = ((v * r) * g_ref[...]).astype(BF16)

    return pl.pallas_call(
        body, name="latent_norm", grid=(s // ts,),
        out_shape=(jax.ShapeDtypeStruct((s, lat), BF16), jax.ShapeDtypeStruct((s, lat), BF16)),
        in_specs=[_row(ts, lat, off_q // lat), _row(ts, lat, off_kv // lat), _vec(lat), _vec(lat)],
        out_specs=(_row(ts, lat), _row(ts, lat)), compiler_params=_params(),
    )(proj, proj, g_q, g_kv)


def _latent_norm_bwd(proj, off_q, off_kv, lat, g_q, g_kv, d_cqn, d_ckvn):
    s = proj.shape[0]
    ts = _tile(s, 512)

    def body(cq_ref, ckv_ref, gq_ref, gkv_ref, dq_ref, dkv_ref, oq_ref, okv_ref, sums_ref):
        @pl.when(pl.program_id(0) == 0)
        def _():
            sums_ref[...] = jnp.zeros_like(sums_ref)

        for row, (c_ref, g_ref, d_ref, o_ref) in enumerate(((cq_ref, gq_ref, dq_ref, oq_ref),
                                                             (ckv_ref, gkv_ref, dkv_ref, okv_ref))):
            v = c_ref[...]
            r = lax.rsqrt(jnp.mean(v * v, axis=-1, keepdims=True) + EPS)
            vn = v * r
            dn = d_ref[...]
            sums_ref[row:row + 1, :] += jnp.sum(dn * vn, axis=0, keepdims=True)
            dvn = dn * g_ref[...]
            o_ref[...] = (r * (dvn - vn * jnp.mean(dvn * vn, axis=-1, keepdims=True))).astype(BF16)

    return pl.pallas_call(
        body, name="latent_norm_bwd", grid=(s // ts,),
        out_shape=(jax.ShapeDtypeStruct((s, lat), BF16), jax.ShapeDtypeStruct((s, lat), BF16),
                   jax.ShapeDtypeStruct((8, lat), F32)),
        in_specs=[_row(ts, lat, off_q // lat), _row(ts, lat, off_kv // lat), _vec(lat), _vec(lat),
                  _row(ts, lat), _row(ts, lat)],
        out_specs=(_row(ts, lat), _row(ts, lat), pl.BlockSpec((8, lat), lambda i: (0, 0))),
        compiler_params=_params(),
    )(proj, proj, g_q, g_kv, d_cqn, d_ckvn)


def _merge(proj, off_a, off_b, ya_p, yb_p):
    s, d = ya_p.shape
    ts = _tile(s, 256)

    def body(ga_ref, gb_ref, ya_ref, yb_ref, o_ref):
        o_ref[...] = (_sigmoid(ga_ref[...]) * ya_ref[...] + _sigmoid(gb_ref[...]) * yb_ref[...]).astype(BF16)

    return pl.pallas_call(
        body, name="merge", grid=(s // ts,),
        out_shape=jax.ShapeDtypeStruct((s, d), BF16),
        in_specs=[_row(ts, d, off_a // d), _row(ts, d, off_b // d), _row(ts, d), _row(ts, d)],
        out_specs=_row(ts, d), compiler_params=_params(),
    )(proj, proj, ya_p, yb_p)


def _merge_bwd(proj, off_a, off_b, ya_p, yb_p, d_merged):
    s, d = ya_p.shape
    ts = _tile(s, 256)

    def body(ga_ref, gb_ref, ya_ref, yb_ref, dm_ref, dya_ref, dyb_ref, dga_ref, dgb_ref):
        dm = dm_ref[...]
        for g_ref, y_ref, dy_ref, dg_ref in ((ga_ref, ya_ref, dya_ref, dga_ref), (gb_ref, yb_ref, dyb_ref, dgb_ref)):
            sg = _sigmoid(g_ref[...])
            dy_ref[...] = (dm * sg).astype(BF16)
            dg_ref[...] = (dm * y_ref[...] * sg * (1.0 - sg)).astype(BF16)

    sd = jax.ShapeDtypeStruct((s, d), BF16)
    return pl.pallas_call(
        body, name="merge_bwd", grid=(s // ts,),
        out_shape=(sd, sd, sd, sd),
        in_specs=[_row(ts, d, off_a // d), _row(ts, d, off_b // d), _row(ts, d), _row(ts, d), _row(ts, d)],
        out_specs=(_row(ts, d),) * 4, compiler_params=_params(),
    )(proj, proj, ya_p, yb_p, d_merged)


def _resid_norm_mod(x, o, gt, g, sc, sh):
    s, d = x.shape
    ts = _tile(s, 256)

    def body(x_ref, o_ref, gt_ref, g_ref, sc_ref, sh_ref, x1_ref, h_ref):
        x1 = x_ref[...] + gt_ref[...] * o_ref[...]
        x1_ref[...] = x1
        r = lax.rsqrt(jnp.mean(x1 * x1, axis=-1, keepdims=True) + EPS)
        h_ref[...] = ((x1 * r) * g_ref[...] * (1.0 + sc_ref[...]) + sh_ref[...]).astype(BF16)

    return pl.pallas_call(
        body, name="resid_norm_mod", grid=(s // ts,),
        out_shape=(jax.ShapeDtypeStruct((s, d), F32), jax.ShapeDtypeStruct((s, d), BF16)),
        in_specs=[_row(ts, d), _row(ts, d), _vec(d), _vec(d), _vec(d), _vec(d)],
        out_specs=(_row(ts, d), _row(ts, d)), compiler_params=_params(),
    )(x, o, gt, g, sc, sh)


def _loss_head(x1, o2, gt2, target):
    s, d = x1.shape
    ts = _tile(s, 256)

    def body(x1_ref, o2_ref, gt_ref, t_ref, dy_ref, do_ref, sums_ref):
        @pl.when(pl.program_id(0) == 0)
        def _():
            sums_ref[...] = jnp.zeros_like(sums_ref)

        o2 = o2_ref[...]
        e = x1_ref[...] + gt_ref[...] * o2 - t_ref[...]
        dy = e / d
        dy_ref[...] = dy
        do_ref[...] = (dy * gt_ref[...]).astype(BF16)
        sums_ref[0:1, :] += jnp.sum(dy * o2, axis=0, keepdims=True)
        sums_ref[1:2, :] += jnp.sum(e * e, axis=0, keepdims=True)

    return pl.pallas_call(
        body, name="loss_head", grid=(s // ts,),
        out_shape=(jax.ShapeDtypeStruct((s, d), F32), jax.ShapeDtypeStruct((s, d), BF16),
                   jax.ShapeDtypeStruct((8, d), F32)),
        in_specs=[_row(ts, d), _row(ts, d), _vec(d), _row(ts, d)],
        out_specs=(_row(ts, d), _row(ts, d), pl.BlockSpec((8, d), lambda i: (0, 0))),
        compiler_params=_params(),
    )(x1, o2, gt2, target)


def _norm_mod_bwd(dh, xin, g, sc, dres, gate=None):
    s, d = xin.shape
    ts = _tile(s, 256)
    gated = gate is not None

    def body(*refs):
        if gated:
            dh_ref, x_ref, g_ref, sc_ref, dr_ref, o_ref, gt_ref, dx_ref, do_ref, sums_ref = refs
        else:
            dh_ref, x_ref, g_ref, sc_ref, dr_ref, dx_ref, sums_ref = refs

        @pl.when(pl.program_id(0) == 0)
        def _():
            sums_ref[...] = jnp.zeros_like(sums_ref)

        xv, dhv = x_ref[...], dh_ref[...]
        r = lax.rsqrt(jnp.mean(xv * xv, axis=-1, keepdims=True) + EPS)
        xn = xv * r
        one_sc = 1.0 + sc_ref[...]
        sums_ref[0:1, :] += jnp.sum(dhv, axis=0, keepdims=True)
        sums_ref[1:2, :] += jnp.sum(dhv * (xn * g_ref[...]), axis=0, keepdims=True)
        sums_ref[2:3, :] += jnp.sum(dhv * one_sc * xn, axis=0, keepdims=True)
        dxn = dhv * one_sc * g_ref[...]
        dx = dr_ref[...] + r * (dxn - xn * jnp.mean(dxn * xn, axis=-1, keepdims=True))
        dx_ref[...] = dx
        if gated:
            sums_ref[3:4, :] += jnp.sum(dx * o_ref[...], axis=0, keepdims=True)
            do_ref[...] = (dx * gt_ref[...]).astype(BF16)

    in_specs = [_row(ts, d), _row(ts, d), _vec(d), _vec(d), _row(ts, d)]
    args = [dh, xin, g, sc, dres]
    out_shape = [jax.ShapeDtypeStruct((s, d), F32)]
    out_specs = [_row(ts, d)]
    if gated:
        in_specs += [_row(ts, d), _vec(d)]
        args += list(gate)
        out_shape.append(jax.ShapeDtypeStruct((s, d), BF16))
        out_specs.append(_row(ts, d))
    out_shape.append(jax.ShapeDtypeStruct((8, d), F32))
    out_specs.append(pl.BlockSpec((8, d), lambda i: (0, 0)))
    return pl.pallas_call(
        body, name="norm_mod_bwd_gated" if gated else "norm_mod_bwd", grid=(s // ts,),
        out_shape=tuple(out_shape), in_specs=in_specs, out_specs=tuple(out_specs), compiler_params=_params(),
    )(*args)


def _ffn_tile(f):
    return 2 * f // N_DEV


def _swiglu(gu):
    s, f2 = gu.shape
    f = f2 // 2
    ts, tc = _tile(s, 512), _ffn_tile(f)

    def body(gu_ref, o_ref):
        gv, uv = gu_ref[:, :tc].astype(F32), gu_ref[:, tc:].astype(F32)
        o_ref[...] = (gv * _sigmoid(gv) * uv).astype(BF16)

    return pl.pallas_call(
        body, name="swiglu", grid=(s // ts, f // tc),
        out_shape=jax.ShapeDtypeStruct((s, f), BF16),
        in_specs=[pl.BlockSpec((ts, 2 * tc), lambda i, j: (i, j))],
        out_specs=pl.BlockSpec((ts, tc), lambda i, j: (i, j)), compiler_params=_params(),
    )(gu)


def _swiglu_bwd(gu, d_act):
    s, f2 = gu.shape
    f = f2 // 2
    ts, tc = _tile(s, 512), _ffn_tile(f)

    def body(gu_ref, da_ref, o_ref):
        gv, uv, da = gu_ref[:, :tc].astype(F32), gu_ref[:, tc:].astype(F32), da_ref[...]
        sg = _sigmoid(gv)
        o_ref[:, :tc] = (da * uv * (sg * (1.0 + gv * (1.0 - sg)))).astype(BF16)
        o_ref[:, tc:] = (da * (gv * sg)).astype(BF16)

    return pl.pallas_call(
        body, name="swiglu_bwd", grid=(s // ts, f // tc),
        out_shape=jax.ShapeDtypeStruct((s, f2), BF16),
        in_specs=[pl.BlockSpec((ts, 2 * tc), lambda i, j: (i, j)), pl.BlockSpec((ts, tc), lambda i, j: (i, j))],
        out_specs=pl.BlockSpec((ts, 2 * tc), lambda i, j: (i, j)), compiler_params=_params(),
    )(gu, d_act)


def _swap_halves(t):
    return pltpu.roll(t, ROPE // 2, 1) + pltpu.roll(t, LANES - ROPE // 2, 1)


def _head_norm(raw):
    r = lax.rsqrt(jnp.sum(raw * raw, axis=-1, keepdims=True) / QK_DIM + EPS)
    return raw * r, r


def _rope_fwd(v, cos, sin):
    rope_tile = v[:, NOPE:]
    return jnp.concatenate([v[:, :NOPE], rope_tile * cos + _swap_halves(rope_tile) * sin], axis=1)


def _rope_bwd(d, cos, sin, lane_ok):
    d_tile = d[:, NOPE:]
    return jnp.concatenate([d[:, :NOPE], d_tile * cos + _swap_halves(d_tile * sin) * lane_ok], axis=1)


def _qk_prep(q_raw, kv, proj, off_pe, cos, sin, g_q, g_k):
    s, hw = q_raw.shape
    heads = hw // QK_PAD
    ts = _tile(s, 512)

    def body(q_ref, kv_ref, pe_ref, cos_ref, sin_ref, gq_ref, gk_ref, qo_ref, ko_ref, vo_ref):
        cos_v, sin_v = cos_ref[...], sin_ref[...]
        qn, _ = _head_norm(q_ref[...])
        qo_ref[...] = _rope_fwd(qn * gq_ref[...], cos_v, sin_v).astype(BF16)
        kvv = kv_ref[...]
        kn, _ = _head_norm(jnp.concatenate([kvv[:, :NOPE], pe_ref[...]], axis=1))
        ko_ref[...] = _rope_fwd(kn * gk_ref[...], cos_v, sin_v).astype(BF16)
        vo_ref[...] = kvv[:, NOPE:].astype(BF16)

    blk = lambda w: pl.BlockSpec((ts, w), lambda i, h: (i, h))
    fixed = lambda w, col=0: pl.BlockSpec((ts, w), lambda i, h, col=col: (i, col))
    vec = pl.BlockSpec((1, QK_PAD), lambda i, h: (0, 0))
    return pl.pallas_call(
        body, name="qk_prep", grid=(s // ts, heads),
        out_shape=(jax.ShapeDtypeStruct((s, hw), BF16), jax.ShapeDtypeStruct((s, hw), BF16),
                   jax.ShapeDtypeStruct((s, heads * V_DIM), BF16)),
        in_specs=[blk(QK_PAD), blk(QK_PAD), fixed(LANES, off_pe // LANES), fixed(LANES), fixed(LANES), vec, vec],
        out_specs=(blk(QK_PAD), blk(QK_PAD), blk(V_DIM)), compiler_params=_params(),
    )(q_raw, kv, proj, cos, sin, g_q, g_k)


def _qk_prep_bwd(q_raw, kv, proj, off_pe, cos, sin, g_q, g_k, dq, dk, dv):
    s, hw = q_raw.shape
    heads = hw // QK_PAD
    ts = _tile(s, 512)

    def body(q_ref, kv_ref, pe_ref, cos_ref, sin_ref, gq_ref, gk_ref, dq_ref, dk_ref, dv_ref,
             dqr_ref, dkv_ref, dpe_ref, sums_ref):
        i, h = pl.program_id(0), pl.program_id(1)

        @pl.when((i == 0) & (h == 0))
        def _():
            sums_ref[...] = jnp.zeros_like(sums_ref)

        cos_v, sin_v = cos_ref[...], sin_ref[...]
        lane_ok = (lax.broadcasted_iota(jnp.int32, (ts, LANES), 1) < ROPE).astype(F32)

        def one(raw, g, d_post, row):
            vn, r = _head_norm(raw)
            d_pre = _rope_bwd(d_post, cos_v, sin_v, lane_ok)
            sums_ref[row:row + 1, :] += jnp.sum(d_pre * vn, axis=0, keepdims=True)
            dvn = d_pre * g
            return r * (dvn - vn * (jnp.sum(dvn * vn, axis=-1, keepdims=True) / QK_DIM))

        dqr_ref[...] = one(q_ref[...], gq_ref[...], dq_ref[...], 0).astype(BF16)
        kvv = kv_ref[...]
        d_kraw = one(jnp.concatenate([kvv[:, :NOPE], pe_ref[...]], axis=1), gk_ref[...], dk_ref[...], 1)
        dkv_ref[...] = jnp.concatenate([d_kraw[:, :NOPE], dv_ref[...]], axis=1).astype(BF16)

        @pl.when(h == 0)
        def _():
            dpe_ref[...] = jnp.zeros_like(dpe_ref)

        dpe_ref[...] += d_kraw[:, NOPE:]

    blk = lambda w: pl.BlockSpec((ts, w), lambda i, h: (i, h))
    fixed = lambda w, col=0: pl.BlockSpec((ts, w), lambda i, h, col=col: (i, col))
    vec = pl.BlockSpec((1, QK_PAD), lambda i, h: (0, 0))
    return pl.pallas_call(
        body, name="qk_prep_bwd", grid=(s // ts, heads),
        out_shape=(jax.ShapeDtypeStruct((s, hw), BF16), jax.ShapeDtypeStruct((s, hw), BF16),
                   jax.ShapeDtypeStruct((s, LANES), F32), jax.ShapeDtypeStruct((8, QK_PAD), F32)),
        in_specs=[blk(QK_PAD), blk(QK_PAD), fixed(LANES, off_pe // LANES), fixed(LANES), fixed(LANES), vec, vec,
                  blk(QK_PAD), blk(QK_PAD), blk(V_DIM)],
        out_specs=(blk(QK_PAD), blk(QK_PAD), fixed(LANES), pl.BlockSpec((8, QK_PAD), lambda i, h: (0, 0))),
        compiler_params=_params(),
    )(q_raw, kv, proj, cos, sin, g_q, g_k, dq, dk, dv)


def _att_blocks(s):
    tq = _tile(s, ATT_Q)
    tk = _tile(tq, ATT_K)
    return tq, tk, tq // tk


def _visible(qi, kb, tq, tk, strict):
    row = qi * tq + lax.broadcasted_iota(jnp.int32, (tq, tk), 0)
    col = kb * tk + lax.broadcasted_iota(jnp.int32, (tq, tk), 1)
    return (col < row) if strict else (col <= row)


def _first_last(heads, nq):
    h, qi = pl.program_id(0), pl.program_id(1)
    return (h == 0) & (qi == 0), (h == heads - 1) & (qi == nq - 1)


def _mla_fwd(q, k, v, carry=None):
    s = q.shape[0]
    heads = q.shape[1] // QK_PAD
    tq, tk, ratio = _att_blocks(s)
    nq = s // tq
    scale = QK_DIM ** -0.5

    def body(*refs):
        (q_ref, k_ref, v_ref), (o_ref, lse_ref), (acc_ref,), xrefs = _split_refs(refs, 3, 2, carry)
        if carry is not None:
            _carried_exchange(xrefs, carry[1], *_first_last(heads, nq))
        qi = pl.program_id(1)
        qv = q_ref[...]
        acc_ref[...] = jnp.zeros_like(acc_ref)

        def step(kb, state, masked):
            m, l = state
            off = pl.multiple_of(kb * tk, tk)
            sc = _dot_nt(qv, k_ref[pl.ds(off, tk), :]) * scale
            if masked:
                sc = jnp.where(_visible(qi, kb, tq, tk, False), sc, -1e30)
            m_new = jnp.maximum(m, jnp.max(sc, axis=-1, keepdims=True))
            alpha = jnp.exp(m - m_new)
            p = jnp.exp(sc - m_new)
            acc_ref[...] = alpha * acc_ref[...] + _dot(p.astype(BF16), v_ref[pl.ds(off, tk), :])
            return m_new, alpha * l + jnp.sum(p, axis=-1, keepdims=True)

        state = (jnp.full((tq, 1), -1e30, F32), jnp.zeros((tq, 1), F32))
        state = lax.fori_loop(0, qi * ratio, lambda kb, st: step(kb, st, False), state)
        for i in range(ratio):
            state = step(qi * ratio + i, state, True)
        m, l = state
        o_ref[...] = (acc_ref[...] / l).astype(BF16)
        lse_ref[0] = m + jnp.log(l)

    args = [q, k, v]
    in_specs = [pl.BlockSpec((tq, QK_PAD), lambda h, i: (i, h)), pl.BlockSpec((s, QK_PAD), lambda h, i: (0, h)),
                pl.BlockSpec((s, V_DIM), lambda h, i: (0, h))]
    out_shape = [jax.ShapeDtypeStruct((s, heads * V_DIM), BF16), jax.ShapeDtypeStruct((heads, s, 1), F32)]
    out_specs = [pl.BlockSpec((tq, V_DIM), lambda h, i: (i, h)), pl.BlockSpec((1, tq, 1), lambda h, i: (h, i, 0))]
    scratch = [pltpu.VMEM((tq, V_DIM), F32)]
    _with_carry(carry, args, in_specs, out_shape, out_specs, scratch)
    return pl.pallas_call(
        body, name="mla_fwd", grid=(heads, nq), out_shape=tuple(out_shape), in_specs=in_specs,
        out_specs=tuple(out_specs), scratch_shapes=scratch, compiler_params=_params(),
    )(*args)


def _mla_bwd(q, k, v, o, lse, do, carry=None):
    s = q.shape[0]
    heads = q.shape[1] // QK_PAD
    tq, tk, ratio = _att_blocks(s)
    nq = s // tq
    scale = QK_DIM ** -0.5

    def body(*refs):
        (q_ref, k_ref, v_ref, o_ref, lse_ref, do_ref), (dq_ref, dk_ref, dv_ref), _, xrefs = _split_refs(refs, 6, 3, carry)
        if carry is not None:
            _carried_exchange(xrefs, carry[1], *_first_last(heads, nq))
        qi = pl.program_id(1)

        @pl.when(qi == 0)
        def _():
            dk_ref[...] = jnp.zeros_like(dk_ref)
            dv_ref[...] = jnp.zeros_like(dv_ref)

        qv = q_ref[...]
        dov = do_ref[...]
        do_b = dov.astype(BF16)
        delta = jnp.sum(dov * o_ref[...].astype(F32), axis=-1, keepdims=True)
        lse_v = lse_ref[0]

        dq_ref[...] = jnp.zeros_like(dq_ref)

        def step(kb, masked):
            off = pl.multiple_of(kb * tk, tk)
            kv_ = k_ref[pl.ds(off, tk), :]
            p = jnp.exp(_dot_nt(qv, kv_) * scale - lse_v)
            if masked:
                p = jnp.where(_visible(qi, kb, tq, tk, False), p, 0.0)
            dp = _dot_nt(do_b, v_ref[pl.ds(off, tk), :])
            ds = (p * (dp - delta) * scale).astype(BF16)
            dk_ref[pl.ds(off, tk), :] += _dot_tn(ds, qv)
            dv_ref[pl.ds(off, tk), :] += _dot_tn(p.astype(BF16), do_b)
            dq_ref[...] += _dot(ds, kv_)
            return 0

        lax.fori_loop(0, qi * ratio, lambda kb, _: step(kb, False), 0)
        for i in range(ratio):
            step(qi * ratio + i, True)

    args = [q, k, v, o, lse, do]
    in_specs = [pl.BlockSpec((tq, QK_PAD), lambda h, i: (i, h)), pl.BlockSpec((s, QK_PAD), lambda h, i: (0, h)),
                pl.BlockSpec((s, V_DIM), lambda h, i: (0, h)), pl.BlockSpec((tq, V_DIM), lambda h, i: (i, h)),
                pl.BlockSpec((1, tq, 1), lambda h, i: (h, i, 0)), pl.BlockSpec((tq, V_DIM), lambda h, i: (i, h))]
    out_shape = [jax.ShapeDtypeStruct(q.shape, F32), jax.ShapeDtypeStruct(k.shape, F32),
                 jax.ShapeDtypeStruct(v.shape, F32)]
    out_specs = [pl.BlockSpec((tq, QK_PAD), lambda h, i: (i, h)), pl.BlockSpec((s, QK_PAD), lambda h, i: (0, h)),
                 pl.BlockSpec((s, V_DIM), lambda h, i: (0, h))]
    scratch = []
    _with_carry(carry, args, in_specs, out_shape, out_specs, scratch)
    return pl.pallas_call(
        body, name="mla_bwd", grid=(heads, nq), out_shape=tuple(out_shape), in_specs=in_specs,
        out_specs=tuple(out_specs), scratch_shapes=scratch, compiler_params=_params(),
    )(*args)


def _sb_terms(qv, k_blk, scale, mask):
    z = _dot_nt(qv, k_blk) * scale
    log_beta = jnp.minimum(z, 0.0) - jnp.log(1.0 + jnp.exp(-jnp.abs(z)))
    log_rest = log_beta - z
    if mask is not None:
        log_rest = jnp.where(mask, log_rest, 0.0)
    return log_beta, log_rest


def _sb_specs(s, tq, off_q, off_k, off_v):
    cb = lambda off: off // SB_DIM
    return [pl.BlockSpec((tq, SB_DIM), lambda h, i: (i, cb(off_q) + h)),
            pl.BlockSpec((s, SB_DIM), lambda h, i: (0, cb(off_k) + h)),
            pl.BlockSpec((s, SB_DIM), lambda h, i: (0, cb(off_v) + h))]


def _sb_fwd(proj, off_q, off_k, off_v, heads, carry=None):
    s = proj.shape[0]
    tq, tk, ratio = _att_blocks(s)
    nq = s // tq
    scale = SB_DIM ** -0.5

    def body(*refs):
        (q_ref, k_ref, v_ref), (o_ref,), (acc_ref,), xrefs = _split_refs(refs, 3, 1, carry)
        if carry is not None:
            _carried_exchange(xrefs, carry[1], *_first_last(heads, nq))
        qi = pl.program_id(1)
        qv = q_ref[...].astype(BF16)
        after = (lax.broadcasted_iota(jnp.int32, (tk, tk), 0) > lax.broadcasted_iota(jnp.int32, (tk, tk), 1)).astype(BF16)
        acc_ref[...] = jnp.zeros_like(acc_ref)

        def step(kb, tail_right, masked):
            off = pl.multiple_of(kb * tk, tk)
            mask = _visible(qi, kb, tq, tk, True) if masked else None
            log_beta, log_rest = _sb_terms(qv, k_ref[pl.ds(off, tk), :].astype(BF16), scale, mask)
            tail = _split_dot(log_rest, after) + tail_right
            a = jnp.exp(log_beta + tail)
            if masked:
                a = jnp.where(mask, a, 0.0)
            acc_ref[...] += _dot(a.astype(BF16), v_ref[pl.ds(off, tk), :].astype(BF16))
            return tail_right + jnp.sum(log_rest, axis=-1, keepdims=True)

        tail_right = jnp.zeros((tq, 1), F32)
        for i in range(ratio):
            tail_right = step((qi + 1) * ratio - 1 - i, tail_right, True)
        lax.fori_loop(0, qi * ratio, lambda i, t_: step(qi * ratio - 1 - i, t_, False), tail_right)
        o_ref[...] = acc_ref[...].astype(BF16)

    args, in_specs = [proj, proj, proj], _sb_specs(s, tq, off_q, off_k, off_v)
    out_shape = [jax.ShapeDtypeStruct((s, heads * SB_DIM), BF16)]
    out_specs = [pl.BlockSpec((tq, SB_DIM), lambda h, i: (i, h))]
    scratch = [pltpu.VMEM((tq, SB_DIM), F32)]
    _with_carry(carry, args, in_specs, out_shape, out_specs, scratch)
    out = pl.pallas_call(
        body, name="sb_fwd", grid=(heads, nq), out_shape=tuple(out_shape), in_specs=in_specs,
        out_specs=tuple(out_specs), scratch_shapes=scratch, compiler_params=_params(),
    )(*args)
    return out if carry is not None else out[0]


def _sb_bwd(proj, off_q, off_k, off_v, heads, dy, carry=None):
    s = proj.shape[0]
    tq, tk, ratio = _att_blocks(s)
    nq = s // tq
    scale = SB_DIM ** -0.5

    def body(*refs):
        (q_ref, k_ref, v_ref, dy_ref), (dq_ref, dk_ref, dv_ref), (g_s, beta_s), xrefs = _split_refs(refs, 4, 3, carry)
        if carry is not None:
            _carried_exchange(xrefs, carry[1], *_first_last(heads, nq))
        qi = pl.program_id(1)

        @pl.when(qi == 0)
        def _():
            dk_ref[...] = jnp.zeros_like(dk_ref)
            dv_ref[...] = jnp.zeros_like(dv_ref)

        qv = q_ref[...].astype(BF16)
        dy_b = dy_ref[...].astype(BF16)
        rows = lax.broadcasted_iota(jnp.int32, (tk, tk), 0)
        cols = lax.broadcasted_iota(jnp.int32, (tk, tk), 1)
        after = (rows > cols).astype(BF16)
        before = (rows < cols).astype(BF16)

        def pass1(kb, tail_right, masked):
            off = pl.multiple_of(kb * tk, tk)
            mask = _visible(qi, kb, tq, tk, True) if masked else None
            log_beta, log_rest = _sb_terms(qv, k_ref[pl.ds(off, tk), :].astype(BF16), scale, mask)
            tail = _split_dot(log_rest, after) + tail_right
            a = jnp.exp(log_beta + tail)
            beta = jnp.exp(log_beta)
            if masked:
                a = jnp.where(mask, a, 0.0)
                beta = jnp.where(mask, beta, 0.0)
            da = _dot_nt(dy_b, v_ref[pl.ds(off, tk), :].astype(BF16))
            dv_ref[pl.ds(off, tk), :] += _dot_tn(a.astype(BF16), dy_b)
            g_s[kb] = a * da
            beta_s[kb] = beta
            return tail_right + jnp.sum(log_rest, axis=-1, keepdims=True)

        tail_right = jnp.zeros((tq, 1), F32)
        for i in range(ratio):
            tail_right = pass1((qi + 1) * ratio - 1 - i, tail_right, True)
        lax.fori_loop(0, qi * ratio, lambda i, t_: pass1(qi * ratio - 1 - i, t_, False), tail_right)

        dq_ref[...] = jnp.zeros_like(dq_ref)

        def pass2(kb, g_left):
            off = pl.multiple_of(kb * tk, tk)
            g, beta = g_s[kb], beta_s[kb]
            g_before = _split_dot(g, before) + g_left
            dz = ((g * (1.0 - beta) - g_before * beta) * scale).astype(BF16)
            dk_ref[pl.ds(off, tk), :] += _dot_tn(dz, qv)
            dq_ref[...] += _dot(dz, k_ref[pl.ds(off, tk), :].astype(BF16))
            return g_left + jnp.sum(g, axis=-1, keepdims=True)

        lax.fori_loop(0, (qi + 1) * ratio, pass2, jnp.zeros((tq, 1), F32))

    args = [proj, proj, proj, dy]
    in_specs = _sb_specs(s, tq, off_q, off_k, off_v) + [pl.BlockSpec((tq, SB_DIM), lambda h, i: (i, h))]
    out_shape = [jax.ShapeDtypeStruct((s, heads * SB_DIM), F32)] * 3
    out_specs = [pl.BlockSpec((tq, SB_DIM), lambda h, i: (i, h)), pl.BlockSpec((s, SB_DIM), lambda h, i: (0, h)),
                 pl.BlockSpec((s, SB_DIM), lambda h, i: (0, h))]
    scratch = [pltpu.VMEM((s // tk, tq, tk), F32), pltpu.VMEM((s // tk, tq, tk), F32)]
    _with_carry(carry, args, in_specs, out_shape, out_specs, scratch)
    return pl.pallas_call(
        body, name="sb_bwd", grid=(heads, nq), out_shape=tuple(out_shape), in_specs=in_specs,
        out_specs=tuple(out_specs), scratch_shapes=scratch, compiler_params=_params(),
    )(*args)


def _slot_sum(recv, name):
    _, r, w = recv.shape
    tr = _tile(r, 256)

    def body(r_ref, o_ref):
        acc = r_ref[0].astype(F32)
        for d in range(1, N_DEV):
            acc = acc + r_ref[d].astype(F32)
        o_ref[...] = acc

    return pl.pallas_call(
        body, name=name, grid=(r // tr,),
        out_shape=jax.ShapeDtypeStruct((r, w), F32),
        in_specs=[pl.BlockSpec((N_DEV, tr, w), lambda i: (0, i, 0))],
        out_specs=pl.BlockSpec((tr, w), lambda i: (i, 0)), compiler_params=_params(),
    )(recv)


def _small_sum(rows, loss_lo, loss_hi, d_model):
    n = rows.shape[1]

    def body(r_ref, o_ref, loss_ref):
        rv = r_ref[...]
        acc = rv[0:1, :]
        for d in range(1, N_DEV):
            acc = acc + rv[d:d + 1, :]
        o_ref[...] = acc
        total = jnp.sum(acc[:, loss_lo:loss_hi], axis=-1, keepdims=True) * (0.5 / d_model)
        loss_ref[...] = jnp.broadcast_to(total, (1, LANES))

    return pl.pallas_call(
        body, name="small_sum",
        out_shape=(jax.ShapeDtypeStruct((1, n), F32), jax.ShapeDtypeStruct((1, LANES), F32)),
        compiler_params=_params(),
    )(rows)


def _adamw(w, g, m, v, name):
    r, c = w.shape
    tr = _tile(r, max(8, (1 << 18) // c // 8 * 8))

    def body(w_ref, g_ref, m_ref, v_ref, d_ref, mo_ref, vo_ref):
        gv = g_ref[...]
        m_new = ADAM_B1 * m_ref[...] + (1.0 - ADAM_B1) * gv
        v_new = ADAM_B2 * v_ref[...] + (1.0 - ADAM_B2) * (gv * gv)
        m_hat = m_new / (1.0 - ADAM_B1 ** ADAM_STEP)
        v_hat = v_new / (1.0 - ADAM_B2 ** ADAM_STEP)
        d_ref[...] = -ADAM_LR * (m_hat / (jnp.sqrt(v_hat) + ADAM_EPS) + ADAM_WD * w_ref[...])
        mo_ref[...] = m_new
        vo_ref[...] = v_new

    spec = pl.BlockSpec((tr, c), lambda i: (i, 0))
    sd = jax.ShapeDtypeStruct((r, c), F32)
    return pl.pallas_call(
        body, name=name, grid=(r // tr,), out_shape=(sd, sd, sd),
        in_specs=[spec] * 4, out_specs=(spec,) * 3, compiler_params=_params(),
    )(w, g, m, v)


def _pack_rows(a):
    return a.reshape(-1, PACK_W)


def _unshard(slots, shape, col_sharded):
    r, c = shape
    if col_sharded:
        return slots.reshape(N_DEV, r, c).transpose(1, 0, 2).reshape(r, N_DEV * c)
    return slots.reshape(N_DEV * r, c)


def _to_shards(full, col_sharded, packed=True):
    r, c = full.shape
    shards = (full.reshape(r, N_DEV, c // N_DEV).transpose(1, 0, 2) if col_sharded
              else full.reshape(N_DEV, r // N_DEV, c))
    return shards.reshape(N_DEV, -1, PACK_W) if packed else shards


def kernel(x, c, positions, w_ada, b_ada, g_norm1, g_norm2, w_in, g_q_latent, g_kv_latent, w_uq, w_ukv, g_q_head, g_k_head, w_proj_mla, w_proj_sb, w_out, w_ffn_in, w_ffn_out, loss_target, m_w_ada, m_b_ada, m_g_norm1, m_g_norm2, m_w_in, m_g_q_latent, m_g_kv_latent, m_w_uq, m_w_ukv, m_g_q_head, m_g_k_head, m_w_proj_mla, m_w_proj_sb, m_w_out, m_w_ffn_in, m_w_ffn_out, v_w_ada, v_b_ada, v_g_norm1, v_g_norm2, v_w_in, v_g_q_latent, v_g_kv_latent, v_w_uq, v_w_ukv, v_g_q_head, v_g_k_head, v_w_proj_mla, v_w_proj_sb, v_w_out, v_w_ffn_in, v_w_ffn_out):
    env = dict(locals())
    drop = lambda a: a[0] if a.ndim == 3 else a
    wts = {n: drop(env[n]) for n in WEIGHT_NAMES}
    mom = {n: drop(env["m_" + n]) for n in WEIGHT_NAMES}
    var = {n: drop(env["v_" + n]) for n in WEIGHT_NAMES}
    xs, tgt, pos = x[0], loss_target[0], positions[0]
    s, d = xs.shape
    lat = wts["w_uq"].shape[0]
    assert wts["w_ukv"].shape[0] == lat
    h_mla = wts["w_uq"].shape[1] * N_DEV // QK_DIM
    sb_w = wts["w_proj_sb"].shape[0]
    h_sb = sb_w // SB_DIM
    d_ff = wts["w_ffn_out"].shape[0] * N_DEV
    me = 4 * lax.axis_index("x") + 2 * lax.axis_index("y") + lax.axis_index("c")

    ref_w = (("c_q", lat), ("c_kv", lat), ("k_pe", ROPE), ("q_sb", sb_w), ("k_sb", sb_w), ("v_sb", sb_w),
             ("gl_a", d), ("gl_b", d))
    ref_off, o = {}, 0
    for n_, w_ in ref_w:
        ref_off[n_] = (o, w_)
        o += w_
    order = ("gl_a", "gl_b", "q_sb", "k_sb", "v_sb", "c_q", "c_kv", "k_pe")
    off, o = {}, 0
    for n_ in order:
        w_ = LANES if n_ == "k_pe" else ref_off[n_][1]
        assert o % w_ == 0
        off[n_] = o
        o += w_

    used_w = o
    proj_w = -(-used_w // (2 * LANES)) * (2 * LANES)

    col_sharded = dict(BIG)
    rows_of = {n: wts[n].size // PACK_W for n, _ in BIG}
    full, grads = {}, {}

    def own_shape(names):
        return len(names) == 1 and wts[names[0]].shape[1] % LANES == 0

    def pack_weights(names):
        if own_shape(names):
            return wts[names[0]].astype(BF16)
        return jnp.concatenate([_pack_rows(wts[n].astype(BF16)) for n in names], axis=0)

    def unpack_weights(slots, names):
        r0 = 0
        for n in names:
            part = slots if own_shape(names) else slots[:, r0:r0 + rows_of[n]]
            full[n] = _unshard(part, wts[n].shape, col_sharded[n])
            r0 += rows_of[n]

    def pack_grads(names):
        if own_shape(names):
            return _to_shards(grads[names[0]].astype(BF16), col_sharded[names[0]], packed=False)
        return jnp.concatenate([_to_shards(grads[n].astype(BF16), col_sharded[n]) for n in names], axis=1)

    def unpack_grads(recv, names):
        summed_rows, r0 = _slot_sum(recv, "slot_sum_" + names[0]), 0
        for n in names:
            part = summed_rows if own_shape(names) else summed_rows[r0:r0 + rows_of[n]]
            grads[n] = part.reshape(wts[n].shape)
            r0 += rows_of[n]

    unpack_weights(_gather_two_level(pack_weights(("w_in",)), "gather_w_in"), ("w_in",))
    seg = lambda a, n_: a[:, ref_off[n_][0]:ref_off[n_][0] + ref_off[n_][1]]
    w_in_k = jnp.concatenate([seg(full["w_in"], n_) for n_ in order]
                             + [jnp.zeros((d, proj_w - used_w + LANES - ROPE), BF16)], axis=1)
    pad_gain = lambda g: jnp.pad(g, ((0, 0), (0, QK_PAD - QK_DIM)))
    g_qh, g_kh = pad_gain(wts["g_q_head"]), pad_gain(wts["g_k_head"])

    c_all = _gather_rows(c, "gather_c")
    ada_cols = _exchange(_ada_fwd(c_all, wts["w_ada"]), gather=True, name="gather_ada")
    ada = lax.dynamic_index_in_dim(ada_cols, me, axis=1, keepdims=False).reshape(1, 6 * d) + wts["b_ada"]
    sh1, sc1, gt1, sh2, sc2, gt2 = [ada[:, i * d:(i + 1) * d] for i in range(6)]

    half = ROPE // 2
    ang = pos.astype(F32)[:, None] * (ROPE_THETA ** (-jnp.arange(half, dtype=F32) / half))
    zeros = jnp.zeros((s, LANES - ROPE), F32)
    cos_t = jnp.concatenate([jnp.cos(ang), jnp.cos(ang), zeros], axis=1)
    sin_t = jnp.concatenate([-jnp.sin(ang), jnp.sin(ang), zeros], axis=1)

    h1 = _norm_mod(xs, wts["g_norm1"], sc1, sh1)
    mixer_w = ("w_uq", "w_ukv", "w_proj_mla", "w_proj_sb", "w_out")
    proj, slots = _mm(h1, w_in_k, name="mm_in", carry=(pack_weights(mixer_w), True))
    unpack_weights(slots, mixer_w)
    w_uq_k = jnp.pad(full["w_uq"].reshape(lat, h_mla, QK_DIM), ((0, 0), (0, 0), (0, QK_PAD - QK_DIM))
                     ).reshape(lat, h_mla * QK_PAD)
    cqn, ckvn = _latent_norm(proj, off["c_q"], off["c_kv"], lat, wts["g_q_latent"], wts["g_kv_latent"])
    q_raw = _mm(cqn, w_uq_k, name="mm_uq")
    kv = _mm(ckvn, full["w_ukv"], name="mm_ukv")
    q, k, v = _qk_prep(q_raw, kv, proj, off["k_pe"], cos_t, sin_t, g_qh, g_kh)
    y_a, lse, slots = _mla_fwd(q, k, v, carry=(pack_weights(("w_ffn_out",)), True))
    unpack_weights(slots, ("w_ffn_out",))
    y_b, w_fi_slots = _sb_fwd(proj, off["q_sb"], off["k_sb"], off["v_sb"], h_sb,
                              carry=(wts["w_ffn_in"].astype(BF16), True))
    ya_p = _mm(y_a, full["w_proj_mla"], name="mm_proj_mla")
    yb_p = _mm(y_b, full["w_proj_sb"], name="mm_proj_sb")
    merged = _merge(proj, off["gl_a"], off["gl_b"], ya_p, yb_p)
    o1 = _mm(merged, full["w_out"], name="mm_out")
    x1, h2 = _resid_norm_mod(xs, o1, gt1, wts["g_norm2"], sc2, sh2)
    gu = _mm(h2, w_fi_slots, shards="b", out_dtype=BF16, name="mm_ffn_in")
    act = _swiglu(gu)
    o2 = _mm(act, full["w_ffn_out"], name="mm_ffn_out")
    dy, d_o2, sums_l = _loss_head(x1, o2, gt2, tgt)

    recv = {}
    d_act = _mm(d_o2, full["w_ffn_out"], tb=True, name="mm_d_act")
    grads["w_ffn_out"] = _mm(act, d_o2, ta=True, out_dtype=BF16, name="mm_g_ffn_out")
    d_gu = _swiglu_bwd(gu, d_act)
    g_fi_slots = _mm(h2, d_gu, ta=True, shards="out", out_dtype=BF16, name="mm_g_ffn_in")
    d_h2, recv["w_ffn_out",] = _mm(d_gu, w_fi_slots, tb=True, shards="b", name="mm_d_h2",
                                   carry=(pack_grads(("w_ffn_out",)), False))
    d_x1, d_o1, sums_2 = _norm_mod_bwd(d_h2, x1, wts["g_norm2"], sc2, dy, gate=(o1, gt1))
    grads["w_out"] = _mm(merged, d_o1, ta=True, out_dtype=BF16, name="mm_g_out")
    d_merged = _mm(d_o1, full["w_out"], tb=True, name="mm_d_merged")
    d_yap, d_ybp, d_gla, d_glb = _merge_bwd(proj, off["gl_a"], off["gl_b"], ya_p, yb_p, d_merged)
    grads["w_proj_mla"] = _mm(y_a, d_yap, ta=True, out_dtype=BF16, name="mm_g_proj_mla")
    grads["w_proj_sb"] = _mm(y_b, d_ybp, ta=True, out_dtype=BF16, name="mm_g_proj_sb")
    d_ya = _mm(d_yap, full["w_proj_mla"], tb=True, name="mm_d_ya")
    d_yb = _mm(d_ybp, full["w_proj_sb"], tb=True, name="mm_d_yb")
    dq_sb, dk_sb, dv_sb, recv["w_ffn_in",] = _sb_bwd(proj, off["q_sb"], off["k_sb"], off["v_sb"], h_sb, d_yb,
                                                     carry=(g_fi_slots, False))
    merge_w = ("w_out", "w_proj_mla", "w_proj_sb")
    dq, dk, dv, recv[merge_w] = _mla_bwd(q, k, v, y_a, lse, d_ya, carry=(pack_grads(merge_w), False))
    d_qraw, d_kv, d_kpe, sums_h = _qk_prep_bwd(q_raw, kv, proj, off["k_pe"], cos_t, sin_t, g_qh, g_kh, dq, dk, dv)
    g_uq_k = _mm(cqn, d_qraw, ta=True, out_dtype=BF16, name="mm_g_uq")
    grads["w_uq"] = g_uq_k.reshape(lat, h_mla, QK_PAD)[:, :, :QK_DIM].reshape(lat, h_mla * QK_DIM)
    grads["w_ukv"] = _mm(ckvn, d_kv, ta=True, out_dtype=BF16, name="mm_g_ukv")
    d_cqn = _mm(d_qraw, w_uq_k, tb=True, name="mm_d_cqn")
    d_ckvn = _mm(d_kv, full["w_ukv"], tb=True, name="mm_d_ckvn")
    d_cq, d_ckv, sums_lat = _latent_norm_bwd(proj, off["c_q"], off["c_kv"], lat, wts["g_q_latent"],
                                             wts["g_kv_latent"], d_cqn, d_ckvn)
    d_parts = {"gl_a": d_gla, "gl_b": d_glb, "q_sb": dq_sb, "k_sb": dk_sb, "v_sb": dv_sb, "c_q": d_cq, "c_kv": d_ckv,
               "k_pe": d_kpe}
    d_proj = jnp.concatenate([d_parts[n_].astype(BF16) for n_ in order]
                             + ([jnp.zeros((s, proj_w - used_w), BF16)] if proj_w > used_w else []), axis=1)
    g_in_k = _mm(h1, d_proj, ta=True, out_dtype=BF16, name="mm_g_in")
    grads["w_in"] = jnp.concatenate([g_in_k[:, off[n_]:off[n_] + w_] for n_, w_ in ref_w], axis=1)
    latent_w = ("w_uq", "w_ukv", "w_in")
    d_h1, recv[latent_w] = _mm(d_proj, w_in_k, tb=True, name="mm_d_h1", carry=(pack_grads(latent_w), False))
    grad_x, sums_1 = _norm_mod_bwd(d_h1, xs, wts["g_norm1"], sc1, d_x1)

    parts = [sums_1[0:1], sums_1[1:2], sums_2[3:4], sums_2[0:1], sums_2[1:2], sums_l[0:1],
             sums_1[2:3], sums_2[2:3], sums_lat[0:1], sums_lat[1:2], sums_h[0:1], sums_h[1:2], sums_l[1:2]]
    part_off, o = [], 0
    for p in parts:
        part_off.append(o)
        o += p.shape[1]
    all_rows = _gather_rows(jnp.concatenate(parts, axis=1), "gather_small")
    summed, loss_row = _small_sum(all_rows, part_off[12], part_off[12] + d, d)
    take = lambda i, w: summed[:, part_off[i]:part_off[i] + w]
    grads["b_ada"] = summed[:, :6 * d]
    grads["g_norm1"], grads["g_norm2"] = take(6, d), take(7, d)
    grads["g_q_latent"], grads["g_kv_latent"] = take(8, lat), take(9, lat)
    grads["g_q_head"], grads["g_k_head"] = take(10, QK_DIM), take(11, QK_DIM)
    n_ada = 6 * d // N_DEV
    d_ada_mine = lax.dynamic_slice_in_dim(all_rows[:, :6 * d], me * n_ada, n_ada, axis=1)
    grads["w_ada"] = _ada_grad(c_all.T, d_ada_mine)

    for names, slots in recv.items():
        unpack_grads(slots, names)

    delta, new_m, new_v = {}, {}, {}
    for n in ("w_ada",) + tuple(n for n, _ in BIG):
        delta[n], new_m[n], new_v[n] = _adamw(wts[n], grads[n], mom[n], var[n], "adamw_" + n)
    cat = lambda t: jnp.concatenate([t[n] for n in SMALL], axis=1)
    d_s, m_s, v_s = _adamw(cat(wts), cat(grads), cat(mom), cat(var), "adamw_small")
    o = 0
    for n in SMALL:
        w_ = wts[n].shape[1]
        delta[n], new_m[n], new_v[n] = d_s[:, o:o + w_], m_s[:, o:o + w_], v_s[:, o:o + w_]
        o += w_

    lead = lambda t: [t[n].reshape(env[n].shape) for n in WEIGHT_NAMES]
    return (loss_row[0, 0], grad_x[None], *lead(grads), *lead(delta), *lead(new_m), *lead(new_v))
```

```python
import jax
import jax.numpy as jnp
from jax import lax
from jax.experimental import pallas as pl
from jax.experimental.pallas import tpu as pltpu

F32 = jnp.float32
BF16 = jnp.bfloat16

N_DEV = 8
LANES = 128
PACK_W = 1024
VMEM_LIMIT = 48 * 1024 * 1024

EPS = 1e-6
ROPE_THETA = 10000.0
NOPE = 128
ROPE = 64
QK_DIM = NOPE + ROPE
QK_PAD = 2 * LANES
V_DIM = 128
SB_DIM = 128
ATT_Q = 512
ATT_K = 256
MM_TK = 2816
PAIR = 2
ROW_PARTS = 2

ADAM_LR = 0.001
ADAM_B1 = 0.9
ADAM_B2 = 0.999
ADAM_EPS = 1e-08
ADAM_WD = 0.01
ADAM_STEP = 10

WEIGHT_NAMES = ("w_ada", "b_ada", "g_norm1", "g_norm2", "w_in", "g_q_latent", "g_kv_latent", "w_uq", "w_ukv",
                "g_q_head", "g_k_head", "w_proj_mla", "w_proj_sb", "w_out", "w_ffn_in", "w_ffn_out")
BIG = (("w_in", True), ("w_uq", True), ("w_ukv", True), ("w_proj_mla", True), ("w_proj_sb", True),
       ("w_out", False), ("w_ffn_in", True), ("w_ffn_out", False))
SMALL = ("b_ada", "g_norm1", "g_norm2", "g_q_latent", "g_kv_latent", "g_q_head", "g_k_head")


def _tile(n, pref):
    if n <= pref:
        return n
    for step in (LANES, 8):
        for t in range(pref - pref % step, 0, -step):
            if n % t == 0:
                return t
    return n


def _params():
    return pltpu.CompilerParams(vmem_limit_bytes=VMEM_LIMIT)


def _sigmoid(x):
    return 1.0 / (1.0 + jnp.exp(-x))


def _dot(a, b):
    return lax.dot_general(a, b, (((1,), (0,)), ((), ())), preferred_element_type=F32)


def _dot_nt(a, b):
    return lax.dot_general(a, b, (((1,), (1,)), ((), ())), preferred_element_type=F32)


def _dot_tn(a, b):
    return lax.dot_general(a, b, (((0,), (0,)), ((), ())), preferred_element_type=F32)


def _split_dot(x, tri):
    hi = x.astype(BF16)
    lo = (x - hi.astype(F32)).astype(BF16)
    return _dot(hi, tri) + _dot(lo, tri)


_HBM = pl.BlockSpec(memory_space=pltpu.HBM)


def _carry_parts(carry):
    src, gather = carry
    slot_shape = tuple(src.shape) if gather else tuple(src.shape[1:])
    return (src, _HBM, jax.ShapeDtypeStruct((N_DEV,) + slot_shape, src.dtype), _HBM,
            [pltpu.SemaphoreType.DMA((N_DEV - 1,)), pltpu.SemaphoreType.DMA((N_DEV - 1,)), pltpu.SemaphoreType.DMA(())])


def _exchange_copies(src_ref, dst_ref, send_sems, recv_sems, local_sem, gather):
    x, y, c = lax.axis_index("x"), lax.axis_index("y"), lax.axis_index("c")
    me = 4 * x + 2 * y + c

    def slot_for(idx):
        return src_ref if gather else src_ref.at[idx]

    copies = [pltpu.make_async_copy(slot_for(me), dst_ref.at[me], local_sem)]
    for k in range(1, N_DEV):
        peer = ((1 - x) if (k >> 2) & 1 else x, (1 - y) if (k >> 1) & 1 else y, (1 - c) if k & 1 else c)
        peer_idx = 4 * peer[0] + 2 * peer[1] + peer[2]
        copies.append(pltpu.make_async_remote_copy(
            src_ref=slot_for(peer_idx), dst_ref=dst_ref.at[me],
            send_sem=send_sems.at[k - 1], recv_sem=recv_sems.at[k - 1],
            device_id=peer, device_id_type=pl.DeviceIdType.MESH))
    return copies


def _carried_exchange(refs, gather, first, last):
    @pl.when(first)
    def _():
        for cp in _exchange_copies(*refs, gather):
            cp.start()

    @pl.when(last)
    def _():
        for cp in _exchange_copies(*refs, gather):
            cp.wait()


def _exchange(src, *, gather, name):
    operand, in_spec, out_shape, out_spec, scratch = _carry_parts((src, gather))

    def body(src_ref, dst_ref, send_sems, recv_sems, local_sem):
        copies = _exchange_copies(src_ref, dst_ref, send_sems, recv_sems, local_sem, gather)
        for cp in copies:
            cp.start()
        for cp in copies:
            cp.wait()

    return pl.pallas_call(body, name=name, out_shape=out_shape, in_specs=[in_spec], out_specs=out_spec,
                          scratch_shapes=scratch)(operand)


def _gather_two_level(src, name):
    operand, in_spec, out_shape, out_spec, scratch = _carry_parts((src, True))

    def body(src_ref, dst_ref, send_sems, recv_sems, local_sem):
        x, y, c = lax.axis_index("x"), lax.axis_index("y"), lax.axis_index("c")
        me, sibling = (x, y, c), (x, y, 1 - c)
        chips = [(1 - x, y), (x, 1 - y), (1 - x, 1 - y)]

        def slot(px, py, pc):
            return dst_ref.at[4 * px + 2 * py + pc]

        def copy(k, block, to, own=False):
            return pltpu.make_async_remote_copy(
                src_ref=src_ref if own else slot(*block), dst_ref=slot(*block),
                send_sem=send_sems.at[k], recv_sem=recv_sems.at[k],
                device_id=to, device_id_type=pl.DeviceIdType.MESH)

        mine = pltpu.make_async_copy(src_ref, slot(*me), local_sem)
        mine.start()
        first = [copy(0, me, sibling, own=True)] + [copy(1 + j, me, (*chip, c), own=True) for j, chip in enumerate(chips)]
        for cp in first:
            cp.start()
        passed = [copy(4 + j, (*chip, c), sibling) for j, chip in enumerate(chips)]
        for j, chip in enumerate(chips):
            copy(1 + j, (*chip, c), me).wait_recv()
            passed[j].start()
        copy(0, sibling, me).wait_recv()
        for j, chip in enumerate(chips):
            copy(4 + j, (*chip, 1 - c), me).wait_recv()
        for cp in first + passed:
            cp.wait_send()
        mine.wait()

    return pl.pallas_call(body, name=name, out_shape=out_shape, in_specs=[in_spec], out_specs=out_spec,
                          scratch_shapes=scratch)(operand)


def _gather_rows(v, name):
    n = v.shape[1]
    padded = -(-n // (8 * LANES)) * (8 * LANES)
    tiles = jnp.pad(v, ((0, 0), (0, padded - n))).reshape(padded // LANES, LANES)
    return _exchange(tiles, gather=True, name=name).reshape(N_DEV, padded)[:, :n]


def _split_refs(refs, n_in, n_out, carry):
    if carry is None:
        return refs[:n_in], refs[n_in:n_in + n_out], refs[n_in + n_out:], None
    ins, src_ref = refs[:n_in], refs[n_in]
    outs, dst_ref = refs[n_in + 1:n_in + 1 + n_out], refs[n_in + 1 + n_out]
    rest = refs[n_in + n_out + 2:]
    return ins, outs, rest[:-3], (src_ref, dst_ref) + tuple(rest[-3:])


def _with_carry(carry, args, in_specs, out_shape, out_specs, scratch):
    if carry is not None:
        operand, c_in, c_shape, c_out, c_scratch = _carry_parts(carry)
        args.append(operand)
        in_specs.append(c_in)
        out_shape.append(c_shape)
        out_specs.append(c_out)
        scratch.extend(c_scratch)


def _slot_of_chunk(j):
    return j // 2 + (N_DEV // 2) * (j % 2)


def _mm(a, b, *, ta=False, tb=False, out_dtype=F32, name, carry=None, shards=None):
    kdim, m = a.shape if ta else a.shape[::-1]
    if shards == "b":
        chunk = b.shape[2]
        n, kdim_b = (b.shape[1], N_DEV * chunk) if tb else (N_DEV * chunk, b.shape[1])
    else:
        n, kdim_b = b.shape if tb else b.shape[::-1]
    assert kdim == kdim_b, (a.shape, b.shape, ta, tb)
    tm, tn, tk = _tile(m, 1024), _tile(n, 1024), _tile(kdim, MM_TK)
    if shards == "b":
        tn, tk = (tn, chunk) if tb else (chunk, tk)
    elif shards == "out":
        chunk = tn = n // N_DEV
    grid = (m // tm, n // tn, kdim // tk)
    nk = grid[2]
    dims = (((0 if ta else 1,), (1 if tb else 0,)), ((), ()))

    def body(*refs):
        (a_ref, b_ref), (o_ref,), scratch_refs, xrefs = _split_refs(refs, 2, 1, carry)
        i, j, k = pl.program_id(0), pl.program_id(1), pl.program_id(2)
        if carry is not None:
            _carried_exchange(xrefs, carry[1], (i == 0) & (j == 0) & (k == 0),
                              (i == grid[0] - 1) & (j == grid[1] - 1) & (k == nk - 1))
        prod = lax.dot_general(a_ref[...].astype(BF16), b_ref[...].astype(BF16), dims, preferred_element_type=F32)
        if nk == 1:
            o_ref[...] = prod.astype(o_ref.dtype)
        else:
            acc_ref = scratch_refs[0]

            @pl.when(k == 0)
            def _():
                acc_ref[...] = prod

            @pl.when(k > 0)
            def _():
                acc_ref[...] += prod

            @pl.when(k == nk - 1)
            def _():
                o_ref[...] = acc_ref[...].astype(o_ref.dtype)

    a_spec = (pl.BlockSpec((tk, tm), lambda i, j, k: (k, i)) if ta else pl.BlockSpec((tm, tk), lambda i, j, k: (i, k)))
    if shards == "b":
        b_spec = (pl.BlockSpec((None, tn, tk), lambda i, j, k: (_slot_of_chunk(k), j, 0)) if tb
                  else pl.BlockSpec((None, tk, tn), lambda i, j, k: (_slot_of_chunk(j), k, 0)))
    else:
        b_spec = (pl.BlockSpec((tn, tk), lambda i, j, k: (j, k)) if tb else pl.BlockSpec((tk, tn), lambda i, j, k: (k, j)))
    args, in_specs = [a, b], [a_spec, b_spec]
    if shards == "out":
        out_shape = [jax.ShapeDtypeStruct((N_DEV, m, tn), out_dtype)]
        out_specs = [pl.BlockSpec((None, tm, tn), lambda i, j, k: (_slot_of_chunk(j), i, 0))]
    else:
        out_shape, out_specs = [jax.ShapeDtypeStruct((m, n), out_dtype)], [pl.BlockSpec((tm, tn), lambda i, j, k: (i, j))]
    scratch = [] if nk == 1 else [pltpu.VMEM((tm, tn), F32)]
    _with_carry(carry, args, in_specs, out_shape, out_specs, scratch)
    out = pl.pallas_call(
        body, name=name, grid=grid, out_shape=tuple(out_shape), in_specs=in_specs, out_specs=tuple(out_specs),
        scratch_shapes=scratch, compiler_params=_params(),
    )(*args)
    return out if carry is not None else out[0]


def _ada_fwd(c_all, w_shard):
    d, n = w_shard.shape
    tn = _tile(n, 512)

    def body(c_ref, w_ref, o_ref):
        cv = c_ref[...]
        o_ref[...] = jnp.dot(cv * _sigmoid(cv), w_ref[...], precision=lax.Precision.HIGHEST,
                             preferred_element_type=F32)

    return pl.pallas_call(
        body, name="ada_fwd", grid=(n // tn,),
        out_shape=jax.ShapeDtypeStruct((N_DEV, n), F32),
        in_specs=[pl.BlockSpec((N_DEV, d), lambda j: (0, 0)), pl.BlockSpec((d, tn), lambda j: (0, j))],
        out_specs=pl.BlockSpec((N_DEV, tn), lambda j: (0, j)),
        compiler_params=_params(),
    )(c_all, w_shard)


def _ada_grad(c_all_t, d_rows):
    d, n = c_all_t.shape[0], d_rows.shape[1]
    tn = _tile(n, 512)

    def body(ct_ref, d_ref, o_ref):
        cv = ct_ref[...]
        s = cv * _sigmoid(cv)
        dv = d_ref[...]
        acc = s[:, 0:1] * dv[0:1, :]
        for b in range(1, N_DEV):
            acc = acc + s[:, b:b + 1] * dv[b:b + 1, :]
        o_ref[...] = acc

    return pl.pallas_call(
        body, name="ada_grad", grid=(n // tn,),
        out_shape=jax.ShapeDtypeStruct((d, n), F32),
        in_specs=[pl.BlockSpec((d, N_DEV), lambda j: (0, 0)), pl.BlockSpec((N_DEV, tn), lambda j: (0, j))],
        out_specs=pl.BlockSpec((d, tn), lambda j: (0, j)),
        compiler_params=_params(),
    )(c_all_t, d_rows)


def _row(ts, w, col=0):
    return pl.BlockSpec((ts, w), lambda i, col=col: (i, col))


def _vec(w):
    return pl.BlockSpec((1, w), lambda i: (0, 0))


def _norm_mod(x, g, sc, sh):
    s, d = x.shape
    ts = _tile(s, 512)

    def body(x_ref, g_ref, sc_ref, sh_ref, h_ref):
        xv = x_ref[...]
        r = lax.rsqrt(jnp.mean(xv * xv, axis=-1, keepdims=True) + EPS)
        h_ref[...] = ((xv * r) * g_ref[...] * (1.0 + sc_ref[...]) + sh_ref[...]).astype(BF16)

    return pl.pallas_call(
        body, name="norm_mod", grid=(s // ts,),
        out_shape=jax.ShapeDtypeStruct((s, d), BF16),
        in_specs=[_row(ts, d), _vec(d), _vec(d), _vec(d)],
        out_specs=_row(ts, d), compiler_params=_params(),
    )(x, g, sc, sh)


def _latent_norm(proj, off_q, off_kv, lat, g_q, g_kv):
    s = proj.shape[0]
    ts = _tile(s, 512)

    def body(cq_ref, ckv_ref, gq_ref, gkv_ref, oq_ref, okv_ref):
        for c_ref, g_ref, o_ref in ((cq_ref, gq_ref, oq_ref), (ckv_ref, gkv_ref, okv_ref)):
            v = c_ref[...]
            r = lax.rsqrt(jnp.mean(v * v, axis=-1, keepdims=True) + EPS)
            o_ref[...] = ((v * r) * g_ref[...]).astype(BF16)

    return pl.pallas_call(
        body, name="latent_norm", grid=(s // ts,),
        out_shape=(jax.ShapeDtypeStruct((s, lat), BF16), jax.ShapeDtypeStruct((s, lat), BF16)),
        in_specs=[_row(ts, lat, off_q // lat), _row(ts, lat, off_kv // lat), _vec(lat), _vec(lat)],
        out_specs=(_row(ts, lat), _row(ts, lat)), compiler_params=_params(),
    )(proj, proj, g_q, g_kv)


def _latent_norm_bwd(proj, off_q, off_kv, lat, g_q, g_kv, d_cqn, d_ckvn):
    s = proj.shape[0]
    ts = _tile(s, 512)

    def body(cq_ref, ckv_ref, gq_ref, gkv_ref, dq_ref, dkv_ref, oq_ref, okv_ref, sums_ref):
        @pl.when(pl.program_id(0) == 0)
        def _():
            sums_ref[...] = jnp.zeros_like(sums_ref)

        for row, (c_ref, g_ref, d_ref, o_ref) in enumerate(((cq_ref, gq_ref, dq_ref, oq_ref),
                                                             (ckv_ref, gkv_ref, dkv_ref, okv_ref))):
            v = c_ref[...]
            r = lax.rsqrt(jnp.mean(v * v, axis=-1, keepdims=True) + EPS)
            vn = v * r
            dn = d_ref[...]
            sums_ref[row:row + 1, :] += jnp.sum(dn * vn, axis=0, keepdims=True)
            dvn = dn * g_ref[...]
            o_ref[...] = (r * (dvn - vn * jnp.mean(dvn * vn, axis=-1, keepdims=True))).astype(BF16)

    return pl.pallas_call(
        body, name="latent_norm_bwd", grid=(s // ts,),
        out_shape=(jax.ShapeDtypeStruct((s, lat), BF16), jax.ShapeDtypeStruct((s, lat), BF16),
                   jax.ShapeDtypeStruct((8, lat), F32)),
        in_specs=[_row(ts, lat, off_q // lat), _row(ts, lat, off_kv // lat), _vec(lat), _vec(lat),
                  _row(ts, lat), _row(ts, lat)],
        out_specs=(_row(ts, lat), _row(ts, lat), pl.BlockSpec((8, lat), lambda i: (0, 0))),
        compiler_params=_params(),
    )(proj, proj, g_q, g_kv, d_cqn, d_ckvn)


def _merge(proj, off_a, off_b, ya_p, yb_p):
    s, d = ya_p.shape
    ts = _tile(s, 256)

    def body(ga_ref, gb_ref, ya_ref, yb_ref, o_ref):
        o_ref[...] = (_sigmoid(ga_ref[...]) * ya_ref[...] + _sigmoid(gb_ref[...]) * yb_ref[...]).astype(BF16)

    return pl.pallas_call(
        body, name="merge", grid=(s // ts,),
        out_shape=jax.ShapeDtypeStruct((s, d), BF16),
        in_specs=[_row(ts, d, off_a // d), _row(ts, d, off_b // d), _row(ts, d), _row(ts, d)],
        out_specs=_row(ts, d), compiler_params=_params(),
    )(proj, proj, ya_p, yb_p)


def _merge_bwd(proj, off_a, off_b, ya_p, yb_p, d_merged):
    s, d = ya_p.shape
    ts = _tile(s, 256)

    def body(ga_ref, gb_ref, ya_ref, yb_ref, dm_ref, dya_ref, dyb_ref, dga_ref, dgb_ref):
        dm = dm_ref[...]
        for g_ref, y_ref, dy_ref, dg_ref in ((ga_ref, ya_ref, dya_ref, dga_ref), (gb_ref, yb_ref, dyb_ref, dgb_ref)):
            sg = _sigmoid(g_ref[...])
            dy_ref[...] = (dm * sg).astype(BF16)
            dg_ref[...] = (dm * y_ref[...] * sg * (1.0 - sg)).astype(BF16)

    sd = jax.ShapeDtypeStruct((s, d), BF16)
    return pl.pallas_call(
        body, name="merge_bwd", grid=(s // ts,),
        out_shape=(sd, sd, sd, sd),
        in_specs=[_row(ts, d, off_a // d), _row(ts, d, off_b // d), _row(ts, d), _row(ts, d), _row(ts, d)],
        out_specs=(_row(ts, d),) * 4, compiler_params=_params(),
    )(proj, proj, ya_p, yb_p, d_merged)


def _resid_norm_mod(x, o, gt, g, sc, sh):
    s, d = x.shape
    ts = _tile(s, 256)

    def body(x_ref, o_ref, gt_ref, g_ref, sc_ref, sh_ref, x1_ref, h_ref):
        x1 = x_ref[...] + gt_ref[...] * o_ref[...]
        x1_ref[...] = x1
        r = lax.rsqrt(jnp.mean(x1 * x1, axis=-1, keepdims=True) + EPS)
        h_ref[...] = ((x1 * r) * g_ref[...] * (1.0 + sc_ref[...]) + sh_ref[...]).astype(BF16)

    return pl.pallas_call(
        body, name="resid_norm_mod", grid=(s // ts,),
        out_shape=(jax.ShapeDtypeStruct((s, d), F32), jax.ShapeDtypeStruct((s, d), BF16)),
        in_specs=[_row(ts, d), _row(ts, d), _vec(d), _vec(d), _vec(d), _vec(d)],
        out_specs=(_row(ts, d), _row(ts, d)), compiler_params=_params(),
    )(x, o, gt, g, sc, sh)


def _loss_head(x1, o2, gt2, target):
    s, d = x1.shape
    ts = _tile(s, 256)

    def body(x1_ref, o2_ref, gt_ref, t_ref, dy_ref, do_ref, sums_ref):
        @pl.when(pl.program_id(0) == 0)
        def _():
            sums_ref[...] = jnp.zeros_like(sums_ref)

        o2 = o2_ref[...]
        e = x1_ref[...] + gt_ref[...] * o2 - t_ref[...]
        dy = e / d
        dy_ref[...] = dy
        do_ref[...] = (dy * gt_ref[...]).astype(BF16)
        sums_ref[0:1, :] += jnp.sum(dy * o2, axis=0, keepdims=True)
        sums_ref[1:2, :] += jnp.sum(e * e, axis=0, keepdims=True)

    return pl.pallas_call(
        body, name="loss_head", grid=(s // ts,),
        out_shape=(jax.ShapeDtypeStruct((s, d), F32), jax.ShapeDtypeStruct((s, d), BF16),
                   jax.ShapeDtypeStruct((8, d), F32)),
        in_specs=[_row(ts, d), _row(ts, d), _vec(d), _row(ts, d)],
        out_specs=(_row(ts, d), _row(ts, d), pl.BlockSpec((8, d), lambda i: (0, 0))),
        compiler_params=_params(),
    )(x1, o2, gt2, target)


def _norm_mod_bwd(dh, xin, g, sc, dres, gate=None):
    s, d = xin.shape
    ts = _tile(s, 256)
    gated = gate is not None

    def body(*refs):
        if gated:
            dh_ref, x_ref, g_ref, sc_ref, dr_ref, o_ref, gt_ref, dx_ref, do_ref, sums_ref = refs
        else:
            dh_ref, x_ref, g_ref, sc_ref, dr_ref, dx_ref, sums_ref = refs

        @pl.when(pl.program_id(0) == 0)
        def _():
            sums_ref[...] = jnp.zeros_like(sums_ref)

        xv, dhv = x_ref[...], dh_ref[...]
        r = lax.rsqrt(jnp.mean(xv * xv, axis=-1, keepdims=True) + EPS)
        xn = xv * r
        one_sc = 1.0 + sc_ref[...]
        sums_ref[0:1, :] += jnp.sum(dhv, axis=0, keepdims=True)
        sums_ref[1:2, :] += jnp.sum(dhv * (xn * g_ref[...]), axis=0, keepdims=True)
        sums_ref[2:3, :] += jnp.sum(dhv * one_sc * xn, axis=0, keepdims=True)
        dxn = dhv * one_sc * g_ref[...]
        dx = dr_ref[...] + r * (dxn - xn * jnp.mean(dxn * xn, axis=-1, keepdims=True))
        dx_ref[...] = dx
        if gated:
            sums_ref[3:4, :] += jnp.sum(dx * o_ref[...], axis=0, keepdims=True)
            do_ref[...] = (dx * gt_ref[...]).astype(BF16)

    in_specs = [_row(ts, d), _row(ts, d), _vec(d), _vec(d), _row(ts, d)]
    args = [dh, xin, g, sc, dres]
    out_shape = [jax.ShapeDtypeStruct((s, d), F32)]
    out_specs = [_row(ts, d)]
    if gated:
        in_specs += [_row(ts, d), _vec(d)]
        args += list(gate)
        out_shape.append(jax.ShapeDtypeStruct((s, d), BF16))
        out_specs.append(_row(ts, d))
    out_shape.append(jax.ShapeDtypeStruct((8, d), F32))
    out_specs.append(pl.BlockSpec((8, d), lambda i: (0, 0)))
    return pl.pallas_call(
        body, name="norm_mod_bwd_gated" if gated else "norm_mod_bwd", grid=(s // ts,),
        out_shape=tuple(out_shape), in_specs=in_specs, out_specs=tuple(out_specs), compiler_params=_params(),
    )(*args)


def _ffn_tile(f):
    return 2 * f // N_DEV


def _swiglu(gu):
    s, f2 = gu.shape
    f = f2 // 2
    ts, tc = _tile(s, 512), _ffn_tile(f)

    def body(gu_ref, o_ref):
        gv, uv = gu_ref[:, :tc].astype(F32), gu_ref[:, tc:].astype(F32)
        o_ref[...] = (gv * _sigmoid(gv) * uv).astype(BF16)

    return pl.pallas_call(
        body, name="swiglu", grid=(s // ts, f // tc),
        out_shape=jax.ShapeDtypeStruct((s, f), BF16),
        in_specs=[pl.BlockSpec((ts, 2 * tc), lambda i, j: (i, j))],
        out_specs=pl.BlockSpec((ts, tc), lambda i, j: (i, j)), compiler_params=_params(),
    )(gu)


def _swiglu_bwd(gu, d_act):
    s, f2 = gu.shape
    f = f2 // 2
    ts, tc = _tile(s, 512), _ffn_tile(f)

    def body(gu_ref, da_ref, o_ref):
        gv, uv, da = gu_ref[:, :tc].astype(F32), gu_ref[:, tc:].astype(F32), da_ref[...]
        sg = _sigmoid(gv)
        o_ref[:, :tc] = (da * uv * (sg * (1.0 + gv * (1.0 - sg)))).astype(BF16)
        o_ref[:, tc:] = (da * (gv * sg)).astype(BF16)

    return pl.pallas_call(
        body, name="swiglu_bwd", grid=(s // ts, f // tc),
        out_shape=jax.ShapeDtypeStruct((s, f2), BF16),
        in_specs=[pl.BlockSpec((ts, 2 * tc), lambda i, j: (i, j)), pl.BlockSpec((ts, tc), lambda i, j: (i, j))],
        out_specs=pl.BlockSpec((ts, 2 * tc), lambda i, j: (i, j)), compiler_params=_params(),
    )(gu, d_act)


def _swap_halves(t):
    return pltpu.roll(t, ROPE // 2, 1) + pltpu.roll(t, LANES - ROPE // 2, 1)


def _head_norm(raw):
    r = lax.rsqrt(jnp.sum(raw * raw, axis=-1, keepdims=True) / QK_DIM + EPS)
    return raw * r, r


def _rope_fwd(v, cos, sin):
    rope_tile = v[:, NOPE:]
    return jnp.concatenate([v[:, :NOPE], rope_tile * cos + _swap_halves(rope_tile) * sin], axis=1)


def _rope_bwd(d, cos, sin, lane_ok):
    d_tile = d[:, NOPE:]
    return jnp.concatenate([d[:, :NOPE], d_tile * cos + _swap_halves(d_tile * sin) * lane_ok], axis=1)


def _qk_prep(q_raw, kv, proj, off_pe, cos, sin, g_q, g_k):
    s, hw = q_raw.shape
    heads = hw // QK_PAD
    ts = _tile(s, 512)

    def body(q_ref, kv_ref, pe_ref, cos_ref, sin_ref, gq_ref, gk_ref, qo_ref, ko_ref, vo_ref):
        cos_v, sin_v = cos_ref[...], sin_ref[...]
        qn, _ = _head_norm(q_ref[...])
        qo_ref[...] = _rope_fwd(qn * gq_ref[...], cos_v, sin_v).astype(BF16)
        kvv = kv_ref[...]
        kn, _ = _head_norm(jnp.concatenate([kvv[:, :NOPE], pe_ref[...]], axis=1))
        ko_ref[...] = _rope_fwd(kn * gk_ref[...], cos_v, sin_v).astype(BF16)
        vo_ref[...] = kvv[:, NOPE:].astype(BF16)

    blk = lambda w: pl.BlockSpec((ts, w), lambda i, h: (i, h))
    fixed = lambda w, col=0: pl.BlockSpec((ts, w), lambda i, h, col=col: (i, col))
    vec = pl.BlockSpec((1, QK_PAD), lambda i, h: (0, 0))
    return pl.pallas_call(
        body, name="qk_prep", grid=(s // ts, heads),
        out_shape=(jax.ShapeDtypeStruct((s, hw), BF16), jax.ShapeDtypeStruct((s, hw), BF16),
                   jax.ShapeDtypeStruct((s, heads * V_DIM), BF16)),
        in_specs=[blk(QK_PAD), blk(QK_PAD), fixed(LANES, off_pe // LANES), fixed(LANES), fixed(LANES), vec, vec],
        out_specs=(blk(QK_PAD), blk(QK_PAD), blk(V_DIM)), compiler_params=_params(),
    )(q_raw, kv, proj, cos, sin, g_q, g_k)


def _qk_prep_bwd(q_raw, kv, proj, off_pe, cos, sin, g_q, g_k, dq, dk, dv):
    s, hw = q_raw.shape
    heads = hw // QK_PAD
    ts = _tile(s, 512)

    def body(q_ref, kv_ref, pe_ref, cos_ref, sin_ref, gq_ref, gk_ref, dq_ref, dk_ref, dv_ref,
             dqr_ref, dkv_ref, dpe_ref, sums_ref):
        i, h = pl.program_id(0), pl.program_id(1)

        @pl.when((i == 0) & (h == 0))
        def _():
            sums_ref[...] = jnp.zeros_like(sums_ref)

        cos_v, sin_v = cos_ref[...], sin_ref[...]
        lane_ok = (lax.broadcasted_iota(jnp.int32, (ts, LANES), 1) < ROPE).astype(F32)

        def one(raw, g, d_post, row):
            vn, r = _head_norm(raw)
            d_pre = _rope_bwd(d_post, cos_v, sin_v, lane_ok)
            sums_ref[row:row + 1, :] += jnp.sum(d_pre * vn, axis=0, keepdims=True)
            dvn = d_pre * g
            return r * (dvn - vn * (jnp.sum(dvn * vn, axis=-1, keepdims=True) / QK_DIM))

        dqr_ref[...] = one(q_ref[...], gq_ref[...], dq_ref[...], 0).astype(BF16)
        kvv = kv_ref[...]
        d_kraw = one(jnp.concatenate([kvv[:, :NOPE], pe_ref[...]], axis=1), gk_ref[...], dk_ref[...], 1)
        dkv_ref[...] = jnp.concatenate([d_kraw[:, :NOPE], dv_ref[...]], axis=1).astype(BF16)

        @pl.when(h == 0)
        def _():
            dpe_ref[...] = jnp.zeros_like(dpe_ref)

        dpe_ref[...] += d_kraw[:, NOPE:]

    blk = lambda w: pl.BlockSpec((ts, w), lambda i, h: (i, h))
    fixed = lambda w, col=0: pl.BlockSpec((ts, w), lambda i, h, col=col: (i, col))
    vec = pl.BlockSpec((1, QK_PAD), lambda i, h: (0, 0))
    return pl.pallas_call(
        body, name="qk_prep_bwd", grid=(s // ts, heads),
        out_shape=(jax.ShapeDtypeStruct((s, hw), BF16), jax.ShapeDtypeStruct((s, hw), BF16),
                   jax.ShapeDtypeStruct((s, LANES), F32), jax.ShapeDtypeStruct((8, QK_PAD), F32)),
        in_specs=[blk(QK_PAD), blk(QK_PAD), fixed(LANES, off_pe // LANES), fixed(LANES), fixed(LANES), vec, vec,
                  blk(QK_PAD), blk(QK_PAD), blk(V_DIM)],
        out_specs=(blk(QK_PAD), blk(QK_PAD), fixed(LANES), pl.BlockSpec((8, QK_PAD), lambda i, h: (0, 0))),
        compiler_params=_params(),
    )(q_raw, kv, proj, cos, sin, g_q, g_k, dq, dk, dv)


def _att_blocks(s):
    tq = _tile(s, ATT_Q)
    tk = _tile(tq, ATT_K)
    return tq, tk, tq // tk


def _pairing(heads):
    return PAIR if heads % PAIR == 0 else 1


def _lanes(e, width):
    return slice(e * width, (e + 1) * width)


def _visible(qi, kb, tq, tk, strict):
    row = qi * tq + lax.broadcasted_iota(jnp.int32, (tq, tk), 0)
    col = kb * tk + lax.broadcasted_iota(jnp.int32, (tq, tk), 1)
    return (col < row) if strict else (col <= row)


def _first_last(heads, nq):
    h, qi = pl.program_id(0), pl.program_id(1)
    return (h == 0) & (qi == 0), (h == heads - 1) & (qi == nq - 1)


def _mla_fwd(q, k, v, carry=None):
    s = q.shape[0]
    heads = q.shape[1] // QK_PAD
    tq, tk, ratio = _att_blocks(s)
    nq = s // tq
    scale = QK_DIM ** -0.5

    hp = _pairing(heads)
    ev = range(hp)

    def body(*refs):
        (q_ref, k_ref, v_ref), (o_ref, lse_ref), (acc_ref,), xrefs = _split_refs(refs, 3, 2, carry)
        if carry is not None:
            _carried_exchange(xrefs, carry[1], *_first_last(heads // hp, nq))
        qi = pl.program_id(1)
        qs = [q_ref[:, _lanes(e, QK_PAD)] for e in ev]
        acc_ref[...] = jnp.zeros_like(acc_ref)

        def step(kb, state, masked):
            ms, ls = state
            off = pl.multiple_of(kb * tk, tk)
            sc = [_dot_nt(qs[e], k_ref[pl.ds(off, tk), _lanes(e, QK_PAD)]) * scale for e in ev]
            if masked:
                mask = _visible(qi, kb, tq, tk, False)
                sc = [jnp.where(mask, sc[e], -1e30) for e in ev]
            m_new = [jnp.maximum(ms[e], jnp.max(sc[e], axis=-1, keepdims=True)) for e in ev]
            alpha = [jnp.exp(ms[e] - m_new[e]) for e in ev]
            p = [jnp.exp(sc[e] - m_new[e]) for e in ev]
            for e in ev:
                lanes = _lanes(e, V_DIM)
                acc_ref[:, lanes] = alpha[e] * acc_ref[:, lanes] + _dot(p[e].astype(BF16), v_ref[pl.ds(off, tk), lanes])
            return tuple(m_new), tuple(alpha[e] * ls[e] + jnp.sum(p[e], axis=-1, keepdims=True) for e in ev)

        state = (tuple(jnp.full((tq, 1), -1e30, F32) for _ in ev), tuple(jnp.zeros((tq, 1), F32) for _ in ev))
        state = lax.fori_loop(0, qi * ratio, lambda kb, st: step(kb, st, False), state)
        for i in range(ratio):
            state = step(qi * ratio + i, state, True)
        ms, ls = state
        for e in ev:
            o_ref[:, _lanes(e, V_DIM)] = (acc_ref[:, _lanes(e, V_DIM)] / ls[e]).astype(BF16)
            lse_ref[e] = ms[e] + jnp.log(ls[e])

    args = [q, k, v]
    in_specs = [pl.BlockSpec((tq, hp * QK_PAD), lambda h, i: (i, h)), pl.BlockSpec((s, hp * QK_PAD), lambda h, i: (0, h)),
                pl.BlockSpec((s, hp * V_DIM), lambda h, i: (0, h))]
    out_shape = [jax.ShapeDtypeStruct((s, heads * V_DIM), BF16), jax.ShapeDtypeStruct((heads, s, 1), F32)]
    out_specs = [pl.BlockSpec((tq, hp * V_DIM), lambda h, i: (i, h)), pl.BlockSpec((hp, tq, 1), lambda h, i: (h, i, 0))]
    scratch = [pltpu.VMEM((tq, hp * V_DIM), F32)]
    _with_carry(carry, args, in_specs, out_shape, out_specs, scratch)
    return pl.pallas_call(
        body, name="mla_fwd", grid=(heads // hp, nq), out_shape=tuple(out_shape), in_specs=in_specs,
        out_specs=tuple(out_specs), scratch_shapes=scratch, compiler_params=_params(),
    )(*args)


def _mla_bwd(q, k, v, o, lse, do, carry=None):
    s = q.shape[0]
    heads = q.shape[1] // QK_PAD
    tq, tk, ratio = _att_blocks(s)
    nq = s // tq
    scale = QK_DIM ** -0.5

    hp = _pairing(heads)
    ev = range(hp)

    def body(*refs):
        (q_ref, k_ref, v_ref, o_ref, lse_ref, do_ref), (dq_ref, dk_ref, dv_ref), _, xrefs = _split_refs(refs, 6, 3, carry)
        if carry is not None:
            _carried_exchange(xrefs, carry[1], *_first_last(heads // hp, nq))
        qi = pl.program_id(1)

        @pl.when(qi == 0)
        def _():
            dk_ref[...] = jnp.zeros_like(dk_ref)
            dv_ref[...] = jnp.zeros_like(dv_ref)

        qs = [q_ref[:, _lanes(e, QK_PAD)] for e in ev]
        dos = [do_ref[:, _lanes(e, V_DIM)] for e in ev]
        do_b = [dos[e].astype(BF16) for e in ev]
        delta = [jnp.sum(dos[e] * o_ref[:, _lanes(e, V_DIM)].astype(F32), axis=-1, keepdims=True) for e in ev]
        lse_v = [lse_ref[e] for e in ev]
        dq_ref[...] = jnp.zeros_like(dq_ref)

        def step(kb, masked):
            off = pl.multiple_of(kb * tk, tk)
            ks = [k_ref[pl.ds(off, tk), _lanes(e, QK_PAD)] for e in ev]
            vs = [v_ref[pl.ds(off, tk), _lanes(e, V_DIM)] for e in ev]
            sc = [_dot_nt(qs[e], ks[e]) for e in ev]
            dp = [_dot_nt(do_b[e], vs[e]) for e in ev]
            p = [jnp.exp(sc[e] * scale - lse_v[e]) for e in ev]
            if masked:
                mask = _visible(qi, kb, tq, tk, False)
                p = [jnp.where(mask, p[e], 0.0) for e in ev]
            ds = [(p[e] * (dp[e] - delta[e]) * scale).astype(BF16) for e in ev]
            for e in ev:
                dv_ref[pl.ds(off, tk), _lanes(e, V_DIM)] += _dot_tn(p[e].astype(BF16), do_b[e])
            for e in ev:
                dk_ref[pl.ds(off, tk), _lanes(e, QK_PAD)] += _dot_tn(ds[e], qs[e])
            for e in ev:
                dq_ref[:, _lanes(e, QK_PAD)] += _dot(ds[e], ks[e])
            return 0

        lax.fori_loop(0, qi * ratio, lambda kb, _: step(kb, False), 0)
        for i in range(ratio):
            step(qi * ratio + i, True)

    args = [q, k, v, o, lse, do]
    wide, narrow = hp * QK_PAD, hp * V_DIM
    in_specs = [pl.BlockSpec((tq, wide), lambda h, i: (i, h)), pl.BlockSpec((s, wide), lambda h, i: (0, h)),
                pl.BlockSpec((s, narrow), lambda h, i: (0, h)), pl.BlockSpec((tq, narrow), lambda h, i: (i, h)),
                pl.BlockSpec((hp, tq, 1), lambda h, i: (h, i, 0)), pl.BlockSpec((tq, narrow), lambda h, i: (i, h))]
    out_shape = [jax.ShapeDtypeStruct(q.shape, F32), jax.ShapeDtypeStruct(k.shape, F32),
                 jax.ShapeDtypeStruct(v.shape, F32)]
    out_specs = [pl.BlockSpec((tq, wide), lambda h, i: (i, h)), pl.BlockSpec((s, wide), lambda h, i: (0, h)),
                 pl.BlockSpec((s, narrow), lambda h, i: (0, h))]
    scratch = []
    _with_carry(carry, args, in_specs, out_shape, out_specs, scratch)
    return pl.pallas_call(
        body, name="mla_bwd", grid=(heads // hp, nq), out_shape=tuple(out_shape), in_specs=in_specs,
        out_specs=tuple(out_specs), scratch_shapes=scratch, compiler_params=_params(),
    )(*args)


def _sb_terms(qv, k_blk, scale, mask):
    z = _dot_nt(qv, k_blk) * scale
    log_beta = jnp.minimum(z, 0.0) - jnp.log(1.0 + jnp.exp(-jnp.abs(z)))
    log_rest = log_beta - z
    if mask is not None:
        log_rest = jnp.where(mask, log_rest, 0.0)
    return log_beta, log_rest


def _head_lanes(e):
    return slice(e * SB_DIM, (e + 1) * SB_DIM)


def _sb_specs(s, tq, off_q, off_k, off_v, hp):
    w = hp * SB_DIM
    assert off_q % w == 0 and off_k % w == 0 and off_v % w == 0
    return [pl.BlockSpec((tq, w), lambda h, i: (i, off_q // w + h)),
            pl.BlockSpec((s, w), lambda h, i: (0, off_k // w + h)),
            pl.BlockSpec((s, w), lambda h, i: (0, off_v // w + h))]


def _sb_fwd(qkv, off_q, off_k, off_v, heads, carry=None):
    s = qkv.shape[0]
    tq, tk, ratio = _att_blocks(s)
    nq = s // tq
    hp = _pairing(heads)
    scale = SB_DIM ** -0.5

    def body(*refs):
        (q_ref, k_ref, v_ref), (o_ref,), (acc_ref,), xrefs = _split_refs(refs, 3, 1, carry)
        if carry is not None:
            _carried_exchange(xrefs, carry[1], *_first_last(heads // hp, nq))
        qi = pl.program_id(1)
        qs = [q_ref[:, _head_lanes(e)] for e in range(hp)]
        after = (lax.broadcasted_iota(jnp.int32, (tk, tk), 0) > lax.broadcasted_iota(jnp.int32, (tk, tk), 1)).astype(BF16)
        acc_ref[...] = jnp.zeros_like(acc_ref)
        row_parts = [slice(r * (tq // ROW_PARTS), (r + 1) * (tq // ROW_PARTS)) for r in range(ROW_PARTS)]

        def step(kb, tails, masked):
            off = pl.multiple_of(kb * tk, tk)
            mask = _visible(qi, kb, tq, tk, True) if masked else None
            units = [(e, rows) for e in range(hp) for rows in row_parts]
            ks = [k_ref[pl.ds(off, tk), _head_lanes(e)] for e in range(hp)]
            vs = [v_ref[pl.ds(off, tk), _head_lanes(e)] for e in range(hp)]
            terms = [_sb_terms(qs[e][rows], ks[e], scale, None if mask is None else mask[rows]) for e, rows in units]
            sums = [_split_dot(t[1], after) for t in terms]
            a = [jnp.exp(t[0] + (sm + tails[e][rows])) for t, sm, (e, rows) in zip(terms, sums, units)]
            if masked:
                a = [jnp.where(mask[rows], a_, 0.0) for a_, (e, rows) in zip(a, units)]
            for a_, (e, rows) in zip(a, units):
                acc_ref[rows, _head_lanes(e)] += _dot(a_.astype(BF16), vs[e])
            rest = [jnp.sum(t[1], axis=-1, keepdims=True) for t in terms]
            return tuple(tails[e] + jnp.concatenate(rest[e * len(row_parts):(e + 1) * len(row_parts)], axis=0)
                         for e in range(hp))

        tails = tuple(jnp.zeros((tq, 1), F32) for _ in range(hp))
        for i in range(ratio):
            tails = step((qi + 1) * ratio - 1 - i, tails, True)
        lax.fori_loop(0, qi * ratio, lambda i, t_: step(qi * ratio - 1 - i, t_, False), tails)
        o_ref[...] = acc_ref[...].astype(BF16)

    w = hp * SB_DIM
    args, in_specs = [qkv, qkv, qkv], _sb_specs(s, tq, off_q, off_k, off_v, hp)
    out_shape = [jax.ShapeDtypeStruct((s, heads * SB_DIM), BF16)]
    out_specs = [pl.BlockSpec((tq, w), lambda h, i: (i, h))]
    scratch = [pltpu.VMEM((tq, w), F32)]
    _with_carry(carry, args, in_specs, out_shape, out_specs, scratch)
    out = pl.pallas_call(
        body, name="sb_fwd", grid=(heads // hp, nq), out_shape=tuple(out_shape), in_specs=in_specs,
        out_specs=tuple(out_specs), scratch_shapes=scratch, compiler_params=_params(),
    )(*args)
    return out if carry is not None else out[0]


def _sb_bwd(qkv, off_q, off_k, off_v, heads, dy, carry=None):
    s = qkv.shape[0]
    t = _tile(s, ATT_K)
    nq = s // t
    hp = _pairing(heads)
    ev = range(hp)
    scale = SB_DIM ** -0.5

    def body(*refs):
        (q_ref, k_ref, v_ref, dy_ref), (dq_ref, dk_ref, dv_ref), (g_s, beta_s), xrefs = _split_refs(refs, 4, 3, carry)
        if carry is not None:
            _carried_exchange(xrefs, carry[1], *_first_last(heads // hp, nq))
        qi = pl.program_id(1)

        @pl.when(qi == 0)
        def _():
            dk_ref[...] = jnp.zeros_like(dk_ref)
            dv_ref[...] = jnp.zeros_like(dv_ref)

        qs = [q_ref[:, _head_lanes(e)] for e in ev]
        dy_b = [dy_ref[:, _head_lanes(e)].astype(BF16) for e in ev]
        rows = lax.broadcasted_iota(jnp.int32, (t, t), 0)
        cols = lax.broadcasted_iota(jnp.int32, (t, t), 1)
        after = (rows > cols).astype(BF16)
        before = (rows < cols).astype(BF16)

        def pass1(kb, tails, masked):
            off = pl.multiple_of(kb * t, t)
            mask = _visible(qi, kb, t, t, True) if masked else None
            ks = [k_ref[pl.ds(off, t), _head_lanes(e)] for e in ev]
            vs = [v_ref[pl.ds(off, t), _head_lanes(e)] for e in ev]
            terms = [_sb_terms(qs[e], ks[e], scale, mask) for e in ev]
            da = [_dot_nt(dy_b[e], vs[e]) for e in ev]
            sums = [_split_dot(terms[e][1], after) for e in ev]
            a = [jnp.exp(terms[e][0] + (sums[e] + tails[e])) for e in ev]
            beta = [jnp.exp(terms[e][0]) for e in ev]
            if masked:
                a = [jnp.where(mask, a[e], 0.0) for e in ev]
                beta = [jnp.where(mask, beta[e], 0.0) for e in ev]
            for e in ev:
                dv_ref[pl.ds(off, t), _head_lanes(e)] += _dot_tn(a[e].astype(BF16), dy_b[e])
            for e in ev:
                g_s[kb, e] = a[e] * da[e]
                beta_s[kb, e] = beta[e]
            return tuple(tails[e] + jnp.sum(terms[e][1], axis=-1, keepdims=True) for e in ev)

        zeros = tuple(jnp.zeros((t, 1), F32) for _ in ev)
        tails = pass1(qi, zeros, True)
        lax.fori_loop(0, qi, lambda i, t_: pass1(qi - 1 - i, t_, False), tails)

        dq_ref[...] = jnp.zeros_like(dq_ref)

        def pass2(kb, lefts):
            off = pl.multiple_of(kb * t, t)
            g = [g_s[kb, e] for e in ev]
            beta = [beta_s[kb, e] for e in ev]
            g_before = [_split_dot(g[e], before) + lefts[e] for e in ev]
            dz = [((g[e] * (1.0 - beta[e]) - g_before[e] * beta[e]) * scale).astype(BF16) for e in ev]
            for e in ev:
                dk_ref[pl.ds(off, t), _head_lanes(e)] += _dot_tn(dz[e], qs[e])
            for e in ev:
                dq_ref[:, _head_lanes(e)] += _dot(dz[e], k_ref[pl.ds(off, t), _head_lanes(e)])
            return tuple(lefts[e] + jnp.sum(g[e], axis=-1, keepdims=True) for e in ev)

        lax.fori_loop(0, qi + 1, pass2, zeros)

    w = hp * SB_DIM
    args = [qkv, qkv, qkv, dy]
    in_specs = _sb_specs(s, t, off_q, off_k, off_v, hp) + [pl.BlockSpec((t, w), lambda h, i: (i, h))]
    out_shape = [jax.ShapeDtypeStruct((s, heads * SB_DIM), F32)] * 3
    out_specs = [pl.BlockSpec((t, w), lambda h, i: (i, h)), pl.BlockSpec((s, w), lambda h, i: (0, h)),
                 pl.BlockSpec((s, w), lambda h, i: (0, h))]
    scratch = [pltpu.VMEM((nq, hp, t, t), F32), pltpu.VMEM((nq, hp, t, t), F32)]
    _with_carry(carry, args, in_specs, out_shape, out_specs, scratch)
    return pl.pallas_call(
        body, name="sb_bwd", grid=(heads // hp, nq), out_shape=tuple(out_shape), in_specs=in_specs,
        out_specs=tuple(out_specs), scratch_shapes=scratch, compiler_params=_params(),
    )(*args)


def _slot_sum(recv, name):
    _, r, w = recv.shape
    tr = _tile(r, 256)

    def body(r_ref, o_ref):
        acc = r_ref[0].astype(F32)
        for d in range(1, N_DEV):
            acc = acc + r_ref[d].astype(F32)
        o_ref[...] = acc

    return pl.pallas_call(
        body, name=name, grid=(r // tr,),
        out_shape=jax.ShapeDtypeStruct((r, w), F32),
        in_specs=[pl.BlockSpec((N_DEV, tr, w), lambda i: (0, i, 0))],
        out_specs=pl.BlockSpec((tr, w), lambda i: (i, 0)), compiler_params=_params(),
    )(recv)


def _small_sum(rows, loss_lo, loss_hi, d_model):
    n = rows.shape[1]

    def body(r_ref, o_ref, loss_ref):
        rv = r_ref[...]
        acc = rv[0:1, :]
        for d in range(1, N_DEV):
            acc = acc + rv[d:d + 1, :]
        o_ref[...] = acc
        total = jnp.sum(acc[:, loss_lo:loss_hi], axis=-1, keepdims=True) * (0.5 / d_model)
        loss_ref[...] = jnp.broadcast_to(total, (1, LANES))

    return pl.pallas_call(
        body, name="small_sum",
        out_shape=(jax.ShapeDtypeStruct((1, n), F32), jax.ShapeDtypeStruct((1, LANES), F32)),
        compiler_params=_params(),
    )(rows)


def _adamw(w, g, m, v, name):
    r, c = w.shape
    tr = _tile(r, max(8, (1 << 18) // c // 8 * 8))

    def body(w_ref, g_ref, m_ref, v_ref, d_ref, mo_ref, vo_ref):
        gv = g_ref[...]
        m_new = ADAM_B1 * m_ref[...] + (1.0 - ADAM_B1) * gv
        v_new = ADAM_B2 * v_ref[...] + (1.0 - ADAM_B2) * (gv * gv)
        m_hat = m_new / (1.0 - ADAM_B1 ** ADAM_STEP)
        v_hat = v_new / (1.0 - ADAM_B2 ** ADAM_STEP)
        d_ref[...] = -ADAM_LR * (m_hat / (jnp.sqrt(v_hat) + ADAM_EPS) + ADAM_WD * w_ref[...])
        mo_ref[...] = m_new
        vo_ref[...] = v_new

    spec = pl.BlockSpec((tr, c), lambda i: (i, 0))
    sd = jax.ShapeDtypeStruct((r, c), F32)
    return pl.pallas_call(
        body, name=name, grid=(r // tr,), out_shape=(sd, sd, sd),
        in_specs=[spec] * 4, out_specs=(spec,) * 3, compiler_params=_params(),
    )(w, g, m, v)


def _pack_rows(a):
    return a.reshape(-1, PACK_W)


def _unshard(slots, shape, col_sharded):
    r, c = shape
    if col_sharded:
        return slots.reshape(N_DEV, r, c).transpose(1, 0, 2).reshape(r, N_DEV * c)
    return slots.reshape(N_DEV * r, c)


def _to_shards(full, col_sharded, packed=True):
    r, c = full.shape
    shards = (full.reshape(r, N_DEV, c // N_DEV).transpose(1, 0, 2) if col_sharded
              else full.reshape(N_DEV, r // N_DEV, c))
    return shards.reshape(N_DEV, -1, PACK_W) if packed else shards


def kernel(x, c, positions, w_ada, b_ada, g_norm1, g_norm2, w_in, g_q_latent, g_kv_latent, w_uq, w_ukv, g_q_head, g_k_head, w_proj_mla, w_proj_sb, w_out, w_ffn_in, w_ffn_out, loss_target, m_w_ada, m_b_ada, m_g_norm1, m_g_norm2, m_w_in, m_g_q_latent, m_g_kv_latent, m_w_uq, m_w_ukv, m_g_q_head, m_g_k_head, m_w_proj_mla, m_w_proj_sb, m_w_out, m_w_ffn_in, m_w_ffn_out, v_w_ada, v_b_ada, v_g_norm1, v_g_norm2, v_w_in, v_g_q_latent, v_g_kv_latent, v_w_uq, v_w_ukv, v_g_q_head, v_g_k_head, v_w_proj_mla, v_w_proj_sb, v_w_out, v_w_ffn_in, v_w_ffn_out):
    env = dict(locals())
    drop = lambda a: a[0] if a.ndim == 3 else a
    wts = {n: drop(env[n]) for n in WEIGHT_NAMES}
    mom = {n: drop(env["m_" + n]) for n in WEIGHT_NAMES}
    var = {n: drop(env["v_" + n]) for n in WEIGHT_NAMES}
    xs, tgt, pos = x[0], loss_target[0], positions[0]
    s, d = xs.shape
    lat = wts["w_uq"].shape[0]
    assert wts["w_ukv"].shape[0] == lat
    h_mla = wts["w_uq"].shape[1] * N_DEV // QK_DIM
    sb_w = wts["w_proj_sb"].shape[0]
    h_sb = sb_w // SB_DIM
    d_ff = wts["w_ffn_out"].shape[0] * N_DEV
    me = 4 * lax.axis_index("x") + 2 * lax.axis_index("y") + lax.axis_index("c")

    ref_w = (("c_q", lat), ("c_kv", lat), ("k_pe", ROPE), ("q_sb", sb_w), ("k_sb", sb_w), ("v_sb", sb_w),
             ("gl_a", d), ("gl_b", d))
    ref_off, o = {}, 0
    for n_, w_ in ref_w:
        ref_off[n_] = (o, w_)
        o += w_
    order = ("gl_a", "gl_b", "q_sb", "k_sb", "v_sb", "c_q", "c_kv", "k_pe")
    off, o = {}, 0
    for n_ in order:
        w_ = LANES if n_ == "k_pe" else ref_off[n_][1]
        assert o % w_ == 0
        off[n_] = o
        o += w_

    used_w = o
    proj_w = -(-used_w // (2 * LANES)) * (2 * LANES)

    col_sharded = dict(BIG)
    rows_of = {n: wts[n].size // PACK_W for n, _ in BIG}
    full, grads = {}, {}

    def own_shape(names):
        return len(names) == 1

    def pack_weights(names):
        if own_shape(names):
            return wts[names[0]].astype(BF16)
        return jnp.concatenate([_pack_rows(wts[n].astype(BF16)) for n in names], axis=0)

    def unpack_weights(slots, names):
        r0 = 0
        for n in names:
            part = slots if own_shape(names) else slots[:, r0:r0 + rows_of[n]]
            full[n] = _unshard(part, wts[n].shape, col_sharded[n])
            r0 += rows_of[n]

    def pack_grads(names):
        if own_shape(names):
            return _to_shards(grads[names[0]].astype(BF16), col_sharded[names[0]], packed=False)
        return jnp.concatenate([_to_shards(grads[n].astype(BF16), col_sharded[n]) for n in names], axis=1)

    def unpack_grads(recv, names):
        summed_rows, r0 = _slot_sum(recv, "slot_sum_" + names[0]), 0
        for n in names:
            part = summed_rows if own_shape(names) else summed_rows[r0:r0 + rows_of[n]]
            grads[n] = part.reshape(wts[n].shape)
            r0 += rows_of[n]

    unpack_weights(_gather_two_level(pack_weights(("w_in",)), "gather_w_in"), ("w_in",))
    seg = lambda a, n_: a[:, ref_off[n_][0]:ref_off[n_][0] + ref_off[n_][1]]
    w_in_k = jnp.concatenate([seg(full["w_in"], n_) for n_ in order]
                             + [jnp.zeros((d, proj_w - used_w + LANES - ROPE), BF16)], axis=1)
    pad_gain = lambda g: jnp.pad(g, ((0, 0), (0, QK_PAD - QK_DIM)))
    g_qh, g_kh = pad_gain(wts["g_q_head"]), pad_gain(wts["g_k_head"])

    c_all = _gather_rows(c, "gather_c")
    ada_cols = _exchange(_ada_fwd(c_all, wts["w_ada"]), gather=True, name="gather_ada")
    ada = lax.dynamic_index_in_dim(ada_cols, me, axis=1, keepdims=False).reshape(1, 6 * d) + wts["b_ada"]
    sh1, sc1, gt1, sh2, sc2, gt2 = [ada[:, i * d:(i + 1) * d] for i in range(6)]

    half = ROPE // 2
    ang = pos.astype(F32)[:, None] * (ROPE_THETA ** (-jnp.arange(half, dtype=F32) / half))
    zeros = jnp.zeros((s, LANES - ROPE), F32)
    cos_t = jnp.concatenate([jnp.cos(ang), jnp.cos(ang), zeros], axis=1)
    sin_t = jnp.concatenate([-jnp.sin(ang), jnp.sin(ang), zeros], axis=1)

    h1 = _norm_mod(xs, wts["g_norm1"], sc1, sh1)
    mixer_w = ("w_uq", "w_ukv", "w_proj_mla", "w_proj_sb", "w_out")
    proj, slots = _mm(h1, w_in_k, name="mm_in", carry=(pack_weights(mixer_w), True))
    unpack_weights(slots, mixer_w)
    w_uq_k = jnp.pad(full["w_uq"].reshape(lat, h_mla, QK_DIM), ((0, 0), (0, 0), (0, QK_PAD - QK_DIM))
                     ).reshape(lat, h_mla * QK_PAD)
    cqn, ckvn = _latent_norm(proj, off["c_q"], off["c_kv"], lat, wts["g_q_latent"], wts["g_kv_latent"])
    q_raw = _mm(cqn, w_uq_k, name="mm_uq")
    kv = _mm(ckvn, full["w_ukv"], name="mm_ukv")
    q, k, v = _qk_prep(q_raw, kv, proj, off["k_pe"], cos_t, sin_t, g_qh, g_kh)
    y_a, lse, slots = _mla_fwd(q, k, v, carry=(pack_weights(("w_ffn_out",)), True))
    unpack_weights(slots, ("w_ffn_out",))
    assert off["k_sb"] == off["q_sb"] + sb_w and off["v_sb"] == off["k_sb"] + sb_w
    qkv_sb = proj[:, off["q_sb"]:off["q_sb"] + 3 * sb_w].astype(BF16)
    y_b, w_fi_slots = _sb_fwd(qkv_sb, 0, sb_w, 2 * sb_w, h_sb,
                              carry=(wts["w_ffn_in"].astype(BF16), True))
    ya_p = _mm(y_a, full["w_proj_mla"], name="mm_proj_mla")
    yb_p = _mm(y_b, full["w_proj_sb"], name="mm_proj_sb")
    merged = _merge(proj, off["gl_a"], off["gl_b"], ya_p, yb_p)
    o1 = _mm(merged, full["w_out"], name="mm_out")
    x1, h2 = _resid_norm_mod(xs, o1, gt1, wts["g_norm2"], sc2, sh2)
    gu = _mm(h2, w_fi_slots, shards="b", out_dtype=BF16, name="mm_ffn_in")
    act = _swiglu(gu)
    o2 = _mm(act, full["w_ffn_out"], name="mm_ffn_out")
    dy, d_o2, sums_l = _loss_head(x1, o2, gt2, tgt)

    recv = {}
    d_act = _mm(d_o2, full["w_ffn_out"], tb=True, name="mm_d_act")
    grads["w_ffn_out"] = _mm(act, d_o2, ta=True, out_dtype=BF16, name="mm_g_ffn_out")
    d_gu = _swiglu_bwd(gu, d_act)
    g_fi_slots = _mm(h2, d_gu, ta=True, shards="out", out_dtype=BF16, name="mm_g_ffn_in")
    d_h2, recv["w_ffn_out",] = _mm(d_gu, w_fi_slots, tb=True, shards="b", name="mm_d_h2",
                                   carry=(pack_grads(("w_ffn_out",)), False))
    d_x1, d_o1, sums_2 = _norm_mod_bwd(d_h2, x1, wts["g_norm2"], sc2, dy, gate=(o1, gt1))
    grads["w_out"] = _mm(merged, d_o1, ta=True, out_dtype=BF16, name="mm_g_out")
    d_merged = _mm(d_o1, full["w_out"], tb=True, name="mm_d_merged")
    d_yap, d_ybp, d_gla, d_glb = _merge_bwd(proj, off["gl_a"], off["gl_b"], ya_p, yb_p, d_merged)
    grads["w_proj_mla"] = _mm(y_a, d_yap, ta=True, out_dtype=BF16, name="mm_g_proj_mla")
    grads["w_proj_sb"] = _mm(y_b, d_ybp, ta=True, out_dtype=BF16, name="mm_g_proj_sb")
    d_ya = _mm(d_yap, full["w_proj_mla"], tb=True, name="mm_d_ya")
    d_yb = _mm(d_ybp, full["w_proj_sb"], tb=True, name="mm_d_yb")
    dq_sb, dk_sb, dv_sb, recv["w_ffn_in",] = _sb_bwd(qkv_sb, 0, sb_w, 2 * sb_w, h_sb, d_yb,
                                                     carry=(g_fi_slots, False))
    merge_w = ("w_out", "w_proj_mla", "w_proj_sb")
    dq, dk, dv, recv[merge_w] = _mla_bwd(q, k, v, y_a, lse, d_ya, carry=(pack_grads(merge_w), False))
    d_qraw, d_kv, d_kpe, sums_h = _qk_prep_bwd(q_raw, kv, proj, off["k_pe"], cos_t, sin_t, g_qh, g_kh, dq, dk, dv)
    g_uq_k = _mm(cqn, d_qraw, ta=True, out_dtype=BF16, name="mm_g_uq")
    grads["w_uq"] = g_uq_k.reshape(lat, h_mla, QK_PAD)[:, :, :QK_DIM].reshape(lat, h_mla * QK_DIM)
    grads["w_ukv"] = _mm(ckvn, d_kv, ta=True, out_dtype=BF16, name="mm_g_ukv")
    d_cqn = _mm(d_qraw, w_uq_k, tb=True, name="mm_d_cqn")
    d_ckvn = _mm(d_kv, full["w_ukv"], tb=True, name="mm_d_ckvn")
    d_cq, d_ckv, sums_lat = _latent_norm_bwd(proj, off["c_q"], off["c_kv"], lat, wts["g_q_latent"],
                                             wts["g_kv_latent"], d_cqn, d_ckvn)
    d_parts = {"gl_a": d_gla, "gl_b": d_glb, "q_sb": dq_sb, "k_sb": dk_sb, "v_sb": dv_sb, "c_q": d_cq, "c_kv": d_ckv,
               "k_pe": d_kpe}
    d_proj = jnp.concatenate([d_parts[n_].astype(BF16) for n_ in order]
                             + ([jnp.zeros((s, proj_w - used_w), BF16)] if proj_w > used_w else []), axis=1)
    latent_w = ("w_uq", "w_ukv")
    g_in_k, recv[latent_w] = _mm(h1, d_proj, ta=True, out_dtype=BF16, name="mm_g_in",
                                 carry=(pack_grads(latent_w), False))
    grads["w_in"] = jnp.concatenate([g_in_k[:, off[n_]:off[n_] + w_] for n_, w_ in ref_w], axis=1)
    d_h1, recv["w_in",] = _mm(d_proj, w_in_k, tb=True, name="mm_d_h1", carry=(pack_grads(("w_in",)), False))
    grad_x, sums_1 = _norm_mod_bwd(d_h1, xs, wts["g_norm1"], sc1, d_x1)

    parts = [sums_1[0:1], sums_1[1:2], sums_2[3:4], sums_2[0:1], sums_2[1:2], sums_l[0:1],
             sums_1[2:3], sums_2[2:3], sums_lat[0:1], sums_lat[1:2], sums_h[0:1], sums_h[1:2], sums_l[1:2]]
    part_off, o = [], 0
    for p in parts:
        part_off.append(o)
        o += p.shape[1]
    all_rows = _gather_rows(jnp.concatenate(parts, axis=1), "gather_small")
    summed, loss_row = _small_sum(all_rows, part_off[12], part_off[12] + d, d)
    take = lambda i, w: summed[:, part_off[i]:part_off[i] + w]
    grads["b_ada"] = summed[:, :6 * d]
    grads["g_norm1"], grads["g_norm2"] = take(6, d), take(7, d)
    grads["g_q_latent"], grads["g_kv_latent"] = take(8, lat), take(9, lat)
    grads["g_q_head"], grads["g_k_head"] = take(10, QK_DIM), take(11, QK_DIM)
    n_ada = 6 * d // N_DEV
    d_ada_mine = lax.dynamic_slice_in_dim(all_rows[:, :6 * d], me * n_ada, n_ada, axis=1)
    grads["w_ada"] = _ada_grad(c_all.T, d_ada_mine)

    for names, slots in recv.items():
        unpack_grads(slots, names)

    delta, new_m, new_v = {}, {}, {}
    for n in ("w_ada",) + tuple(n for n, _ in BIG):
        delta[n], new_m[n], new_v[n] = _adamw(wts[n], grads[n], mom[n], var[n], "adamw_" + n)
    cat = lambda t: jnp.concatenate([t[n] for n in SMALL], axis=1)
    d_s, m_s, v_s = _adamw(cat(wts), cat(grads), cat(mom), cat(var), "adamw_small")
    o = 0
    for n in SMALL:
        w_ = wts[n].shape[1]
        delta[n], new_m[n], new_v[n] = d_s[:, o:o + w_], m_s[:, o:o + w_], v_s[:, o:o + w_]
        o += w_

    lead = lambda t: [t[n].reshape(env[n].shape) for n in WEIGHT_NAMES]
    return (loss_row[0, 0], grad_x[None], *lead(grads), *lead(delta), *lead(new_m), *lead(new_v))
```

```python
import jax
import jax.numpy as jnp
from jax import lax
from jax.experimental import pallas as pl
from jax.experimental.pallas import tpu as pltpu

F32 = jnp.float32
BF16 = jnp.bfloat16

N_DEV = 8
LANES = 128
PACK_W = 1024
VMEM_LIMIT = 48 * 1024 * 1024

EPS = 1e-6
ROPE_THETA = 10000.0
NOPE = 128
ROPE = 64
QK_DIM = NOPE + ROPE
QK_PAD = 2 * LANES
V_DIM = 128
SB_DIM = 128
ATT_Q = 512
ATT_K = 256
MM_TK = 2816
PAIR = 2
ROW_PARTS = 2

ADAM_LR = 0.001
ADAM_B1 = 0.9
ADAM_B2 = 0.999
ADAM_EPS = 1e-08
ADAM_WD = 0.01
ADAM_STEP = 10

WEIGHT_NAMES = ("w_ada", "b_ada", "g_norm1", "g_norm2", "w_in", "g_q_latent", "g_kv_latent", "w_uq", "w_ukv",
                "g_q_head", "g_k_head", "w_proj_mla", "w_proj_sb", "w_out", "w_ffn_in", "w_ffn_out")
BIG = (("w_in", True), ("w_uq", True), ("w_ukv", True), ("w_proj_mla", True), ("w_proj_sb", True),
       ("w_out", False), ("w_ffn_in", True), ("w_ffn_out", False))
SMALL = ("b_ada", "g_norm1", "g_norm2", "g_q_latent", "g_kv_latent", "g_q_head", "g_k_head")


def _tile(n, pref):
    if n <= pref:
        return n
    for step in (LANES, 8):
        for t in range(pref - pref % step, 0, -step):
            if n % t == 0:
                return t
    return n


def _params():
    return pltpu.CompilerParams(vmem_limit_bytes=VMEM_LIMIT)


def _sigmoid(x):
    return 1.0 / (1.0 + jnp.exp(-x))


def _dot(a, b):
    return lax.dot_general(a, b, (((1,), (0,)), ((), ())), preferred_element_type=F32)


def _dot_nt(a, b):
    return lax.dot_general(a, b, (((1,), (1,)), ((), ())), preferred_element_type=F32)


def _dot_tn(a, b):
    return lax.dot_general(a, b, (((0,), (0,)), ((), ())), preferred_element_type=F32)


def _split_dot(x, tri):
    hi = x.astype(BF16)
    lo = (x - hi.astype(F32)).astype(BF16)
    return _dot(hi, tri) + _dot(lo, tri)


_HBM = pl.BlockSpec(memory_space=pltpu.HBM)


def _carry_parts(carry):
    src, gather = carry
    slot_shape = tuple(src.shape) if gather else tuple(src.shape[1:])
    return (src, _HBM, jax.ShapeDtypeStruct((N_DEV,) + slot_shape, src.dtype), _HBM,
            [pltpu.SemaphoreType.DMA((N_DEV - 1,)), pltpu.SemaphoreType.DMA((N_DEV - 1,)), pltpu.SemaphoreType.DMA(())])


def _exchange_copies(src_ref, dst_ref, send_sems, recv_sems, local_sem, gather):
    x, y, c = lax.axis_index("x"), lax.axis_index("y"), lax.axis_index("c")
    me = 4 * x + 2 * y + c

    def slot_for(idx):
        return src_ref if gather else src_ref.at[idx]

    copies = [pltpu.make_async_copy(slot_for(me), dst_ref.at[me], local_sem)]
    for k in range(1, N_DEV):
        peer = ((1 - x) if (k >> 2) & 1 else x, (1 - y) if (k >> 1) & 1 else y, (1 - c) if k & 1 else c)
        peer_idx = 4 * peer[0] + 2 * peer[1] + peer[2]
        copies.append(pltpu.make_async_remote_copy(
            src_ref=slot_for(peer_idx), dst_ref=dst_ref.at[me],
            send_sem=send_sems.at[k - 1], recv_sem=recv_sems.at[k - 1],
            device_id=peer, device_id_type=pl.DeviceIdType.MESH))
    return copies


def _two_level_copies(src_ref, dst_ref, send_sems, recv_sems, local_sem):
    x, y, c = lax.axis_index("x"), lax.axis_index("y"), lax.axis_index("c")
    me, sibling = (x, y, c), (x, y, 1 - c)
    chips = [(1 - x, y), (x, 1 - y), (1 - x, 1 - y)]

    def slot(px, py, pc):
        return dst_ref.at[4 * px + 2 * py + pc]

    def copy(k, block, to, own=False):
        return pltpu.make_async_remote_copy(
            src_ref=src_ref if own else slot(*block), dst_ref=slot(*block),
            send_sem=send_sems.at[k], recv_sem=recv_sems.at[k],
            device_id=to, device_id_type=pl.DeviceIdType.MESH)

    mine = pltpu.make_async_copy(src_ref, slot(*me), local_sem)
    first = [copy(0, me, sibling, own=True)] + [copy(1 + j, me, (*chip, c), own=True) for j, chip in enumerate(chips)]
    relays = [(copy(1 + j, (*chip, c), me), copy(4 + j, (*chip, c), sibling)) for j, chip in enumerate(chips)]
    late = [copy(0, sibling, me)] + [copy(4 + j, (*chip, 1 - c), me) for j, chip in enumerate(chips)]
    return mine, first, relays, late


def _two_level_start(refs):
    mine, first, _, _ = _two_level_copies(*refs)
    mine.start()
    for cp in first:
        cp.start()


def _two_level_finish(refs):
    mine, first, relays, late = _two_level_copies(*refs)
    for arrival, onward in relays:
        arrival.wait_recv()
        onward.start()
    for cp in late:
        cp.wait_recv()
    for cp in first + [onward for _, onward in relays]:
        cp.wait_send()
    mine.wait()


def _carried_exchange(refs, gather, first, last):
    @pl.when(first)
    def _():
        if gather == "two_level":
            _two_level_start(refs)
        else:
            for cp in _exchange_copies(*refs, gather):
                cp.start()

    @pl.when(last)
    def _():
        if gather == "two_level":
            _two_level_finish(refs)
        else:
            for cp in _exchange_copies(*refs, gather):
                cp.wait()


def _exchange(src, *, gather, name):
    operand, in_spec, out_shape, out_spec, scratch = _carry_parts((src, gather))

    def body(src_ref, dst_ref, send_sems, recv_sems, local_sem):
        copies = _exchange_copies(src_ref, dst_ref, send_sems, recv_sems, local_sem, gather)
        for cp in copies:
            cp.start()
        for cp in copies:
            cp.wait()

    return pl.pallas_call(body, name=name, out_shape=out_shape, in_specs=[in_spec], out_specs=out_spec,
                          scratch_shapes=scratch)(operand)


def _gather_two_level(src, name):
    operand, in_spec, out_shape, out_spec, scratch = _carry_parts((src, True))

    def body(*refs):
        _two_level_start(refs)
        _two_level_finish(refs)

    return pl.pallas_call(body, name=name, out_shape=out_shape, in_specs=[in_spec], out_specs=out_spec,
                          scratch_shapes=scratch)(operand)


def _gather_rows(v, name):
    n = v.shape[1]
    padded = -(-n // (8 * LANES)) * (8 * LANES)
    tiles = jnp.pad(v, ((0, 0), (0, padded - n))).reshape(padded // LANES, LANES)
    return _exchange(tiles, gather=True, name=name).reshape(N_DEV, padded)[:, :n]


def _split_refs(refs, n_in, n_out, carry):
    if carry is None:
        return refs[:n_in], refs[n_in:n_in + n_out], refs[n_in + n_out:], None
    ins, src_ref = refs[:n_in], refs[n_in]
    outs, dst_ref = refs[n_in + 1:n_in + 1 + n_out], refs[n_in + 1 + n_out]
    rest = refs[n_in + n_out + 2:]
    return ins, outs, rest[:-3], (src_ref, dst_ref) + tuple(rest[-3:])


def _with_carry(carry, args, in_specs, out_shape, out_specs, scratch):
    if carry is not None:
        operand, c_in, c_shape, c_out, c_scratch = _carry_parts(carry)
        args.append(operand)
        in_specs.append(c_in)
        out_shape.append(c_shape)
        out_specs.append(c_out)
        scratch.extend(c_scratch)


def _slot_of_chunk(j):
    return j // 2 + (N_DEV // 2) * (j % 2)


def _mm(a, b, *, ta=False, tb=False, out_dtype=F32, name, carry=None, shards=None):
    kdim, m = a.shape if ta else a.shape[::-1]
    if shards == "b":
        chunk = b.shape[2]
        n, kdim_b = (b.shape[1], N_DEV * chunk) if tb else (N_DEV * chunk, b.shape[1])
    else:
        n, kdim_b = b.shape if tb else b.shape[::-1]
    assert kdim == kdim_b, (a.shape, b.shape, ta, tb)
    tm, tn, tk = _tile(m, 1024), _tile(n, 1024), _tile(kdim, MM_TK)
    if shards == "b":
        tn, tk = (tn, chunk) if tb else (chunk, tk)
    elif shards == "out":
        chunk = tn = n // N_DEV
    grid = (m // tm, n // tn, kdim // tk)
    nk = grid[2]
    dims = (((0 if ta else 1,), (1 if tb else 0,)), ((), ()))

    def body(*refs):
        (a_ref, b_ref), (o_ref,), scratch_refs, xrefs = _split_refs(refs, 2, 1, carry)
        i, j, k = pl.program_id(0), pl.program_id(1), pl.program_id(2)
        if carry is not None:
            _carried_exchange(xrefs, carry[1], (i == 0) & (j == 0) & (k == 0),
                              (i == grid[0] - 1) & (j == grid[1] - 1) & (k == nk - 1))
        prod = lax.dot_general(a_ref[...].astype(BF16), b_ref[...].astype(BF16), dims, preferred_element_type=F32)
        if nk == 1:
            o_ref[...] = prod.astype(o_ref.dtype)
        else:
            acc_ref = scratch_refs[0]

            @pl.when(k == 0)
            def _():
                acc_ref[...] = prod

            @pl.when(k > 0)
            def _():
                acc_ref[...] += prod

            @pl.when(k == nk - 1)
            def _():
                o_ref[...] = acc_ref[...].astype(o_ref.dtype)

    a_spec = (pl.BlockSpec((tk, tm), lambda i, j, k: (k, i)) if ta else pl.BlockSpec((tm, tk), lambda i, j, k: (i, k)))
    if shards == "b":
        b_spec = (pl.BlockSpec((None, tn, tk), lambda i, j, k: (_slot_of_chunk(k), j, 0)) if tb
                  else pl.BlockSpec((None, tk, tn), lambda i, j, k: (_slot_of_chunk(j), k, 0)))
    else:
        b_spec = (pl.BlockSpec((tn, tk), lambda i, j, k: (j, k)) if tb else pl.BlockSpec((tk, tn), lambda i, j, k: (k, j)))
    args, in_specs = [a, b], [a_spec, b_spec]
    if shards == "out":
        out_shape = [jax.ShapeDtypeStruct((N_DEV, m, tn), out_dtype)]
        out_specs = [pl.BlockSpec((None, tm, tn), lambda i, j, k: (_slot_of_chunk(j), i, 0))]
    else:
        out_shape, out_specs = [jax.ShapeDtypeStruct((m, n), out_dtype)], [pl.BlockSpec((tm, tn), lambda i, j, k: (i, j))]
    scratch = [] if nk == 1 else [pltpu.VMEM((tm, tn), F32)]
    _with_carry(carry, args, in_specs, out_shape, out_specs, scratch)
    out = pl.pallas_call(
        body, name=name, grid=grid, out_shape=tuple(out_shape), in_specs=in_specs, out_specs=tuple(out_specs),
        scratch_shapes=scratch, compiler_params=_params(),
    )(*args)
    return out if carry is not None else out[0]


def _ada_fwd(c_all, w_shard):
    d, n = w_shard.shape
    tn = _tile(n, 512)

    def body(c_ref, w_ref, o_ref):
        cv = c_ref[...]
        o_ref[...] = jnp.dot(cv * _sigmoid(cv), w_ref[...], precision=lax.Precision.HIGHEST,
                             preferred_element_type=F32)

    return pl.pallas_call(
        body, name="ada_fwd", grid=(n // tn,),
        out_shape=jax.ShapeDtypeStruct((N_DEV, n), F32),
        in_specs=[pl.BlockSpec((N_DEV, d), lambda j: (0, 0)), pl.BlockSpec((d, tn), lambda j: (0, j))],
        out_specs=pl.BlockSpec((N_DEV, tn), lambda j: (0, j)),
        compiler_params=_params(),
    )(c_all, w_shard)


def _ada_grad(c_all_t, d_rows):
    d, n = c_all_t.shape[0], d_rows.shape[1]
    tn = _tile(n, 512)

    def body(ct_ref, d_ref, o_ref):
        cv = ct_ref[...]
        s = cv * _sigmoid(cv)
        dv = d_ref[...]
        acc = s[:, 0:1] * dv[0:1, :]
        for b in range(1, N_DEV):
            acc = acc + s[:, b:b + 1] * dv[b:b + 1, :]
        o_ref[...] = acc

    return pl.pallas_call(
        body, name="ada_grad", grid=(n // tn,),
        out_shape=jax.ShapeDtypeStruct((d, n), F32),
        in_specs=[pl.BlockSpec((d, N_DEV), lambda j: (0, 0)), pl.BlockSpec((N_DEV, tn), lambda j: (0, j))],
        out_specs=pl.BlockSpec((d, tn), lambda j: (0, j)),
        compiler_params=_params(),
    )(c_all_t, d_rows)


def _row(ts, w, col=0):
    return pl.BlockSpec((ts, w), lambda i, col=col: (i, col))


def _vec(w):
    return pl.BlockSpec((1, w), lambda i: (0, 0))


def _norm_mod(x, g, sc, sh):
    s, d = x.shape
    ts = _tile(s, 512)

    def body(x_ref, g_ref, sc_ref, sh_ref, h_ref):
        xv = x_ref[...]
        r = lax.rsqrt(jnp.mean(xv * xv, axis=-1, keepdims=True) + EPS)
        h_ref[...] = ((xv * r) * g_ref[...] * (1.0 + sc_ref[...]) + sh_ref[...]).astype(BF16)

    return pl.pallas_call(
        body, name="norm_mod", grid=(s // ts,),
        out_shape=jax.ShapeDtypeStruct((s, d), BF16),
        in_specs=[_row(ts, d), _vec(d), _vec(d), _vec(d)],
        out_specs=_row(ts, d), compiler_params=_params(),
    )(x, g, sc, sh)


def _latent_norm(proj, off_q, off_kv, lat, g_q, g_kv):
    s = proj.shape[0]
    ts = _tile(s, 512)

    def body(cq_ref, ckv_ref, gq_ref, gkv_ref, oq_ref, okv_ref):
        for c_ref, g_ref, o_ref in ((cq_ref, gq_ref, oq_ref), (ckv_ref, gkv_ref, okv_ref)):
            v = c_ref[...]
            r = lax.rsqrt(jnp.mean(v * v, axis=-1, keepdims=True) + EPS)
            o_ref[...] = ((v * r) * g_ref[...]).astype(BF16)

    return pl.pallas_call(
        body, name="latent_norm", grid=(s // ts,),
        out_shape=(jax.ShapeDtypeStruct((s, lat), BF16), jax.ShapeDtypeStruct((s, lat), BF16)),
        in_specs=[_row(ts, lat, off_q // lat), _row(ts, lat, off_kv // lat), _vec(lat), _vec(lat)],
        out_specs=(_row(ts, lat), _row(ts, lat)), compiler_params=_params(),
    )(proj, proj, g_q, g_kv)


def _latent_norm_bwd(proj, off_q, off_kv, lat, g_q, g_kv, d_cqn, d_ckvn):
    s = proj.shape[0]
    ts = _tile(s, 512)

    def body(cq_ref, ckv_ref, gq_ref, gkv_ref, dq_ref, dkv_ref, oq_ref, okv_ref, sums_ref):
        @pl.when(pl.program_id(0) == 0)
        def _():
            sums_ref[...] = jnp.zeros_like(sums_ref)

        for row, (c_ref, g_ref, d_ref, o_ref) in enumerate(((cq_ref, gq_ref, dq_ref, oq_ref),
                                                             (ckv_ref, gkv_ref, dkv_ref, okv_ref))):
            v = c_ref[...]
            r = lax.rsqrt(jnp.mean(v * v, axis=-1, keepdims=True) + EPS)
            vn = v * r
            dn = d_ref[...]
            sums_ref[row:row + 1, :] += jnp.sum(dn * vn, axis=0, keepdims=True)
            dvn = dn * g_ref[...]
            o_ref[...] = (r * (dvn - vn * jnp.mean(dvn * vn, axis=-1, keepdims=True))).astype(BF16)

    return pl.pallas_call(
        body, name="latent_norm_bwd", grid=(s // ts,),
        out_shape=(jax.ShapeDtypeStruct((s, lat), BF16), jax.ShapeDtypeStruct((s, lat), BF16),
                   jax.ShapeDtypeStruct((8, lat), F32)),
        in_specs=[_row(ts, lat, off_q // lat), _row(ts, lat, off_kv // lat), _vec(lat), _vec(lat),
                  _row(ts, lat), _row(ts, lat)],
        out_specs=(_row(ts, lat), _row(ts, lat), pl.BlockSpec((8, lat), lambda i: (0, 0))),
        compiler_params=_params(),
    )(proj, proj, g_q, g_kv, d_cqn, d_ckvn)


def _merge(proj, off_a, off_b, ya_p, yb_p):
    s, d = ya_p.shape
    ts = _tile(s, 256)

    def body(ga_ref, gb_ref, ya_ref, yb_ref, o_ref):
        o_ref[...] = (_sigmoid(ga_ref[...]) * ya_ref[...] + _sigmoid(gb_ref[...]) * yb_ref[...]).astype(BF16)

    return pl.pallas_call(
        body, name="merge", grid=(s // ts,),
        out_shape=jax.ShapeDtypeStruct((s, d), BF16),
        in_specs=[_row(ts, d, off_a // d), _row(ts, d, off_b // d), _row(ts, d), _row(ts, d)],
        out_specs=_row(ts, d), compiler_params=_params(),
    )(proj, proj, ya_p, yb_p)


def _merge_bwd(proj, off_a, off_b, ya_p, yb_p, d_merged):
    s, d = ya_p.shape
    ts = _tile(s, 256)

    def body(ga_ref, gb_ref, ya_ref, yb_ref, dm_ref, dya_ref, dyb_ref, dga_ref, dgb_ref):
        dm = dm_ref[...]
        for g_ref, y_ref, dy_ref, dg_ref in ((ga_ref, ya_ref, dya_ref, dga_ref), (gb_ref, yb_ref, dyb_ref, dgb_ref)):
            sg = _sigmoid(g_ref[...])
            dy_ref[...] = (dm * sg).astype(BF16)
            dg_ref[...] = (dm * y_ref[...] * sg * (1.0 - sg)).astype(BF16)

    sd = jax.ShapeDtypeStruct((s, d), BF16)
    return pl.pallas_call(
        body, name="merge_bwd", grid=(s // ts,),
        out_shape=(sd, sd, sd, sd),
        in_specs=[_row(ts, d, off_a // d), _row(ts, d, off_b // d), _row(ts, d), _row(ts, d), _row(ts, d)],
        out_specs=(_row(ts, d),) * 4, compiler_params=_params(),
    )(proj, proj, ya_p, yb_p, d_merged)


def _resid_norm_mod(x, o, gt, g, sc, sh):
    s, d = x.shape
    ts = _tile(s, 256)

    def body(x_ref, o_ref, gt_ref, g_ref, sc_ref, sh_ref, x1_ref, h_ref):
        x1 = x_ref[...] + gt_ref[...] * o_ref[...]
        x1_ref[...] = x1
        r = lax.rsqrt(jnp.mean(x1 * x1, axis=-1, keepdims=True) + EPS)
        h_ref[...] = ((x1 * r) * g_ref[...] * (1.0 + sc_ref[...]) + sh_ref[...]).astype(BF16)

    return pl.pallas_call(
        body, name="resid_norm_mod", grid=(s // ts,),
        out_shape=(jax.ShapeDtypeStruct((s, d), F32), jax.ShapeDtypeStruct((s, d), BF16)),
        in_specs=[_row(ts, d), _row(ts, d), _vec(d), _vec(d), _vec(d), _vec(d)],
        out_specs=(_row(ts, d), _row(ts, d)), compiler_params=_params(),
    )(x, o, gt, g, sc, sh)


def _loss_head(x1, o2, gt2, target):
    s, d = x1.shape
    ts = _tile(s, 256)

    def body(x1_ref, o2_ref, gt_ref, t_ref, dy_ref, do_ref, sums_ref):
        @pl.when(pl.program_id(0) == 0)
        def _():
            sums_ref[...] = jnp.zeros_like(sums_ref)

        o2 = o2_ref[...]
        e = x1_ref[...] + gt_ref[...] * o2 - t_ref[...]
        dy = e / d
        dy_ref[...] = dy
        do_ref[...] = (dy * gt_ref[...]).astype(BF16)
        sums_ref[0:1, :] += jnp.sum(dy * o2, axis=0, keepdims=True)
        sums_ref[1:2, :] += jnp.sum(e * e, axis=0, keepdims=True)

    return pl.pallas_call(
        body, name="loss_head", grid=(s // ts,),
        out_shape=(jax.ShapeDtypeStruct((s, d), F32), jax.ShapeDtypeStruct((s, d), BF16),
                   jax.ShapeDtypeStruct((8, d), F32)),
        in_specs=[_row(ts, d), _row(ts, d), _vec(d), _row(ts, d)],
        out_specs=(_row(ts, d), _row(ts, d), pl.BlockSpec((8, d), lambda i: (0, 0))),
        compiler_params=_params(),
    )(x1, o2, gt2, target)


def _norm_mod_bwd(dh, xin, g, sc, dres, gate=None):
    s, d = xin.shape
    ts = _tile(s, 256)
    gated = gate is not None

    def body(*refs):
        if gated:
            dh_ref, x_ref, g_ref, sc_ref, dr_ref, o_ref, gt_ref, dx_ref, do_ref, sums_ref = refs
        else:
            dh_ref, x_ref, g_ref, sc_ref, dr_ref, dx_ref, sums_ref = refs

        @pl.when(pl.program_id(0) == 0)
        def _():
            sums_ref[...] = jnp.zeros_like(sums_ref)

        xv, dhv = x_ref[...], dh_ref[...]
        r = lax.rsqrt(jnp.mean(xv * xv, axis=-1, keepdims=True) + EPS)
        xn = xv * r
        one_sc = 1.0 + sc_ref[...]
        sums_ref[0:1, :] += jnp.sum(dhv, axis=0, keepdims=True)
        sums_ref[1:2, :] += jnp.sum(dhv * (xn * g_ref[...]), axis=0, keepdims=True)
        sums_ref[2:3, :] += jnp.sum(dhv * one_sc * xn, axis=0, keepdims=True)
        dxn = dhv * one_sc * g_ref[...]
        dx = dr_ref[...] + r * (dxn - xn * jnp.mean(dxn * xn, axis=-1, keepdims=True))
        dx_ref[...] = dx
        if gated:
            sums_ref[3:4, :] += jnp.sum(dx * o_ref[...], axis=0, keepdims=True)
            do_ref[...] = (dx * gt_ref[...]).astype(BF16)

    in_specs = [_row(ts, d), _row(ts, d), _vec(d), _vec(d), _row(ts, d)]
    args = [dh, xin, g, sc, dres]
    out_shape = [jax.ShapeDtypeStruct((s, d), F32)]
    out_specs = [_row(ts, d)]
    if gated:
        in_specs += [_row(ts, d), _vec(d)]
        args += list(gate)
        out_shape.append(jax.ShapeDtypeStruct((s, d), BF16))
        out_specs.append(_row(ts, d))
    out_shape.append(jax.ShapeDtypeStruct((8, d), F32))
    out_specs.append(pl.BlockSpec((8, d), lambda i: (0, 0)))
    return pl.pallas_call(
        body, name="norm_mod_bwd_gated" if gated else "norm_mod_bwd", grid=(s // ts,),
        out_shape=tuple(out_shape), in_specs=in_specs, out_specs=tuple(out_specs), compiler_params=_params(),
    )(*args)


def _ffn_tile(f):
    return 2 * f // N_DEV


def _swiglu(gu):
    s, f2 = gu.shape
    f = f2 // 2
    ts, tc = _tile(s, 512), _ffn_tile(f)

    def body(gu_ref, o_ref):
        gv, uv = gu_ref[:, :tc].astype(F32), gu_ref[:, tc:].astype(F32)
        o_ref[...] = (gv * _sigmoid(gv) * uv).astype(BF16)

    return pl.pallas_call(
        body, name="swiglu", grid=(s // ts, f // tc),
        out_shape=jax.ShapeDtypeStruct((s, f), BF16),
        in_specs=[pl.BlockSpec((ts, 2 * tc), lambda i, j: (i, j))],
        out_specs=pl.BlockSpec((ts, tc), lambda i, j: (i, j)), compiler_params=_params(),
    )(gu)


def _swiglu_bwd(gu, d_act):
    s, f2 = gu.shape
    f = f2 // 2
    ts, tc = _tile(s, 512), _ffn_tile(f)

    def body(gu_ref, da_ref, o_ref):
        gv, uv, da = gu_ref[:, :tc].astype(F32), gu_ref[:, tc:].astype(F32), da_ref[...]
        sg = _sigmoid(gv)
        o_ref[:, :tc] = (da * uv * (sg * (1.0 + gv * (1.0 - sg)))).astype(BF16)
        o_ref[:, tc:] = (da * (gv * sg)).astype(BF16)

    return pl.pallas_call(
        body, name="swiglu_bwd", grid=(s // ts, f // tc),
        out_shape=jax.ShapeDtypeStruct((s, f2), BF16),
        in_specs=[pl.BlockSpec((ts, 2 * tc), lambda i, j: (i, j)), pl.BlockSpec((ts, tc), lambda i, j: (i, j))],
        out_specs=pl.BlockSpec((ts, 2 * tc), lambda i, j: (i, j)), compiler_params=_params(),
    )(gu, d_act)


def _swap_halves(t):
    return pltpu.roll(t, ROPE // 2, 1) + pltpu.roll(t, LANES - ROPE // 2, 1)


def _head_norm(raw):
    r = lax.rsqrt(jnp.sum(raw * raw, axis=-1, keepdims=True) / QK_DIM + EPS)
    return raw * r, r


def _rope_fwd(v, cos, sin):
    rope_tile = v[:, NOPE:]
    return jnp.concatenate([v[:, :NOPE], rope_tile * cos + _swap_halves(rope_tile) * sin], axis=1)


def _rope_bwd(d, cos, sin, lane_ok):
    d_tile = d[:, NOPE:]
    return jnp.concatenate([d[:, :NOPE], d_tile * cos + _swap_halves(d_tile * sin) * lane_ok], axis=1)


def _qk_prep(q_raw, kv, proj, off_pe, cos, sin, g_q, g_k):
    s, hw = q_raw.shape
    heads = hw // QK_PAD
    ts = _tile(s, 512)

    def body(q_ref, kv_ref, pe_ref, cos_ref, sin_ref, gq_ref, gk_ref, qo_ref, ko_ref, vo_ref):
        cos_v, sin_v = cos_ref[...], sin_ref[...]
        qn, _ = _head_norm(q_ref[...])
        qo_ref[...] = _rope_fwd(qn * gq_ref[...], cos_v, sin_v).astype(BF16)
        kvv = kv_ref[...]
        kn, _ = _head_norm(jnp.concatenate([kvv[:, :NOPE], pe_ref[...]], axis=1))
        ko_ref[...] = _rope_fwd(kn * gk_ref[...], cos_v, sin_v).astype(BF16)
        vo_ref[...] = kvv[:, NOPE:].astype(BF16)

    blk = lambda w: pl.BlockSpec((ts, w), lambda i, h: (i, h))
    fixed = lambda w, col=0: pl.BlockSpec((ts, w), lambda i, h, col=col: (i, col))
    vec = pl.BlockSpec((1, QK_PAD), lambda i, h: (0, 0))
    return pl.pallas_call(
        body, name="qk_prep", grid=(s // ts, heads),
        out_shape=(jax.ShapeDtypeStruct((s, hw), BF16), jax.ShapeDtypeStruct((s, hw), BF16),
                   jax.ShapeDtypeStruct((s, heads * V_DIM), BF16)),
        in_specs=[blk(QK_PAD), blk(QK_PAD), fixed(LANES, off_pe // LANES), fixed(LANES), fixed(LANES), vec, vec],
        out_specs=(blk(QK_PAD), blk(QK_PAD), blk(V_DIM)), compiler_params=_params(),
    )(q_raw, kv, proj, cos, sin, g_q, g_k)


def _qk_prep_bwd(q_raw, kv, proj, off_pe, cos, sin, g_q, g_k, dq, dk, dv):
    s, hw = q_raw.shape
    heads = hw // QK_PAD
    ts = _tile(s, 512)

    def body(q_ref, kv_ref, pe_ref, cos_ref, sin_ref, gq_ref, gk_ref, dq_ref, dk_ref, dv_ref,
             dqr_ref, dkv_ref, dpe_ref, sums_ref):
        i, h = pl.program_id(0), pl.program_id(1)

        @pl.when((i == 0) & (h == 0))
        def _():
            sums_ref[...] = jnp.zeros_like(sums_ref)

        cos_v, sin_v = cos_ref[...], sin_ref[...]
        lane_ok = (lax.broadcasted_iota(jnp.int32, (ts, LANES), 1) < ROPE).astype(F32)

        def one(raw, g, d_post, row):
            vn, r = _head_norm(raw)
            d_pre = _rope_bwd(d_post, cos_v, sin_v, lane_ok)
            sums_ref[row:row + 1, :] += jnp.sum(d_pre * vn, axis=0, keepdims=True)
            dvn = d_pre * g
            return r * (dvn - vn * (jnp.sum(dvn * vn, axis=-1, keepdims=True) / QK_DIM))

        dqr_ref[...] = one(q_ref[...], gq_ref[...], dq_ref[...], 0).astype(BF16)
        kvv = kv_ref[...]
        d_kraw = one(jnp.concatenate([kvv[:, :NOPE], pe_ref[...]], axis=1), gk_ref[...], dk_ref[...], 1)
        dkv_ref[...] = jnp.concatenate([d_kraw[:, :NOPE], dv_ref[...]], axis=1).astype(BF16)

        @pl.when(h == 0)
        def _():
            dpe_ref[...] = jnp.zeros_like(dpe_ref)

        dpe_ref[...] += d_kraw[:, NOPE:]

    blk = lambda w: pl.BlockSpec((ts, w), lambda i, h: (i, h))
    fixed = lambda w, col=0: pl.BlockSpec((ts, w), lambda i, h, col=col: (i, col))
    vec = pl.BlockSpec((1, QK_PAD), lambda i, h: (0, 0))
    return pl.pallas_call(
        body, name="qk_prep_bwd", grid=(s // ts, heads),
        out_shape=(jax.ShapeDtypeStruct((s, hw), BF16), jax.ShapeDtypeStruct((s, hw), BF16),
                   jax.ShapeDtypeStruct((s, LANES), F32), jax.ShapeDtypeStruct((8, QK_PAD), F32)),
        in_specs=[blk(QK_PAD), blk(QK_PAD), fixed(LANES, off_pe // LANES), fixed(LANES), fixed(LANES), vec, vec,
                  blk(QK_PAD), blk(QK_PAD), blk(V_DIM)],
        out_specs=(blk(QK_PAD), blk(QK_PAD), fixed(LANES), pl.BlockSpec((8, QK_PAD), lambda i, h: (0, 0))),
        compiler_params=_params(),
    )(q_raw, kv, proj, cos, sin, g_q, g_k, dq, dk, dv)


def _att_blocks(s):
    tq = _tile(s, ATT_Q)
    tk = _tile(tq, ATT_K)
    return tq, tk, tq // tk


def _pairing(heads):
    return PAIR if heads % PAIR == 0 else 1


def _lanes(e, width):
    return slice(e * width, (e + 1) * width)


def _visible(qi, kb, tq, tk, strict):
    row = qi * tq + lax.broadcasted_iota(jnp.int32, (tq, tk), 0)
    col = kb * tk + lax.broadcasted_iota(jnp.int32, (tq, tk), 1)
    return (col < row) if strict else (col <= row)


def _first_last(heads, nq):
    h, qi = pl.program_id(0), pl.program_id(1)
    return (h == 0) & (qi == 0), (h == heads - 1) & (qi == nq - 1)


def _mla_fwd(q, k, v, carry=None):
    s = q.shape[0]
    heads = q.shape[1] // QK_PAD
    tq, tk, ratio = _att_blocks(s)
    nq = s // tq
    scale = QK_DIM ** -0.5

    hp = _pairing(heads)
    ev = range(hp)

    def body(*refs):
        (q_ref, k_ref, v_ref), (o_ref, lse_ref), (acc_ref,), xrefs = _split_refs(refs, 3, 2, carry)
        if carry is not None:
            _carried_exchange(xrefs, carry[1], *_first_last(heads // hp, nq))
        qi = pl.program_id(1)
        qs = [q_ref[:, _lanes(e, QK_PAD)] for e in ev]
        acc_ref[...] = jnp.zeros_like(acc_ref)

        def step(kb, state, masked):
            ms, ls = state
            off = pl.multiple_of(kb * tk, tk)
            sc = [_dot_nt(qs[e], k_ref[pl.ds(off, tk), _lanes(e, QK_PAD)]) * scale for e in ev]
            if masked:
                mask = _visible(qi, kb, tq, tk, False)
                sc = [jnp.where(mask, sc[e], -1e30) for e in ev]
            m_new = [jnp.maximum(ms[e], jnp.max(sc[e], axis=-1, keepdims=True)) for e in ev]
            alpha = [jnp.exp(ms[e] - m_new[e]) for e in ev]
            p = [jnp.exp(sc[e] - m_new[e]) for e in ev]
            for e in ev:
                lanes = _lanes(e, V_DIM)
                acc_ref[:, lanes] = alpha[e] * acc_ref[:, lanes] + _dot(p[e].astype(BF16), v_ref[pl.ds(off, tk), lanes])
            return tuple(m_new), tuple(alpha[e] * ls[e] + jnp.sum(p[e], axis=-1, keepdims=True) for e in ev)

        state = (tuple(jnp.full((tq, 1), -1e30, F32) for _ in ev), tuple(jnp.zeros((tq, 1), F32) for _ in ev))
        state = lax.fori_loop(0, qi * ratio, lambda kb, st: step(kb, st, False), state)
        for i in range(ratio):
            state = step(qi * ratio + i, state, True)
        ms, ls = state
        for e in ev:
            o_ref[:, _lanes(e, V_DIM)] = (acc_ref[:, _lanes(e, V_DIM)] / ls[e]).astype(BF16)
            lse_ref[e] = ms[e] + jnp.log(ls[e])

    args = [q, k, v]
    in_specs = [pl.BlockSpec((tq, hp * QK_PAD), lambda h, i: (i, h)), pl.BlockSpec((s, hp * QK_PAD), lambda h, i: (0, h)),
                pl.BlockSpec((s, hp * V_DIM), lambda h, i: (0, h))]
    out_shape = [jax.ShapeDtypeStruct((s, heads * V_DIM), BF16), jax.ShapeDtypeStruct((heads, s, 1), F32)]
    out_specs = [pl.BlockSpec((tq, hp * V_DIM), lambda h, i: (i, h)), pl.BlockSpec((hp, tq, 1), lambda h, i: (h, i, 0))]
    scratch = [pltpu.VMEM((tq, hp * V_DIM), F32)]
    _with_carry(carry, args, in_specs, out_shape, out_specs, scratch)
    return pl.pallas_call(
        body, name="mla_fwd", grid=(heads // hp, nq), out_shape=tuple(out_shape), in_specs=in_specs,
        out_specs=tuple(out_specs), scratch_shapes=scratch, compiler_params=_params(),
    )(*args)


def _mla_bwd(q, k, v, o, lse, do, carry=None):
    s = q.shape[0]
    heads = q.shape[1] // QK_PAD
    tq, tk, ratio = _att_blocks(s)
    nq = s // tq
    scale = QK_DIM ** -0.5

    hp = _pairing(heads)
    ev = range(hp)

    def body(*refs):
        (q_ref, k_ref, v_ref, o_ref, lse_ref, do_ref), (dq_ref, dk_ref, dv_ref), _, xrefs = _split_refs(refs, 6, 3, carry)
        if carry is not None:
            _carried_exchange(xrefs, carry[1], *_first_last(heads // hp, nq))
        qi = pl.program_id(1)

        @pl.when(qi == 0)
        def _():
            dk_ref[...] = jnp.zeros_like(dk_ref)
            dv_ref[...] = jnp.zeros_like(dv_ref)

        qs = [q_ref[:, _lanes(e, QK_PAD)] for e in ev]
        dos = [do_ref[:, _lanes(e, V_DIM)] for e in ev]
        do_b = [dos[e].astype(BF16) for e in ev]
        delta = [jnp.sum(dos[e] * o_ref[:, _lanes(e, V_DIM)].astype(F32), axis=-1, keepdims=True) for e in ev]
        lse_v = [lse_ref[e] for e in ev]
        dq_ref[...] = jnp.zeros_like(dq_ref)

        def step(kb, masked):
            off = pl.multiple_of(kb * tk, tk)
            ks = [k_ref[pl.ds(off, tk), _lanes(e, QK_PAD)] for e in ev]
            vs = [v_ref[pl.ds(off, tk), _lanes(e, V_DIM)] for e in ev]
            sc = [_dot_nt(qs[e], ks[e]) for e in ev]
            dp = [_dot_nt(do_b[e], vs[e]) for e in ev]
            p = [jnp.exp(sc[e] * scale - lse_v[e]) for e in ev]
            if masked:
                mask = _visible(qi, kb, tq, tk, False)
                p = [jnp.where(mask, p[e], 0.0) for e in ev]
            ds = [(p[e] * (dp[e] - delta[e]) * scale).astype(BF16) for e in ev]
            for e in ev:
                dv_ref[pl.ds(off, tk), _lanes(e, V_DIM)] += _dot_tn(p[e].astype(BF16), do_b[e])
            for e in ev:
                dk_ref[pl.ds(off, tk), _lanes(e, QK_PAD)] += _dot_tn(ds[e], qs[e])
            for e in ev:
                dq_ref[:, _lanes(e, QK_PAD)] += _dot(ds[e], ks[e])
            return 0

        lax.fori_loop(0, qi * ratio, lambda kb, _: step(kb, False), 0)
        for i in range(ratio):
            step(qi * ratio + i, True)

    args = [q, k, v, o, lse, do]
    wide, narrow = hp * QK_PAD, hp * V_DIM
    in_specs = [pl.BlockSpec((tq, wide), lambda h, i: (i, h)), pl.BlockSpec((s, wide), lambda h, i: (0, h)),
                pl.BlockSpec((s, narrow), lambda h, i: (0, h)), pl.BlockSpec((tq, narrow), lambda h, i: (i, h)),
                pl.BlockSpec((hp, tq, 1), lambda h, i: (h, i, 0)), pl.BlockSpec((tq, narrow), lambda h, i: (i, h))]
    out_shape = [jax.ShapeDtypeStruct(q.shape, F32), jax.ShapeDtypeStruct(k.shape, F32),
                 jax.ShapeDtypeStruct(v.shape, F32)]
    out_specs = [pl.BlockSpec((tq, wide), lambda h, i: (i, h)), pl.BlockSpec((s, wide), lambda h, i: (0, h)),
                 pl.BlockSpec((s, narrow), lambda h, i: (0, h))]
    scratch = []
    _with_carry(carry, args, in_specs, out_shape, out_specs, scratch)
    return pl.pallas_call(
        body, name="mla_bwd", grid=(heads // hp, nq), out_shape=tuple(out_shape), in_specs=in_specs,
        out_specs=tuple(out_specs), scratch_shapes=scratch, compiler_params=_params(),
    )(*args)


def _sb_terms(qv, k_blk, scale, mask):
    z = _dot_nt(qv, k_blk) * scale
    log_beta = jnp.minimum(z, 0.0) - jnp.log(1.0 + jnp.exp(-jnp.abs(z)))
    log_rest = log_beta - z
    if mask is not None:
        log_rest = jnp.where(mask, log_rest, 0.0)
    return log_beta, log_rest


def _head_lanes(e):
    return slice(e * SB_DIM, (e + 1) * SB_DIM)


def _sb_specs(s, tq, off_q, off_k, off_v, hp):
    w = hp * SB_DIM
    assert off_q % w == 0 and off_k % w == 0 and off_v % w == 0
    return [pl.BlockSpec((tq, w), lambda h, i: (i, off_q // w + h)),
            pl.BlockSpec((s, w), lambda h, i: (0, off_k // w + h)),
            pl.BlockSpec((s, w), lambda h, i: (0, off_v // w + h))]


def _sb_fwd(qkv, off_q, off_k, off_v, heads, carry=None):
    s = qkv.shape[0]
    tq, tk, ratio = _att_blocks(s)
    nq = s // tq
    hp = _pairing(heads)
    scale = SB_DIM ** -0.5

    def body(*refs):
        (q_ref, k_ref, v_ref), (o_ref,), (acc_ref,), xrefs = _split_refs(refs, 3, 1, carry)
        if carry is not None:
            _carried_exchange(xrefs, carry[1], *_first_last(heads // hp, nq))
        qi = pl.program_id(1)
        qs = [q_ref[:, _head_lanes(e)] for e in range(hp)]
        after = (lax.broadcasted_iota(jnp.int32, (tk, tk), 0) > lax.broadcasted_iota(jnp.int32, (tk, tk), 1)).astype(BF16)
        acc_ref[...] = jnp.zeros_like(acc_ref)
        row_parts = [slice(r * (tq // ROW_PARTS), (r + 1) * (tq // ROW_PARTS)) for r in range(ROW_PARTS)]

        def step(kb, tails, masked):
            off = pl.multiple_of(kb * tk, tk)
            mask = _visible(qi, kb, tq, tk, True) if masked else None
            units = [(e, rows) for e in range(hp) for rows in row_parts]
            ks = [k_ref[pl.ds(off, tk), _head_lanes(e)] for e in range(hp)]
            vs = [v_ref[pl.ds(off, tk), _head_lanes(e)] for e in range(hp)]
            terms = [_sb_terms(qs[e][rows], ks[e], scale, None if mask is None else mask[rows]) for e, rows in units]
            sums = [_split_dot(t[1], after) for t in terms]
            a = [jnp.exp(t[0] + (sm + tails[e][rows])) for t, sm, (e, rows) in zip(terms, sums, units)]
            if masked:
                a = [jnp.where(mask[rows], a_, 0.0) for a_, (e, rows) in zip(a, units)]
            for a_, (e, rows) in zip(a, units):
                acc_ref[rows, _head_lanes(e)] += _dot(a_.astype(BF16), vs[e])
            rest = [jnp.sum(t[1], axis=-1, keepdims=True) for t in terms]
            return tuple(tails[e] + jnp.concatenate(rest[e * len(row_parts):(e + 1) * len(row_parts)], axis=0)
                         for e in range(hp))

        tails = tuple(jnp.zeros((tq, 1), F32) for _ in range(hp))
        for i in range(ratio):
            tails = step((qi + 1) * ratio - 1 - i, tails, True)
        lax.fori_loop(0, qi * ratio, lambda i, t_: step(qi * ratio - 1 - i, t_, False), tails)
        o_ref[...] = acc_ref[...].astype(BF16)

    w = hp * SB_DIM
    args, in_specs = [qkv, qkv, qkv], _sb_specs(s, tq, off_q, off_k, off_v, hp)
    out_shape = [jax.ShapeDtypeStruct((s, heads * SB_DIM), BF16)]
    out_specs = [pl.BlockSpec((tq, w), lambda h, i: (i, h))]
    scratch = [pltpu.VMEM((tq, w), F32)]
    _with_carry(carry, args, in_specs, out_shape, out_specs, scratch)
    out = pl.pallas_call(
        body, name="sb_fwd", grid=(heads // hp, nq), out_shape=tuple(out_shape), in_specs=in_specs,
        out_specs=tuple(out_specs), scratch_shapes=scratch, compiler_params=_params(),
    )(*args)
    return out if carry is not None else out[0]


def _sb_bwd(qkv, off_q, off_k, off_v, heads, dy, carry=None):
    s = qkv.shape[0]
    t = _tile(s, ATT_K)
    nq = s // t
    hp = _pairing(heads)
    ev = range(hp)
    scale = SB_DIM ** -0.5

    def body(*refs):
        (q_ref, k_ref, v_ref, dy_ref), (dq_ref, dk_ref, dv_ref), (g_s, beta_s), xrefs = _split_refs(refs, 4, 3, carry)
        if carry is not None:
            _carried_exchange(xrefs, carry[1], *_first_last(heads // hp, nq))
        qi = pl.program_id(1)

        @pl.when(qi == 0)
        def _():
            dk_ref[...] = jnp.zeros_like(dk_ref)
            dv_ref[...] = jnp.zeros_like(dv_ref)

        qs = [q_ref[:, _head_lanes(e)] for e in ev]
        dy_b = [dy_ref[:, _head_lanes(e)].astype(BF16) for e in ev]
        rows = lax.broadcasted_iota(jnp.int32, (t, t), 0)
        cols = lax.broadcasted_iota(jnp.int32, (t, t), 1)
        after = (rows > cols).astype(BF16)
        before = (rows < cols).astype(BF16)

        def pass1(kb, tails, masked):
            off = pl.multiple_of(kb * t, t)
            mask = _visible(qi, kb, t, t, True) if masked else None
            ks = [k_ref[pl.ds(off, t), _head_lanes(e)] for e in ev]
            vs = [v_ref[pl.ds(off, t), _head_lanes(e)] for e in ev]
            terms = [_sb_terms(qs[e], ks[e], scale, mask) for e in ev]
            da = [_dot_nt(dy_b[e], vs[e]) for e in ev]
            sums = [_split_dot(terms[e][1], after) for e in ev]
            a = [jnp.exp(terms[e][0] + (sums[e] + tails[e])) for e in ev]
            beta = [jnp.exp(terms[e][0]) for e in ev]
            if masked:
                a = [jnp.where(mask, a[e], 0.0) for e in ev]
                beta = [jnp.where(mask, beta[e], 0.0) for e in ev]
            for e in ev:
                dv_ref[pl.ds(off, t), _head_lanes(e)] += _dot_tn(a[e].astype(BF16), dy_b[e])
            for e in ev:
                g_s[kb, e] = a[e] * da[e]
                beta_s[kb, e] = beta[e]
            return tuple(tails[e] + jnp.sum(terms[e][1], axis=-1, keepdims=True) for e in ev)

        zeros = tuple(jnp.zeros((t, 1), F32) for _ in ev)
        tails = pass1(qi, zeros, True)
        lax.fori_loop(0, qi, lambda i, t_: pass1(qi - 1 - i, t_, False), tails)

        dq_ref[...] = jnp.zeros_like(dq_ref)

        def pass2(kb, lefts):
            off = pl.multiple_of(kb * t, t)
            g = [g_s[kb, e] for e in ev]
            beta = [beta_s[kb, e] for e in ev]
            g_before = [_split_dot(g[e], before) + lefts[e] for e in ev]
            dz = [((g[e] * (1.0 - beta[e]) - g_before[e] * beta[e]) * scale).astype(BF16) for e in ev]
            for e in ev:
                dk_ref[pl.ds(off, t), _head_lanes(e)] += _dot_tn(dz[e], qs[e])
            for e in ev:
                dq_ref[:, _head_lanes(e)] += _dot(dz[e], k_ref[pl.ds(off, t), _head_lanes(e)])
            return tuple(lefts[e] + jnp.sum(g[e], axis=-1, keepdims=True) for e in ev)

        lax.fori_loop(0, qi + 1, pass2, zeros)

    w = hp * SB_DIM
    args = [qkv, qkv, qkv, dy]
    in_specs = _sb_specs(s, t, off_q, off_k, off_v, hp) + [pl.BlockSpec((t, w), lambda h, i: (i, h))]
    out_shape = [jax.ShapeDtypeStruct((s, heads * SB_DIM), F32)] * 3
    out_specs = [pl.BlockSpec((t, w), lambda h, i: (i, h)), pl.BlockSpec((s, w), lambda h, i: (0, h)),
                 pl.BlockSpec((s, w), lambda h, i: (0, h))]
    scratch = [pltpu.VMEM((nq, hp, t, t), F32), pltpu.VMEM((nq, hp, t, t), F32)]
    _with_carry(carry, args, in_specs, out_shape, out_specs, scratch)
    return pl.pallas_call(
        body, name="sb_bwd", grid=(heads // hp, nq), out_shape=tuple(out_shape), in_specs=in_specs,
        out_specs=tuple(out_specs), scratch_shapes=scratch, compiler_params=_params(),
    )(*args)


def _slot_sum(recv, name):
    _, r, w = recv.shape
    tr = _tile(r, 256)

    def body(r_ref, o_ref):
        acc = r_ref[0].astype(F32)
        for d in range(1, N_DEV):
            acc = acc + r_ref[d].astype(F32)
        o_ref[...] = acc

    return pl.pallas_call(
        body, name=name, grid=(r // tr,),
        out_shape=jax.ShapeDtypeStruct((r, w), F32),
        in_specs=[pl.BlockSpec((N_DEV, tr, w), lambda i: (0, i, 0))],
        out_specs=pl.BlockSpec((tr, w), lambda i: (i, 0)), compiler_params=_params(),
    )(recv)


def _small_sum(rows, loss_lo, loss_hi, d_model):
    n = rows.shape[1]

    def body(r_ref, o_ref, loss_ref):
        rv = r_ref[...]
        acc = rv[0:1, :]
        for d in range(1, N_DEV):
            acc = acc + rv[d:d + 1, :]
        o_ref[...] = acc
        total = jnp.sum(acc[:, loss_lo:loss_hi], axis=-1, keepdims=True) * (0.5 / d_model)
        loss_ref[...] = jnp.broadcast_to(total, (1, LANES))

    return pl.pallas_call(
        body, name="small_sum",
        out_shape=(jax.ShapeDtypeStruct((1, n), F32), jax.ShapeDtypeStruct((1, LANES), F32)),
        compiler_params=_params(),
    )(rows)


def _adamw(w, g, m, v, name):
    r, c = w.shape
    tr = _tile(r, max(8, (1 << 18) // c // 8 * 8))

    def body(w_ref, g_ref, m_ref, v_ref, d_ref, mo_ref, vo_ref):
        gv = g_ref[...]
        m_new = ADAM_B1 * m_ref[...] + (1.0 - ADAM_B1) * gv
        v_new = ADAM_B2 * v_ref[...] + (1.0 - ADAM_B2) * (gv * gv)
        m_hat = m_new / (1.0 - ADAM_B1 ** ADAM_STEP)
        v_hat = v_new / (1.0 - ADAM_B2 ** ADAM_STEP)
        d_ref[...] = -ADAM_LR * (m_hat / (jnp.sqrt(v_hat) + ADAM_EPS) + ADAM_WD * w_ref[...])
        mo_ref[...] = m_new
        vo_ref[...] = v_new

    spec = pl.BlockSpec((tr, c), lambda i: (i, 0))
    sd = jax.ShapeDtypeStruct((r, c), F32)
    return pl.pallas_call(
        body, name=name, grid=(r // tr,), out_shape=(sd, sd, sd),
        in_specs=[spec] * 4, out_specs=(spec,) * 3, compiler_params=_params(),
    )(w, g, m, v)


def _pack_rows(a):
    return a.reshape(-1, PACK_W)


def _unshard(slots, shape, col_sharded):
    r, c = shape
    if col_sharded:
        return slots.reshape(N_DEV, r, c).transpose(1, 0, 2).reshape(r, N_DEV * c)
    return slots.reshape(N_DEV * r, c)


def _to_shards(full, col_sharded, packed=True):
    r, c = full.shape
    shards = (full.reshape(r, N_DEV, c // N_DEV).transpose(1, 0, 2) if col_sharded
              else full.reshape(N_DEV, r // N_DEV, c))
    return shards.reshape(N_DEV, -1, PACK_W) if packed else shards


def kernel(x, c, positions, w_ada, b_ada, g_norm1, g_norm2, w_in, g_q_latent, g_kv_latent, w_uq, w_ukv, g_q_head, g_k_head, w_proj_mla, w_proj_sb, w_out, w_ffn_in, w_ffn_out, loss_target, m_w_ada, m_b_ada, m_g_norm1, m_g_norm2, m_w_in, m_g_q_latent, m_g_kv_latent, m_w_uq, m_w_ukv, m_g_q_head, m_g_k_head, m_w_proj_mla, m_w_proj_sb, m_w_out, m_w_ffn_in, m_w_ffn_out, v_w_ada, v_b_ada, v_g_norm1, v_g_norm2, v_w_in, v_g_q_latent, v_g_kv_latent, v_w_uq, v_w_ukv, v_g_q_head, v_g_k_head, v_w_proj_mla, v_w_proj_sb, v_w_out, v_w_ffn_in, v_w_ffn_out):
    env = dict(locals())
    drop = lambda a: a[0] if a.ndim == 3 else a
    wts = {n: drop(env[n]) for n in WEIGHT_NAMES}
    mom = {n: drop(env["m_" + n]) for n in WEIGHT_NAMES}
    var = {n: drop(env["v_" + n]) for n in WEIGHT_NAMES}
    xs, tgt, pos = x[0], loss_target[0], positions[0]
    s, d = xs.shape
    lat = wts["w_uq"].shape[0]
    assert wts["w_ukv"].shape[0] == lat
    h_mla = wts["w_uq"].shape[1] * N_DEV // QK_DIM
    sb_w = wts["w_proj_sb"].shape[0]
    h_sb = sb_w // SB_DIM
    d_ff = wts["w_ffn_out"].shape[0] * N_DEV
    me = 4 * lax.axis_index("x") + 2 * lax.axis_index("y") + lax.axis_index("c")

    ref_w = (("c_q", lat), ("c_kv", lat), ("k_pe", ROPE), ("q_sb", sb_w), ("k_sb", sb_w), ("v_sb", sb_w),
             ("gl_a", d), ("gl_b", d))
    ref_off, o = {}, 0
    for n_, w_ in ref_w:
        ref_off[n_] = (o, w_)
        o += w_
    order = ("gl_a", "gl_b", "q_sb", "k_sb", "v_sb", "c_q", "c_kv", "k_pe")
    off, o = {}, 0
    for n_ in order:
        w_ = LANES if n_ == "k_pe" else ref_off[n_][1]
        assert o % w_ == 0
        off[n_] = o
        o += w_

    used_w = o
    proj_w = -(-used_w // (2 * LANES)) * (2 * LANES)

    col_sharded = dict(BIG)
    rows_of = {n: wts[n].size // PACK_W for n, _ in BIG}
    full, grads = {}, {}

    def own_shape(names):
        return len(names) == 1

    def pack_weights(names):
        if own_shape(names):
            return wts[names[0]].astype(BF16)
        return jnp.concatenate([_pack_rows(wts[n].astype(BF16)) for n in names], axis=0)

    def unpack_weights(slots, names):
        r0 = 0
        for n in names:
            part = slots if own_shape(names) else slots[:, r0:r0 + rows_of[n]]
            full[n] = _unshard(part, wts[n].shape, col_sharded[n])
            r0 += rows_of[n]

    def pack_grads(names):
        if own_shape(names):
            return _to_shards(grads[names[0]].astype(BF16), col_sharded[names[0]], packed=False)
        return jnp.concatenate([_to_shards(grads[n].astype(BF16), col_sharded[n]) for n in names], axis=1)

    def unpack_grads(recv, names):
        if isinstance(recv, tuple):
            summed_rows = jnp.concatenate([_slot_sum(part, "slot_sum_%s_%d" % (names[0], i))
                                           for i, part in enumerate(recv)], axis=0)
        else:
            summed_rows = _slot_sum(recv, "slot_sum_" + names[0])
        r0 = 0
        for n in names:
            part = summed_rows if own_shape(names) else summed_rows[r0:r0 + rows_of[n]]
            grads[n] = part.reshape(wts[n].shape)
            r0 += rows_of[n]

    unpack_weights(_gather_two_level(pack_weights(("w_in",)), "gather_w_in"), ("w_in",))
    seg = lambda a, n_: a[:, ref_off[n_][0]:ref_off[n_][0] + ref_off[n_][1]]
    w_in_k = jnp.concatenate([seg(full["w_in"], n_) for n_ in order]
                             + [jnp.zeros((d, proj_w - used_w + LANES - ROPE), BF16)], axis=1)
    pad_gain = lambda g: jnp.pad(g, ((0, 0), (0, QK_PAD - QK_DIM)))
    g_qh, g_kh = pad_gain(wts["g_q_head"]), pad_gain(wts["g_k_head"])

    c_all = _gather_rows(c, "gather_c")
    ada_cols = _exchange(_ada_fwd(c_all, wts["w_ada"]), gather=True, name="gather_ada")
    ada = lax.dynamic_index_in_dim(ada_cols, me, axis=1, keepdims=False).reshape(1, 6 * d) + wts["b_ada"]
    sh1, sc1, gt1, sh2, sc2, gt2 = [ada[:, i * d:(i + 1) * d] for i in range(6)]

    half = ROPE // 2
    ang = pos.astype(F32)[:, None] * (ROPE_THETA ** (-jnp.arange(half, dtype=F32) / half))
    zeros = jnp.zeros((s, LANES - ROPE), F32)
    cos_t = jnp.concatenate([jnp.cos(ang), jnp.cos(ang), zeros], axis=1)
    sin_t = jnp.concatenate([-jnp.sin(ang), jnp.sin(ang), zeros], axis=1)

    h1 = _norm_mod(xs, wts["g_norm1"], sc1, sh1)
    mixer_w = ("w_uq", "w_ukv", "w_proj_mla", "w_proj_sb", "w_out")
    proj, slots = _mm(h1, w_in_k, name="mm_in", carry=(pack_weights(mixer_w), "two_level"))
    unpack_weights(slots, mixer_w)
    w_uq_k = jnp.pad(full["w_uq"].reshape(lat, h_mla, QK_DIM), ((0, 0), (0, 0), (0, QK_PAD - QK_DIM))
                     ).reshape(lat, h_mla * QK_PAD)
    cqn, ckvn = _latent_norm(proj, off["c_q"], off["c_kv"], lat, wts["g_q_latent"], wts["g_kv_latent"])
    q_raw = _mm(cqn, w_uq_k, name="mm_uq")
    kv = _mm(ckvn, full["w_ukv"], name="mm_ukv")
    q, k, v = _qk_prep(q_raw, kv, proj, off["k_pe"], cos_t, sin_t, g_qh, g_kh)
    y_a, lse, slots = _mla_fwd(q, k, v, carry=(pack_weights(("w_ffn_out",)), "two_level"))
    unpack_weights(slots, ("w_ffn_out",))
    assert off["k_sb"] == off["q_sb"] + sb_w and off["v_sb"] == off["k_sb"] + sb_w
    qkv_sb = proj[:, off["q_sb"]:off["q_sb"] + 3 * sb_w].astype(BF16)
    y_b, w_fi_slots = _sb_fwd(qkv_sb, 0, sb_w, 2 * sb_w, h_sb,
                              carry=(wts["w_ffn_in"].astype(BF16), "two_level"))
    ya_p = _mm(y_a, full["w_proj_mla"], name="mm_proj_mla")
    yb_p = _mm(y_b, full["w_proj_sb"], name="mm_proj_sb")
    merged = _merge(proj, off["gl_a"], off["gl_b"], ya_p, yb_p)
    o1 = _mm(merged, full["w_out"], name="mm_out")
    x1, h2 = _resid_norm_mod(xs, o1, gt1, wts["g_norm2"], sc2, sh2)
    gu = _mm(h2, w_fi_slots, shards="b", out_dtype=BF16, name="mm_ffn_in")
    act = _swiglu(gu)
    o2 = _mm(act, full["w_ffn_out"], name="mm_ffn_out")
    dy, d_o2, sums_l = _loss_head(x1, o2, gt2, tgt)

    recv = {}
    d_act = _mm(d_o2, full["w_ffn_out"], tb=True, name="mm_d_act")
    grads["w_ffn_out"] = _mm(act, d_o2, ta=True, out_dtype=BF16, name="mm_g_ffn_out")
    d_gu = _swiglu_bwd(gu, d_act)
    g_fo_send = pack_grads(("w_ffn_out",))
    half = g_fo_send.shape[1] // 2
    g_fi_slots, recv_lo = _mm(h2, d_gu, ta=True, shards="out", out_dtype=BF16, name="mm_g_ffn_in",
                              carry=(g_fo_send[:, :half], False))
    d_h2, recv_hi = _mm(d_gu, w_fi_slots, tb=True, shards="b", name="mm_d_h2", carry=(g_fo_send[:, half:], False))
    recv["w_ffn_out",] = (recv_lo, recv_hi)
    d_x1, d_o1, sums_2 = _norm_mod_bwd(d_h2, x1, wts["g_norm2"], sc2, dy, gate=(o1, gt1))
    grads["w_out"] = _mm(merged, d_o1, ta=True, out_dtype=BF16, name="mm_g_out")
    d_merged = _mm(d_o1, full["w_out"], tb=True, name="mm_d_merged")
    d_yap, d_ybp, d_gla, d_glb = _merge_bwd(proj, off["gl_a"], off["gl_b"], ya_p, yb_p, d_merged)
    grads["w_proj_mla"] = _mm(y_a, d_yap, ta=True, out_dtype=BF16, name="mm_g_proj_mla")
    grads["w_proj_sb"] = _mm(y_b, d_ybp, ta=True, out_dtype=BF16, name="mm_g_proj_sb")
    d_ya = _mm(d_yap, full["w_proj_mla"], tb=True, name="mm_d_ya")
    d_yb = _mm(d_ybp, full["w_proj_sb"], tb=True, name="mm_d_yb")
    dq_sb, dk_sb, dv_sb, recv["w_ffn_in",] = _sb_bwd(qkv_sb, 0, sb_w, 2 * sb_w, h_sb, d_yb,
                                                     carry=(g_fi_slots, False))
    merge_w = ("w_out", "w_proj_mla", "w_proj_sb")
    dq, dk, dv, recv[merge_w] = _mla_bwd(q, k, v, y_a, lse, d_ya, carry=(pack_grads(merge_w), False))
    d_qraw, d_kv, d_kpe, sums_h = _qk_prep_bwd(q_raw, kv, proj, off["k_pe"], cos_t, sin_t, g_qh, g_kh, dq, dk, dv)
    g_uq_k = _mm(cqn, d_qraw, ta=True, out_dtype=BF16, name="mm_g_uq")
    grads["w_uq"] = g_uq_k.reshape(lat, h_mla, QK_PAD)[:, :, :QK_DIM].reshape(lat, h_mla * QK_DIM)
    grads["w_ukv"] = _mm(ckvn, d_kv, ta=True, out_dtype=BF16, name="mm_g_ukv")
    d_cqn = _mm(d_qraw, w_uq_k, tb=True, name="mm_d_cqn")
    d_ckvn = _mm(d_kv, full["w_ukv"], tb=True, name="mm_d_ckvn")
    d_cq, d_ckv, sums_lat = _latent_norm_bwd(proj, off["c_q"], off["c_kv"], lat, wts["g_q_latent"],
                                             wts["g_kv_latent"], d_cqn, d_ckvn)
    d_parts = {"gl_a": d_gla, "gl_b": d_glb, "q_sb": dq_sb, "k_sb": dk_sb, "v_sb": dv_sb, "c_q": d_cq, "c_kv": d_ckv,
               "k_pe": d_kpe}
    d_proj = jnp.concatenate([d_parts[n_].astype(BF16) for n_ in order]
                             + ([jnp.zeros((s, proj_w - used_w), BF16)] if proj_w > used_w else []), axis=1)
    latent_w = ("w_uq", "w_ukv")
    g_in_k, recv[latent_w] = _mm(h1, d_proj, ta=True, out_dtype=BF16, name="mm_g_in",
                                 carry=(pack_grads(latent_w), False))
    grads["w_in"] = jnp.concatenate([g_in_k[:, off[n_]:off[n_] + w_] for n_, w_ in ref_w], axis=1)
    d_h1, recv["w_in",] = _mm(d_proj, w_in_k, tb=True, name="mm_d_h1", carry=(pack_grads(("w_in",)), False))
    grad_x, sums_1 = _norm_mod_bwd(d_h1, xs, wts["g_norm1"], sc1, d_x1)

    parts = [sums_1[0:1], sums_1[1:2], sums_2[3:4], sums_2[0:1], sums_2[1:2], sums_l[0:1],
             sums_1[2:3], sums_2[2:3], sums_lat[0:1], sums_lat[1:2], sums_h[0:1], sums_h[1:2], sums_l[1:2]]
    part_off, o = [], 0
    for p in parts:
        part_off.append(o)
        o += p.shape[1]
    all_rows = _gather_rows(jnp.concatenate(parts, axis=1), "gather_small")
    summed, loss_row = _small_sum(all_rows, part_off[12], part_off[12] + d, d)
    take = lambda i, w: summed[:, part_off[i]:part_off[i] + w]
    grads["b_ada"] = summed[:, :6 * d]
    grads["g_norm1"], grads["g_norm2"] = take(6, d), take(7, d)
    grads["g_q_latent"], grads["g_kv_latent"] = take(8, lat), take(9, lat)
    grads["g_q_head"], grads["g_k_head"] = take(10, QK_DIM), take(11, QK_DIM)
    n_ada = 6 * d // N_DEV
    d_ada_mine = lax.dynamic_slice_in_dim(all_rows[:, :6 * d], me * n_ada, n_ada, axis=1)
    grads["w_ada"] = _ada_grad(c_all.T, d_ada_mine)

    for names, slots in recv.items():
        unpack_grads(slots, names)

    delta, new_m, new_v = {}, {}, {}
    for n in ("w_ada",) + tuple(n for n, _ in BIG):
        delta[n], new_m[n], new_v[n] = _adamw(wts[n], grads[n], mom[n], var[n], "adamw_" + n)
    cat = lambda t: jnp.concatenate([t[n] for n in SMALL], axis=1)
    d_s, m_s, v_s = _adamw(cat(wts), cat(grads), cat(mom), cat(var), "adamw_small")
    o = 0
    for n in SMALL:
        w_ = wts[n].shape[1]
        delta[n], new_m[n], new_v[n] = d_s[:, o:o + w_], m_s[:, o:o + w_], v_s[:, o:o + w_]
        o += w_

    lead = lambda t: [t[n].reshape(env[n].shape) for n in WEIGHT_NAMES]
    return (loss_row[0, 0], grad_x[None], *lead(grads), *lead(delta), *lead(new_m), *lead(new_v))
```

```python
import jax
import jax.numpy as jnp
from jax import lax
from jax.experimental import pallas as pl
from jax.experimental.pallas import tpu as pltpu

F32 = jnp.float32
BF16 = jnp.bfloat16

N_DEV = 8
LANES = 128
PACK_W = 1024
VMEM_LIMIT = 48 * 1024 * 1024

EPS = 1e-6
ROPE_THETA = 10000.0
NOPE = 128
ROPE = 64
QK_DIM = NOPE + ROPE
QK_PAD = 2 * LANES
V_DIM = 128
SB_DIM = 128
ATT_Q = 512
ATT_K = 256
MM_TK = 2816
PAIR = 2
ROW_PARTS = 2

ADAM_LR = 0.001
ADAM_B1 = 0.9
ADAM_B2 = 0.999
ADAM_EPS = 1e-08
ADAM_WD = 0.01
ADAM_STEP = 10

WEIGHT_NAMES = ("w_ada", "b_ada", "g_norm1", "g_norm2", "w_in", "g_q_latent", "g_kv_latent", "w_uq", "w_ukv",
                "g_q_head", "g_k_head", "w_proj_mla", "w_proj_sb", "w_out", "w_ffn_in", "w_ffn_out")
BIG = (("w_in", True), ("w_uq", True), ("w_ukv", True), ("w_proj_mla", True), ("w_proj_sb", True),
       ("w_out", False), ("w_ffn_in", True), ("w_ffn_out", False))
SMALL = ("b_ada", "g_norm1", "g_norm2", "g_q_latent", "g_kv_latent", "g_q_head", "g_k_head")


def _tile(n, pref):
    if n <= pref:
        return n
    for step in (LANES, 8):
        for t in range(pref - pref % step, 0, -step):
            if n % t == 0:
                return t
    return n


def _params():
    return pltpu.CompilerParams(vmem_limit_bytes=VMEM_LIMIT)


def _sigmoid(x):
    return 1.0 / (1.0 + jnp.exp(-x))


def _dot(a, b):
    return lax.dot_general(a, b, (((1,), (0,)), ((), ())), preferred_element_type=F32)


def _dot_nt(a, b):
    return lax.dot_general(a, b, (((1,), (1,)), ((), ())), preferred_element_type=F32)


def _dot_tn(a, b):
    return lax.dot_general(a, b, (((0,), (0,)), ((), ())), preferred_element_type=F32)


def _split_dot(x, tri):
    hi = x.astype(BF16)
    lo = (x - hi.astype(F32)).astype(BF16)
    return _dot(hi, tri) + _dot(lo, tri)


_HBM = pl.BlockSpec(memory_space=pltpu.HBM)


def _carry_parts(carry):
    src, gather = carry
    if gather == "sibling":
        shape = tuple(src.shape[1:])
    elif gather == "chips" or not gather:
        shape = tuple(src.shape)
    else:
        shape = (N_DEV,) + tuple(src.shape)
    return (src, _HBM, jax.ShapeDtypeStruct(shape, src.dtype), _HBM,
            [pltpu.SemaphoreType.DMA((N_DEV - 1,)), pltpu.SemaphoreType.DMA((N_DEV - 1,)), pltpu.SemaphoreType.DMA(())])


def _exchange_copies(src_ref, dst_ref, send_sems, recv_sems, local_sem, gather):
    x, y, c = lax.axis_index("x"), lax.axis_index("y"), lax.axis_index("c")
    if gather == "sibling":
        return [pltpu.make_async_remote_copy(
            src_ref=src_ref.at[1 - c], dst_ref=dst_ref, send_sem=send_sems.at[0], recv_sem=recv_sems.at[0],
            device_id=(x, y, 1 - c), device_id_type=pl.DeviceIdType.MESH)]
    if gather == "chips":
        chip = 2 * x + y
        copies = [pltpu.make_async_copy(src_ref.at[chip], dst_ref.at[chip], local_sem)]
        for k in range(1, N_DEV // 2):
            peer = ((1 - x) if (k >> 1) & 1 else x, (1 - y) if k & 1 else y, c)
            copies.append(pltpu.make_async_remote_copy(
                src_ref=src_ref.at[2 * peer[0] + peer[1]], dst_ref=dst_ref.at[chip],
                send_sem=send_sems.at[k - 1], recv_sem=recv_sems.at[k - 1],
                device_id=peer, device_id_type=pl.DeviceIdType.MESH))
        return copies
    me = 4 * x + 2 * y + c

    def slot_for(idx):
        return src_ref if gather else src_ref.at[idx]

    copies = [pltpu.make_async_copy(slot_for(me), dst_ref.at[me], local_sem)]
    for k in range(1, N_DEV):
        peer = ((1 - x) if (k >> 2) & 1 else x, (1 - y) if (k >> 1) & 1 else y, (1 - c) if k & 1 else c)
        peer_idx = 4 * peer[0] + 2 * peer[1] + peer[2]
        copies.append(pltpu.make_async_remote_copy(
            src_ref=slot_for(peer_idx), dst_ref=dst_ref.at[me],
            send_sem=send_sems.at[k - 1], recv_sem=recv_sems.at[k - 1],
            device_id=peer, device_id_type=pl.DeviceIdType.MESH))
    return copies


def _two_level_copies(src_ref, dst_ref, send_sems, recv_sems, local_sem):
    x, y, c = lax.axis_index("x"), lax.axis_index("y"), lax.axis_index("c")
    me, sibling = (x, y, c), (x, y, 1 - c)
    chips = [(1 - x, y), (x, 1 - y), (1 - x, 1 - y)]

    def slot(px, py, pc):
        return dst_ref.at[4 * px + 2 * py + pc]

    def copy(k, block, to, own=False):
        return pltpu.make_async_remote_copy(
            src_ref=src_ref if own else slot(*block), dst_ref=slot(*block),
            send_sem=send_sems.at[k], recv_sem=recv_sems.at[k],
            device_id=to, device_id_type=pl.DeviceIdType.MESH)

    mine = pltpu.make_async_copy(src_ref, slot(*me), local_sem)
    first = [copy(0, me, sibling, own=True)] + [copy(1 + j, me, (*chip, c), own=True) for j, chip in enumerate(chips)]
    relays = [(copy(1 + j, (*chip, c), me), copy(4 + j, (*chip, c), sibling)) for j, chip in enumerate(chips)]
    late = [copy(0, sibling, me)] + [copy(4 + j, (*chip, 1 - c), me) for j, chip in enumerate(chips)]
    return mine, first, relays, late


def _two_level_start(refs):
    mine, first, _, _ = _two_level_copies(*refs)
    mine.start()
    for cp in first:
        cp.start()


def _two_level_finish(refs):
    mine, first, relays, late = _two_level_copies(*refs)
    for arrival, onward in relays:
        arrival.wait_recv()
        onward.start()
    for cp in late:
        cp.wait_recv()
    for cp in first + [onward for _, onward in relays]:
        cp.wait_send()
    mine.wait()


def _carried_exchange(refs, gather, first, last):
    @pl.when(first)
    def _():
        if gather == "two_level":
            _two_level_start(refs)
        else:
            for cp in _exchange_copies(*refs, gather):
                cp.start()

    @pl.when(last)
    def _():
        if gather == "two_level":
            _two_level_finish(refs)
        else:
            for cp in _exchange_copies(*refs, gather):
                cp.wait()


def _exchange(src, *, gather, name):
    operand, in_spec, out_shape, out_spec, scratch = _carry_parts((src, gather))

    def body(src_ref, dst_ref, send_sems, recv_sems, local_sem):
        copies = _exchange_copies(src_ref, dst_ref, send_sems, recv_sems, local_sem, gather)
        for cp in copies:
            cp.start()
        for cp in copies:
            cp.wait()

    return pl.pallas_call(body, name=name, out_shape=out_shape, in_specs=[in_spec], out_specs=out_spec,
                          scratch_shapes=scratch)(operand)


def _gather_two_level(src, name):
    operand, in_spec, out_shape, out_spec, scratch = _carry_parts((src, True))

    def body(*refs):
        _two_level_start(refs)
        _two_level_finish(refs)

    return pl.pallas_call(body, name=name, out_shape=out_shape, in_specs=[in_spec], out_specs=out_spec,
                          scratch_shapes=scratch)(operand)


def _gather_rows(v, name):
    n = v.shape[1]
    padded = -(-n // (8 * LANES)) * (8 * LANES)
    tiles = jnp.pad(v, ((0, 0), (0, padded - n))).reshape(padded // LANES, LANES)
    return _exchange(tiles, gather=True, name=name).reshape(N_DEV, padded)[:, :n]


def _split_refs(refs, n_in, n_out, carry):
    if carry is None:
        return refs[:n_in], refs[n_in:n_in + n_out], refs[n_in + n_out:], None
    ins, src_ref = refs[:n_in], refs[n_in]
    outs, dst_ref = refs[n_in + 1:n_in + 1 + n_out], refs[n_in + 1 + n_out]
    rest = refs[n_in + n_out + 2:]
    return ins, outs, rest[:-3], (src_ref, dst_ref) + tuple(rest[-3:])


def _with_carry(carry, args, in_specs, out_shape, out_specs, scratch):
    if carry is not None:
        operand, c_in, c_shape, c_out, c_scratch = _carry_parts(carry)
        args.append(operand)
        in_specs.append(c_in)
        out_shape.append(c_shape)
        out_specs.append(c_out)
        scratch.extend(c_scratch)


def _slot_of_chunk(j):
    return j // 2 + (N_DEV // 2) * (j % 2)


def _mm(a, b, *, ta=False, tb=False, out_dtype=F32, name, carry=None, shards=None):
    kdim, m = a.shape if ta else a.shape[::-1]
    if shards == "b":
        chunk = b.shape[2]
        n, kdim_b = (b.shape[1], N_DEV * chunk) if tb else (N_DEV * chunk, b.shape[1])
    else:
        n, kdim_b = b.shape if tb else b.shape[::-1]
    assert kdim == kdim_b, (a.shape, b.shape, ta, tb)
    tm, tn, tk = _tile(m, 1024), _tile(n, 1024), _tile(kdim, MM_TK)
    if shards == "b":
        tn, tk = (tn, chunk) if tb else (chunk, tk)
    elif shards == "out":
        chunk = tn = n // N_DEV
    grid = (m // tm, n // tn, kdim // tk)
    nk = grid[2]
    dims = (((0 if ta else 1,), (1 if tb else 0,)), ((), ()))

    def body(*refs):
        (a_ref, b_ref), (o_ref,), scratch_refs, xrefs = _split_refs(refs, 2, 1, carry)
        i, j, k = pl.program_id(0), pl.program_id(1), pl.program_id(2)
        if carry is not None:
            _carried_exchange(xrefs, carry[1], (i == 0) & (j == 0) & (k == 0),
                              (i == grid[0] - 1) & (j == grid[1] - 1) & (k == nk - 1))
        prod = lax.dot_general(a_ref[...].astype(BF16), b_ref[...].astype(BF16), dims, preferred_element_type=F32)
        if nk == 1:
            o_ref[...] = prod.astype(o_ref.dtype)
        else:
            acc_ref = scratch_refs[0]

            @pl.when(k == 0)
            def _():
                acc_ref[...] = prod

            @pl.when(k > 0)
            def _():
                acc_ref[...] += prod

            @pl.when(k == nk - 1)
            def _():
                o_ref[...] = acc_ref[...].astype(o_ref.dtype)

    a_spec = (pl.BlockSpec((tk, tm), lambda i, j, k: (k, i)) if ta else pl.BlockSpec((tm, tk), lambda i, j, k: (i, k)))
    if shards == "b":
        b_spec = (pl.BlockSpec((None, tn, tk), lambda i, j, k: (_slot_of_chunk(k), j, 0)) if tb
                  else pl.BlockSpec((None, tk, tn), lambda i, j, k: (_slot_of_chunk(j), k, 0)))
    else:
        b_spec = (pl.BlockSpec((tn, tk), lambda i, j, k: (j, k)) if tb else pl.BlockSpec((tk, tn), lambda i, j, k: (k, j)))
    args, in_specs = [a, b], [a_spec, b_spec]
    if shards == "out":
        out_shape = [jax.ShapeDtypeStruct((N_DEV, m, tn), out_dtype)]
        out_specs = [pl.BlockSpec((None, tm, tn), lambda i, j, k: (_slot_of_chunk(j), i, 0))]
    else:
        out_shape, out_specs = [jax.ShapeDtypeStruct((m, n), out_dtype)], [pl.BlockSpec((tm, tn), lambda i, j, k: (i, j))]
    scratch = [] if nk == 1 else [pltpu.VMEM((tm, tn), F32)]
    _with_carry(carry, args, in_specs, out_shape, out_specs, scratch)
    out = pl.pallas_call(
        body, name=name, grid=grid, out_shape=tuple(out_shape), in_specs=in_specs, out_specs=tuple(out_specs),
        scratch_shapes=scratch, compiler_params=_params(),
    )(*args)
    return out if carry is not None else out[0]


def _ada_fwd(c_all, w_shard):
    d, n = w_shard.shape
    tn = _tile(n, 512)

    def body(c_ref, w_ref, o_ref):
        cv = c_ref[...]
        o_ref[...] = jnp.dot(cv * _sigmoid(cv), w_ref[...], precision=lax.Precision.HIGHEST,
                             preferred_element_type=F32)

    return pl.pallas_call(
        body, name="ada_fwd", grid=(n // tn,),
        out_shape=jax.ShapeDtypeStruct((N_DEV, n), F32),
        in_specs=[pl.BlockSpec((N_DEV, d), lambda j: (0, 0)), pl.BlockSpec((d, tn), lambda j: (0, j))],
        out_specs=pl.BlockSpec((N_DEV, tn), lambda j: (0, j)),
        compiler_params=_params(),
    )(c_all, w_shard)


def _ada_grad(c_all_t, d_rows):
    d, n = c_all_t.shape[0], d_rows.shape[1]
    tn = _tile(n, 512)

    def body(ct_ref, d_ref, o_ref):
        cv = ct_ref[...]
        s = cv * _sigmoid(cv)
        dv = d_ref[...]
        acc = s[:, 0:1] * dv[0:1, :]
        for b in range(1, N_DEV):
            acc = acc + s[:, b:b + 1] * dv[b:b + 1, :]
        o_ref[...] = acc

    return pl.pallas_call(
        body, name="ada_grad", grid=(n // tn,),
        out_shape=jax.ShapeDtypeStruct((d, n), F32),
        in_specs=[pl.BlockSpec((d, N_DEV), lambda j: (0, 0)), pl.BlockSpec((N_DEV, tn), lambda j: (0, j))],
        out_specs=pl.BlockSpec((d, tn), lambda j: (0, j)),
        compiler_params=_params(),
    )(c_all_t, d_rows)


def _row(ts, w, col=0):
    return pl.BlockSpec((ts, w), lambda i, col=col: (i, col))


def _vec(w):
    return pl.BlockSpec((1, w), lambda i: (0, 0))


def _norm_mod(x, g, sc, sh):
    s, d = x.shape
    ts = _tile(s, 512)

    def body(x_ref, g_ref, sc_ref, sh_ref, h_ref):
        xv = x_ref[...]
        r = lax.rsqrt(jnp.mean(xv * xv, axis=-1, keepdims=True) + EPS)
        h_ref[...] = ((xv * r) * g_ref[...] * (1.0 + sc_ref[...]) + sh_ref[...]).astype(BF16)

    return pl.pallas_call(
        body, name="norm_mod", grid=(s // ts,),
        out_shape=jax.ShapeDtypeStruct((s, d), BF16),
        in_specs=[_row(ts, d), _vec(d), _vec(d), _vec(d)],
        out_specs=_row(ts, d), compiler_params=_params(),
    )(x, g, sc, sh)


def _latent_norm(proj, off_q, off_kv, lat, g_q, g_kv):
    s = proj.shape[0]
    ts = _tile(s, 512)

    def body(cq_ref, ckv_ref, gq_ref, gkv_ref, oq_ref, okv_ref):
        for c_ref, g_ref, o_ref in ((cq_ref, gq_ref, oq_ref), (ckv_ref, gkv_ref, okv_ref)):
            v = c_ref[...]
            r = lax.rsqrt(jnp.mean(v * v, axis=-1, keepdims=True) + EPS)
            o_ref[...] = ((v * r) * g_ref[...]).astype(BF16)

    return pl.pallas_call(
        body, name="latent_norm", grid=(s // ts,),
        out_shape=(jax.ShapeDtypeStruct((s, lat), BF16), jax.ShapeDtypeStruct((s, lat), BF16)),
        in_specs=[_row(ts, lat, off_q // lat), _row(ts, lat, off_kv // lat), _vec(lat), _vec(lat)],
        out_specs=(_row(ts, lat), _row(ts, lat)), compiler_params=_params(),
    )(proj, proj, g_q, g_kv)


def _latent_norm_bwd(proj, off_q, off_kv, lat, g_q, g_kv, d_cqn, d_ckvn):
    s = proj.shape[0]
    ts = _tile(s, 512)

    def body(cq_ref, ckv_ref, gq_ref, gkv_ref, dq_ref, dkv_ref, oq_ref, okv_ref, sums_ref):
        @pl.when(pl.program_id(0) == 0)
        def _():
            sums_ref[...] = jnp.zeros_like(sums_ref)

        for row, (c_ref, g_ref, d_ref, o_ref) in enumerate(((cq_ref, gq_ref, dq_ref, oq_ref),
                                                             (ckv_ref, gkv_ref, dkv_ref, okv_ref))):
            v = c_ref[...]
            r = lax.rsqrt(jnp.mean(v * v, axis=-1, keepdims=True) + EPS)
            vn = v * r
            dn = d_ref[...]
            sums_ref[row:row + 1, :] += jnp.sum(dn * vn, axis=0, keepdims=True)
            dvn = dn * g_ref[...]
            o_ref[...] = (r * (dvn - vn * jnp.mean(dvn * vn, axis=-1, keepdims=True))).astype(BF16)

    return pl.pallas_call(
        body, name="latent_norm_bwd", grid=(s // ts,),
        out_shape=(jax.ShapeDtypeStruct((s, lat), BF16), jax.ShapeDtypeStruct((s, lat), BF16),
                   jax.ShapeDtypeStruct((8, lat), F32)),
        in_specs=[_row(ts, lat, off_q // lat), _row(ts, lat, off_kv // lat), _vec(lat), _vec(lat),
                  _row(ts, lat), _row(ts, lat)],
        out_specs=(_row(ts, lat), _row(ts, lat), pl.BlockSpec((8, lat), lambda i: (0, 0))),
        compiler_params=_params(),
    )(proj, proj, g_q, g_kv, d_cqn, d_ckvn)


def _merge(proj, off_a, off_b, ya_p, yb_p):
    s, d = ya_p.shape
    ts = _tile(s, 256)

    def body(ga_ref, gb_ref, ya_ref, yb_ref, o_ref):
        o_ref[...] = (_sigmoid(ga_ref[...]) * ya_ref[...] + _sigmoid(gb_ref[...]) * yb_ref[...]).astype(BF16)

    return pl.pallas_call(
        body, name="merge", grid=(s // ts,),
        out_shape=jax.ShapeDtypeStruct((s, d), BF16),
        in_specs=[_row(ts, d, off_a // d), _row(ts, d, off_b // d), _row(ts, d), _row(ts, d)],
        out_specs=_row(ts, d), compiler_params=_params(),
    )(proj, proj, ya_p, yb_p)


def _merge_bwd(proj, off_a, off_b, ya_p, yb_p, d_merged):
    s, d = ya_p.shape
    ts = _tile(s, 256)

    def body(ga_ref, gb_ref, ya_ref, yb_ref, dm_ref, dya_ref, dyb_ref, dga_ref, dgb_ref):
        dm = dm_ref[...]
        for g_ref, y_ref, dy_ref, dg_ref in ((ga_ref, ya_ref, dya_ref, dga_ref), (gb_ref, yb_ref, dyb_ref, dgb_ref)):
            sg = _sigmoid(g_ref[...])
            dy_ref[...] = (dm * sg).astype(BF16)
            dg_ref[...] = (dm * y_ref[...] * sg * (1.0 - sg)).astype(BF16)

    sd = jax.ShapeDtypeStruct((s, d), BF16)
    return pl.pallas_call(
        body, name="merge_bwd", grid=(s // ts,),
        out_shape=(sd, sd, sd, sd),
        in_specs=[_row(ts, d, off_a // d), _row(ts, d, off_b // d), _row(ts, d), _row(ts, d), _row(ts, d)],
        out_specs=(_row(ts, d),) * 4, compiler_params=_params(),
    )(proj, proj, ya_p, yb_p, d_merged)


def _resid_norm_mod(x, o, gt, g, sc, sh):
    s, d = x.shape
    ts = _tile(s, 256)

    def body(x_ref, o_ref, gt_ref, g_ref, sc_ref, sh_ref, x1_ref, h_ref):
        x1 = x_ref[...] + gt_ref[...] * o_ref[...]
        x1_ref[...] = x1
        r = lax.rsqrt(jnp.mean(x1 * x1, axis=-1, keepdims=True) + EPS)
        h_ref[...] = ((x1 * r) * g_ref[...] * (1.0 + sc_ref[...]) + sh_ref[...]).astype(BF16)

    return pl.pallas_call(
        body, name="resid_norm_mod", grid=(s // ts,),
        out_shape=(jax.ShapeDtypeStruct((s, d), F32), jax.ShapeDtypeStruct((s, d), BF16)),
        in_specs=[_row(ts, d), _row(ts, d), _vec(d), _vec(d), _vec(d), _vec(d)],
        out_specs=(_row(ts, d), _row(ts, d)), compiler_params=_params(),
    )(x, o, gt, g, sc, sh)


def _loss_head(x1, o2, gt2, target):
    s, d = x1.shape
    ts = _tile(s, 256)

    def body(x1_ref, o2_ref, gt_ref, t_ref, dy_ref, do_ref, sums_ref):
        @pl.when(pl.program_id(0) == 0)
        def _():
            sums_ref[...] = jnp.zeros_like(sums_ref)

        o2 = o2_ref[...]
        e = x1_ref[...] + gt_ref[...] * o2 - t_ref[...]
        dy = e / d
        dy_ref[...] = dy
        do_ref[...] = (dy * gt_ref[...]).astype(BF16)
        sums_ref[0:1, :] += jnp.sum(dy * o2, axis=0, keepdims=True)
        sums_ref[1:2, :] += jnp.sum(e * e, axis=0, keepdims=True)

    return pl.pallas_call(
        body, name="loss_head", grid=(s // ts,),
        out_shape=(jax.ShapeDtypeStruct((s, d), F32), jax.ShapeDtypeStruct((s, d), BF16),
                   jax.ShapeDtypeStruct((8, d), F32)),
        in_specs=[_row(ts, d), _row(ts, d), _vec(d), _row(ts, d)],
        out_specs=(_row(ts, d), _row(ts, d), pl.BlockSpec((8, d), lambda i: (0, 0))),
        compiler_params=_params(),
    )(x1, o2, gt2, target)


def _norm_mod_bwd(dh, xin, g, sc, dres, gate=None):
    s, d = xin.shape
    ts = _tile(s, 256)
    gated = gate is not None

    def body(*refs):
        if gated:
            dh_ref, x_ref, g_ref, sc_ref, dr_ref, o_ref, gt_ref, dx_ref, do_ref, sums_ref = refs
        else:
            dh_ref, x_ref, g_ref, sc_ref, dr_ref, dx_ref, sums_ref = refs

        @pl.when(pl.program_id(0) == 0)
        def _():
            sums_ref[...] = jnp.zeros_like(sums_ref)

        xv, dhv = x_ref[...], dh_ref[...]
        r = lax.rsqrt(jnp.mean(xv * xv, axis=-1, keepdims=True) + EPS)
        xn = xv * r
        one_sc = 1.0 + sc_ref[...]
        sums_ref[0:1, :] += jnp.sum(dhv, axis=0, keepdims=True)
        sums_ref[1:2, :] += jnp.sum(dhv * (xn * g_ref[...]), axis=0, keepdims=True)
        sums_ref[2:3, :] += jnp.sum(dhv * one_sc * xn, axis=0, keepdims=True)
        dxn = dhv * one_sc * g_ref[...]
        dx = dr_ref[...] + r * (dxn - xn * jnp.mean(dxn * xn, axis=-1, keepdims=True))
        dx_ref[...] = dx
        if gated:
            sums_ref[3:4, :] += jnp.sum(dx * o_ref[...], axis=0, keepdims=True)
            do_ref[...] = (dx * gt_ref[...]).astype(BF16)

    in_specs = [_row(ts, d), _row(ts, d), _vec(d), _vec(d), _row(ts, d)]
    args = [dh, xin, g, sc, dres]
    out_shape = [jax.ShapeDtypeStruct((s, d), F32)]
    out_specs = [_row(ts, d)]
    if gated:
        in_specs += [_row(ts, d), _vec(d)]
        args += list(gate)
        out_shape.append(jax.ShapeDtypeStruct((s, d), BF16))
        out_specs.append(_row(ts, d))
    out_shape.append(jax.ShapeDtypeStruct((8, d), F32))
    out_specs.append(pl.BlockSpec((8, d), lambda i: (0, 0)))
    return pl.pallas_call(
        body, name="norm_mod_bwd_gated" if gated else "norm_mod_bwd", grid=(s // ts,),
        out_shape=tuple(out_shape), in_specs=in_specs, out_specs=tuple(out_specs), compiler_params=_params(),
    )(*args)


def _ffn_tile(f):
    return 2 * f // N_DEV


def _swiglu(gu):
    s, f2 = gu.shape
    f = f2 // 2
    ts, tc = _tile(s, 512), _ffn_tile(f)

    def body(gu_ref, o_ref):
        gv, uv = gu_ref[:, :tc].astype(F32), gu_ref[:, tc:].astype(F32)
        o_ref[...] = (gv * _sigmoid(gv) * uv).astype(BF16)

    return pl.pallas_call(
        body, name="swiglu", grid=(s // ts, f // tc),
        out_shape=jax.ShapeDtypeStruct((s, f), BF16),
        in_specs=[pl.BlockSpec((ts, 2 * tc), lambda i, j: (i, j))],
        out_specs=pl.BlockSpec((ts, tc), lambda i, j: (i, j)), compiler_params=_params(),
    )(gu)


def _swiglu_bwd(gu, d_act):
    s, f2 = gu.shape
    f = f2 // 2
    ts, tc = _tile(s, 512), _ffn_tile(f)

    def body(gu_ref, da_ref, o_ref):
        gv, uv, da = gu_ref[:, :tc].astype(F32), gu_ref[:, tc:].astype(F32), da_ref[...]
        sg = _sigmoid(gv)
        o_ref[:, :tc] = (da * uv * (sg * (1.0 + gv * (1.0 - sg)))).astype(BF16)
        o_ref[:, tc:] = (da * (gv * sg)).astype(BF16)

    return pl.pallas_call(
        body, name="swiglu_bwd", grid=(s // ts, f // tc),
        out_shape=jax.ShapeDtypeStruct((s, f2), BF16),
        in_specs=[pl.BlockSpec((ts, 2 * tc), lambda i, j: (i, j)), pl.BlockSpec((ts, tc), lambda i, j: (i, j))],
        out_specs=pl.BlockSpec((ts, 2 * tc), lambda i, j: (i, j)), compiler_params=_params(),
    )(gu, d_act)


def _swap_halves(t):
    return pltpu.roll(t, ROPE // 2, 1) + pltpu.roll(t, LANES - ROPE // 2, 1)


def _head_norm(raw):
    r = lax.rsqrt(jnp.sum(raw * raw, axis=-1, keepdims=True) / QK_DIM + EPS)
    return raw * r, r


def _rope_fwd(v, cos, sin):
    rope_tile = v[:, NOPE:]
    return jnp.concatenate([v[:, :NOPE], rope_tile * cos + _swap_halves(rope_tile) * sin], axis=1)


def _rope_bwd(d, cos, sin, lane_ok):
    d_tile = d[:, NOPE:]
    return jnp.concatenate([d[:, :NOPE], d_tile * cos + _swap_halves(d_tile * sin) * lane_ok], axis=1)


def _qk_prep(q_raw, kv, proj, off_pe, cos, sin, g_q, g_k):
    s, hw = q_raw.shape
    heads = hw // QK_PAD
    ts = _tile(s, 512)

    def body(q_ref, kv_ref, pe_ref, cos_ref, sin_ref, gq_ref, gk_ref, qo_ref, ko_ref, vo_ref):
        cos_v, sin_v = cos_ref[...], sin_ref[...]
        qn, _ = _head_norm(q_ref[...])
        qo_ref[...] = _rope_fwd(qn * gq_ref[...], cos_v, sin_v).astype(BF16)
        kvv = kv_ref[...]
        kn, _ = _head_norm(jnp.concatenate([kvv[:, :NOPE], pe_ref[...]], axis=1))
        ko_ref[...] = _rope_fwd(kn * gk_ref[...], cos_v, sin_v).astype(BF16)
        vo_ref[...] = kvv[:, NOPE:].astype(BF16)

    blk = lambda w: pl.BlockSpec((ts, w), lambda i, h: (i, h))
    fixed = lambda w, col=0: pl.BlockSpec((ts, w), lambda i, h, col=col: (i, col))
    vec = pl.BlockSpec((1, QK_PAD), lambda i, h: (0, 0))
    return pl.pallas_call(
        body, name="qk_prep", grid=(s // ts, heads),
        out_shape=(jax.ShapeDtypeStruct((s, hw), BF16), jax.ShapeDtypeStruct((s, hw), BF16),
                   jax.ShapeDtypeStruct((s, heads * V_DIM), BF16)),
        in_specs=[blk(QK_PAD), blk(QK_PAD), fixed(LANES, off_pe // LANES), fixed(LANES), fixed(LANES), vec, vec],
        out_specs=(blk(QK_PAD), blk(QK_PAD), blk(V_DIM)), compiler_params=_params(),
    )(q_raw, kv, proj, cos, sin, g_q, g_k)


def _qk_prep_bwd(q_raw, kv, proj, off_pe, cos, sin, g_q, g_k, dq, dk, dv):
    s, hw = q_raw.shape
    heads = hw // QK_PAD
    ts = _tile(s, 512)

    def body(q_ref, kv_ref, pe_ref, cos_ref, sin_ref, gq_ref, gk_ref, dq_ref, dk_ref, dv_ref,
             dqr_ref, dkv_ref, dpe_ref, sums_ref):
        i, h = pl.program_id(0), pl.program_id(1)

        @pl.when((i == 0) & (h == 0))
        def _():
            sums_ref[...] = jnp.zeros_like(sums_ref)

        cos_v, sin_v = cos_ref[...], sin_ref[...]
        lane_ok = (lax.broadcasted_iota(jnp.int32, (ts, LANES), 1) < ROPE).astype(F32)

        def one(raw, g, d_post, row):
            vn, r = _head_norm(raw)
            d_pre = _rope_bwd(d_post, cos_v, sin_v, lane_ok)
            sums_ref[row:row + 1, :] += jnp.sum(d_pre * vn, axis=0, keepdims=True)
            dvn = d_pre * g
            return r * (dvn - vn * (jnp.sum(dvn * vn, axis=-1, keepdims=True) / QK_DIM))

        dqr_ref[...] = one(q_ref[...], gq_ref[...], dq_ref[...], 0).astype(BF16)
        kvv = kv_ref[...]
        d_kraw = one(jnp.concatenate([kvv[:, :NOPE], pe_ref[...]], axis=1), gk_ref[...], dk_ref[...], 1)
        dkv_ref[...] = jnp.concatenate([d_kraw[:, :NOPE], dv_ref[...]], axis=1).astype(BF16)

        @pl.when(h == 0)
        def _():
            dpe_ref[...] = jnp.zeros_like(dpe_ref)

        dpe_ref[...] += d_kraw[:, NOPE:]

    blk = lambda w: pl.BlockSpec((ts, w), lambda i, h: (i, h))
    fixed = lambda w, col=0: pl.BlockSpec((ts, w), lambda i, h, col=col: (i, col))
    vec = pl.BlockSpec((1, QK_PAD), lambda i, h: (0, 0))
    return pl.pallas_call(
        body, name="qk_prep_bwd", grid=(s // ts, heads),
        out_shape=(jax.ShapeDtypeStruct((s, hw), BF16), jax.ShapeDtypeStruct((s, hw), BF16),
                   jax.ShapeDtypeStruct((s, LANES), F32), jax.ShapeDtypeStruct((8, QK_PAD), F32)),
        in_specs=[blk(QK_PAD), blk(QK_PAD), fixed(LANES, off_pe // LANES), fixed(LANES), fixed(LANES), vec, vec,
                  blk(QK_PAD), blk(QK_PAD), blk(V_DIM)],
        out_specs=(blk(QK_PAD), blk(QK_PAD), fixed(LANES), pl.BlockSpec((8, QK_PAD), lambda i, h: (0, 0))),
        compiler_params=_params(),
    )(q_raw, kv, proj, cos, sin, g_q, g_k, dq, dk, dv)


def _att_blocks(s):
    tq = _tile(s, ATT_Q)
    tk = _tile(tq, ATT_K)
    return tq, tk, tq // tk


def _pairing(heads):
    return PAIR if heads % PAIR == 0 else 1


def _lanes(e, width):
    return slice(e * width, (e + 1) * width)


def _visible(qi, kb, tq, tk, strict):
    row = qi * tq + lax.broadcasted_iota(jnp.int32, (tq, tk), 0)
    col = kb * tk + lax.broadcasted_iota(jnp.int32, (tq, tk), 1)
    return (col < row) if strict else (col <= row)


def _first_last(heads, nq):
    h, qi = pl.program_id(0), pl.program_id(1)
    return (h == 0) & (qi == 0), (h == heads - 1) & (qi == nq - 1)


def _mla_fwd(q, k, v, carry=None):
    s = q.shape[0]
    heads = q.shape[1] // QK_PAD
    tq, tk, ratio = _att_blocks(s)
    nq = s // tq
    scale = QK_DIM ** -0.5

    hp = _pairing(heads)
    ev = range(hp)

    def body(*refs):
        (q_ref, k_ref, v_ref), (o_ref, lse_ref), (acc_ref,), xrefs = _split_refs(refs, 3, 2, carry)
        if carry is not None:
            _carried_exchange(xrefs, carry[1], *_first_last(heads // hp, nq))
        qi = pl.program_id(1)
        qs = [q_ref[:, _lanes(e, QK_PAD)] for e in ev]
        acc_ref[...] = jnp.zeros_like(acc_ref)

        def step(kb, state, masked):
            ms, ls = state
            off = pl.multiple_of(kb * tk, tk)
            sc = [_dot_nt(qs[e], k_ref[pl.ds(off, tk), _lanes(e, QK_PAD)]) * scale for e in ev]
            if masked:
                mask = _visible(qi, kb, tq, tk, False)
                sc = [jnp.where(mask, sc[e], -1e30) for e in ev]
            m_new = [jnp.maximum(ms[e], jnp.max(sc[e], axis=-1, keepdims=True)) for e in ev]
            alpha = [jnp.exp(ms[e] - m_new[e]) for e in ev]
            p = [jnp.exp(sc[e] - m_new[e]) for e in ev]
            for e in ev:
                lanes = _lanes(e, V_DIM)
                acc_ref[:, lanes] = alpha[e] * acc_ref[:, lanes] + _dot(p[e].astype(BF16), v_ref[pl.ds(off, tk), lanes])
            return tuple(m_new), tuple(alpha[e] * ls[e] + jnp.sum(p[e], axis=-1, keepdims=True) for e in ev)

        state = (tuple(jnp.full((tq, 1), -1e30, F32) for _ in ev), tuple(jnp.zeros((tq, 1), F32) for _ in ev))
        state = lax.fori_loop(0, qi * ratio, lambda kb, st: step(kb, st, False), state)
        for i in range(ratio):
            state = step(qi * ratio + i, state, True)
        ms, ls = state
        for e in ev:
            o_ref[:, _lanes(e, V_DIM)] = (acc_ref[:, _lanes(e, V_DIM)] / ls[e]).astype(BF16)
            lse_ref[e] = ms[e] + jnp.log(ls[e])

    args = [q, k, v]
    in_specs = [pl.BlockSpec((tq, hp * QK_PAD), lambda h, i: (i, h)), pl.BlockSpec((s, hp * QK_PAD), lambda h, i: (0, h)),
                pl.BlockSpec((s, hp * V_DIM), lambda h, i: (0, h))]
    out_shape = [jax.ShapeDtypeStruct((s, heads * V_DIM), BF16), jax.ShapeDtypeStruct((heads, s, 1), F32)]
    out_specs = [pl.BlockSpec((tq, hp * V_DIM), lambda h, i: (i, h)), pl.BlockSpec((hp, tq, 1), lambda h, i: (h, i, 0))]
    scratch = [pltpu.VMEM((tq, hp * V_DIM), F32)]
    _with_carry(carry, args, in_specs, out_shape, out_specs, scratch)
    return pl.pallas_call(
        body, name="mla_fwd", grid=(heads // hp, nq), out_shape=tuple(out_shape), in_specs=in_specs,
        out_specs=tuple(out_specs), scratch_shapes=scratch, compiler_params=_params(),
    )(*args)


def _mla_bwd(q, k, v, o, lse, do, carry=None):
    s = q.shape[0]
    heads = q.shape[1] // QK_PAD
    tq, tk, ratio = _att_blocks(s)
    nq = s // tq
    scale = QK_DIM ** -0.5

    hp = _pairing(heads)
    ev = range(hp)

    def body(*refs):
        (q_ref, k_ref, v_ref, o_ref, lse_ref, do_ref), (dq_ref, dk_ref, dv_ref), _, xrefs = _split_refs(refs, 6, 3, carry)
        if carry is not None:
            _carried_exchange(xrefs, carry[1], *_first_last(heads // hp, nq))
        qi = pl.program_id(1)

        @pl.when(qi == 0)
        def _():
            dk_ref[...] = jnp.zeros_like(dk_ref)
            dv_ref[...] = jnp.zeros_like(dv_ref)

        qs = [q_ref[:, _lanes(e, QK_PAD)] for e in ev]
        dos = [do_ref[:, _lanes(e, V_DIM)] for e in ev]
        do_b = [dos[e].astype(BF16) for e in ev]
        delta = [jnp.sum(dos[e] * o_ref[:, _lanes(e, V_DIM)].astype(F32), axis=-1, keepdims=True) for e in ev]
        lse_v = [lse_ref[e] for e in ev]
        dq_ref[...] = jnp.zeros_like(dq_ref)

        def step(kb, masked):
            off = pl.multiple_of(kb * tk, tk)
            ks = [k_ref[pl.ds(off, tk), _lanes(e, QK_PAD)] for e in ev]
            vs = [v_ref[pl.ds(off, tk), _lanes(e, V_DIM)] for e in ev]
            sc = [_dot_nt(qs[e], ks[e]) for e in ev]
            dp = [_dot_nt(do_b[e], vs[e]) for e in ev]
            p = [jnp.exp(sc[e] * scale - lse_v[e]) for e in ev]
            if masked:
                mask = _visible(qi, kb, tq, tk, False)
                p = [jnp.where(mask, p[e], 0.0) for e in ev]
            ds = [(p[e] * (dp[e] - delta[e]) * scale).astype(BF16) for e in ev]
            for e in ev:
                dv_ref[pl.ds(off, tk), _lanes(e, V_DIM)] += _dot_tn(p[e].astype(BF16), do_b[e])
            for e in ev:
                dk_ref[pl.ds(off, tk), _lanes(e, QK_PAD)] += _dot_tn(ds[e], qs[e])
            for e in ev:
                dq_ref[:, _lanes(e, QK_PAD)] += _dot(ds[e], ks[e])
            return 0

        lax.fori_loop(0, qi * ratio, lambda kb, _: step(kb, False), 0)
        for i in range(ratio):
            step(qi * ratio + i, True)

    args = [q, k, v, o, lse, do]
    wide, narrow = hp * QK_PAD, hp * V_DIM
    in_specs = [pl.BlockSpec((tq, wide), lambda h, i: (i, h)), pl.BlockSpec((s, wide), lambda h, i: (0, h)),
                pl.BlockSpec((s, narrow), lambda h, i: (0, h)), pl.BlockSpec((tq, narrow), lambda h, i: (i, h)),
                pl.BlockSpec((hp, tq, 1), lambda h, i: (h, i, 0)), pl.BlockSpec((tq, narrow), lambda h, i: (i, h))]
    out_shape = [jax.ShapeDtypeStruct(q.shape, F32), jax.ShapeDtypeStruct(k.shape, F32),
                 jax.ShapeDtypeStruct(v.shape, F32)]
    out_specs = [pl.BlockSpec((tq, wide), lambda h, i: (i, h)), pl.BlockSpec((s, wide), lambda h, i: (0, h)),
                 pl.BlockSpec((s, narrow), lambda h, i: (0, h))]
    scratch = []
    _with_carry(carry, args, in_specs, out_shape, out_specs, scratch)
    return pl.pallas_call(
        body, name="mla_bwd", grid=(heads // hp, nq), out_shape=tuple(out_shape), in_specs=in_specs,
        out_specs=tuple(out_specs), scratch_shapes=scratch, compiler_params=_params(),
    )(*args)


def _sb_terms(qv, k_blk, scale, mask):
    z = _dot_nt(qv, k_blk) * scale
    log_beta = jnp.minimum(z, 0.0) - jnp.log(1.0 + jnp.exp(-jnp.abs(z)))
    log_rest = log_beta - z
    if mask is not None:
        log_rest = jnp.where(mask, log_rest, 0.0)
    return log_beta, log_rest


def _head_lanes(e):
    return slice(e * SB_DIM, (e + 1) * SB_DIM)


def _sb_specs(s, tq, off_q, off_k, off_v, hp):
    w = hp * SB_DIM
    assert off_q % w == 0 and off_k % w == 0 and off_v % w == 0
    return [pl.BlockSpec((tq, w), lambda h, i: (i, off_q // w + h)),
            pl.BlockSpec((s, w), lambda h, i: (0, off_k // w + h)),
            pl.BlockSpec((s, w), lambda h, i: (0, off_v // w + h))]


def _sb_fwd(qkv, off_q, off_k, off_v, heads, carry=None):
    s = qkv.shape[0]
    tq, tk, ratio = _att_blocks(s)
    nq = s // tq
    hp = _pairing(heads)
    scale = SB_DIM ** -0.5

    def body(*refs):
        (q_ref, k_ref, v_ref), (o_ref, tot_ref), (acc_ref,), xrefs = _split_refs(refs, 3, 2, carry)
        if carry is not None:
            _carried_exchange(xrefs, carry[1], *_first_last(heads // hp, nq))
        qi = pl.program_id(1)
        qs = [q_ref[:, _head_lanes(e)] for e in range(hp)]
        after = (lax.broadcasted_iota(jnp.int32, (tk, tk), 0) > lax.broadcasted_iota(jnp.int32, (tk, tk), 1)).astype(BF16)
        acc_ref[...] = jnp.zeros_like(acc_ref)
        row_parts = [slice(r * (tq // ROW_PARTS), (r + 1) * (tq // ROW_PARTS)) for r in range(ROW_PARTS)]

        def step(kb, tails, masked):
            off = pl.multiple_of(kb * tk, tk)
            mask = _visible(qi, kb, tq, tk, True) if masked else None
            units = [(e, rows) for e in range(hp) for rows in row_parts]
            ks = [k_ref[pl.ds(off, tk), _head_lanes(e)] for e in range(hp)]
            vs = [v_ref[pl.ds(off, tk), _head_lanes(e)] for e in range(hp)]
            terms = [_sb_terms(qs[e][rows], ks[e], scale, None if mask is None else mask[rows]) for e, rows in units]
            sums = [_split_dot(t[1], after) for t in terms]
            a = [jnp.exp(t[0] + (sm + tails[e][rows])) for t, sm, (e, rows) in zip(terms, sums, units)]
            if masked:
                a = [jnp.where(mask[rows], a_, 0.0) for a_, (e, rows) in zip(a, units)]
            for a_, (e, rows) in zip(a, units):
                acc_ref[rows, _head_lanes(e)] += _dot(a_.astype(BF16), vs[e])
            rest = [jnp.sum(t[1], axis=-1, keepdims=True) for t in terms]
            return tuple(tails[e] + jnp.concatenate(rest[e * len(row_parts):(e + 1) * len(row_parts)], axis=0)
                         for e in range(hp))

        tails = tuple(jnp.zeros((tq, 1), F32) for _ in range(hp))
        for i in range(ratio):
            tails = step((qi + 1) * ratio - 1 - i, tails, True)
        tails = lax.fori_loop(0, qi * ratio, lambda i, t_: step(qi * ratio - 1 - i, t_, False), tails)
        o_ref[...] = acc_ref[...].astype(BF16)
        for e in range(hp):
            tot_ref[e] = tails[e]

    w = hp * SB_DIM
    args, in_specs = [qkv, qkv, qkv], _sb_specs(s, tq, off_q, off_k, off_v, hp)
    out_shape = [jax.ShapeDtypeStruct((s, heads * SB_DIM), BF16), jax.ShapeDtypeStruct((heads, s, 1), F32)]
    out_specs = [pl.BlockSpec((tq, w), lambda h, i: (i, h)), pl.BlockSpec((hp, tq, 1), lambda h, i: (h, i, 0))]
    scratch = [pltpu.VMEM((tq, w), F32)]
    _with_carry(carry, args, in_specs, out_shape, out_specs, scratch)
    return pl.pallas_call(
        body, name="sb_fwd", grid=(heads // hp, nq), out_shape=tuple(out_shape), in_specs=in_specs,
        out_specs=tuple(out_specs), scratch_shapes=scratch, compiler_params=_params(),
    )(*args)


def _sb_bwd(qkv, off_q, off_k, off_v, heads, dy, tot, carry=None):
    s = qkv.shape[0]
    t = _tile(s, ATT_K)
    nq = s // t
    hp = _pairing(heads)
    ev = range(hp)
    scale = SB_DIM ** -0.5

    def body(*refs):
        (q_ref, k_ref, v_ref, dy_ref, tot_ref), (dq_ref, dk_ref, dv_ref), _, xrefs = _split_refs(refs, 5, 3, carry)
        if carry is not None:
            _carried_exchange(xrefs, carry[1], *_first_last(heads // hp, nq))
        qi = pl.program_id(1)

        @pl.when(qi == 0)
        def _():
            dk_ref[...] = jnp.zeros_like(dk_ref)
            dv_ref[...] = jnp.zeros_like(dv_ref)

        qs = [q_ref[:, _head_lanes(e)] for e in ev]
        dy_b = [dy_ref[:, _head_lanes(e)].astype(BF16) for e in ev]
        tots = [tot_ref[e] for e in ev]
        rows = lax.broadcasted_iota(jnp.int32, (t, t), 0)
        cols = lax.broadcasted_iota(jnp.int32, (t, t), 1)
        upto = (rows <= cols).astype(BF16)
        before = (rows < cols).astype(BF16)
        dq_ref[...] = jnp.zeros_like(dq_ref)

        def logits(kb):
            off = pl.multiple_of(kb * t, t)
            terms = [_sb_terms(qs[e], k_ref[pl.ds(off, t), _head_lanes(e)], scale, None) for e in ev]
            da = [_dot_nt(dy_b[e], v_ref[pl.ds(off, t), _head_lanes(e)]) for e in ev]
            return tuple(terms[e] + (da[e],) for e in ev)

        def step(kb, ahead, state, masked, prefetch):
            rest_left, g_left = state
            off = pl.multiple_of(kb * t, t)
            log_beta = [ahead[e][0] for e in ev]
            log_rest = [ahead[e][1] for e in ev]
            da = [ahead[e][2] for e in ev]
            if masked:
                mask = _visible(qi, kb, t, t, True)
                log_rest = [jnp.where(mask, log_rest[e], 0.0) for e in ev]
            rest_upto = [_split_dot(log_rest[e], upto) + rest_left[e] for e in ev]
            nxt = logits(kb + 1) if prefetch else None
            a = [jnp.exp(log_beta[e] + (tots[e] - rest_upto[e])) for e in ev]
            beta = [jnp.exp(log_beta[e]) for e in ev]
            if masked:
                a = [jnp.where(mask, a[e], 0.0) for e in ev]
                beta = [jnp.where(mask, beta[e], 0.0) for e in ev]
            for e in ev:
                dv_ref[pl.ds(off, t), _head_lanes(e)] += _dot_tn(a[e].astype(BF16), dy_b[e])
            g = [a[e] * da[e] for e in ev]
            g_before = [_split_dot(g[e], before) + g_left[e] for e in ev]
            dz = [((g[e] * (1.0 - beta[e]) - g_before[e] * beta[e]) * scale).astype(BF16) for e in ev]
            for e in ev:
                dk_ref[pl.ds(off, t), _head_lanes(e)] += _dot_tn(dz[e], qs[e])
            for e in ev:
                dq_ref[:, _head_lanes(e)] += _dot(dz[e], k_ref[pl.ds(off, t), _head_lanes(e)])
            state = (tuple(rest_left[e] + jnp.sum(log_rest[e], axis=-1, keepdims=True) for e in ev),
                     tuple(g_left[e] + jnp.sum(g[e], axis=-1, keepdims=True) for e in ev))
            return nxt, state

        zeros = tuple(jnp.zeros((t, 1), F32) for _ in ev)
        ahead, state = lax.fori_loop(0, qi, lambda kb, c: step(kb, c[0], c[1], False, True),
                                     (logits(0), (zeros, zeros)))
        step(qi, ahead, state, True, False)

    w = hp * SB_DIM
    args = [qkv, qkv, qkv, dy, tot]
    in_specs = _sb_specs(s, t, off_q, off_k, off_v, hp) + [pl.BlockSpec((t, w), lambda h, i: (i, h)),
                                                           pl.BlockSpec((hp, t, 1), lambda h, i: (h, i, 0))]
    out_shape = [jax.ShapeDtypeStruct((s, heads * SB_DIM), F32)] * 3
    out_specs = [pl.BlockSpec((t, w), lambda h, i: (i, h)), pl.BlockSpec((s, w), lambda h, i: (0, h)),
                 pl.BlockSpec((s, w), lambda h, i: (0, h))]
    scratch = []
    _with_carry(carry, args, in_specs, out_shape, out_specs, scratch)
    return pl.pallas_call(
        body, name="sb_bwd", grid=(heads // hp, nq), out_shape=tuple(out_shape), in_specs=in_specs,
        out_specs=tuple(out_specs), scratch_shapes=scratch, compiler_params=_params(),
    )(*args)


def _slot_sum(recv, name):
    slots, r, w = recv.shape
    tr = _tile(r, 256)

    def body(r_ref, o_ref):
        acc = r_ref[0].astype(F32)
        for d in range(1, slots):
            acc = acc + r_ref[d].astype(F32)
        o_ref[...] = acc

    return pl.pallas_call(
        body, name=name, grid=(r // tr,),
        out_shape=jax.ShapeDtypeStruct((r, w), F32),
        in_specs=[pl.BlockSpec((slots, tr, w), lambda i: (0, i, 0))],
        out_specs=pl.BlockSpec((tr, w), lambda i: (i, 0)), compiler_params=_params(),
    )(recv)


def _pair_sum(mine, theirs):
    slots, r, w = mine.shape
    tr = _tile(r, 256)

    def body(a_ref, b_ref, o_ref):
        o_ref[...] = (a_ref[...].astype(F32) + b_ref[...].astype(F32)).astype(BF16)

    spec = pl.BlockSpec((slots, tr, w), lambda i: (0, i, 0))
    return pl.pallas_call(
        body, name="pair_sum", grid=(r // tr,), out_shape=jax.ShapeDtypeStruct(mine.shape, BF16),
        in_specs=[spec, spec], out_specs=spec, compiler_params=_params(),
    )(mine, theirs)


def _small_sum(rows, loss_lo, loss_hi, d_model):
    n = rows.shape[1]

    def body(r_ref, o_ref, loss_ref):
        rv = r_ref[...]
        acc = rv[0:1, :]
        for d in range(1, N_DEV):
            acc = acc + rv[d:d + 1, :]
        o_ref[...] = acc
        total = jnp.sum(acc[:, loss_lo:loss_hi], axis=-1, keepdims=True) * (0.5 / d_model)
        loss_ref[...] = jnp.broadcast_to(total, (1, LANES))

    return pl.pallas_call(
        body, name="small_sum",
        out_shape=(jax.ShapeDtypeStruct((1, n), F32), jax.ShapeDtypeStruct((1, LANES), F32)),
        compiler_params=_params(),
    )(rows)


def _adamw(w, g, m, v, name):
    r, c = w.shape
    tr = _tile(r, max(8, (1 << 18) // c // 8 * 8))

    def body(w_ref, g_ref, m_ref, v_ref, d_ref, mo_ref, vo_ref):
        gv = g_ref[...]
        m_new = ADAM_B1 * m_ref[...] + (1.0 - ADAM_B1) * gv
        v_new = ADAM_B2 * v_ref[...] + (1.0 - ADAM_B2) * (gv * gv)
        m_hat = m_new / (1.0 - ADAM_B1 ** ADAM_STEP)
        v_hat = v_new / (1.0 - ADAM_B2 ** ADAM_STEP)
        d_ref[...] = -ADAM_LR * (m_hat / (jnp.sqrt(v_hat) + ADAM_EPS) + ADAM_WD * w_ref[...])
        mo_ref[...] = m_new
        vo_ref[...] = v_new

    spec = pl.BlockSpec((tr, c), lambda i: (i, 0))
    sd = jax.ShapeDtypeStruct((r, c), F32)
    return pl.pallas_call(
        body, name=name, grid=(r // tr,), out_shape=(sd, sd, sd),
        in_specs=[spec] * 4, out_specs=(spec,) * 3, compiler_params=_params(),
    )(w, g, m, v)


def _pack_rows(a):
    return a.reshape(-1, PACK_W)


def _unshard(slots, shape, col_sharded):
    r, c = shape
    if col_sharded:
        return slots.reshape(N_DEV, r, c).transpose(1, 0, 2).reshape(r, N_DEV * c)
    return slots.reshape(N_DEV * r, c)


def _to_shards(full, col_sharded, packed=True):
    r, c = full.shape
    shards = (full.reshape(r, N_DEV, c // N_DEV).transpose(1, 0, 2) if col_sharded
              else full.reshape(N_DEV, r // N_DEV, c))
    return shards.reshape(N_DEV, -1, PACK_W) if packed else shards


def kernel(x, c, positions, w_ada, b_ada, g_norm1, g_norm2, w_in, g_q_latent, g_kv_latent, w_uq, w_ukv, g_q_head, g_k_head, w_proj_mla, w_proj_sb, w_out, w_ffn_in, w_ffn_out, loss_target, m_w_ada, m_b_ada, m_g_norm1, m_g_norm2, m_w_in, m_g_q_latent, m_g_kv_latent, m_w_uq, m_w_ukv, m_g_q_head, m_g_k_head, m_w_proj_mla, m_w_proj_sb, m_w_out, m_w_ffn_in, m_w_ffn_out, v_w_ada, v_b_ada, v_g_norm1, v_g_norm2, v_w_in, v_g_q_latent, v_g_kv_latent, v_w_uq, v_w_ukv, v_g_q_head, v_g_k_head, v_w_proj_mla, v_w_proj_sb, v_w_out, v_w_ffn_in, v_w_ffn_out):
    env = dict(locals())
    drop = lambda a: a[0] if a.ndim == 3 else a
    wts = {n: drop(env[n]) for n in WEIGHT_NAMES}
    mom = {n: drop(env["m_" + n]) for n in WEIGHT_NAMES}
    var = {n: drop(env["v_" + n]) for n in WEIGHT_NAMES}
    xs, tgt, pos = x[0], loss_target[0], positions[0]
    s, d = xs.shape
    lat = wts["w_uq"].shape[0]
    assert wts["w_ukv"].shape[0] == lat
    h_mla = wts["w_uq"].shape[1] * N_DEV // QK_DIM
    sb_w = wts["w_proj_sb"].shape[0]
    h_sb = sb_w // SB_DIM
    d_ff = wts["w_ffn_out"].shape[0] * N_DEV
    me = 4 * lax.axis_index("x") + 2 * lax.axis_index("y") + lax.axis_index("c")

    ref_w = (("c_q", lat), ("c_kv", lat), ("k_pe", ROPE), ("q_sb", sb_w), ("k_sb", sb_w), ("v_sb", sb_w),
             ("gl_a", d), ("gl_b", d))
    ref_off, o = {}, 0
    for n_, w_ in ref_w:
        ref_off[n_] = (o, w_)
        o += w_
    order = ("gl_a", "gl_b", "q_sb", "k_sb", "v_sb", "c_q", "c_kv", "k_pe")
    off, o = {}, 0
    for n_ in order:
        w_ = LANES if n_ == "k_pe" else ref_off[n_][1]
        assert o % w_ == 0
        off[n_] = o
        o += w_

    used_w = o
    proj_w = -(-used_w // (2 * LANES)) * (2 * LANES)

    col_sharded = dict(BIG)
    rows_of = {n: wts[n].size // PACK_W for n, _ in BIG}
    full, grads = {}, {}

    def own_shape(names):
        return len(names) == 1

    def pack_weights(names):
        if own_shape(names):
            return wts[names[0]].astype(BF16)
        return jnp.concatenate([_pack_rows(wts[n].astype(BF16)) for n in names], axis=0)

    def unpack_weights(slots, names):
        r0 = 0
        for n in names:
            part = slots if own_shape(names) else slots[:, r0:r0 + rows_of[n]]
            full[n] = _unshard(part, wts[n].shape, col_sharded[n])
            r0 += rows_of[n]

    def pack_grads(names):
        if own_shape(names):
            return _to_shards(grads[names[0]].astype(BF16), col_sharded[names[0]], packed=False)
        return jnp.concatenate([_to_shards(grads[n].astype(BF16), col_sharded[n]) for n in names], axis=1)

    def unpack_grads(recv, names):
        if isinstance(recv, tuple):
            summed_rows = jnp.concatenate([_slot_sum(part, "slot_sum_%s_%d" % (names[0], i))
                                           for i, part in enumerate(recv)], axis=0)
        else:
            summed_rows = _slot_sum(recv, "slot_sum_" + names[0])
        r0 = 0
        for n in names:
            part = summed_rows if own_shape(names) else summed_rows[r0:r0 + rows_of[n]]
            grads[n] = part.reshape(wts[n].shape)
            r0 += rows_of[n]

    unpack_weights(_gather_two_level(pack_weights(("w_in",)), "gather_w_in"), ("w_in",))
    seg = lambda a, n_: a[:, ref_off[n_][0]:ref_off[n_][0] + ref_off[n_][1]]
    w_in_k = jnp.concatenate([seg(full["w_in"], n_) for n_ in order]
                             + [jnp.zeros((d, proj_w - used_w + LANES - ROPE), BF16)], axis=1)
    pad_gain = lambda g: jnp.pad(g, ((0, 0), (0, QK_PAD - QK_DIM)))
    g_qh, g_kh = pad_gain(wts["g_q_head"]), pad_gain(wts["g_k_head"])

    c_all = _gather_rows(c, "gather_c")
    ada_cols = _exchange(_ada_fwd(c_all, wts["w_ada"]), gather=True, name="gather_ada")
    ada = lax.dynamic_index_in_dim(ada_cols, me, axis=1, keepdims=False).reshape(1, 6 * d) + wts["b_ada"]
    sh1, sc1, gt1, sh2, sc2, gt2 = [ada[:, i * d:(i + 1) * d] for i in range(6)]

    half = ROPE // 2
    ang = pos.astype(F32)[:, None] * (ROPE_THETA ** (-jnp.arange(half, dtype=F32) / half))
    zeros = jnp.zeros((s, LANES - ROPE), F32)
    cos_t = jnp.concatenate([jnp.cos(ang), jnp.cos(ang), zeros], axis=1)
    sin_t = jnp.concatenate([-jnp.sin(ang), jnp.sin(ang), zeros], axis=1)

    h1 = _norm_mod(xs, wts["g_norm1"], sc1, sh1)
    mixer_w = ("w_uq", "w_ukv", "w_proj_mla", "w_proj_sb", "w_out")
    proj, slots = _mm(h1, w_in_k, name="mm_in", carry=(pack_weights(mixer_w), "two_level"))
    unpack_weights(slots, mixer_w)
    w_uq_k = jnp.pad(full["w_uq"].reshape(lat, h_mla, QK_DIM), ((0, 0), (0, 0), (0, QK_PAD - QK_DIM))
                     ).reshape(lat, h_mla * QK_PAD)
    cqn, ckvn = _latent_norm(proj, off["c_q"], off["c_kv"], lat, wts["g_q_latent"], wts["g_kv_latent"])
    q_raw = _mm(cqn, w_uq_k, name="mm_uq")
    kv = _mm(ckvn, full["w_ukv"], name="mm_ukv")
    q, k, v = _qk_prep(q_raw, kv, proj, off["k_pe"], cos_t, sin_t, g_qh, g_kh)
    y_a, lse, slots = _mla_fwd(q, k, v, carry=(pack_weights(("w_ffn_out",)), "two_level"))
    unpack_weights(slots, ("w_ffn_out",))
    assert off["k_sb"] == off["q_sb"] + sb_w and off["v_sb"] == off["k_sb"] + sb_w
    qkv_sb = proj[:, off["q_sb"]:off["q_sb"] + 3 * sb_w].astype(BF16)
    y_b, tot_sb, w_fi_slots = _sb_fwd(qkv_sb, 0, sb_w, 2 * sb_w, h_sb,
                              carry=(wts["w_ffn_in"].astype(BF16), "two_level"))
    ya_p = _mm(y_a, full["w_proj_mla"], name="mm_proj_mla")
    yb_p = _mm(y_b, full["w_proj_sb"], name="mm_proj_sb")
    merged = _merge(proj, off["gl_a"], off["gl_b"], ya_p, yb_p)
    o1 = _mm(merged, full["w_out"], name="mm_out")
    x1, h2 = _resid_norm_mod(xs, o1, gt1, wts["g_norm2"], sc2, sh2)
    gu = _mm(h2, w_fi_slots, shards="b", out_dtype=BF16, name="mm_ffn_in")
    act = _swiglu(gu)
    o2 = _mm(act, full["w_ffn_out"], name="mm_ffn_out")
    dy, d_o2, sums_l = _loss_head(x1, o2, gt2, tgt)

    recv = {}
    d_act = _mm(d_o2, full["w_ffn_out"], tb=True, name="mm_d_act")
    grads["w_ffn_out"] = _mm(act, d_o2, ta=True, out_dtype=BF16, name="mm_g_ffn_out")
    d_gu = _swiglu_bwd(gu, d_act)
    g_fo_send = pack_grads(("w_ffn_out",))
    half = g_fo_send.shape[1] // 2
    g_fi_slots, recv_lo = _mm(h2, d_gu, ta=True, shards="out", out_dtype=BF16, name="mm_g_ffn_in",
                              carry=(g_fo_send[:, :half], False))
    d_h2, recv_hi = _mm(d_gu, w_fi_slots, tb=True, shards="b", name="mm_d_h2", carry=(g_fo_send[:, half:], False))
    recv["w_ffn_out",] = (recv_lo, recv_hi)
    d_x1, d_o1, sums_2 = _norm_mod_bwd(d_h2, x1, wts["g_norm2"], sc2, dy, gate=(o1, gt1))
    grads["w_out"] = _mm(merged, d_o1, ta=True, out_dtype=BF16, name="mm_g_out")
    d_merged = _mm(d_o1, full["w_out"], tb=True, name="mm_d_merged")
    d_yap, d_ybp, d_gla, d_glb = _merge_bwd(proj, off["gl_a"], off["gl_b"], ya_p, yb_p, d_merged)
    grads["w_proj_mla"] = _mm(y_a, d_yap, ta=True, out_dtype=BF16, name="mm_g_proj_mla")
    grads["w_proj_sb"] = _mm(y_b, d_ybp, ta=True, out_dtype=BF16, name="mm_g_proj_sb")
    d_ya = _mm(d_yap, full["w_proj_mla"], tb=True, name="mm_d_ya")
    d_yb = _mm(d_ybp, full["w_proj_sb"], tb=True, name="mm_d_yb")
    dq_sb, dk_sb, dv_sb, recv["w_ffn_in",] = _sb_bwd(qkv_sb, 0, sb_w, 2 * sb_w, h_sb, d_yb, tot_sb,
                                                     carry=(g_fi_slots, False))
    merge_w = ("w_out", "w_proj_mla", "w_proj_sb")
    dq, dk, dv, recv[merge_w] = _mla_bwd(q, k, v, y_a, lse, d_ya, carry=(pack_grads(merge_w), False))
    d_qraw, d_kv, d_kpe, sums_h = _qk_prep_bwd(q_raw, kv, proj, off["k_pe"], cos_t, sin_t, g_qh, g_kh, dq, dk, dv)
    g_uq_k = _mm(cqn, d_qraw, ta=True, out_dtype=BF16, name="mm_g_uq")
    grads["w_uq"] = g_uq_k.reshape(lat, h_mla, QK_PAD)[:, :, :QK_DIM].reshape(lat, h_mla * QK_DIM)
    grads["w_ukv"] = _mm(ckvn, d_kv, ta=True, out_dtype=BF16, name="mm_g_ukv")
    d_cqn = _mm(d_qraw, w_uq_k, tb=True, name="mm_d_cqn")
    d_ckvn = _mm(d_kv, full["w_ukv"], tb=True, name="mm_d_ckvn")
    d_cq, d_ckv, sums_lat = _latent_norm_bwd(proj, off["c_q"], off["c_kv"], lat, wts["g_q_latent"],
                                             wts["g_kv_latent"], d_cqn, d_ckvn)
    d_parts = {"gl_a": d_gla, "gl_b": d_glb, "q_sb": dq_sb, "k_sb": dk_sb, "v_sb": dv_sb, "c_q": d_cq, "c_kv": d_ckv,
               "k_pe": d_kpe}
    d_proj = jnp.concatenate([d_parts[n_].astype(BF16) for n_ in order]
                             + ([jnp.zeros((s, proj_w - used_w), BF16)] if proj_w > used_w else []), axis=1)
    latent_w = ("w_uq", "w_ukv")
    g_in_k, recv[latent_w] = _mm(h1, d_proj, ta=True, out_dtype=BF16, name="mm_g_in",
                                 carry=(pack_grads(latent_w), False))
    grads["w_in"] = jnp.concatenate([g_in_k[:, off[n_]:off[n_] + w_] for n_, w_ in ref_w], axis=1)
    by_core = pack_grads(("w_in",))
    by_core = by_core.reshape((N_DEV // 2, 2) + by_core.shape[1:]).transpose(1, 0, 2, 3)
    from_sibling = _exchange(by_core, gather="sibling", name="swap_w_in")
    own_half = lax.dynamic_index_in_dim(by_core, lax.axis_index("c"), axis=0, keepdims=False)
    d_h1, recv_chips = _mm(d_proj, w_in_k, tb=True, name="mm_d_h1", carry=(_pair_sum(own_half, from_sibling), "chips"))
    grads["w_in"] = _slot_sum(recv_chips, "slot_sum_w_in").reshape(wts["w_in"].shape)
    grad_x, sums_1 = _norm_mod_bwd(d_h1, xs, wts["g_norm1"], sc1, d_x1)

    parts = [sums_1[0:1], sums_1[1:2], sums_2[3:4], sums_2[0:1], sums_2[1:2], sums_l[0:1],
             sums_1[2:3], sums_2[2:3], sums_lat[0:1], sums_lat[1:2], sums_h[0:1], sums_h[1:2], sums_l[1:2]]
    part_off, o = [], 0
    for p in parts:
        part_off.append(o)
        o += p.shape[1]
    all_rows = _gather_rows(jnp.concatenate(parts, axis=1), "gather_small")
    summed, loss_row = _small_sum(all_rows, part_off[12], part_off[12] + d, d)
    take = lambda i, w: summed[:, part_off[i]:part_off[i] + w]
    grads["b_ada"] = summed[:, :6 * d]
    grads["g_norm1"], grads["g_norm2"] = take(6, d), take(7, d)
    grads["g_q_latent"], grads["g_kv_latent"] = take(8, lat), take(9, lat)
    grads["g_q_head"], grads["g_k_head"] = take(10, QK_DIM), take(11, QK_DIM)
    n_ada = 6 * d // N_DEV
    d_ada_mine = lax.dynamic_slice_in_dim(all_rows[:, :6 * d], me * n_ada, n_ada, axis=1)
    grads["w_ada"] = _ada_grad(c_all.T, d_ada_mine)

    for names, slots in recv.items():
        unpack_grads(slots, names)

    delta, new_m, new_v = {}, {}, {}
    for n in ("w_ada",) + tuple(n for n, _ in BIG):
        delta[n], new_m[n], new_v[n] = _adamw(wts[n], grads[n], mom[n], var[n], "adamw_" + n)
    cat = lambda t: jnp.concatenate([t[n] for n in SMALL], axis=1)
    d_s, m_s, v_s = _adamw(cat(wts), cat(grads), cat(mom), cat(var), "adamw_small")
    o = 0
    for n in SMALL:
        w_ = wts[n].shape[1]
        delta[n], new_m[n], new_v[n] = d_s[:, o:o + w_], m_s[:, o:o + w_], v_s[:, o:o + w_]
        o += w_

    lead = lambda t: [t[n].reshape(env[n].shape) for n in WEIGHT_NAMES]
    return (loss_row[0, 0], grad_x[None], *lead(grads), *lead(delta), *lead(new_m), *lead(new_v))
```

```python
import jax
import jax.numpy as jnp
from jax import lax
from jax.experimental import pallas as pl
from jax.experimental.pallas import tpu as pltpu

F32 = jnp.float32
BF16 = jnp.bfloat16

N_DEV = 8
LANES = 128
PACK_W = 1024
VMEM_LIMIT = 48 * 1024 * 1024

EPS = 1e-6
ROPE_THETA = 10000.0
NOPE = 128
ROPE = 64
QK_DIM = NOPE + ROPE
QK_PAD = 2 * LANES
V_DIM = 128
SB_DIM = 128
ATT_Q = 512
ATT_K = 256
MM_TK = 2816
PAIR = 2
ROW_PARTS = 2

ADAM_LR = 0.001
ADAM_B1 = 0.9
ADAM_B2 = 0.999
ADAM_EPS = 1e-08
ADAM_WD = 0.01
ADAM_STEP = 10

WEIGHT_NAMES = ("w_ada", "b_ada", "g_norm1", "g_norm2", "w_in", "g_q_latent", "g_kv_latent", "w_uq", "w_ukv",
                "g_q_head", "g_k_head", "w_proj_mla", "w_proj_sb", "w_out", "w_ffn_in", "w_ffn_out")
BIG = (("w_in", True), ("w_uq", True), ("w_ukv", True), ("w_proj_mla", True), ("w_proj_sb", True),
       ("w_out", False), ("w_ffn_in", True), ("w_ffn_out", False))
SMALL = ("b_ada", "g_norm1", "g_norm2", "g_q_latent", "g_kv_latent", "g_q_head", "g_k_head")


def _tile(n, pref):
    if n <= pref:
        return n
    for step in (LANES, 8):
        for t in range(pref - pref % step, 0, -step):
            if n % t == 0:
                return t
    return n


def _params():
    return pltpu.CompilerParams(vmem_limit_bytes=VMEM_LIMIT)


def _sigmoid(x):
    return 1.0 / (1.0 + jnp.exp(-x))


def _dot(a, b):
    return lax.dot_general(a, b, (((1,), (0,)), ((), ())), preferred_element_type=F32)


def _dot_nt(a, b):
    return lax.dot_general(a, b, (((1,), (1,)), ((), ())), preferred_element_type=F32)


def _dot_tn(a, b):
    return lax.dot_general(a, b, (((0,), (0,)), ((), ())), preferred_element_type=F32)


def _split_dot(x, tri):
    hi = x.astype(BF16)
    lo = (x - hi.astype(F32)).astype(BF16)
    return _dot(hi, tri) + _dot(lo, tri)


_HBM = pl.BlockSpec(memory_space=pltpu.HBM)


def _carry_parts(carry):
    src, gather = carry
    if gather == "sibling":
        shape = tuple(src.shape[1:])
    elif gather == "chips" or not gather:
        shape = tuple(src.shape)
    else:
        shape = (N_DEV,) + tuple(src.shape)
    return (src, _HBM, jax.ShapeDtypeStruct(shape, src.dtype), _HBM,
            [pltpu.SemaphoreType.DMA((N_DEV - 1,)), pltpu.SemaphoreType.DMA((N_DEV - 1,)), pltpu.SemaphoreType.DMA(())])


def _exchange_copies(src_ref, dst_ref, send_sems, recv_sems, local_sem, gather):
    x, y, c = lax.axis_index("x"), lax.axis_index("y"), lax.axis_index("c")
    if gather == "sibling":
        return [pltpu.make_async_remote_copy(
            src_ref=src_ref.at[1 - c], dst_ref=dst_ref, send_sem=send_sems.at[0], recv_sem=recv_sems.at[0],
            device_id=(x, y, 1 - c), device_id_type=pl.DeviceIdType.MESH)]
    if gather == "chips":
        chip = 2 * x + y
        copies = [pltpu.make_async_copy(src_ref.at[chip], dst_ref.at[chip], local_sem)]
        for k in range(1, N_DEV // 2):
            peer = ((1 - x) if (k >> 1) & 1 else x, (1 - y) if k & 1 else y, c)
            copies.append(pltpu.make_async_remote_copy(
                src_ref=src_ref.at[2 * peer[0] + peer[1]], dst_ref=dst_ref.at[chip],
                send_sem=send_sems.at[k - 1], recv_sem=recv_sems.at[k - 1],
                device_id=peer, device_id_type=pl.DeviceIdType.MESH))
        return copies
    me = 4 * x + 2 * y + c

    def slot_for(idx):
        return src_ref if gather else src_ref.at[idx]

    copies = [pltpu.make_async_copy(slot_for(me), dst_ref.at[me], local_sem)]
    for k in range(1, N_DEV):
        peer = ((1 - x) if (k >> 2) & 1 else x, (1 - y) if (k >> 1) & 1 else y, (1 - c) if k & 1 else c)
        peer_idx = 4 * peer[0] + 2 * peer[1] + peer[2]
        copies.append(pltpu.make_async_remote_copy(
            src_ref=slot_for(peer_idx), dst_ref=dst_ref.at[me],
            send_sem=send_sems.at[k - 1], recv_sem=recv_sems.at[k - 1],
            device_id=peer, device_id_type=pl.DeviceIdType.MESH))
    return copies


def _two_level_copies(src_ref, dst_ref, send_sems, recv_sems, local_sem):
    x, y, c = lax.axis_index("x"), lax.axis_index("y"), lax.axis_index("c")
    me, sibling = (x, y, c), (x, y, 1 - c)
    chips = [(1 - x, y), (x, 1 - y), (1 - x, 1 - y)]

    def slot(px, py, pc):
        return dst_ref.at[4 * px + 2 * py + pc]

    def copy(k, block, to, own=False):
        return pltpu.make_async_remote_copy(
            src_ref=src_ref if own else slot(*block), dst_ref=slot(*block),
            send_sem=send_sems.at[k], recv_sem=recv_sems.at[k],
            device_id=to, device_id_type=pl.DeviceIdType.MESH)

    mine = pltpu.make_async_copy(src_ref, slot(*me), local_sem)
    first = [copy(0, me, sibling, own=True)] + [copy(1 + j, me, (*chip, c), own=True) for j, chip in enumerate(chips)]
    relays = [(copy(1 + j, (*chip, c), me), copy(4 + j, (*chip, c), sibling)) for j, chip in enumerate(chips)]
    late = [copy(0, sibling, me)] + [copy(4 + j, (*chip, 1 - c), me) for j, chip in enumerate(chips)]
    return mine, first, relays, late


def _two_level_start(refs):
    mine, first, _, _ = _two_level_copies(*refs)
    mine.start()
    for cp in first:
        cp.start()


def _two_level_finish(refs):
    mine, first, relays, late = _two_level_copies(*refs)
    for arrival, onward in relays:
        arrival.wait_recv()
        onward.start()
    for cp in late:
        cp.wait_recv()
    for cp in first + [onward for _, onward in relays]:
        cp.wait_send()
    mine.wait()


def _carried_exchange(refs, gather, first, last):
    @pl.when(first)
    def _():
        if gather == "two_level":
            _two_level_start(refs)
        else:
            for cp in _exchange_copies(*refs, gather):
                cp.start()

    @pl.when(last)
    def _():
        if gather == "two_level":
            _two_level_finish(refs)
        else:
            for cp in _exchange_copies(*refs, gather):
                cp.wait()


def _exchange(src, *, gather, name):
    operand, in_spec, out_shape, out_spec, scratch = _carry_parts((src, gather))

    def body(src_ref, dst_ref, send_sems, recv_sems, local_sem):
        copies = _exchange_copies(src_ref, dst_ref, send_sems, recv_sems, local_sem, gather)
        for cp in copies:
            cp.start()
        for cp in copies:
            cp.wait()

    return pl.pallas_call(body, name=name, out_shape=out_shape, in_specs=[in_spec], out_specs=out_spec,
                          scratch_shapes=scratch)(operand)


def _gather_two_level(src, name):
    operand, in_spec, out_shape, out_spec, scratch = _carry_parts((src, True))

    def body(*refs):
        _two_level_start(refs)
        _two_level_finish(refs)

    return pl.pallas_call(body, name=name, out_shape=out_shape, in_specs=[in_spec], out_specs=out_spec,
                          scratch_shapes=scratch)(operand)


def _gather_rows(v, name):
    n = v.shape[1]
    padded = -(-n // (8 * LANES)) * (8 * LANES)
    tiles = jnp.pad(v, ((0, 0), (0, padded - n))).reshape(padded // LANES, LANES)
    return _exchange(tiles, gather=True, name=name).reshape(N_DEV, padded)[:, :n]


def _split_refs(refs, n_in, n_out, carry):
    if carry is None:
        return refs[:n_in], refs[n_in:n_in + n_out], refs[n_in + n_out:], None
    ins, src_ref = refs[:n_in], refs[n_in]
    outs, dst_ref = refs[n_in + 1:n_in + 1 + n_out], refs[n_in + 1 + n_out]
    rest = refs[n_in + n_out + 2:]
    return ins, outs, rest[:-3], (src_ref, dst_ref) + tuple(rest[-3:])


def _with_carry(carry, args, in_specs, out_shape, out_specs, scratch):
    if carry is not None:
        operand, c_in, c_shape, c_out, c_scratch = _carry_parts(carry)
        args.append(operand)
        in_specs.append(c_in)
        out_shape.append(c_shape)
        out_specs.append(c_out)
        scratch.extend(c_scratch)


def _slot_of_chunk(j):
    return j // 2 + (N_DEV // 2) * (j % 2)


def _mm(a, b, *, ta=False, tb=False, out_dtype=F32, name, carry=None, shards=None):
    kdim, m = a.shape if ta else a.shape[::-1]
    if shards == "b":
        chunk = b.shape[2]
        n, kdim_b = (b.shape[1], N_DEV * chunk) if tb else (N_DEV * chunk, b.shape[1])
    else:
        n, kdim_b = b.shape if tb else b.shape[::-1]
    assert kdim == kdim_b, (a.shape, b.shape, ta, tb)
    tm, tn, tk = _tile(m, 1024), _tile(n, 1024), _tile(kdim, MM_TK)
    if shards == "b":
        tn, tk = (tn, chunk) if tb else (chunk, tk)
    elif shards == "out":
        chunk = tn = n // N_DEV
    grid = (m // tm, n // tn, kdim // tk)
    nk = grid[2]
    dims = (((0 if ta else 1,), (1 if tb else 0,)), ((), ()))

    def body(*refs):
        (a_ref, b_ref), (o_ref,), scratch_refs, xrefs = _split_refs(refs, 2, 1, carry)
        i, j, k = pl.program_id(0), pl.program_id(1), pl.program_id(2)
        if carry is not None:
            _carried_exchange(xrefs, carry[1], (i == 0) & (j == 0) & (k == 0),
                              (i == grid[0] - 1) & (j == grid[1] - 1) & (k == nk - 1))
        prod = lax.dot_general(a_ref[...].astype(BF16), b_ref[...].astype(BF16), dims, preferred_element_type=F32)
        if nk == 1:
            o_ref[...] = prod.astype(o_ref.dtype)
        else:
            acc_ref = scratch_refs[0]

            @pl.when(k == 0)
            def _():
                acc_ref[...] = prod

            @pl.when(k > 0)
            def _():
                acc_ref[...] += prod

            @pl.when(k == nk - 1)
            def _():
                o_ref[...] = acc_ref[...].astype(o_ref.dtype)

    a_spec = (pl.BlockSpec((tk, tm), lambda i, j, k: (k, i)) if ta else pl.BlockSpec((tm, tk), lambda i, j, k: (i, k)))
    if shards == "b":
        b_spec = (pl.BlockSpec((None, tn, tk), lambda i, j, k: (_slot_of_chunk(k), j, 0)) if tb
                  else pl.BlockSpec((None, tk, tn), lambda i, j, k: (_slot_of_chunk(j), k, 0)))
    else:
        b_spec = (pl.BlockSpec((tn, tk), lambda i, j, k: (j, k)) if tb else pl.BlockSpec((tk, tn), lambda i, j, k: (k, j)))
    args, in_specs = [a, b], [a_spec, b_spec]
    if shards == "out":
        out_shape = [jax.ShapeDtypeStruct((N_DEV, m, tn), out_dtype)]
        out_specs = [pl.BlockSpec((None, tm, tn), lambda i, j, k: (_slot_of_chunk(j), i, 0))]
    else:
        out_shape, out_specs = [jax.ShapeDtypeStruct((m, n), out_dtype)], [pl.BlockSpec((tm, tn), lambda i, j, k: (i, j))]
    scratch = [] if nk == 1 else [pltpu.VMEM((tm, tn), F32)]
    _with_carry(carry, args, in_specs, out_shape, out_specs, scratch)
    out = pl.pallas_call(
        body, name=name, grid=grid, out_shape=tuple(out_shape), in_specs=in_specs, out_specs=tuple(out_specs),
        scratch_shapes=scratch, compiler_params=_params(),
    )(*args)
    return out if carry is not None else out[0]


def _ada_fwd(c_all, w_shard):
    d, n = w_shard.shape
    tn = _tile(n, 512)

    def body(c_ref, w_ref, o_ref):
        cv = c_ref[...]
        o_ref[...] = jnp.dot(cv * _sigmoid(cv), w_ref[...], precision=lax.Precision.HIGHEST,
                             preferred_element_type=F32)

    return pl.pallas_call(
        body, name="ada_fwd", grid=(n // tn,),
        out_shape=jax.ShapeDtypeStruct((N_DEV, n), F32),
        in_specs=[pl.BlockSpec((N_DEV, d), lambda j: (0, 0)), pl.BlockSpec((d, tn), lambda j: (0, j))],
        out_specs=pl.BlockSpec((N_DEV, tn), lambda j: (0, j)),
        compiler_params=_params(),
    )(c_all, w_shard)


def _ada_grad(c_all_t, d_rows):
    d, n = c_all_t.shape[0], d_rows.shape[1]
    tn = _tile(n, 512)

    def body(ct_ref, d_ref, o_ref):
        cv = ct_ref[...]
        s = cv * _sigmoid(cv)
        dv = d_ref[...]
        acc = s[:, 0:1] * dv[0:1, :]
        for b in range(1, N_DEV):
            acc = acc + s[:, b:b + 1] * dv[b:b + 1, :]
        o_ref[...] = acc

    return pl.pallas_call(
        body, name="ada_grad", grid=(n // tn,),
        out_shape=jax.ShapeDtypeStruct((d, n), F32),
        in_specs=[pl.BlockSpec((d, N_DEV), lambda j: (0, 0)), pl.BlockSpec((N_DEV, tn), lambda j: (0, j))],
        out_specs=pl.BlockSpec((d, tn), lambda j: (0, j)),
        compiler_params=_params(),
    )(c_all_t, d_rows)


def _row(ts, w, col=0):
    return pl.BlockSpec((ts, w), lambda i, col=col: (i, col))


def _vec(w):
    return pl.BlockSpec((1, w), lambda i: (0, 0))


def _norm_mod(x, g, sc, sh):
    s, d = x.shape
    ts = _tile(s, 512)

    def body(x_ref, g_ref, sc_ref, sh_ref, h_ref):
        xv = x_ref[...]
        r = lax.rsqrt(jnp.mean(xv * xv, axis=-1, keepdims=True) + EPS)
        h_ref[...] = ((xv * r) * g_ref[...] * (1.0 + sc_ref[...]) + sh_ref[...]).astype(BF16)

    return pl.pallas_call(
        body, name="norm_mod", grid=(s // ts,),
        out_shape=jax.ShapeDtypeStruct((s, d), BF16),
        in_specs=[_row(ts, d), _vec(d), _vec(d), _vec(d)],
        out_specs=_row(ts, d), compiler_params=_params(),
    )(x, g, sc, sh)


def _latent_norm(proj, off_q, off_kv, lat, g_q, g_kv):
    s = proj.shape[0]
    ts = _tile(s, 512)

    def body(cq_ref, ckv_ref, gq_ref, gkv_ref, oq_ref, okv_ref):
        for c_ref, g_ref, o_ref in ((cq_ref, gq_ref, oq_ref), (ckv_ref, gkv_ref, okv_ref)):
            v = c_ref[...]
            r = lax.rsqrt(jnp.mean(v * v, axis=-1, keepdims=True) + EPS)
            o_ref[...] = ((v * r) * g_ref[...]).astype(BF16)

    return pl.pallas_call(
        body, name="latent_norm", grid=(s // ts,),
        out_shape=(jax.ShapeDtypeStruct((s, lat), BF16), jax.ShapeDtypeStruct((s, lat), BF16)),
        in_specs=[_row(ts, lat, off_q // lat), _row(ts, lat, off_kv // lat), _vec(lat), _vec(lat)],
        out_specs=(_row(ts, lat), _row(ts, lat)), compiler_params=_params(),
    )(proj, proj, g_q, g_kv)


def _latent_norm_bwd(proj, off_q, off_kv, lat, g_q, g_kv, d_cqn, d_ckvn):
    s = proj.shape[0]
    ts = _tile(s, 512)

    def body(cq_ref, ckv_ref, gq_ref, gkv_ref, dq_ref, dkv_ref, oq_ref, okv_ref, sums_ref):
        @pl.when(pl.program_id(0) == 0)
        def _():
            sums_ref[...] = jnp.zeros_like(sums_ref)

        for row, (c_ref, g_ref, d_ref, o_ref) in enumerate(((cq_ref, gq_ref, dq_ref, oq_ref),
                                                             (ckv_ref, gkv_ref, dkv_ref, okv_ref))):
            v = c_ref[...]
            r = lax.rsqrt(jnp.mean(v * v, axis=-1, keepdims=True) + EPS)
            vn = v * r
            dn = d_ref[...]
            sums_ref[row:row + 1, :] += jnp.sum(dn * vn, axis=0, keepdims=True)
            dvn = dn * g_ref[...]
            o_ref[...] = (r * (dvn - vn * jnp.mean(dvn * vn, axis=-1, keepdims=True))).astype(BF16)

    return pl.pallas_call(
        body, name="latent_norm_bwd", grid=(s // ts,),
        out_shape=(jax.ShapeDtypeStruct((s, lat), BF16), jax.ShapeDtypeStruct((s, lat), BF16),
                   jax.ShapeDtypeStruct((8, lat), F32)),
        in_specs=[_row(ts, lat, off_q // lat), _row(ts, lat, off_kv // lat), _vec(lat), _vec(lat),
                  _row(ts, lat), _row(ts, lat)],
        out_specs=(_row(ts, lat), _row(ts, lat), pl.BlockSpec((8, lat), lambda i: (0, 0))),
        compiler_params=_params(),
    )(proj, proj, g_q, g_kv, d_cqn, d_ckvn)


def _merge(proj, off_a, off_b, ya_p, yb_p):
    s, d = ya_p.shape
    ts = _tile(s, 256)

    def body(ga_ref, gb_ref, ya_ref, yb_ref, o_ref):
        o_ref[...] = (_sigmoid(ga_ref[...]) * ya_ref[...] + _sigmoid(gb_ref[...]) * yb_ref[...]).astype(BF16)

    return pl.pallas_call(
        body, name="merge", grid=(s // ts,),
        out_shape=jax.ShapeDtypeStruct((s, d), BF16),
        in_specs=[_row(ts, d, off_a // d), _row(ts, d, off_b // d), _row(ts, d), _row(ts, d)],
        out_specs=_row(ts, d), compiler_params=_params(),
    )(proj, proj, ya_p, yb_p)


def _merge_bwd(proj, off_a, off_b, ya_p, yb_p, d_merged):
    s, d = ya_p.shape
    ts = _tile(s, 256)

    def body(ga_ref, gb_ref, ya_ref, yb_ref, dm_ref, dya_ref, dyb_ref, dga_ref, dgb_ref):
        dm = dm_ref[...]
        for g_ref, y_ref, dy_ref, dg_ref in ((ga_ref, ya_ref, dya_ref, dga_ref), (gb_ref, yb_ref, dyb_ref, dgb_ref)):
            sg = _sigmoid(g_ref[...])
            dy_ref[...] = (dm * sg).astype(BF16)
            dg_ref[...] = (dm * y_ref[...] * sg * (1.0 - sg)).astype(BF16)

    sd = jax.ShapeDtypeStruct((s, d), BF16)
    return pl.pallas_call(
        body, name="merge_bwd", grid=(s // ts,),
        out_shape=(sd, sd, sd, sd),
        in_specs=[_row(ts, d, off_a // d), _row(ts, d, off_b // d), _row(ts, d), _row(ts, d), _row(ts, d)],
        out_specs=(_row(ts, d),) * 4, compiler_params=_params(),
    )(proj, proj, ya_p, yb_p, d_merged)


def _resid_norm_mod(x, o, gt, g, sc, sh):
    s, d = x.shape
    ts = _tile(s, 256)

    def body(x_ref, o_ref, gt_ref, g_ref, sc_ref, sh_ref, x1_ref, h_ref):
        x1 = x_ref[...] + gt_ref[...] * o_ref[...]
        x1_ref[...] = x1
        r = lax.rsqrt(jnp.mean(x1 * x1, axis=-1, keepdims=True) + EPS)
        h_ref[...] = ((x1 * r) * g_ref[...] * (1.0 + sc_ref[...]) + sh_ref[...]).astype(BF16)

    return pl.pallas_call(
        body, name="resid_norm_mod", grid=(s // ts,),
        out_shape=(jax.ShapeDtypeStruct((s, d), F32), jax.ShapeDtypeStruct((s, d), BF16)),
        in_specs=[_row(ts, d), _row(ts, d), _vec(d), _vec(d), _vec(d), _vec(d)],
        out_specs=(_row(ts, d), _row(ts, d)), compiler_params=_params(),
    )(x, o, gt, g, sc, sh)


def _loss_head(x1, o2, gt2, target):
    s, d = x1.shape
    ts = _tile(s, 256)

    def body(x1_ref, o2_ref, gt_ref, t_ref, dy_ref, do_ref, sums_ref):
        @pl.when(pl.program_id(0) == 0)
        def _():
            sums_ref[...] = jnp.zeros_like(sums_ref)

        o2 = o2_ref[...]
        e = x1_ref[...] + gt_ref[...] * o2 - t_ref[...]
        dy = e / d
        dy_ref[...] = dy
        do_ref[...] = (dy * gt_ref[...]).astype(BF16)
        sums_ref[0:1, :] += jnp.sum(dy * o2, axis=0, keepdims=True)
        sums_ref[1:2, :] += jnp.sum(e * e, axis=0, keepdims=True)

    return pl.pallas_call(
        body, name="loss_head", grid=(s // ts,),
        out_shape=(jax.ShapeDtypeStruct((s, d), F32), jax.ShapeDtypeStruct((s, d), BF16),
                   jax.ShapeDtypeStruct((8, d), F32)),
        in_specs=[_row(ts, d), _row(ts, d), _vec(d), _row(ts, d)],
        out_specs=(_row(ts, d), _row(ts, d), pl.BlockSpec((8, d), lambda i: (0, 0))),
        compiler_params=_params(),
    )(x1, o2, gt2, target)


def _norm_mod_bwd(dh, xin, g, sc, dres, gate=None):
    s, d = xin.shape
    ts = _tile(s, 256)
    gated = gate is not None

    def body(*refs):
        if gated:
            dh_ref, x_ref, g_ref, sc_ref, dr_ref, o_ref, gt_ref, dx_ref, do_ref, sums_ref = refs
        else:
            dh_ref, x_ref, g_ref, sc_ref, dr_ref, dx_ref, sums_ref = refs

        @pl.when(pl.program_id(0) == 0)
        def _():
            sums_ref[...] = jnp.zeros_like(sums_ref)

        xv, dhv = x_ref[...], dh_ref[...]
        r = lax.rsqrt(jnp.mean(xv * xv, axis=-1, keepdims=True) + EPS)
        xn = xv * r
        one_sc = 1.0 + sc_ref[...]
        sums_ref[0:1, :] += jnp.sum(dhv, axis=0, keepdims=True)
        sums_ref[1:2, :] += jnp.sum(dhv * (xn * g_ref[...]), axis=0, keepdims=True)
        sums_ref[2:3, :] += jnp.sum(dhv * one_sc * xn, axis=0, keepdims=True)
        dxn = dhv * one_sc * g_ref[...]
        dx = dr_ref[...] + r * (dxn - xn * jnp.mean(dxn * xn, axis=-1, keepdims=True))
        dx_ref[...] = dx
        if gated:
            sums_ref[3:4, :] += jnp.sum(dx * o_ref[...], axis=0, keepdims=True)
            do_ref[...] = (dx * gt_ref[...]).astype(BF16)

    in_specs = [_row(ts, d), _row(ts, d), _vec(d), _vec(d), _row(ts, d)]
    args = [dh, xin, g, sc, dres]
    out_shape = [jax.ShapeDtypeStruct((s, d), F32)]
    out_specs = [_row(ts, d)]
    if gated:
        in_specs += [_row(ts, d), _vec(d)]
        args += list(gate)
        out_shape.append(jax.ShapeDtypeStruct((s, d), BF16))
        out_specs.append(_row(ts, d))
    out_shape.append(jax.ShapeDtypeStruct((8, d), F32))
    out_specs.append(pl.BlockSpec((8, d), lambda i: (0, 0)))
    return pl.pallas_call(
        body, name="norm_mod_bwd_gated" if gated else "norm_mod_bwd", grid=(s // ts,),
        out_shape=tuple(out_shape), in_specs=in_specs, out_specs=tuple(out_specs), compiler_params=_params(),
    )(*args)


def _ffn_tile(f):
    return 2 * f // N_DEV


def _swiglu(gu):
    s, f2 = gu.shape
    f = f2 // 2
    ts, tc = _tile(s, 512), _ffn_tile(f)

    def body(gu_ref, o_ref):
        gv, uv = gu_ref[:, :tc].astype(F32), gu_ref[:, tc:].astype(F32)
        o_ref[...] = (gv * _sigmoid(gv) * uv).astype(BF16)

    return pl.pallas_call(
        body, name="swiglu", grid=(s // ts, f // tc),
        out_shape=jax.ShapeDtypeStruct((s, f), BF16),
        in_specs=[pl.BlockSpec((ts, 2 * tc), lambda i, j: (i, j))],
        out_specs=pl.BlockSpec((ts, tc), lambda i, j: (i, j)), compiler_params=_params(),
    )(gu)


def _swiglu_bwd(gu, d_act):
    s, f2 = gu.shape
    f = f2 // 2
    ts, tc = _tile(s, 512), _ffn_tile(f)

    def body(gu_ref, da_ref, o_ref):
        gv, uv, da = gu_ref[:, :tc].astype(F32), gu_ref[:, tc:].astype(F32), da_ref[...]
        sg = _sigmoid(gv)
        o_ref[:, :tc] = (da * uv * (sg * (1.0 + gv * (1.0 - sg)))).astype(BF16)
        o_ref[:, tc:] = (da * (gv * sg)).astype(BF16)

    return pl.pallas_call(
        body, name="swiglu_bwd", grid=(s // ts, f // tc),
        out_shape=jax.ShapeDtypeStruct((s, f2), BF16),
        in_specs=[pl.BlockSpec((ts, 2 * tc), lambda i, j: (i, j)), pl.BlockSpec((ts, tc), lambda i, j: (i, j))],
        out_specs=pl.BlockSpec((ts, 2 * tc), lambda i, j: (i, j)), compiler_params=_params(),
    )(gu, d_act)


def _swap_halves(t):
    return pltpu.roll(t, ROPE // 2, 1) + pltpu.roll(t, LANES - ROPE // 2, 1)


def _head_norm(raw):
    r = lax.rsqrt(jnp.sum(raw * raw, axis=-1, keepdims=True) / QK_DIM + EPS)
    return raw * r, r


def _rope_fwd(v, cos, sin):
    rope_tile = v[:, NOPE:]
    return jnp.concatenate([v[:, :NOPE], rope_tile * cos + _swap_halves(rope_tile) * sin], axis=1)


def _rope_bwd(d, cos, sin, lane_ok):
    d_tile = d[:, NOPE:]
    return jnp.concatenate([d[:, :NOPE], d_tile * cos + _swap_halves(d_tile * sin) * lane_ok], axis=1)


def _qk_prep(q_raw, kv, proj, off_pe, cos, sin, g_q, g_k):
    s, hw = q_raw.shape
    heads = hw // QK_PAD
    ts = _tile(s, 512)

    def body(q_ref, kv_ref, pe_ref, cos_ref, sin_ref, gq_ref, gk_ref, qo_ref, ko_ref, vo_ref):
        cos_v, sin_v = cos_ref[...], sin_ref[...]
        qn, _ = _head_norm(q_ref[...])
        qo_ref[...] = _rope_fwd(qn * gq_ref[...], cos_v, sin_v).astype(BF16)
        kvv = kv_ref[...]
        kn, _ = _head_norm(jnp.concatenate([kvv[:, :NOPE], pe_ref[...]], axis=1))
        ko_ref[...] = _rope_fwd(kn * gk_ref[...], cos_v, sin_v).astype(BF16)
        vo_ref[...] = kvv[:, NOPE:].astype(BF16)

    blk = lambda w: pl.BlockSpec((ts, w), lambda i, h: (i, h))
    fixed = lambda w, col=0: pl.BlockSpec((ts, w), lambda i, h, col=col: (i, col))
    vec = pl.BlockSpec((1, QK_PAD), lambda i, h: (0, 0))
    return pl.pallas_call(
        body, name="qk_prep", grid=(s // ts, heads),
        out_shape=(jax.ShapeDtypeStruct((s, hw), BF16), jax.ShapeDtypeStruct((s, hw), BF16),
                   jax.ShapeDtypeStruct((s, heads * V_DIM), BF16)),
        in_specs=[blk(QK_PAD), blk(QK_PAD), fixed(LANES, off_pe // LANES), fixed(LANES), fixed(LANES), vec, vec],
        out_specs=(blk(QK_PAD), blk(QK_PAD), blk(V_DIM)), compiler_params=_params(),
    )(q_raw, kv, proj, cos, sin, g_q, g_k)


def _qk_prep_bwd(q_raw, kv, proj, off_pe, cos, sin, g_q, g_k, dq, dk, dv):
    s, hw = q_raw.shape
    heads = hw // QK_PAD
    ts = _tile(s, 512)

    def body(q_ref, kv_ref, pe_ref, cos_ref, sin_ref, gq_ref, gk_ref, dq_ref, dk_ref, dv_ref,
             dqr_ref, dkv_ref, dpe_ref, sums_ref):
        i, h = pl.program_id(0), pl.program_id(1)

        @pl.when((i == 0) & (h == 0))
        def _():
            sums_ref[...] = jnp.zeros_like(sums_ref)

        cos_v, sin_v = cos_ref[...], sin_ref[...]
        lane_ok = (lax.broadcasted_iota(jnp.int32, (ts, LANES), 1) < ROPE).astype(F32)

        def one(raw, g, d_post, row):
            vn, r = _head_norm(raw)
            d_pre = _rope_bwd(d_post, cos_v, sin_v, lane_ok)
            sums_ref[row:row + 1, :] += jnp.sum(d_pre * vn, axis=0, keepdims=True)
            dvn = d_pre * g
            return r * (dvn - vn * (jnp.sum(dvn * vn, axis=-1, keepdims=True) / QK_DIM))

        dqr_ref[...] = one(q_ref[...], gq_ref[...], dq_ref[...], 0).astype(BF16)
        kvv = kv_ref[...]
        d_kraw = one(jnp.concatenate([kvv[:, :NOPE], pe_ref[...]], axis=1), gk_ref[...], dk_ref[...], 1)
        dkv_ref[...] = jnp.concatenate([d_kraw[:, :NOPE], dv_ref[...]], axis=1).astype(BF16)

        @pl.when(h == 0)
        def _():
            dpe_ref[...] = jnp.zeros_like(dpe_ref)

        dpe_ref[...] += d_kraw[:, NOPE:]

    blk = lambda w: pl.BlockSpec((ts, w), lambda i, h: (i, h))
    fixed = lambda w, col=0: pl.BlockSpec((ts, w), lambda i, h, col=col: (i, col))
    vec = pl.BlockSpec((1, QK_PAD), lambda i, h: (0, 0))
    return pl.pallas_call(
        body, name="qk_prep_bwd", grid=(s // ts, heads),
        out_shape=(jax.ShapeDtypeStruct((s, hw), BF16), jax.ShapeDtypeStruct((s, hw), BF16),
                   jax.ShapeDtypeStruct((s, LANES), F32), jax.ShapeDtypeStruct((8, QK_PAD), F32)),
        in_specs=[blk(QK_PAD), blk(QK_PAD), fixed(LANES, off_pe // LANES), fixed(LANES), fixed(LANES), vec, vec,
                  blk(QK_PAD), blk(QK_PAD), blk(V_DIM)],
        out_specs=(blk(QK_PAD), blk(QK_PAD), fixed(LANES), pl.BlockSpec((8, QK_PAD), lambda i, h: (0, 0))),
        compiler_params=_params(),
    )(q_raw, kv, proj, cos, sin, g_q, g_k, dq, dk, dv)


def _att_blocks(s):
    tq = _tile(s, ATT_Q)
    tk = _tile(tq, ATT_K)
    return tq, tk, tq // tk


def _pairing(heads):
    return PAIR if heads % PAIR == 0 else 1


def _lanes(e, width):
    return slice(e * width, (e + 1) * width)


def _visible(qi, kb, tq, tk, strict):
    row = qi * tq + lax.broadcasted_iota(jnp.int32, (tq, tk), 0)
    col = kb * tk + lax.broadcasted_iota(jnp.int32, (tq, tk), 1)
    return (col < row) if strict else (col <= row)


def _first_last(heads, nq):
    h, qi = pl.program_id(0), pl.program_id(1)
    return (h == 0) & (qi == 0), (h == heads - 1) & (qi == nq - 1)


def _mla_fwd(q, k, v, carry=None):
    s = q.shape[0]
    heads = q.shape[1] // QK_PAD
    tq, tk, ratio = _att_blocks(s)
    nq = s // tq
    scale = QK_DIM ** -0.5

    hp = _pairing(heads)
    ev = range(hp)

    def body(*refs):
        (q_ref, k_ref, v_ref), (o_ref, lse_ref), (acc_ref,), xrefs = _split_refs(refs, 3, 2, carry)
        if carry is not None:
            _carried_exchange(xrefs, carry[1], *_first_last(heads // hp, nq))
        qi = pl.program_id(1)
        qs = [q_ref[:, _lanes(e, QK_PAD)] for e in ev]
        acc_ref[...] = jnp.zeros_like(acc_ref)

        def step(kb, state, masked):
            ms, ls = state
            off = pl.multiple_of(kb * tk, tk)
            sc = [_dot_nt(qs[e], k_ref[pl.ds(off, tk), _lanes(e, QK_PAD)]) * scale for e in ev]
            if masked:
                mask = _visible(qi, kb, tq, tk, False)
                sc = [jnp.where(mask, sc[e], -1e30) for e in ev]
            m_new = [jnp.maximum(ms[e], jnp.max(sc[e], axis=-1, keepdims=True)) for e in ev]
            alpha = [jnp.exp(ms[e] - m_new[e]) for e in ev]
            p = [jnp.exp(sc[e] - m_new[e]) for e in ev]
            for e in ev:
                lanes = _lanes(e, V_DIM)
                acc_ref[:, lanes] = alpha[e] * acc_ref[:, lanes] + _dot(p[e].astype(BF16), v_ref[pl.ds(off, tk), lanes])
            return tuple(m_new), tuple(alpha[e] * ls[e] + jnp.sum(p[e], axis=-1, keepdims=True) for e in ev)

        state = (tuple(jnp.full((tq, 1), -1e30, F32) for _ in ev), tuple(jnp.zeros((tq, 1), F32) for _ in ev))
        state = lax.fori_loop(0, qi * ratio, lambda kb, st: step(kb, st, False), state)
        for i in range(ratio):
            state = step(qi * ratio + i, state, True)
        ms, ls = state
        for e in ev:
            o_ref[:, _lanes(e, V_DIM)] = (acc_ref[:, _lanes(e, V_DIM)] / ls[e]).astype(BF16)
            lse_ref[e] = ms[e] + jnp.log(ls[e])

    args = [q, k, v]
    in_specs = [pl.BlockSpec((tq, hp * QK_PAD), lambda h, i: (i, h)), pl.BlockSpec((s, hp * QK_PAD), lambda h, i: (0, h)),
                pl.BlockSpec((s, hp * V_DIM), lambda h, i: (0, h))]
    out_shape = [jax.ShapeDtypeStruct((s, heads * V_DIM), BF16), jax.ShapeDtypeStruct((heads, s, 1), F32)]
    out_specs = [pl.BlockSpec((tq, hp * V_DIM), lambda h, i: (i, h)), pl.BlockSpec((hp, tq, 1), lambda h, i: (h, i, 0))]
    scratch = [pltpu.VMEM((tq, hp * V_DIM), F32)]
    _with_carry(carry, args, in_specs, out_shape, out_specs, scratch)
    return pl.pallas_call(
        body, name="mla_fwd", grid=(heads // hp, nq), out_shape=tuple(out_shape), in_specs=in_specs,
        out_specs=tuple(out_specs), scratch_shapes=scratch, compiler_params=_params(),
    )(*args)


def _mla_bwd(q, k, v, o, lse, do, carry=None):
    s = q.shape[0]
    heads = q.shape[1] // QK_PAD
    tq, tk, ratio = _att_blocks(s)
    nq = s // tq
    scale = QK_DIM ** -0.5

    hp = _pairing(heads)
    ev = range(hp)

    def body(*refs):
        (q_ref, k_ref, v_ref, o_ref, lse_ref, do_ref), (dq_ref, dk_ref, dv_ref), _, xrefs = _split_refs(refs, 6, 3, carry)
        if carry is not None:
            _carried_exchange(xrefs, carry[1], *_first_last(heads // hp, nq))
        qi = pl.program_id(1)

        @pl.when(qi == 0)
        def _():
            dk_ref[...] = jnp.zeros_like(dk_ref)
            dv_ref[...] = jnp.zeros_like(dv_ref)

        qs = [q_ref[:, _lanes(e, QK_PAD)] for e in ev]
        dos = [do_ref[:, _lanes(e, V_DIM)] for e in ev]
        do_b = [dos[e].astype(BF16) for e in ev]
        delta = [jnp.sum(dos[e] * o_ref[:, _lanes(e, V_DIM)].astype(F32), axis=-1, keepdims=True) for e in ev]
        lse_v = [lse_ref[e] for e in ev]
        dq_ref[...] = jnp.zeros_like(dq_ref)

        def step(kb, masked):
            off = pl.multiple_of(kb * tk, tk)
            ks = [k_ref[pl.ds(off, tk), _lanes(e, QK_PAD)] for e in ev]
            vs = [v_ref[pl.ds(off, tk), _lanes(e, V_DIM)] for e in ev]
            sc = [_dot_nt(qs[e], ks[e]) for e in ev]
            dp = [_dot_nt(do_b[e], vs[e]) for e in ev]
            p = [jnp.exp(sc[e] * scale - lse_v[e]) for e in ev]
            if masked:
                mask = _visible(qi, kb, tq, tk, False)
                p = [jnp.where(mask, p[e], 0.0) for e in ev]
            ds = [(p[e] * (dp[e] - delta[e]) * scale).astype(BF16) for e in ev]
            for e in ev:
                dv_ref[pl.ds(off, tk), _lanes(e, V_DIM)] += _dot_tn(p[e].astype(BF16), do_b[e])
            for e in ev:
                dk_ref[pl.ds(off, tk), _lanes(e, QK_PAD)] += _dot_tn(ds[e], qs[e])
            for e in ev:
                dq_ref[:, _lanes(e, QK_PAD)] += _dot(ds[e], ks[e])
            return 0

        lax.fori_loop(0, qi * ratio, lambda kb, _: step(kb, False), 0)
        for i in range(ratio):
            step(qi * ratio + i, True)

    args = [q, k, v, o, lse, do]
    wide, narrow = hp * QK_PAD, hp * V_DIM
    in_specs = [pl.BlockSpec((tq, wide), lambda h, i: (i, h)), pl.BlockSpec((s, wide), lambda h, i: (0, h)),
                pl.BlockSpec((s, narrow), lambda h, i: (0, h)), pl.BlockSpec((tq, narrow), lambda h, i: (i, h)),
                pl.BlockSpec((hp, tq, 1), lambda h, i: (h, i, 0)), pl.BlockSpec((tq, narrow), lambda h, i: (i, h))]
    out_shape = [jax.ShapeDtypeStruct(q.shape, F32), jax.ShapeDtypeStruct(k.shape, F32),
                 jax.ShapeDtypeStruct(v.shape, F32)]
    out_specs = [pl.BlockSpec((tq, wide), lambda h, i: (i, h)), pl.BlockSpec((s, wide), lambda h, i: (0, h)),
                 pl.BlockSpec((s, narrow), lambda h, i: (0, h))]
    scratch = []
    _with_carry(carry, args, in_specs, out_shape, out_specs, scratch)
    return pl.pallas_call(
        body, name="mla_bwd", grid=(heads // hp, nq), out_shape=tuple(out_shape), in_specs=in_specs,
        out_specs=tuple(out_specs), scratch_shapes=scratch, compiler_params=_params(),
    )(*args)


def _sb_terms(qv, k_blk, scale, mask):
    z = _dot_nt(qv, k_blk) * scale
    log_beta = jnp.minimum(z, 0.0) - jnp.log(1.0 + jnp.exp(-jnp.abs(z)))
    log_rest = log_beta - z
    if mask is not None:
        log_rest = jnp.where(mask, log_rest, 0.0)
    return log_beta, log_rest


def _head_lanes(e):
    return slice(e * SB_DIM, (e + 1) * SB_DIM)


def _sb_specs(s, tq, off_q, off_k, off_v, hp):
    w = hp * SB_DIM
    assert off_q % w == 0 and off_k % w == 0 and off_v % w == 0
    return [pl.BlockSpec((tq, w), lambda h, i: (i, off_q // w + h)),
            pl.BlockSpec((s, w), lambda h, i: (0, off_k // w + h)),
            pl.BlockSpec((s, w), lambda h, i: (0, off_v // w + h))]


def _sb_fwd(qkv, off_q, off_k, off_v, heads, carry=None):
    s = qkv.shape[0]
    tq, tk, ratio = _att_blocks(s)
    nq = s // tq
    hp = _pairing(heads)
    scale = SB_DIM ** -0.5

    def body(*refs):
        (q_ref, k_ref, v_ref), (o_ref, tot_ref), (acc_ref,), xrefs = _split_refs(refs, 3, 2, carry)
        if carry is not None:
            _carried_exchange(xrefs, carry[1], *_first_last(heads // hp, nq))
        qi = pl.program_id(1)
        qs = [q_ref[:, _head_lanes(e)] for e in range(hp)]
        after = (lax.broadcasted_iota(jnp.int32, (tk, tk), 0) > lax.broadcasted_iota(jnp.int32, (tk, tk), 1)).astype(BF16)
        acc_ref[...] = jnp.zeros_like(acc_ref)
        row_parts = [slice(r * (tq // ROW_PARTS), (r + 1) * (tq // ROW_PARTS)) for r in range(ROW_PARTS)]

        def step(kb, tails, masked):
            off = pl.multiple_of(kb * tk, tk)
            mask = _visible(qi, kb, tq, tk, True) if masked else None
            units = [(e, rows) for e in range(hp) for rows in row_parts]
            ks = [k_ref[pl.ds(off, tk), _head_lanes(e)] for e in range(hp)]
            vs = [v_ref[pl.ds(off, tk), _head_lanes(e)] for e in range(hp)]
            terms = [_sb_terms(qs[e][rows], ks[e], scale, None if mask is None else mask[rows]) for e, rows in units]
            sums = [_split_dot(t[1], after) for t in terms]
            a = [jnp.exp(t[0] + (sm + tails[e][rows])) for t, sm, (e, rows) in zip(terms, sums, units)]
            if masked:
                a = [jnp.where(mask[rows], a_, 0.0) for a_, (e, rows) in zip(a, units)]
            for a_, (e, rows) in zip(a, units):
                acc_ref[rows, _head_lanes(e)] += _dot(a_.astype(BF16), vs[e])
            rest = [jnp.sum(t[1], axis=-1, keepdims=True) for t in terms]
            return tuple(tails[e] + jnp.concatenate(rest[e * len(row_parts):(e + 1) * len(row_parts)], axis=0)
                         for e in range(hp))

        tails = tuple(jnp.zeros((tq, 1), F32) for _ in range(hp))
        for i in range(ratio):
            tails = step((qi + 1) * ratio - 1 - i, tails, True)
        tails = lax.fori_loop(0, qi * ratio, lambda i, t_: step(qi * ratio - 1 - i, t_, False), tails)
        o_ref[...] = acc_ref[...].astype(BF16)
        for e in range(hp):
            tot_ref[e] = tails[e]

    w = hp * SB_DIM
    args, in_specs = [qkv, qkv, qkv], _sb_specs(s, tq, off_q, off_k, off_v, hp)
    out_shape = [jax.ShapeDtypeStruct((s, heads * SB_DIM), BF16), jax.ShapeDtypeStruct((heads, s, 1), F32)]
    out_specs = [pl.BlockSpec((tq, w), lambda h, i: (i, h)), pl.BlockSpec((hp, tq, 1), lambda h, i: (h, i, 0))]
    scratch = [pltpu.VMEM((tq, w), F32)]
    _with_carry(carry, args, in_specs, out_shape, out_specs, scratch)
    return pl.pallas_call(
        body, name="sb_fwd", grid=(heads // hp, nq), out_shape=tuple(out_shape), in_specs=in_specs,
        out_specs=tuple(out_specs), scratch_shapes=scratch, compiler_params=_params(),
    )(*args)


def _sb_bwd(qkv, off_q, off_k, off_v, heads, dy, tot, carry=None):
    s = qkv.shape[0]
    tq = tk = _tile(s, ATT_K)
    ratio = 1
    nq = s // tq
    hp = _pairing(heads)
    ev = range(hp)
    scale = SB_DIM ** -0.5

    def body(*refs):
        (q_ref, k_ref, v_ref, dy_ref, tot_ref), (dq_ref, dk_ref, dv_ref), _, xrefs = _split_refs(refs, 5, 3, carry)
        if carry is not None:
            _carried_exchange(xrefs, carry[1], *_first_last(heads // hp, nq))
        qi = pl.program_id(1)

        @pl.when(qi == 0)
        def _():
            dk_ref[...] = jnp.zeros_like(dk_ref)
            dv_ref[...] = jnp.zeros_like(dv_ref)

        qs = [q_ref[:, _head_lanes(e)] for e in ev]
        dy_b = [dy_ref[:, _head_lanes(e)].astype(BF16) for e in ev]
        tots = [tot_ref[e] for e in ev]
        rows = lax.broadcasted_iota(jnp.int32, (tk, tk), 0)
        cols = lax.broadcasted_iota(jnp.int32, (tk, tk), 1)
        upto = (rows <= cols).astype(BF16)
        before = (rows < cols).astype(BF16)
        dq_ref[...] = jnp.zeros_like(dq_ref)

        def logits(kb):
            off = pl.multiple_of(kb * tk, tk)
            terms = [_sb_terms(qs[e], k_ref[pl.ds(off, tk), _head_lanes(e)], scale, None) for e in ev]
            da = [_dot_nt(dy_b[e], v_ref[pl.ds(off, tk), _head_lanes(e)]) for e in ev]
            return tuple(terms[e] + (da[e],) for e in ev)

        def step(kb, ahead, state, masked, prefetch):
            rest_left, g_left = state
            off = pl.multiple_of(kb * tk, tk)
            log_beta = [ahead[e][0] for e in ev]
            log_rest = [ahead[e][1] for e in ev]
            da = [ahead[e][2] for e in ev]
            if masked:
                mask = _visible(qi, kb, tq, tk, True)
                log_rest = [jnp.where(mask, log_rest[e], 0.0) for e in ev]
            rest_upto = [_split_dot(log_rest[e], upto) + rest_left[e] for e in ev]
            a = [jnp.exp(log_beta[e] + (tots[e] - rest_upto[e])) for e in ev]
            beta = [jnp.exp(log_beta[e]) for e in ev]
            if masked:
                a = [jnp.where(mask, a[e], 0.0) for e in ev]
                beta = [jnp.where(mask, beta[e], 0.0) for e in ev]
            for e in ev:
                dv_ref[pl.ds(off, tk), _head_lanes(e)] += _dot_tn(a[e].astype(BF16), dy_b[e])
            g = [a[e] * da[e] for e in ev]
            g_before = [_dot(g[e].astype(BF16), before) + g_left[e] for e in ev]
            nxt = logits(kb + 1) if prefetch else None
            dz = [((g[e] * (1.0 - beta[e]) - g_before[e] * beta[e]) * scale).astype(BF16) for e in ev]
            for e in ev:
                dk_ref[pl.ds(off, tk), _head_lanes(e)] += _dot_tn(dz[e], qs[e])
            for e in ev:
                dq_ref[:, _head_lanes(e)] += _dot(dz[e], k_ref[pl.ds(off, tk), _head_lanes(e)])
            state = (tuple(rest_left[e] + jnp.sum(log_rest[e], axis=-1, keepdims=True) for e in ev),
                     tuple(g_left[e] + jnp.sum(g[e], axis=-1, keepdims=True) for e in ev))
            return nxt, state

        zeros = tuple(jnp.zeros((tq, 1), F32) for _ in ev)
        ahead, state = lax.fori_loop(0, qi * ratio, lambda kb, c: step(kb, c[0], c[1], False, True),
                                     (logits(0), (zeros, zeros)))
        for i in range(ratio):
            ahead, state = step(qi * ratio + i, ahead, state, True, i < ratio - 1)

    w = hp * SB_DIM
    args = [qkv, qkv, qkv, dy, tot]
    in_specs = _sb_specs(s, tq, off_q, off_k, off_v, hp) + [pl.BlockSpec((tq, w), lambda h, i: (i, h)),
                                                            pl.BlockSpec((hp, tq, 1), lambda h, i: (h, i, 0))]
    out_shape = [jax.ShapeDtypeStruct((s, heads * SB_DIM), F32)] * 3
    out_specs = [pl.BlockSpec((tq, w), lambda h, i: (i, h)), pl.BlockSpec((s, w), lambda h, i: (0, h)),
                 pl.BlockSpec((s, w), lambda h, i: (0, h))]
    scratch = []
    _with_carry(carry, args, in_specs, out_shape, out_specs, scratch)
    return pl.pallas_call(
        body, name="sb_bwd", grid=(heads // hp, nq), out_shape=tuple(out_shape), in_specs=in_specs,
        out_specs=tuple(out_specs), scratch_shapes=scratch, compiler_params=_params(),
    )(*args)


def _slot_sum(recv, name):
    slots, r, w = recv.shape
    tr = _tile(r, 256)

    def body(r_ref, o_ref):
        acc = r_ref[0].astype(F32)
        for d in range(1, slots):
            acc = acc + r_ref[d].astype(F32)
        o_ref[...] = acc

    return pl.pallas_call(
        body, name=name, grid=(r // tr,),
        out_shape=jax.ShapeDtypeStruct((r, w), F32),
        in_specs=[pl.BlockSpec((slots, tr, w), lambda i: (0, i, 0))],
        out_specs=pl.BlockSpec((tr, w), lambda i: (i, 0)), compiler_params=_params(),
    )(recv)


def _pair_sum(mine, theirs):
    slots, r, w = mine.shape
    tr = _tile(r, 256)

    def body(a_ref, b_ref, o_ref):
        o_ref[...] = (a_ref[...].astype(F32) + b_ref[...].astype(F32)).astype(BF16)

    spec = pl.BlockSpec((slots, tr, w), lambda i: (0, i, 0))
    return pl.pallas_call(
        body, name="pair_sum", grid=(r // tr,), out_shape=jax.ShapeDtypeStruct(mine.shape, BF16),
        in_specs=[spec, spec], out_specs=spec, compiler_params=_params(),
    )(mine, theirs)


def _small_sum(rows, loss_lo, loss_hi, d_model):
    n = rows.shape[1]

    def body(r_ref, o_ref, loss_ref):
        rv = r_ref[...]
        acc = rv[0:1, :]
        for d in range(1, N_DEV):
            acc = acc + rv[d:d + 1, :]
        o_ref[...] = acc
        total = jnp.sum(acc[:, loss_lo:loss_hi], axis=-1, keepdims=True) * (0.5 / d_model)
        loss_ref[...] = jnp.broadcast_to(total, (1, LANES))

    return pl.pallas_call(
        body, name="small_sum",
        out_shape=(jax.ShapeDtypeStruct((1, n), F32), jax.ShapeDtypeStruct((1, LANES), F32)),
        compiler_params=_params(),
    )(rows)


def _adamw(w, g, m, v, name):
    r, c = w.shape
    tr = _tile(r, max(8, (1 << 18) // c // 8 * 8))

    def body(w_ref, g_ref, m_ref, v_ref, d_ref, mo_ref, vo_ref):
        gv = g_ref[...]
        m_new = ADAM_B1 * m_ref[...] + (1.0 - ADAM_B1) * gv
        v_new = ADAM_B2 * v_ref[...] + (1.0 - ADAM_B2) * (gv * gv)
        m_hat = m_new / (1.0 - ADAM_B1 ** ADAM_STEP)
        v_hat = v_new / (1.0 - ADAM_B2 ** ADAM_STEP)
        d_ref[...] = -ADAM_LR * (m_hat / (jnp.sqrt(v_hat) + ADAM_EPS) + ADAM_WD * w_ref[...])
        mo_ref[...] = m_new
        vo_ref[...] = v_new

    spec = pl.BlockSpec((tr, c), lambda i: (i, 0))
    sd = jax.ShapeDtypeStruct((r, c), F32)
    return pl.pallas_call(
        body, name=name, grid=(r // tr,), out_shape=(sd, sd, sd),
        in_specs=[spec] * 4, out_specs=(spec,) * 3, compiler_params=_params(),
    )(w, g, m, v)


def _pack_rows(a):
    return a.reshape(-1, PACK_W)


def _unshard(slots, shape, col_sharded):
    r, c = shape
    if col_sharded:
        return slots.reshape(N_DEV, r, c).transpose(1, 0, 2).reshape(r, N_DEV * c)
    return slots.reshape(N_DEV * r, c)


def _to_shards(full, col_sharded, packed=True):
    r, c = full.shape
    shards = (full.reshape(r, N_DEV, c // N_DEV).transpose(1, 0, 2) if col_sharded
              else full.reshape(N_DEV, r // N_DEV, c))
    return shards.reshape(N_DEV, -1, PACK_W) if packed else shards


def kernel(x, c, positions, w_ada, b_ada, g_norm1, g_norm2, w_in, g_q_latent, g_kv_latent, w_uq, w_ukv, g_q_head, g_k_head, w_proj_mla, w_proj_sb, w_out, w_ffn_in, w_ffn_out, loss_target, m_w_ada, m_b_ada, m_g_norm1, m_g_norm2, m_w_in, m_g_q_latent, m_g_kv_latent, m_w_uq, m_w_ukv, m_g_q_head, m_g_k_head, m_w_proj_mla, m_w_proj_sb, m_w_out, m_w_ffn_in, m_w_ffn_out, v_w_ada, v_b_ada, v_g_norm1, v_g_norm2, v_w_in, v_g_q_latent, v_g_kv_latent, v_w_uq, v_w_ukv, v_g_q_head, v_g_k_head, v_w_proj_mla, v_w_proj_sb, v_w_out, v_w_ffn_in, v_w_ffn_out):
    env = dict(locals())
    drop = lambda a: a[0] if a.ndim == 3 else a
    wts = {n: drop(env[n]) for n in WEIGHT_NAMES}
    mom = {n: drop(env["m_" + n]) for n in WEIGHT_NAMES}
    var = {n: drop(env["v_" + n]) for n in WEIGHT_NAMES}
    xs, tgt, pos = x[0], loss_target[0], positions[0]
    s, d = xs.shape
    lat = wts["w_uq"].shape[0]
    assert wts["w_ukv"].shape[0] == lat
    h_mla = wts["w_uq"].shape[1] * N_DEV // QK_DIM
    sb_w = wts["w_proj_sb"].shape[0]
    h_sb = sb_w // SB_DIM
    d_ff = wts["w_ffn_out"].shape[0] * N_DEV
    me = 4 * lax.axis_index("x") + 2 * lax.axis_index("y") + lax.axis_index("c")

    ref_w = (("c_q", lat), ("c_kv", lat), ("k_pe", ROPE), ("q_sb", sb_w), ("k_sb", sb_w), ("v_sb", sb_w),
             ("gl_a", d), ("gl_b", d))
    ref_off, o = {}, 0
    for n_, w_ in ref_w:
        ref_off[n_] = (o, w_)
        o += w_
    order = ("gl_a", "gl_b", "q_sb", "k_sb", "v_sb", "c_q", "c_kv", "k_pe")
    off, o = {}, 0
    for n_ in order:
        w_ = LANES if n_ == "k_pe" else ref_off[n_][1]
        assert o % w_ == 0
        off[n_] = o
        o += w_

    used_w = o
    proj_w = -(-used_w // (2 * LANES)) * (2 * LANES)

    col_sharded = dict(BIG)
    rows_of = {n: wts[n].size // PACK_W for n, _ in BIG}
    full, grads = {}, {}

    def own_shape(names):
        return len(names) == 1

    def pack_weights(names):
        if own_shape(names):
            return wts[names[0]].astype(BF16)
        return jnp.concatenate([_pack_rows(wts[n].astype(BF16)) for n in names], axis=0)

    def unpack_weights(slots, names):
        r0 = 0
        for n in names:
            part = slots if own_shape(names) else slots[:, r0:r0 + rows_of[n]]
            full[n] = _unshard(part, wts[n].shape, col_sharded[n])
            r0 += rows_of[n]

    def pack_grads(names):
        if own_shape(names):
            return _to_shards(grads[names[0]].astype(BF16), col_sharded[names[0]], packed=False)
        return jnp.concatenate([_to_shards(grads[n].astype(BF16), col_sharded[n]) for n in names], axis=1)

    def unpack_grads(recv, names):
        if isinstance(recv, tuple):
            summed_rows = jnp.concatenate([_slot_sum(part, "slot_sum_%s_%d" % (names[0], i))
                                           for i, part in enumerate(recv)], axis=0)
        else:
            summed_rows = _slot_sum(recv, "slot_sum_" + names[0])
        r0 = 0
        for n in names:
            part = summed_rows if own_shape(names) else summed_rows[r0:r0 + rows_of[n]]
            grads[n] = part.reshape(wts[n].shape)
            r0 += rows_of[n]

    unpack_weights(_gather_two_level(pack_weights(("w_in",)), "gather_w_in"), ("w_in",))
    seg = lambda a, n_: a[:, ref_off[n_][0]:ref_off[n_][0] + ref_off[n_][1]]
    w_in_k = jnp.concatenate([seg(full["w_in"], n_) for n_ in order]
                             + [jnp.zeros((d, proj_w - used_w + LANES - ROPE), BF16)], axis=1)
    pad_gain = lambda g: jnp.pad(g, ((0, 0), (0, QK_PAD - QK_DIM)))
    g_qh, g_kh = pad_gain(wts["g_q_head"]), pad_gain(wts["g_k_head"])

    c_all = _gather_rows(c, "gather_c")
    ada_cols = _exchange(_ada_fwd(c_all, wts["w_ada"]), gather=True, name="gather_ada")
    ada = lax.dynamic_index_in_dim(ada_cols, me, axis=1, keepdims=False).reshape(1, 6 * d) + wts["b_ada"]
    sh1, sc1, gt1, sh2, sc2, gt2 = [ada[:, i * d:(i + 1) * d] for i in range(6)]

    half = ROPE // 2
    ang = pos.astype(F32)[:, None] * (ROPE_THETA ** (-jnp.arange(half, dtype=F32) / half))
    zeros = jnp.zeros((s, LANES - ROPE), F32)
    cos_t = jnp.concatenate([jnp.cos(ang), jnp.cos(ang), zeros], axis=1)
    sin_t = jnp.concatenate([-jnp.sin(ang), jnp.sin(ang), zeros], axis=1)

    h1 = _norm_mod(xs, wts["g_norm1"], sc1, sh1)
    mixer_w = ("w_uq", "w_ukv", "w_proj_mla", "w_proj_sb", "w_out")
    proj, slots = _mm(h1, w_in_k, name="mm_in", carry=(pack_weights(mixer_w), "two_level"))
    unpack_weights(slots, mixer_w)
    w_uq_k = jnp.pad(full["w_uq"].reshape(lat, h_mla, QK_DIM), ((0, 0), (0, 0), (0, QK_PAD - QK_DIM))
                     ).reshape(lat, h_mla * QK_PAD)
    cqn, ckvn = _latent_norm(proj, off["c_q"], off["c_kv"], lat, wts["g_q_latent"], wts["g_kv_latent"])
    q_raw = _mm(cqn, w_uq_k, name="mm_uq")
    kv = _mm(ckvn, full["w_ukv"], name="mm_ukv")
    q, k, v = _qk_prep(q_raw, kv, proj, off["k_pe"], cos_t, sin_t, g_qh, g_kh)
    y_a, lse, slots = _mla_fwd(q, k, v, carry=(pack_weights(("w_ffn_out",)), "two_level"))
    unpack_weights(slots, ("w_ffn_out",))
    assert off["k_sb"] == off["q_sb"] + sb_w and off["v_sb"] == off["k_sb"] + sb_w
    qkv_sb = proj[:, off["q_sb"]:off["q_sb"] + 3 * sb_w].astype(BF16)
    y_b, tot_sb, w_fi_slots = _sb_fwd(qkv_sb, 0, sb_w, 2 * sb_w, h_sb,
                              carry=(wts["w_ffn_in"].astype(BF16), "two_level"))
    ya_p = _mm(y_a, full["w_proj_mla"], name="mm_proj_mla")
    yb_p = _mm(y_b, full["w_proj_sb"], name="mm_proj_sb")
    merged = _merge(proj, off["gl_a"], off["gl_b"], ya_p, yb_p)
    o1 = _mm(merged, full["w_out"], name="mm_out")
    x1, h2 = _resid_norm_mod(xs, o1, gt1, wts["g_norm2"], sc2, sh2)
    gu = _mm(h2, w_fi_slots, shards="b", out_dtype=BF16, name="mm_ffn_in")
    act = _swiglu(gu)
    o2 = _mm(act, full["w_ffn_out"], name="mm_ffn_out")
    dy, d_o2, sums_l = _loss_head(x1, o2, gt2, tgt)

    recv = {}
    d_act = _mm(d_o2, full["w_ffn_out"], tb=True, name="mm_d_act")
    grads["w_ffn_out"] = _mm(act, d_o2, ta=True, out_dtype=BF16, name="mm_g_ffn_out")
    d_gu = _swiglu_bwd(gu, d_act)
    g_fo_send = pack_grads(("w_ffn_out",))
    half = g_fo_send.shape[1] // 2
    g_fi_slots, recv_lo = _mm(h2, d_gu, ta=True, shards="out", out_dtype=BF16, name="mm_g_ffn_in",
                              carry=(g_fo_send[:, :half], False))
    d_h2, recv_hi = _mm(d_gu, w_fi_slots, tb=True, shards="b", name="mm_d_h2", carry=(g_fo_send[:, half:], False))
    recv["w_ffn_out",] = (recv_lo, recv_hi)
    d_x1, d_o1, sums_2 = _norm_mod_bwd(d_h2, x1, wts["g_norm2"], sc2, dy, gate=(o1, gt1))
    grads["w_out"] = _mm(merged, d_o1, ta=True, out_dtype=BF16, name="mm_g_out")
    d_merged = _mm(d_o1, full["w_out"], tb=True, name="mm_d_merged")
    d_yap, d_ybp, d_gla, d_glb = _merge_bwd(proj, off["gl_a"], off["gl_b"], ya_p, yb_p, d_merged)
    grads["w_proj_mla"] = _mm(y_a, d_yap, ta=True, out_dtype=BF16, name="mm_g_proj_mla")
    grads["w_proj_sb"] = _mm(y_b, d_ybp, ta=True, out_dtype=BF16, name="mm_g_proj_sb")
    d_ya = _mm(d_yap, full["w_proj_mla"], tb=True, name="mm_d_ya")
    d_yb = _mm(d_ybp, full["w_proj_sb"], tb=True, name="mm_d_yb")
    dq_sb, dk_sb, dv_sb, recv["w_ffn_in",] = _sb_bwd(qkv_sb, 0, sb_w, 2 * sb_w, h_sb, d_yb, tot_sb,
                                                     carry=(g_fi_slots, False))
    merge_w = ("w_out", "w_proj_mla", "w_proj_sb")
    dq, dk, dv, recv[merge_w] = _mla_bwd(q, k, v, y_a, lse, d_ya, carry=(pack_grads(merge_w), False))
    d_qraw, d_kv, d_kpe, sums_h = _qk_prep_bwd(q_raw, kv, proj, off["k_pe"], cos_t, sin_t, g_qh, g_kh, dq, dk, dv)
    g_uq_k = _mm(cqn, d_qraw, ta=True, out_dtype=BF16, name="mm_g_uq")
    grads["w_uq"] = g_uq_k.reshape(lat, h_mla, QK_PAD)[:, :, :QK_DIM].reshape(lat, h_mla * QK_DIM)
    grads["w_ukv"] = _mm(ckvn, d_kv, ta=True, out_dtype=BF16, name="mm_g_ukv")
    d_cqn = _mm(d_qraw, w_uq_k, tb=True, name="mm_d_cqn")
    d_ckvn = _mm(d_kv, full["w_ukv"], tb=True, name="mm_d_ckvn")
    d_cq, d_ckv, sums_lat = _latent_norm_bwd(proj, off["c_q"], off["c_kv"], lat, wts["g_q_latent"],
                                             wts["g_kv_latent"], d_cqn, d_ckvn)
    d_parts = {"gl_a": d_gla, "gl_b": d_glb, "q_sb": dq_sb, "k_sb": dk_sb, "v_sb": dv_sb, "c_q": d_cq, "c_kv": d_ckv,
               "k_pe": d_kpe}
    d_proj = jnp.concatenate([d_parts[n_].astype(BF16) for n_ in order]
                             + ([jnp.zeros((s, proj_w - used_w), BF16)] if proj_w > used_w else []), axis=1)
    latent_w = ("w_uq", "w_ukv")
    g_in_k, recv[latent_w] = _mm(h1, d_proj, ta=True, out_dtype=BF16, name="mm_g_in",
                                 carry=(pack_grads(latent_w), False))
    grads["w_in"] = jnp.concatenate([g_in_k[:, off[n_]:off[n_] + w_] for n_, w_ in ref_w], axis=1)
    by_core = pack_grads(("w_in",))
    by_core = by_core.reshape((N_DEV // 2, 2) + by_core.shape[1:]).transpose(1, 0, 2, 3)
    from_sibling = _exchange(by_core, gather="sibling", name="swap_w_in")
    own_half = lax.dynamic_index_in_dim(by_core, lax.axis_index("c"), axis=0, keepdims=False)
    d_h1, recv_chips = _mm(d_proj, w_in_k, tb=True, name="mm_d_h1", carry=(_pair_sum(own_half, from_sibling), "chips"))
    grads["w_in"] = _slot_sum(recv_chips, "slot_sum_w_in").reshape(wts["w_in"].shape)
    grad_x, sums_1 = _norm_mod_bwd(d_h1, xs, wts["g_norm1"], sc1, d_x1)

    parts = [sums_1[0:1], sums_1[1:2], sums_2[3:4], sums_2[0:1], sums_2[1:2], sums_l[0:1],
             sums_1[2:3], sums_2[2:3], sums_lat[0:1], sums_lat[1:2], sums_h[0:1], sums_h[1:2], sums_l[1:2]]
    part_off, o = [], 0
    for p in parts:
        part_off.append(o)
        o += p.shape[1]
    all_rows = _gather_rows(jnp.concatenate(parts, axis=1), "gather_small")
    summed, loss_row = _small_sum(all_rows, part_off[12], part_off[12] + d, d)
    take = lambda i, w: summed[:, part_off[i]:part_off[i] + w]
    grads["b_ada"] = summed[:, :6 * d]
    grads["g_norm1"], grads["g_norm2"] = take(6, d), take(7, d)
    grads["g_q_latent"], grads["g_kv_latent"] = take(8, lat), take(9, lat)
    grads["g_q_head"], grads["g_k_head"] = take(10, QK_DIM), take(11, QK_DIM)
    n_ada = 6 * d // N_DEV
    d_ada_mine = lax.dynamic_slice_in_dim(all_rows[:, :6 * d], me * n_ada, n_ada, axis=1)
    grads["w_ada"] = _ada_grad(c_all.T, d_ada_mine)

    for names, slots in recv.items():
        unpack_grads(slots, names)

    delta, new_m, new_v = {}, {}, {}
    for n in ("w_ada",) + tuple(n for n, _ in BIG):
        delta[n], new_m[n], new_v[n] = _adamw(wts[n], grads[n], mom[n], var[n], "adamw_" + n)
    cat = lambda t: jnp.concatenate([t[n] for n in SMALL], axis=1)
    d_s, m_s, v_s = _adamw(cat(wts), cat(grads), cat(mom), cat(var), "adamw_small")
    o = 0
    for n in SMALL:
        w_ = wts[n].shape[1]
        delta[n], new_m[n], new_v[n] = d_s[:, o:o + w_], m_s[:, o:o + w_], v_s[:, o:o + w_]
        o += w_

    lead = lambda t: [t[n].reshape(env[n].shape) for n in WEIGHT_NAMES]
    return (loss_row[0, 0], grad_x[None], *lead(grads), *lead(delta), *lead(new_m), *lead(new_v))
```

```python
import jax
import jax.numpy as jnp
from jax import lax
from jax.experimental import pallas as pl
from jax.experimental.pallas import tpu as pltpu

F32 = jnp.float32
BF16 = jnp.bfloat16

N_DEV = 8
LANES = 128
PACK_W = 1024
VMEM_LIMIT = 48 * 1024 * 1024

EPS = 1e-6
ROPE_THETA = 10000.0
NOPE = 128
ROPE = 64
QK_DIM = NOPE + ROPE
QK_PAD = 2 * LANES
V_DIM = 128
SB_DIM = 128
ATT_Q = 512
ATT_K = 256
MM_TK = 2816
PAIR = 2
ROW_PARTS = 2

ADAM_LR = 0.001
ADAM_B1 = 0.9
ADAM_B2 = 0.999
ADAM_EPS = 1e-08
ADAM_WD = 0.01
ADAM_STEP = 10

WEIGHT_NAMES = ("w_ada", "b_ada", "g_norm1", "g_norm2", "w_in", "g_q_latent", "g_kv_latent", "w_uq", "w_ukv",
                "g_q_head", "g_k_head", "w_proj_mla", "w_proj_sb", "w_out", "w_ffn_in", "w_ffn_out")
BIG = (("w_in", True), ("w_uq", True), ("w_ukv", True), ("w_proj_mla", True), ("w_proj_sb", True),
       ("w_out", False), ("w_ffn_in", True), ("w_ffn_out", False))
SMALL = ("b_ada", "g_norm1", "g_norm2", "g_q_latent", "g_kv_latent", "g_q_head", "g_k_head")


def _tile(n, pref):
    if n <= pref:
        return n
    for step in (LANES, 8):
        for t in range(pref - pref % step, 0, -step):
            if n % t == 0:
                return t
    return n


def _params():
    return pltpu.CompilerParams(vmem_limit_bytes=VMEM_LIMIT)


def _sigmoid(x):
    return 1.0 / (1.0 + jnp.exp(-x))


def _dot(a, b):
    return lax.dot_general(a, b, (((1,), (0,)), ((), ())), preferred_element_type=F32)


def _dot_nt(a, b):
    return lax.dot_general(a, b, (((1,), (1,)), ((), ())), preferred_element_type=F32)


def _dot_tn(a, b):
    return lax.dot_general(a, b, (((0,), (0,)), ((), ())), preferred_element_type=F32)


def _split_dot(x, tri):
    hi = x.astype(BF16)
    lo = (x - hi.astype(F32)).astype(BF16)
    return _dot(hi, tri) + _dot(lo, tri)


_HBM = pl.BlockSpec(memory_space=pltpu.HBM)


def _carry_parts(carry):
    src, gather = carry
    if gather == "sibling":
        shape = tuple(src.shape[1:])
    elif gather == "chips" or not gather:
        shape = tuple(src.shape)
    else:
        shape = (N_DEV,) + tuple(src.shape)
    return (src, _HBM, jax.ShapeDtypeStruct(shape, src.dtype), _HBM,
            [pltpu.SemaphoreType.DMA((N_DEV - 1,)), pltpu.SemaphoreType.DMA((N_DEV - 1,)), pltpu.SemaphoreType.DMA(())])


def _exchange_copies(src_ref, dst_ref, send_sems, recv_sems, local_sem, gather):
    x, y, c = lax.axis_index("x"), lax.axis_index("y"), lax.axis_index("c")
    if gather == "sibling":
        return [pltpu.make_async_remote_copy(
            src_ref=src_ref.at[1 - c], dst_ref=dst_ref, send_sem=send_sems.at[0], recv_sem=recv_sems.at[0],
            device_id=(x, y, 1 - c), device_id_type=pl.DeviceIdType.MESH)]
    if gather == "chips":
        chip = 2 * x + y
        copies = [pltpu.make_async_copy(src_ref.at[chip], dst_ref.at[chip], local_sem)]
        for k in range(1, N_DEV // 2):
            peer = ((1 - x) if (k >> 1) & 1 else x, (1 - y) if k & 1 else y, c)
            copies.append(pltpu.make_async_remote_copy(
                src_ref=src_ref.at[2 * peer[0] + peer[1]], dst_ref=dst_ref.at[chip],
                send_sem=send_sems.at[k - 1], recv_sem=recv_sems.at[k - 1],
                device_id=peer, device_id_type=pl.DeviceIdType.MESH))
        return copies
    me = 4 * x + 2 * y + c

    def slot_for(idx):
        return src_ref if gather else src_ref.at[idx]

    copies = [pltpu.make_async_copy(slot_for(me), dst_ref.at[me], local_sem)]
    for k in range(1, N_DEV):
        peer = ((1 - x) if (k >> 2) & 1 else x, (1 - y) if (k >> 1) & 1 else y, (1 - c) if k & 1 else c)
        peer_idx = 4 * peer[0] + 2 * peer[1] + peer[2]
        copies.append(pltpu.make_async_remote_copy(
            src_ref=slot_for(peer_idx), dst_ref=dst_ref.at[me],
            send_sem=send_sems.at[k - 1], recv_sem=recv_sems.at[k - 1],
            device_id=peer, device_id_type=pl.DeviceIdType.MESH))
    return copies


def _two_level_copies(src_ref, dst_ref, send_sems, recv_sems, local_sem):
    x, y, c = lax.axis_index("x"), lax.axis_index("y"), lax.axis_index("c")
    me, sibling = (x, y, c), (x, y, 1 - c)
    chips = [(1 - x, y), (x, 1 - y), (1 - x, 1 - y)]

    def slot(px, py, pc):
        return dst_ref.at[4 * px + 2 * py + pc]

    def copy(k, block, to, own=False):
        return pltpu.make_async_remote_copy(
            src_ref=src_ref if own else slot(*block), dst_ref=slot(*block),
            send_sem=send_sems.at[k], recv_sem=recv_sems.at[k],
            device_id=to, device_id_type=pl.DeviceIdType.MESH)

    mine = pltpu.make_async_copy(src_ref, slot(*me), local_sem)
    first = [copy(0, me, sibling, own=True)] + [copy(1 + j, me, (*chip, c), own=True) for j, chip in enumerate(chips)]
    relays = [(copy(1 + j, (*chip, c), me), copy(4 + j, (*chip, c), sibling)) for j, chip in enumerate(chips)]
    late = [copy(0, sibling, me)] + [copy(4 + j, (*chip, 1 - c), me) for j, chip in enumerate(chips)]
    return mine, first, relays, late


def _two_level_start(refs):
    mine, first, _, _ = _two_level_copies(*refs)
    mine.start()
    for cp in first:
        cp.start()


def _two_level_finish(refs):
    mine, first, relays, late = _two_level_copies(*refs)
    for arrival, onward in relays:
        arrival.wait_recv()
        onward.start()
    for cp in late:
        cp.wait_recv()
    for cp in first + [onward for _, onward in relays]:
        cp.wait_send()
    mine.wait()


def _carried_exchange(refs, gather, first, last):
    @pl.when(first)
    def _():
        if gather == "two_level":
            _two_level_start(refs)
        else:
            for cp in _exchange_copies(*refs, gather):
                cp.start()

    @pl.when(last)
    def _():
        if gather == "two_level":
            _two_level_finish(refs)
        else:
            for cp in _exchange_copies(*refs, gather):
                cp.wait()


def _exchange(src, *, gather, name):
    operand, in_spec, out_shape, out_spec, scratch = _carry_parts((src, gather))

    def body(src_ref, dst_ref, send_sems, recv_sems, local_sem):
        copies = _exchange_copies(src_ref, dst_ref, send_sems, recv_sems, local_sem, gather)
        for cp in copies:
            cp.start()
        for cp in copies:
            cp.wait()

    return pl.pallas_call(body, name=name, out_shape=out_shape, in_specs=[in_spec], out_specs=out_spec,
                          scratch_shapes=scratch)(operand)


def _gather_two_level(src, name):
    operand, in_spec, out_shape, out_spec, scratch = _carry_parts((src, True))

    def body(*refs):
        _two_level_start(refs)
        _two_level_finish(refs)

    return pl.pallas_call(body, name=name, out_shape=out_shape, in_specs=[in_spec], out_specs=out_spec,
                          scratch_shapes=scratch)(operand)


def _gather_rows(v, name):
    n = v.shape[1]
    padded = -(-n // (8 * LANES)) * (8 * LANES)
    tiles = jnp.pad(v, ((0, 0), (0, padded - n))).reshape(padded // LANES, LANES)
    return _exchange(tiles, gather=True, name=name).reshape(N_DEV, padded)[:, :n]


def _split_refs(refs, n_in, n_out, carry):
    if carry is None:
        return refs[:n_in], refs[n_in:n_in + n_out], refs[n_in + n_out:], None
    ins, src_ref = refs[:n_in], refs[n_in]
    outs, dst_ref = refs[n_in + 1:n_in + 1 + n_out], refs[n_in + 1 + n_out]
    rest = refs[n_in + n_out + 2:]
    return ins, outs, rest[:-3], (src_ref, dst_ref) + tuple(rest[-3:])


def _with_carry(carry, args, in_specs, out_shape, out_specs, scratch):
    if carry is not None:
        operand, c_in, c_shape, c_out, c_scratch = _carry_parts(carry)
        args.append(operand)
        in_specs.append(c_in)
        out_shape.append(c_shape)
        out_specs.append(c_out)
        scratch.extend(c_scratch)


def _slot_of_chunk(j):
    return j // 2 + (N_DEV // 2) * (j % 2)


def _mm(a, b, *, ta=False, tb=False, out_dtype=F32, name, carry=None, shards=None):
    kdim, m = a.shape if ta else a.shape[::-1]
    if shards == "b":
        chunk = b.shape[2]
        n, kdim_b = (b.shape[1], N_DEV * chunk) if tb else (N_DEV * chunk, b.shape[1])
    else:
        n, kdim_b = b.shape if tb else b.shape[::-1]
    assert kdim == kdim_b, (a.shape, b.shape, ta, tb)
    tm, tn, tk = _tile(m, 1024), _tile(n, 1024), _tile(kdim, MM_TK)
    if shards == "b":
        tn, tk = (tn, chunk) if tb else (chunk, tk)
    elif shards == "out":
        chunk = tn = n // N_DEV
    grid = (m // tm, n // tn, kdim // tk)
    nk = grid[2]
    dims = (((0 if ta else 1,), (1 if tb else 0,)), ((), ()))

    def body(*refs):
        (a_ref, b_ref), (o_ref,), scratch_refs, xrefs = _split_refs(refs, 2, 1, carry)
        i, j, k = pl.program_id(0), pl.program_id(1), pl.program_id(2)
        if carry is not None:
            _carried_exchange(xrefs, carry[1], (i == 0) & (j == 0) & (k == 0),
                              (i == grid[0] - 1) & (j == grid[1] - 1) & (k == nk - 1))
        prod = lax.dot_general(a_ref[...].astype(BF16), b_ref[...].astype(BF16), dims, preferred_element_type=F32)
        if nk == 1:
            o_ref[...] = prod.astype(o_ref.dtype)
        else:
            acc_ref = scratch_refs[0]

            @pl.when(k == 0)
            def _():
                acc_ref[...] = prod

            @pl.when(k > 0)
            def _():
                acc_ref[...] += prod

            @pl.when(k == nk - 1)
            def _():
                o_ref[...] = acc_ref[...].astype(o_ref.dtype)

    a_spec = (pl.BlockSpec((tk, tm), lambda i, j, k: (k, i)) if ta else pl.BlockSpec((tm, tk), lambda i, j, k: (i, k)))
    if shards == "b":
        b_spec = (pl.BlockSpec((None, tn, tk), lambda i, j, k: (_slot_of_chunk(k), j, 0)) if tb
                  else pl.BlockSpec((None, tk, tn), lambda i, j, k: (_slot_of_chunk(j), k, 0)))
    else:
        b_spec = (pl.BlockSpec((tn, tk), lambda i, j, k: (j, k)) if tb else pl.BlockSpec((tk, tn), lambda i, j, k: (k, j)))
    args, in_specs = [a, b], [a_spec, b_spec]
    if shards == "out":
        out_shape = [jax.ShapeDtypeStruct((N_DEV, m, tn), out_dtype)]
        out_specs = [pl.BlockSpec((None, tm, tn), lambda i, j, k: (_slot_of_chunk(j), i, 0))]
    else:
        out_shape, out_specs = [jax.ShapeDtypeStruct((m, n), out_dtype)], [pl.BlockSpec((tm, tn), lambda i, j, k: (i, j))]
    scratch = [] if nk == 1 else [pltpu.VMEM((tm, tn), F32)]
    _with_carry(carry, args, in_specs, out_shape, out_specs, scratch)
    out = pl.pallas_call(
        body, name=name, grid=grid, out_shape=tuple(out_shape), in_specs=in_specs, out_specs=tuple(out_specs),
        scratch_shapes=scratch, compiler_params=_params(),
    )(*args)
    return out if carry is not None else out[0]


def _ada_fwd(c_all, w_shard):
    d, n = w_shard.shape
    tn = _tile(n, 512)

    def body(c_ref, w_ref, o_ref):
        cv = c_ref[...]
        o_ref[...] = jnp.dot(cv * _sigmoid(cv), w_ref[...], precision=lax.Precision.HIGHEST,
                             preferred_element_type=F32)

    return pl.pallas_call(
        body, name="ada_fwd", grid=(n // tn,),
        out_shape=jax.ShapeDtypeStruct((N_DEV, n), F32),
        in_specs=[pl.BlockSpec((N_DEV, d), lambda j: (0, 0)), pl.BlockSpec((d, tn), lambda j: (0, j))],
        out_specs=pl.BlockSpec((N_DEV, tn), lambda j: (0, j)),
        compiler_params=_params(),
    )(c_all, w_shard)


def _ada_grad(c_all_t, d_rows):
    d, n = c_all_t.shape[0], d_rows.shape[1]
    tn = _tile(n, 512)

    def body(ct_ref, d_ref, o_ref):
        cv = ct_ref[...]
        s = cv * _sigmoid(cv)
        dv = d_ref[...]
        acc = s[:, 0:1] * dv[0:1, :]
        for b in range(1, N_DEV):
            acc = acc + s[:, b:b + 1] * dv[b:b + 1, :]
        o_ref[...] = acc

    return pl.pallas_call(
        body, name="ada_grad", grid=(n // tn,),
        out_shape=jax.ShapeDtypeStruct((d, n), F32),
        in_specs=[pl.BlockSpec((d, N_DEV), lambda j: (0, 0)), pl.BlockSpec((N_DEV, tn), lambda j: (0, j))],
        out_specs=pl.BlockSpec((d, tn), lambda j: (0, j)),
        compiler_params=_params(),
    )(c_all_t, d_rows)


def _row(ts, w, col=0):
    return pl.BlockSpec((ts, w), lambda i, col=col: (i, col))


def _vec(w):
    return pl.BlockSpec((1, w), lambda i: (0, 0))


def _norm_mod(x, g, sc, sh):
    s, d = x.shape
    ts = _tile(s, 512)

    def body(x_ref, g_ref, sc_ref, sh_ref, h_ref):
        xv = x_ref[...]
        r = lax.rsqrt(jnp.mean(xv * xv, axis=-1, keepdims=True) + EPS)
        h_ref[...] = ((xv * r) * g_ref[...] * (1.0 + sc_ref[...]) + sh_ref[...]).astype(BF16)

    return pl.pallas_call(
        body, name="norm_mod", grid=(s // ts,),
        out_shape=jax.ShapeDtypeStruct((s, d), BF16),
        in_specs=[_row(ts, d), _vec(d), _vec(d), _vec(d)],
        out_specs=_row(ts, d), compiler_params=_params(),
    )(x, g, sc, sh)


def _latent_norm(proj, off_q, off_kv, lat, g_q, g_kv):
    s = proj.shape[0]
    ts = _tile(s, 512)

    def body(cq_ref, ckv_ref, gq_ref, gkv_ref, oq_ref, okv_ref):
        for c_ref, g_ref, o_ref in ((cq_ref, gq_ref, oq_ref), (ckv_ref, gkv_ref, okv_ref)):
            v = c_ref[...]
            r = lax.rsqrt(jnp.mean(v * v, axis=-1, keepdims=True) + EPS)
            o_ref[...] = ((v * r) * g_ref[...]).astype(BF16)

    return pl.pallas_call(
        body, name="latent_norm", grid=(s // ts,),
        out_shape=(jax.ShapeDtypeStruct((s, lat), BF16), jax.ShapeDtypeStruct((s, lat), BF16)),
        in_specs=[_row(ts, lat, off_q // lat), _row(ts, lat, off_kv // lat), _vec(lat), _vec(lat)],
        out_specs=(_row(ts, lat), _row(ts, lat)), compiler_params=_params(),
    )(proj, proj, g_q, g_kv)


def _latent_norm_bwd(proj, off_q, off_kv, lat, g_q, g_kv, d_cqn, d_ckvn):
    s = proj.shape[0]
    ts = _tile(s, 512)

    def body(cq_ref, ckv_ref, gq_ref, gkv_ref, dq_ref, dkv_ref, oq_ref, okv_ref, sums_ref):
        @pl.when(pl.program_id(0) == 0)
        def _():
            sums_ref[...] = jnp.zeros_like(sums_ref)

        for row, (c_ref, g_ref, d_ref, o_ref) in enumerate(((cq_ref, gq_ref, dq_ref, oq_ref),
                                                             (ckv_ref, gkv_ref, dkv_ref, okv_ref))):
            v = c_ref[...]
            r = lax.rsqrt(jnp.mean(v * v, axis=-1, keepdims=True) + EPS)
            vn = v * r
            dn = d_ref[...]
            sums_ref[row:row + 1, :] += jnp.sum(dn * vn, axis=0, keepdims=True)
            dvn = dn * g_ref[...]
            o_ref[...] = (r * (dvn - vn * jnp.mean(dvn * vn, axis=-1, keepdims=True))).astype(BF16)

    return pl.pallas_call(
        body, name="latent_norm_bwd", grid=(s // ts,),
        out_shape=(jax.ShapeDtypeStruct((s, lat), BF16), jax.ShapeDtypeStruct((s, lat), BF16),
                   jax.ShapeDtypeStruct((8, lat), F32)),
        in_specs=[_row(ts, lat, off_q // lat), _row(ts, lat, off_kv // lat), _vec(lat), _vec(lat),
                  _row(ts, lat), _row(ts, lat)],
        out_specs=(_row(ts, lat), _row(ts, lat), pl.BlockSpec((8, lat), lambda i: (0, 0))),
        compiler_params=_params(),
    )(proj, proj, g_q, g_kv, d_cqn, d_ckvn)


def _merge(proj, off_a, off_b, ya_p, yb_p):
    s, d = ya_p.shape
    ts = _tile(s, 256)

    def body(ga_ref, gb_ref, ya_ref, yb_ref, o_ref):
        o_ref[...] = (_sigmoid(ga_ref[...]) * ya_ref[...] + _sigmoid(gb_ref[...]) * yb_ref[...]).astype(BF16)

    return pl.pallas_call(
        body, name="merge", grid=(s // ts,),
        out_shape=jax.ShapeDtypeStruct((s, d), BF16),
        in_specs=[_row(ts, d, off_a // d), _row(ts, d, off_b // d), _row(ts, d), _row(ts, d)],
        out_specs=_row(ts, d), compiler_params=_params(),
    )(proj, proj, ya_p, yb_p)


def _merge_bwd(proj, off_a, off_b, ya_p, yb_p, d_merged):
    s, d = ya_p.shape
    ts = _tile(s, 256)

    def body(ga_ref, gb_ref, ya_ref, yb_ref, dm_ref, dya_ref, dyb_ref, dga_ref, dgb_ref):
        dm = dm_ref[...]
        for g_ref, y_ref, dy_ref, dg_ref in ((ga_ref, ya_ref, dya_ref, dga_ref), (gb_ref, yb_ref, dyb_ref, dgb_ref)):
            sg = _sigmoid(g_ref[...])
            dy_ref[...] = (dm * sg).astype(BF16)
            dg_ref[...] = (dm * y_ref[...] * sg * (1.0 - sg)).astype(BF16)

    sd = jax.ShapeDtypeStruct((s, d), BF16)
    return pl.pallas_call(
        body, name="merge_bwd", grid=(s // ts,),
        out_shape=(sd, sd, sd, sd),
        in_specs=[_row(ts, d, off_a // d), _row(ts, d, off_b // d), _row(ts, d), _row(ts, d), _row(ts, d)],
        out_specs=(_row(ts, d),) * 4, compiler_params=_params(),
    )(proj, proj, ya_p, yb_p, d_merged)


def _resid_norm_mod(x, o, gt, g, sc, sh):
    s, d = x.shape
    ts = _tile(s, 256)

    def body(x_ref, o_ref, gt_ref, g_ref, sc_ref, sh_ref, x1_ref, h_ref):
        x1 = x_ref[...] + gt_ref[...] * o_ref[...]
        x1_ref[...] = x1
        r = lax.rsqrt(jnp.mean(x1 * x1, axis=-1, keepdims=True) + EPS)
        h_ref[...] = ((x1 * r) * g_ref[...] * (1.0 + sc_ref[...]) + sh_ref[...]).astype(BF16)

    return pl.pallas_call(
        body, name="resid_norm_mod", grid=(s // ts,),
        out_shape=(jax.ShapeDtypeStruct((s, d), F32), jax.ShapeDtypeStruct((s, d), BF16)),
        in_specs=[_row(ts, d), _row(ts, d), _vec(d), _vec(d), _vec(d), _vec(d)],
        out_specs=(_row(ts, d), _row(ts, d)), compiler_params=_params(),
    )(x, o, gt, g, sc, sh)


def _loss_head(x1, o2, gt2, target):
    s, d = x1.shape
    ts = _tile(s, 256)

    def body(x1_ref, o2_ref, gt_ref, t_ref, dy_ref, do_ref, sums_ref):
        @pl.when(pl.program_id(0) == 0)
        def _():
            sums_ref[...] = jnp.zeros_like(sums_ref)

        o2 = o2_ref[...]
        e = x1_ref[...] + gt_ref[...] * o2 - t_ref[...]
        dy = e / d
        dy_ref[...] = dy
        do_ref[...] = (dy * gt_ref[...]).astype(BF16)
        sums_ref[0:1, :] += jnp.sum(dy * o2, axis=0, keepdims=True)
        sums_ref[1:2, :] += jnp.sum(e * e, axis=0, keepdims=True)

    return pl.pallas_call(
        body, name="loss_head", grid=(s // ts,),
        out_shape=(jax.ShapeDtypeStruct((s, d), F32), jax.ShapeDtypeStruct((s, d), BF16),
                   jax.ShapeDtypeStruct((8, d), F32)),
        in_specs=[_row(ts, d), _row(ts, d), _vec(d), _row(ts, d)],
        out_specs=(_row(ts, d), _row(ts, d), pl.BlockSpec((8, d), lambda i: (0, 0))),
        compiler_params=_params(),
    )(x1, o2, gt2, target)


def _norm_mod_bwd(dh, xin, g, sc, dres, gate=None):
    s, d = xin.shape
    ts = _tile(s, 256)
    gated = gate is not None

    def body(*refs):
        if gated:
            dh_ref, x_ref, g_ref, sc_ref, dr_ref, o_ref, gt_ref, dx_ref, do_ref, sums_ref = refs
        else:
            dh_ref, x_ref, g_ref, sc_ref, dr_ref, dx_ref, sums_ref = refs

        @pl.when(pl.program_id(0) == 0)
        def _():
            sums_ref[...] = jnp.zeros_like(sums_ref)

        xv, dhv = x_ref[...], dh_ref[...]
        r = lax.rsqrt(jnp.mean(xv * xv, axis=-1, keepdims=True) + EPS)
        xn = xv * r
        one_sc = 1.0 + sc_ref[...]
        sums_ref[0:1, :] += jnp.sum(dhv, axis=0, keepdims=True)
        sums_ref[1:2, :] += jnp.sum(dhv * (xn * g_ref[...]), axis=0, keepdims=True)
        sums_ref[2:3, :] += jnp.sum(dhv * one_sc * xn, axis=0, keepdims=True)
        dxn = dhv * one_sc * g_ref[...]
        dx = dr_ref[...] + r * (dxn - xn * jnp.mean(dxn * xn, axis=-1, keepdims=True))
        dx_ref[...] = dx
        if gated:
            sums_ref[3:4, :] += jnp.sum(dx * o_ref[...], axis=0, keepdims=True)
            do_ref[...] = (dx * gt_ref[...]).astype(BF16)

    in_specs = [_row(ts, d), _row(ts, d), _vec(d), _vec(d), _row(ts, d)]
    args = [dh, xin, g, sc, dres]
    out_shape = [jax.ShapeDtypeStruct((s, d), F32)]
    out_specs = [_row(ts, d)]
    if gated:
        in_specs += [_row(ts, d), _vec(d)]
        args += list(gate)
        out_shape.append(jax.ShapeDtypeStruct((s, d), BF16))
        out_specs.append(_row(ts, d))
    out_shape.append(jax.ShapeDtypeStruct((8, d), F32))
    out_specs.append(pl.BlockSpec((8, d), lambda i: (0, 0)))
    return pl.pallas_call(
        body, name="norm_mod_bwd_gated" if gated else "norm_mod_bwd", grid=(s // ts,),
        out_shape=tuple(out_shape), in_specs=in_specs, out_specs=tuple(out_specs), compiler_params=_params(),
    )(*args)


def _ffn_tile(f):
    return 2 * f // N_DEV


def _swiglu(gu):
    s, f2 = gu.shape
    f = f2 // 2
    ts, tc = _tile(s, 512), _ffn_tile(f)

    def body(gu_ref, o_ref):
        gv, uv = gu_ref[:, :tc].astype(F32), gu_ref[:, tc:].astype(F32)
        o_ref[...] = (gv * _sigmoid(gv) * uv).astype(BF16)

    return pl.pallas_call(
        body, name="swiglu", grid=(s // ts, f // tc),
        out_shape=jax.ShapeDtypeStruct((s, f), BF16),
        in_specs=[pl.BlockSpec((ts, 2 * tc), lambda i, j: (i, j))],
        out_specs=pl.BlockSpec((ts, tc), lambda i, j: (i, j)), compiler_params=_params(),
    )(gu)


def _swiglu_bwd(gu, d_act):
    s, f2 = gu.shape
    f = f2 // 2
    ts, tc = _tile(s, 512), _ffn_tile(f)

    def body(gu_ref, da_ref, o_ref):
        gv, uv, da = gu_ref[:, :tc].astype(F32), gu_ref[:, tc:].astype(F32), da_ref[...]
        sg = _sigmoid(gv)
        o_ref[:, :tc] = (da * uv * (sg * (1.0 + gv * (1.0 - sg)))).astype(BF16)
        o_ref[:, tc:] = (da * (gv * sg)).astype(BF16)

    return pl.pallas_call(
        body, name="swiglu_bwd", grid=(s // ts, f // tc),
        out_shape=jax.ShapeDtypeStruct((s, f2), BF16),
        in_specs=[pl.BlockSpec((ts, 2 * tc), lambda i, j: (i, j)), pl.BlockSpec((ts, tc), lambda i, j: (i, j))],
        out_specs=pl.BlockSpec((ts, 2 * tc), lambda i, j: (i, j)), compiler_params=_params(),
    )(gu, d_act)


def _swap_halves(t):
    return pltpu.roll(t, ROPE // 2, 1) + pltpu.roll(t, LANES - ROPE // 2, 1)


def _head_norm(raw):
    r = lax.rsqrt(jnp.sum(raw * raw, axis=-1, keepdims=True) / QK_DIM + EPS)
    return raw * r, r


def _rope_fwd(v, cos, sin):
    rope_tile = v[:, NOPE:]
    return jnp.concatenate([v[:, :NOPE], rope_tile * cos + _swap_halves(rope_tile) * sin], axis=1)


def _rope_bwd(d, cos, sin, lane_ok):
    d_tile = d[:, NOPE:]
    return jnp.concatenate([d[:, :NOPE], d_tile * cos + _swap_halves(d_tile * sin) * lane_ok], axis=1)


def _qk_prep(q_raw, kv, proj, off_pe, cos, sin, g_q, g_k):
    s, hw = q_raw.shape
    heads = hw // QK_PAD
    ts = _tile(s, 512)

    def body(q_ref, kv_ref, pe_ref, cos_ref, sin_ref, gq_ref, gk_ref, qo_ref, ko_ref, vo_ref):
        cos_v, sin_v = cos_ref[...], sin_ref[...]
        qn, _ = _head_norm(q_ref[...])
        qo_ref[...] = _rope_fwd(qn * gq_ref[...], cos_v, sin_v).astype(BF16)
        kvv = kv_ref[...]
        kn, _ = _head_norm(jnp.concatenate([kvv[:, :NOPE], pe_ref[...]], axis=1))
        ko_ref[...] = _rope_fwd(kn * gk_ref[...], cos_v, sin_v).astype(BF16)
        vo_ref[...] = kvv[:, NOPE:].astype(BF16)

    blk = lambda w: pl.BlockSpec((ts, w), lambda i, h: (i, h))
    fixed = lambda w, col=0: pl.BlockSpec((ts, w), lambda i, h, col=col: (i, col))
    vec = pl.BlockSpec((1, QK_PAD), lambda i, h: (0, 0))
    return pl.pallas_call(
        body, name="qk_prep", grid=(s // ts, heads),
        out_shape=(jax.ShapeDtypeStruct((s, hw), BF16), jax.ShapeDtypeStruct((s, hw), BF16),
                   jax.ShapeDtypeStruct((s, heads * V_DIM), BF16)),
        in_specs=[blk(QK_PAD), blk(QK_PAD), fixed(LANES, off_pe // LANES), fixed(LANES), fixed(LANES), vec, vec],
        out_specs=(blk(QK_PAD), blk(QK_PAD), blk(V_DIM)), compiler_params=_params(),
    )(q_raw, kv, proj, cos, sin, g_q, g_k)


def _qk_prep_bwd(q_raw, kv, proj, off_pe, cos, sin, g_q, g_k, dq, dk, dv):
    s, hw = q_raw.shape
    heads = hw // QK_PAD
    ts = _tile(s, 512)

    def body(q_ref, kv_ref, pe_ref, cos_ref, sin_ref, gq_ref, gk_ref, dq_ref, dk_ref, dv_ref,
             dqr_ref, dkv_ref, dpe_ref, sums_ref):
        i, h = pl.program_id(0), pl.program_id(1)

        @pl.when((i == 0) & (h == 0))
        def _():
            sums_ref[...] = jnp.zeros_like(sums_ref)

        cos_v, sin_v = cos_ref[...], sin_ref[...]
        lane_ok = (lax.broadcasted_iota(jnp.int32, (ts, LANES), 1) < ROPE).astype(F32)

        def one(raw, g, d_post, row):
            vn, r = _head_norm(raw)
            d_pre = _rope_bwd(d_post, cos_v, sin_v, lane_ok)
            sums_ref[row:row + 1, :] += jnp.sum(d_pre * vn, axis=0, keepdims=True)
            dvn = d_pre * g
            return r * (dvn - vn * (jnp.sum(dvn * vn, axis=-1, keepdims=True) / QK_DIM))

        dqr_ref[...] = one(q_ref[...], gq_ref[...], dq_ref[...], 0).astype(BF16)
        kvv = kv_ref[...]
        d_kraw = one(jnp.concatenate([kvv[:, :NOPE], pe_ref[...]], axis=1), gk_ref[...], dk_ref[...], 1)
        dkv_ref[...] = jnp.concatenate([d_kraw[:, :NOPE], dv_ref[...]], axis=1).astype(BF16)

        @pl.when(h == 0)
        def _():
            dpe_ref[...] = jnp.zeros_like(dpe_ref)

        dpe_ref[...] += d_kraw[:, NOPE:]

    blk = lambda w: pl.BlockSpec((ts, w), lambda i, h: (i, h))
    fixed = lambda w, col=0: pl.BlockSpec((ts, w), lambda i, h, col=col: (i, col))
    vec = pl.BlockSpec((1, QK_PAD), lambda i, h: (0, 0))
    return pl.pallas_call(
        body, name="qk_prep_bwd", grid=(s // ts, heads),
        out_shape=(jax.ShapeDtypeStruct((s, hw), BF16), jax.ShapeDtypeStruct((s, hw), BF16),
                   jax.ShapeDtypeStruct((s, LANES), F32), jax.ShapeDtypeStruct((8, QK_PAD), F32)),
        in_specs=[blk(QK_PAD), blk(QK_PAD), fixed(LANES, off_pe // LANES), fixed(LANES), fixed(LANES), vec, vec,
                  blk(QK_PAD), blk(QK_PAD), blk(V_DIM)],
        out_specs=(blk(QK_PAD), blk(QK_PAD), fixed(LANES), pl.BlockSpec((8, QK_PAD), lambda i, h: (0, 0))),
        compiler_params=_params(),
    )(q_raw, kv, proj, cos, sin, g_q, g_k, dq, dk, dv)


def _att_blocks(s):
    tq = _tile(s, ATT_Q)
    tk = _tile(tq, ATT_K)
    return tq, tk, tq // tk


def _pairing(heads):
    return PAIR if heads % PAIR == 0 else 1


def _lanes(e, width):
    return slice(e * width, (e + 1) * width)


def _visible(qi, kb, tq, tk, strict):
    row = qi * tq + lax.broadcasted_iota(jnp.int32, (tq, tk), 0)
    col = kb * tk + lax.broadcasted_iota(jnp.int32, (tq, tk), 1)
    return (col < row) if strict else (col <= row)


def _first_last(heads, nq):
    h, qi = pl.program_id(0), pl.program_id(1)
    return (h == 0) & (qi == 0), (h == heads - 1) & (qi == nq - 1)


def _mla_fwd(q, k, v, carry=None):
    s = q.shape[0]
    heads = q.shape[1] // QK_PAD
    tq = tk = _tile(s, ATT_Q)
    ratio = 1
    nq = s // tq
    scale = QK_DIM ** -0.5

    hp = _pairing(heads)
    ev = range(hp)

    def body(*refs):
        (q_ref, k_ref, v_ref), (o_ref, lse_ref), (acc_ref,), xrefs = _split_refs(refs, 3, 2, carry)
        if carry is not None:
            _carried_exchange(xrefs, carry[1], *_first_last(heads // hp, nq))
        qi = pl.program_id(1)
        qs = [q_ref[:, _lanes(e, QK_PAD)] for e in ev]
        acc_ref[...] = jnp.zeros_like(acc_ref)

        def step(kb, state, masked):
            ms, ls = state
            off = pl.multiple_of(kb * tk, tk)
            sc = [_dot_nt(qs[e], k_ref[pl.ds(off, tk), _lanes(e, QK_PAD)]) * scale for e in ev]
            if masked:
                mask = _visible(qi, kb, tq, tk, False)
                sc = [jnp.where(mask, sc[e], -1e30) for e in ev]
            m_new = [jnp.maximum(ms[e], jnp.max(sc[e], axis=-1, keepdims=True)) for e in ev]
            alpha = [jnp.exp(ms[e] - m_new[e]) for e in ev]
            p = [jnp.exp(sc[e] - m_new[e]) for e in ev]
            for e in ev:
                lanes = _lanes(e, V_DIM)
                acc_ref[:, lanes] = alpha[e] * acc_ref[:, lanes] + _dot(p[e].astype(BF16), v_ref[pl.ds(off, tk), lanes])
            return tuple(m_new), tuple(alpha[e] * ls[e] + jnp.sum(p[e], axis=-1, keepdims=True) for e in ev)

        state = (tuple(jnp.full((tq, 1), -1e30, F32) for _ in ev), tuple(jnp.zeros((tq, 1), F32) for _ in ev))
        state = lax.fori_loop(0, qi * ratio, lambda kb, st: step(kb, st, False), state)
        for i in range(ratio):
            state = step(qi * ratio + i, state, True)
        ms, ls = state
        for e in ev:
            o_ref[:, _lanes(e, V_DIM)] = (acc_ref[:, _lanes(e, V_DIM)] / ls[e]).astype(BF16)
            lse_ref[e] = ms[e] + jnp.log(ls[e])

    args = [q, k, v]
    in_specs = [pl.BlockSpec((tq, hp * QK_PAD), lambda h, i: (i, h)), pl.BlockSpec((s, hp * QK_PAD), lambda h, i: (0, h)),
                pl.BlockSpec((s, hp * V_DIM), lambda h, i: (0, h))]
    out_shape = [jax.ShapeDtypeStruct((s, heads * V_DIM), BF16), jax.ShapeDtypeStruct((heads, s, 1), F32)]
    out_specs = [pl.BlockSpec((tq, hp * V_DIM), lambda h, i: (i, h)), pl.BlockSpec((hp, tq, 1), lambda h, i: (h, i, 0))]
    scratch = [pltpu.VMEM((tq, hp * V_DIM), F32)]
    _with_carry(carry, args, in_specs, out_shape, out_specs, scratch)
    return pl.pallas_call(
        body, name="mla_fwd", grid=(heads // hp, nq), out_shape=tuple(out_shape), in_specs=in_specs,
        out_specs=tuple(out_specs), scratch_shapes=scratch, compiler_params=_params(),
    )(*args)


def _mla_bwd(q, k, v, o, lse, do, carry=None):
    s = q.shape[0]
    heads = q.shape[1] // QK_PAD
    tq, tk, ratio = _att_blocks(s)
    nq = s // tq
    scale = QK_DIM ** -0.5

    hp = _pairing(heads)
    ev = range(hp)

    def body(*refs):
        (q_ref, k_ref, v_ref, o_ref, lse_ref, do_ref), (dq_ref, dk_ref, dv_ref), _, xrefs = _split_refs(refs, 6, 3, carry)
        if carry is not None:
            _carried_exchange(xrefs, carry[1], *_first_last(heads // hp, nq))
        qi = pl.program_id(1)

        @pl.when(qi == 0)
        def _():
            dk_ref[...] = jnp.zeros_like(dk_ref)
            dv_ref[...] = jnp.zeros_like(dv_ref)

        qs = [q_ref[:, _lanes(e, QK_PAD)] for e in ev]
        dos = [do_ref[:, _lanes(e, V_DIM)] for e in ev]
        do_b = [dos[e].astype(BF16) for e in ev]
        delta = [jnp.sum(dos[e] * o_ref[:, _lanes(e, V_DIM)].astype(F32), axis=-1, keepdims=True) for e in ev]
        lse_v = [lse_ref[e] for e in ev]
        dq_ref[...] = jnp.zeros_like(dq_ref)

        def step(kb, masked):
            off = pl.multiple_of(kb * tk, tk)
            ks = [k_ref[pl.ds(off, tk), _lanes(e, QK_PAD)] for e in ev]
            vs = [v_ref[pl.ds(off, tk), _lanes(e, V_DIM)] for e in ev]
            sc = [_dot_nt(qs[e], ks[e]) for e in ev]
            dp = [_dot_nt(do_b[e], vs[e]) for e in ev]
            p = [jnp.exp(sc[e] * scale - lse_v[e]) for e in ev]
            if masked:
                mask = _visible(qi, kb, tq, tk, False)
                p = [jnp.where(mask, p[e], 0.0) for e in ev]
            ds = [(p[e] * (dp[e] - delta[e]) * scale).astype(BF16) for e in ev]
            for e in ev:
                dv_ref[pl.ds(off, tk), _lanes(e, V_DIM)] += _dot_tn(p[e].astype(BF16), do_b[e])
            for e in ev:
                dk_ref[pl.ds(off, tk), _lanes(e, QK_PAD)] += _dot_tn(ds[e], qs[e])
            for e in ev:
                dq_ref[:, _lanes(e, QK_PAD)] += _dot(ds[e], ks[e])
            return 0

        lax.fori_loop(0, qi * ratio, lambda kb, _: step(kb, False), 0)
        for i in range(ratio):
            step(qi * ratio + i, True)

    args = [q, k, v, o, lse, do]
    wide, narrow = hp * QK_PAD, hp * V_DIM
    in_specs = [pl.BlockSpec((tq, wide), lambda h, i: (i, h)), pl.BlockSpec((s, wide), lambda h, i: (0, h)),
                pl.BlockSpec((s, narrow), lambda h, i: (0, h)), pl.BlockSpec((tq, narrow), lambda h, i: (i, h)),
                pl.BlockSpec((hp, tq, 1), lambda h, i: (h, i, 0)), pl.BlockSpec((tq, narrow), lambda h, i: (i, h))]
    out_shape = [jax.ShapeDtypeStruct(q.shape, F32), jax.ShapeDtypeStruct(k.shape, F32),
                 jax.ShapeDtypeStruct(v.shape, F32)]
    out_specs = [pl.BlockSpec((tq, wide), lambda h, i: (i, h)), pl.BlockSpec((s, wide), lambda h, i: (0, h)),
                 pl.BlockSpec((s, narrow), lambda h, i: (0, h))]
    scratch = []
    _with_carry(carry, args, in_specs, out_shape, out_specs, scratch)
    return pl.pallas_call(
        body, name="mla_bwd", grid=(heads // hp, nq), out_shape=tuple(out_shape), in_specs=in_specs,
        out_specs=tuple(out_specs), scratch_shapes=scratch, compiler_params=_params(),
    )(*args)


def _sb_terms(qv, k_blk, scale, mask):
    z = _dot_nt(qv, k_blk) * scale
    log_beta = jnp.minimum(z, 0.0) - jnp.log(1.0 + jnp.exp(-jnp.abs(z)))
    log_rest = log_beta - z
    if mask is not None:
        log_rest = jnp.where(mask, log_rest, 0.0)
    return log_beta, log_rest


def _head_lanes(e):
    return slice(e * SB_DIM, (e + 1) * SB_DIM)


def _sb_specs(s, tq, off_q, off_k, off_v, hp):
    w = hp * SB_DIM
    assert off_q % w == 0 and off_k % w == 0 and off_v % w == 0
    return [pl.BlockSpec((tq, w), lambda h, i: (i, off_q // w + h)),
            pl.BlockSpec((s, w), lambda h, i: (0, off_k // w + h)),
            pl.BlockSpec((s, w), lambda h, i: (0, off_v // w + h))]


def _sb_fwd(qkv, off_q, off_k, off_v, heads, carry=None):
    s = qkv.shape[0]
    tq, tk, ratio = _att_blocks(s)
    nq = s // tq
    hp = _pairing(heads)
    scale = SB_DIM ** -0.5

    def body(*refs):
        (q_ref, k_ref, v_ref), (o_ref, tot_ref), (acc_ref,), xrefs = _split_refs(refs, 3, 2, carry)
        if carry is not None:
            _carried_exchange(xrefs, carry[1], *_first_last(heads // hp, nq))
        qi = pl.program_id(1)
        qs = [q_ref[:, _head_lanes(e)] for e in range(hp)]
        after = (lax.broadcasted_iota(jnp.int32, (tk, tk), 0) > lax.broadcasted_iota(jnp.int32, (tk, tk), 1)).astype(BF16)
        acc_ref[...] = jnp.zeros_like(acc_ref)
        row_parts = [slice(r * (tq // ROW_PARTS), (r + 1) * (tq // ROW_PARTS)) for r in range(ROW_PARTS)]

        def step(kb, tails, masked):
            off = pl.multiple_of(kb * tk, tk)
            mask = _visible(qi, kb, tq, tk, True) if masked else None
            units = [(e, rows) for e in range(hp) for rows in row_parts]
            ks = [k_ref[pl.ds(off, tk), _head_lanes(e)] for e in range(hp)]
            vs = [v_ref[pl.ds(off, tk), _head_lanes(e)] for e in range(hp)]
            terms = [_sb_terms(qs[e][rows], ks[e], scale, None if mask is None else mask[rows]) for e, rows in units]
            sums = [_split_dot(t[1], after) for t in terms]
            a = [jnp.exp(t[0] + (sm + tails[e][rows])) for t, sm, (e, rows) in zip(terms, sums, units)]
            if masked:
                a = [jnp.where(mask[rows], a_, 0.0) for a_, (e, rows) in zip(a, units)]
            for a_, (e, rows) in zip(a, units):
                acc_ref[rows, _head_lanes(e)] += _dot(a_.astype(BF16), vs[e])
            rest = [jnp.sum(t[1], axis=-1, keepdims=True) for t in terms]
            return tuple(tails[e] + jnp.concatenate(rest[e * len(row_parts):(e + 1) * len(row_parts)], axis=0)
                         for e in range(hp))

        tails = tuple(jnp.zeros((tq, 1), F32) for _ in range(hp))
        for i in range(ratio):
            tails = step((qi + 1) * ratio - 1 - i, tails, True)
        tails = lax.fori_loop(0, qi * ratio, lambda i, t_: step(qi * ratio - 1 - i, t_, False), tails)
        o_ref[...] = acc_ref[...].astype(BF16)
        for e in range(hp):
            tot_ref[e] = tails[e]

    w = hp * SB_DIM
    args, in_specs = [qkv, qkv, qkv], _sb_specs(s, tq, off_q, off_k, off_v, hp)
    out_shape = [jax.ShapeDtypeStruct((s, heads * SB_DIM), BF16), jax.ShapeDtypeStruct((heads, s, 1), F32)]
    out_specs = [pl.BlockSpec((tq, w), lambda h, i: (i, h)), pl.BlockSpec((hp, tq, 1), lambda h, i: (h, i, 0))]
    scratch = [pltpu.VMEM((tq, w), F32)]
    _with_carry(carry, args, in_specs, out_shape, out_specs, scratch)
    return pl.pallas_call(
        body, name="sb_fwd", grid=(heads // hp, nq), out_shape=tuple(out_shape), in_specs=in_specs,
        out_specs=tuple(out_specs), scratch_shapes=scratch, compiler_params=_params(),
    )(*args)


def _sb_bwd(qkv, off_q, off_k, off_v, heads, dy, tot, carry=None):
    s = qkv.shape[0]
    tq = tk = _tile(s, ATT_K)
    ratio = 1
    nq = s // tq
    hp = _pairing(heads)
    ev = range(hp)
    scale = SB_DIM ** -0.5

    def body(*refs):
        (q_ref, k_ref, v_ref, dy_ref, tot_ref), (dq_ref, dk_ref, dv_ref), _, xrefs = _split_refs(refs, 5, 3, carry)
        if carry is not None:
            _carried_exchange(xrefs, carry[1], *_first_last(heads // hp, nq))
        qi = pl.program_id(1)

        @pl.when(qi == 0)
        def _():
            dk_ref[...] = jnp.zeros_like(dk_ref)
            dv_ref[...] = jnp.zeros_like(dv_ref)

        qs = [q_ref[:, _head_lanes(e)] for e in ev]
        dy_b = [dy_ref[:, _head_lanes(e)].astype(BF16) for e in ev]
        tots = [tot_ref[e] for e in ev]
        rows = lax.broadcasted_iota(jnp.int32, (tk, tk), 0)
        cols = lax.broadcasted_iota(jnp.int32, (tk, tk), 1)
        upto = (rows <= cols).astype(BF16)
        before = (rows < cols).astype(BF16)
        dq_ref[...] = jnp.zeros_like(dq_ref)

        def logits(kb):
            off = pl.multiple_of(kb * tk, tk)
            terms = [_sb_terms(qs[e], k_ref[pl.ds(off, tk), _head_lanes(e)], scale, None) for e in ev]
            da = [_dot_nt(dy_b[e], v_ref[pl.ds(off, tk), _head_lanes(e)]) for e in ev]
            return tuple(terms[e] + (da[e],) for e in ev)

        def step(kb, ahead, state, masked, prefetch):
            rest_left, g_left = state
            off = pl.multiple_of(kb * tk, tk)
            log_beta = [ahead[e][0] for e in ev]
            log_rest = [ahead[e][1] for e in ev]
            da = [ahead[e][2] for e in ev]
            if masked:
                mask = _visible(qi, kb, tq, tk, True)
                log_rest = [jnp.where(mask, log_rest[e], 0.0) for e in ev]
            rest_upto = [_split_dot(log_rest[e], upto) + rest_left[e] for e in ev]
            a = [jnp.exp(log_beta[e] + (tots[e] - rest_upto[e])) for e in ev]
            beta = [jnp.exp(log_beta[e]) for e in ev]
            if masked:
                a = [jnp.where(mask, a[e], 0.0) for e in ev]
                beta = [jnp.where(mask, beta[e], 0.0) for e in ev]
            for e in ev:
                dv_ref[pl.ds(off, tk), _head_lanes(e)] += _dot_tn(a[e].astype(BF16), dy_b[e])
            g = [a[e] * da[e] for e in ev]
            g_before = [_dot(g[e].astype(BF16), before) + g_left[e] for e in ev]
            nxt = logits(kb + 1) if prefetch else None
            dz = [((g[e] * (1.0 - beta[e]) - g_before[e] * beta[e]) * scale).astype(BF16) for e in ev]
            for e in ev:
                dk_ref[pl.ds(off, tk), _head_lanes(e)] += _dot_tn(dz[e], qs[e])
            for e in ev:
                dq_ref[:, _head_lanes(e)] += _dot(dz[e], k_ref[pl.ds(off, tk), _head_lanes(e)])
            state = (tuple(rest_left[e] + jnp.sum(log_rest[e], axis=-1, keepdims=True) for e in ev),
                     tuple(g_left[e] + jnp.sum(g[e], axis=-1, keepdims=True) for e in ev))
            return nxt, state

        zeros = tuple(jnp.zeros((tq, 1), F32) for _ in ev)
        ahead, state = lax.fori_loop(0, qi * ratio, lambda kb, c: step(kb, c[0], c[1], False, True),
                                     (logits(0), (zeros, zeros)))
        for i in range(ratio):
            ahead, state = step(qi * ratio + i, ahead, state, True, i < ratio - 1)

    w = hp * SB_DIM
    args = [qkv, qkv, qkv, dy, tot]
    in_specs = _sb_specs(s, tq, off_q, off_k, off_v, hp) + [pl.BlockSpec((tq, w), lambda h, i: (i, h)),
                                                            pl.BlockSpec((hp, tq, 1), lambda h, i: (h, i, 0))]
    out_shape = [jax.ShapeDtypeStruct((s, heads * SB_DIM), F32)] * 3
    out_specs = [pl.BlockSpec((tq, w), lambda h, i: (i, h)), pl.BlockSpec((s, w), lambda h, i: (0, h)),
                 pl.BlockSpec((s, w), lambda h, i: (0, h))]
    scratch = []
    _with_carry(carry, args, in_specs, out_shape, out_specs, scratch)
    return pl.pallas_call(
        body, name="sb_bwd", grid=(heads // hp, nq), out_shape=tuple(out_shape), in_specs=in_specs,
        out_specs=tuple(out_specs), scratch_shapes=scratch, compiler_params=_params(),
    )(*args)


def _slot_sum(recv, name):
    slots, r, w = recv.shape
    tr = _tile(r, 256)

    def body(r_ref, o_ref):
        acc = r_ref[0].astype(F32)
        for d in range(1, slots):
            acc = acc + r_ref[d].astype(F32)
        o_ref[...] = acc

    return pl.pallas_call(
        body, name=name, grid=(r // tr,),
        out_shape=jax.ShapeDtypeStruct((r, w), F32),
        in_specs=[pl.BlockSpec((slots, tr, w), lambda i: (0, i, 0))],
        out_specs=pl.BlockSpec((tr, w), lambda i: (i, 0)), compiler_params=_params(),
    )(recv)


def _pair_sum(mine, theirs):
    slots, r, w = mine.shape
    tr = _tile(r, 256)

    def body(a_ref, b_ref, o_ref):
        o_ref[...] = (a_ref[...].astype(F32) + b_ref[...].astype(F32)).astype(BF16)

    spec = pl.BlockSpec((slots, tr, w), lambda i: (0, i, 0))
    return pl.pallas_call(
        body, name="pair_sum", grid=(r // tr,), out_shape=jax.ShapeDtypeStruct(mine.shape, BF16),
        in_specs=[spec, spec], out_specs=spec, compiler_params=_params(),
    )(mine, theirs)


def _small_sum(rows, loss_lo, loss_hi, d_model):
    n = rows.shape[1]

    def body(r_ref, o_ref, loss_ref):
        rv = r_ref[...]
        acc = rv[0:1, :]
        for d in range(1, N_DEV):
            acc = acc + rv[d:d + 1, :]
        o_ref[...] = acc
        total = jnp.sum(acc[:, loss_lo:loss_hi], axis=-1, keepdims=True) * (0.5 / d_model)
        loss_ref[...] = jnp.broadcast_to(total, (1, LANES))

    return pl.pallas_call(
        body, name="small_sum",
        out_shape=(jax.ShapeDtypeStruct((1, n), F32), jax.ShapeDtypeStruct((1, LANES), F32)),
        compiler_params=_params(),
    )(rows)


def _adamw(w, g, m, v, name):
    r, c = w.shape
    tr = _tile(r, max(8, (1 << 18) // c // 8 * 8))

    def body(w_ref, g_ref, m_ref, v_ref, d_ref, mo_ref, vo_ref):
        gv = g_ref[...]
        m_new = ADAM_B1 * m_ref[...] + (1.0 - ADAM_B1) * gv
        v_new = ADAM_B2 * v_ref[...] + (1.0 - ADAM_B2) * (gv * gv)
        m_hat = m_new / (1.0 - ADAM_B1 ** ADAM_STEP)
        v_hat = v_new / (1.0 - ADAM_B2 ** ADAM_STEP)
        d_ref[...] = -ADAM_LR * (m_hat / (jnp.sqrt(v_hat) + ADAM_EPS) + ADAM_WD * w_ref[...])
        mo_ref[...] = m_new
        vo_ref[...] = v_new

    spec = pl.BlockSpec((tr, c), lambda i: (i, 0))
    sd = jax.ShapeDtypeStruct((r, c), F32)
    return pl.pallas_call(
        body, name=name, grid=(r // tr,), out_shape=(sd, sd, sd),
        in_specs=[spec] * 4, out_specs=(spec,) * 3, compiler_params=_params(),
    )(w, g, m, v)


def _pack_rows(a):
    return a.reshape(-1, PACK_W)


def _unshard(slots, shape, col_sharded):
    r, c = shape
    if col_sharded:
        return slots.reshape(N_DEV, r, c).transpose(1, 0, 2).reshape(r, N_DEV * c)
    return slots.reshape(N_DEV * r, c)


def _to_shards(full, col_sharded, packed=True):
    r, c = full.shape
    shards = (full.reshape(r, N_DEV, c // N_DEV).transpose(1, 0, 2) if col_sharded
              else full.reshape(N_DEV, r // N_DEV, c))
    return shards.reshape(N_DEV, -1, PACK_W) if packed else shards


def kernel(x, c, positions, w_ada, b_ada, g_norm1, g_norm2, w_in, g_q_latent, g_kv_latent, w_uq, w_ukv, g_q_head, g_k_head, w_proj_mla, w_proj_sb, w_out, w_ffn_in, w_ffn_out, loss_target, m_w_ada, m_b_ada, m_g_norm1, m_g_norm2, m_w_in, m_g_q_latent, m_g_kv_latent, m_w_uq, m_w_ukv, m_g_q_head, m_g_k_head, m_w_proj_mla, m_w_proj_sb, m_w_out, m_w_ffn_in, m_w_ffn_out, v_w_ada, v_b_ada, v_g_norm1, v_g_norm2, v_w_in, v_g_q_latent, v_g_kv_latent, v_w_uq, v_w_ukv, v_g_q_head, v_g_k_head, v_w_proj_mla, v_w_proj_sb, v_w_out, v_w_ffn_in, v_w_ffn_out):
    env = dict(locals())
    drop = lambda a: a[0] if a.ndim == 3 else a
    wts = {n: drop(env[n]) for n in WEIGHT_NAMES}
    mom = {n: drop(env["m_" + n]) for n in WEIGHT_NAMES}
    var = {n: drop(env["v_" + n]) for n in WEIGHT_NAMES}
    xs, tgt, pos = x[0], loss_target[0], positions[0]
    s, d = xs.shape
    lat = wts["w_uq"].shape[0]
    assert wts["w_ukv"].shape[0] == lat
    h_mla = wts["w_uq"].shape[1] * N_DEV // QK_DIM
    sb_w = wts["w_proj_sb"].shape[0]
    h_sb = sb_w // SB_DIM
    d_ff = wts["w_ffn_out"].shape[0] * N_DEV
    me = 4 * lax.axis_index("x") + 2 * lax.axis_index("y") + lax.axis_index("c")

    ref_w = (("c_q", lat), ("c_kv", lat), ("k_pe", ROPE), ("q_sb", sb_w), ("k_sb", sb_w), ("v_sb", sb_w),
             ("gl_a", d), ("gl_b", d))
    ref_off, o = {}, 0
    for n_, w_ in ref_w:
        ref_off[n_] = (o, w_)
        o += w_
    order = ("gl_a", "gl_b", "q_sb", "k_sb", "v_sb", "c_q", "c_kv", "k_pe")
    off, o = {}, 0
    for n_ in order:
        w_ = LANES if n_ == "k_pe" else ref_off[n_][1]
        assert o % w_ == 0
        off[n_] = o
        o += w_

    used_w = o
    proj_w = -(-used_w // (2 * LANES)) * (2 * LANES)

    col_sharded = dict(BIG)
    rows_of = {n: wts[n].size // PACK_W for n, _ in BIG}
    full, grads = {}, {}

    def own_shape(names):
        return len(names) == 1

    def pack_weights(names):
        if own_shape(names):
            return wts[names[0]].astype(BF16)
        return jnp.concatenate([_pack_rows(wts[n].astype(BF16)) for n in names], axis=0)

    def unpack_weights(slots, names):
        r0 = 0
        for n in names:
            part = slots if own_shape(names) else slots[:, r0:r0 + rows_of[n]]
            full[n] = _unshard(part, wts[n].shape, col_sharded[n])
            r0 += rows_of[n]

    def pack_grads(names):
        if own_shape(names):
            return _to_shards(grads[names[0]].astype(BF16), col_sharded[names[0]], packed=False)
        return jnp.concatenate([_to_shards(grads[n].astype(BF16), col_sharded[n]) for n in names], axis=1)

    def unpack_grads(recv, names):
        if isinstance(recv, tuple):
            summed_rows = jnp.concatenate([_slot_sum(part, "slot_sum_%s_%d" % (names[0], i))
                                           for i, part in enumerate(recv)], axis=0)
        else:
            summed_rows = _slot_sum(recv, "slot_sum_" + names[0])
        r0 = 0
        for n in names:
            part = summed_rows if own_shape(names) else summed_rows[r0:r0 + rows_of[n]]
            grads[n] = part.reshape(wts[n].shape)
            r0 += rows_of[n]

    unpack_weights(_gather_two_level(pack_weights(("w_in",)), "gather_w_in"), ("w_in",))
    seg = lambda a, n_: a[:, ref_off[n_][0]:ref_off[n_][0] + ref_off[n_][1]]
    w_in_k = jnp.concatenate([seg(full["w_in"], n_) for n_ in order]
                             + [jnp.zeros((d, proj_w - used_w + LANES - ROPE), BF16)], axis=1)
    pad_gain = lambda g: jnp.pad(g, ((0, 0), (0, QK_PAD - QK_DIM)))
    g_qh, g_kh = pad_gain(wts["g_q_head"]), pad_gain(wts["g_k_head"])

    c_all = _gather_rows(c, "gather_c")
    ada_cols = _exchange(_ada_fwd(c_all, wts["w_ada"]), gather=True, name="gather_ada")
    ada = lax.dynamic_index_in_dim(ada_cols, me, axis=1, keepdims=False).reshape(1, 6 * d) + wts["b_ada"]
    sh1, sc1, gt1, sh2, sc2, gt2 = [ada[:, i * d:(i + 1) * d] for i in range(6)]

    half = ROPE // 2
    ang = pos.astype(F32)[:, None] * (ROPE_THETA ** (-jnp.arange(half, dtype=F32) / half))
    zeros = jnp.zeros((s, LANES - ROPE), F32)
    cos_t = jnp.concatenate([jnp.cos(ang), jnp.cos(ang), zeros], axis=1)
    sin_t = jnp.concatenate([-jnp.sin(ang), jnp.sin(ang), zeros], axis=1)

    h1 = _norm_mod(xs, wts["g_norm1"], sc1, sh1)
    mixer_w = ("w_uq", "w_ukv", "w_proj_mla", "w_proj_sb", "w_out")
    proj, slots = _mm(h1, w_in_k, name="mm_in", carry=(pack_weights(mixer_w), "two_level"))
    unpack_weights(slots, mixer_w)
    w_uq_k = jnp.pad(full["w_uq"].reshape(lat, h_mla, QK_DIM), ((0, 0), (0, 0), (0, QK_PAD - QK_DIM))
                     ).reshape(lat, h_mla * QK_PAD)
    cqn, ckvn = _latent_norm(proj, off["c_q"], off["c_kv"], lat, wts["g_q_latent"], wts["g_kv_latent"])
    q_raw = _mm(cqn, w_uq_k, name="mm_uq")
    kv = _mm(ckvn, full["w_ukv"], name="mm_ukv")
    q, k, v = _qk_prep(q_raw, kv, proj, off["k_pe"], cos_t, sin_t, g_qh, g_kh)
    y_a, lse, slots = _mla_fwd(q, k, v, carry=(pack_weights(("w_ffn_out",)), "two_level"))
    unpack_weights(slots, ("w_ffn_out",))
    assert off["k_sb"] == off["q_sb"] + sb_w and off["v_sb"] == off["k_sb"] + sb_w
    qkv_sb = proj[:, off["q_sb"]:off["q_sb"] + 3 * sb_w].astype(BF16)
    y_b, tot_sb, w_fi_slots = _sb_fwd(qkv_sb, 0, sb_w, 2 * sb_w, h_sb,
                              carry=(wts["w_ffn_in"].astype(BF16), "two_level"))
    ya_p = _mm(y_a, full["w_proj_mla"], name="mm_proj_mla")
    yb_p = _mm(y_b, full["w_proj_sb"], name="mm_proj_sb")
    merged = _merge(proj, off["gl_a"], off["gl_b"], ya_p, yb_p)
    o1 = _mm(merged, full["w_out"], name="mm_out")
    x1, h2 = _resid_norm_mod(xs, o1, gt1, wts["g_norm2"], sc2, sh2)
    gu = _mm(h2, w_fi_slots, shards="b", out_dtype=BF16, name="mm_ffn_in")
    act = _swiglu(gu)
    o2 = _mm(act, full["w_ffn_out"], name="mm_ffn_out")
    dy, d_o2, sums_l = _loss_head(x1, o2, gt2, tgt)

    recv = {}
    d_act = _mm(d_o2, full["w_ffn_out"], tb=True, name="mm_d_act")
    grads["w_ffn_out"] = _mm(act, d_o2, ta=True, out_dtype=BF16, name="mm_g_ffn_out")
    d_gu = _swiglu_bwd(gu, d_act)
    g_fo_send = pack_grads(("w_ffn_out",))
    half = g_fo_send.shape[1] // 2
    g_fi_slots, recv_lo = _mm(h2, d_gu, ta=True, shards="out", out_dtype=BF16, name="mm_g_ffn_in",
                              carry=(g_fo_send[:, :half], False))
    d_h2, recv_hi = _mm(d_gu, w_fi_slots, tb=True, shards="b", name="mm_d_h2", carry=(g_fo_send[:, half:], False))
    recv["w_ffn_out",] = (recv_lo, recv_hi)
    d_x1, d_o1, sums_2 = _norm_mod_bwd(d_h2, x1, wts["g_norm2"], sc2, dy, gate=(o1, gt1))
    grads["w_out"] = _mm(merged, d_o1, ta=True, out_dtype=BF16, name="mm_g_out")
    d_merged = _mm(d_o1, full["w_out"], tb=True, name="mm_d_merged")
    d_yap, d_ybp, d_gla, d_glb = _merge_bwd(proj, off["gl_a"], off["gl_b"], ya_p, yb_p, d_merged)
    grads["w_proj_mla"] = _mm(y_a, d_yap, ta=True, out_dtype=BF16, name="mm_g_proj_mla")
    grads["w_proj_sb"] = _mm(y_b, d_ybp, ta=True, out_dtype=BF16, name="mm_g_proj_sb")
    d_ya = _mm(d_yap, full["w_proj_mla"], tb=True, name="mm_d_ya")
    d_yb = _mm(d_ybp, full["w_proj_sb"], tb=True, name="mm_d_yb")
    dq_sb, dk_sb, dv_sb, recv["w_ffn_in",] = _sb_bwd(qkv_sb, 0, sb_w, 2 * sb_w, h_sb, d_yb, tot_sb,
                                                     carry=(g_fi_slots, False))
    merge_w = ("w_out", "w_proj_mla", "w_proj_sb")
    dq, dk, dv, recv[merge_w] = _mla_bwd(q, k, v, y_a, lse, d_ya, carry=(pack_grads(merge_w), False))
    d_qraw, d_kv, d_kpe, sums_h = _qk_prep_bwd(q_raw, kv, proj, off["k_pe"], cos_t, sin_t, g_qh, g_kh, dq, dk, dv)
    g_uq_k = _mm(cqn, d_qraw, ta=True, out_dtype=BF16, name="mm_g_uq")
    grads["w_uq"] = g_uq_k.reshape(lat, h_mla, QK_PAD)[:, :, :QK_DIM].reshape(lat, h_mla * QK_DIM)
    grads["w_ukv"] = _mm(ckvn, d_kv, ta=True, out_dtype=BF16, name="mm_g_ukv")
    d_cqn = _mm(d_qraw, w_uq_k, tb=True, name="mm_d_cqn")
    d_ckvn = _mm(d_kv, full["w_ukv"], tb=True, name="mm_d_ckvn")
    d_cq, d_ckv, sums_lat = _latent_norm_bwd(proj, off["c_q"], off["c_kv"], lat, wts["g_q_latent"],
                                             wts["g_kv_latent"], d_cqn, d_ckvn)
    d_parts = {"gl_a": d_gla, "gl_b": d_glb, "q_sb": dq_sb, "k_sb": dk_sb, "v_sb": dv_sb, "c_q": d_cq, "c_kv": d_ckv,
               "k_pe": d_kpe}
    d_proj = jnp.concatenate([d_parts[n_].astype(BF16) for n_ in order]
                             + ([jnp.zeros((s, proj_w - used_w), BF16)] if proj_w > used_w else []), axis=1)
    latent_w = ("w_uq", "w_ukv")
    g_in_k, recv[latent_w] = _mm(h1, d_proj, ta=True, out_dtype=BF16, name="mm_g_in",
                                 carry=(pack_grads(latent_w), False))
    grads["w_in"] = jnp.concatenate([g_in_k[:, off[n_]:off[n_] + w_] for n_, w_ in ref_w], axis=1)
    by_core = pack_grads(("w_in",))
    by_core = by_core.reshape((N_DEV // 2, 2) + by_core.shape[1:]).transpose(1, 0, 2, 3)
    from_sibling = _exchange(by_core, gather="sibling", name="swap_w_in")
    own_half = lax.dynamic_index_in_dim(by_core, lax.axis_index("c"), axis=0, keepdims=False)
    d_h1, recv_chips = _mm(d_proj, w_in_k, tb=True, name="mm_d_h1", carry=(_pair_sum(own_half, from_sibling), "chips"))
    grads["w_in"] = _slot_sum(recv_chips, "slot_sum_w_in").reshape(wts["w_in"].shape)
    grad_x, sums_1 = _norm_mod_bwd(d_h1, xs, wts["g_norm1"], sc1, d_x1)

    parts = [sums_1[0:1], sums_1[1:2], sums_2[3:4], sums_2[0:1], sums_2[1:2], sums_l[0:1],
             sums_1[2:3], sums_2[2:3], sums_lat[0:1], sums_lat[1:2], sums_h[0:1], sums_h[1:2], sums_l[1:2]]
    part_off, o = [], 0
    for p in parts:
        part_off.append(o)
        o += p.shape[1]
    all_rows = _gather_rows(jnp.concatenate(parts, axis=1), "gather_small")
    summed, loss_row = _small_sum(all_rows, part_off[12], part_off[12] + d, d)
    take = lambda i, w: summed[:, part_off[i]:part_off[i] + w]
    grads["b_ada"] = summed[:, :6 * d]
    grads["g_norm1"], grads["g_norm2"] = take(6, d), take(7, d)
    grads["g_q_latent"], grads["g_kv_latent"] = take(8, lat), take(9, lat)
    grads["g_q_head"], grads["g_k_head"] = take(10, QK_DIM), take(11, QK_DIM)
    n_ada = 6 * d // N_DEV
    d_ada_mine = lax.dynamic_slice_in_dim(all_rows[:, :6 * d], me * n_ada, n_ada, axis=1)
    grads["w_ada"] = _ada_grad(c_all.T, d_ada_mine)

    for names, slots in recv.items():
        unpack_grads(slots, names)

    delta, new_m, new_v = {}, {}, {}
    for n in ("w_ada",) + tuple(n for n, _ in BIG):
        delta[n], new_m[n], new_v[n] = _adamw(wts[n], grads[n], mom[n], var[n], "adamw_" + n)
    cat = lambda t: jnp.concatenate([t[n] for n in SMALL], axis=1)
    d_s, m_s, v_s = _adamw(cat(wts), cat(grads), cat(mom), cat(var), "adamw_small")
    o = 0
    for n in SMALL:
        w_ = wts[n].shape[1]
        delta[n], new_m[n], new_v[n] = d_s[:, o:o + w_], m_s[:, o:o + w_], v_s[:, o:o + w_]
        o += w_

    lead = lambda t: [t[n].reshape(env[n].shape) for n in WEIGHT_NAMES]
    return (loss_row[0, 0], grad_x[None], *lead(grads), *lead(delta), *lead(new_m), *lead(new_v))
```

```python
import jax
import jax.numpy as jnp
from jax import lax
from jax.experimental import pallas as pl
from jax.experimental.pallas import tpu as pltpu

F32 = jnp.float32
BF16 = jnp.bfloat16

N_DEV = 8
LANES = 128
PACK_W = 1024
VMEM_LIMIT = 52 * 1024 * 1024

EPS = 1e-6
ROPE_THETA = 10000.0
NOPE = 128
ROPE = 64
QK_DIM = NOPE + ROPE
QK_PAD = 2 * LANES
V_DIM = 128
SB_DIM = 128
ATT_Q = 512
ATT_K = 256
MM_TK = 2816
PAIR = 2
ROW_PARTS = 2

ADAM_LR = 0.001
ADAM_B1 = 0.9
ADAM_B2 = 0.999
ADAM_EPS = 1e-08
ADAM_WD = 0.01
ADAM_STEP = 10

WEIGHT_NAMES = ("w_ada", "b_ada", "g_norm1", "g_norm2", "w_in", "g_q_latent", "g_kv_latent", "w_uq", "w_ukv",
                "g_q_head", "g_k_head", "w_proj_mla", "w_proj_sb", "w_out", "w_ffn_in", "w_ffn_out")
BIG = (("w_in", True), ("w_uq", True), ("w_ukv", True), ("w_proj_mla", True), ("w_proj_sb", True),
       ("w_out", False), ("w_ffn_in", True), ("w_ffn_out", False))
SMALL = ("b_ada", "g_norm1", "g_norm2", "g_q_latent", "g_kv_latent", "g_q_head", "g_k_head")


def _tile(n, pref):
    if n <= pref:
        return n
    for step in (LANES, 8):
        for t in range(pref - pref % step, 0, -step):
            if n % t == 0:
                return t
    return n


def _params():
    return pltpu.CompilerParams(vmem_limit_bytes=VMEM_LIMIT)


def _sigmoid(x):
    return 1.0 / (1.0 + jnp.exp(-x))


def _dot(a, b):
    return lax.dot_general(a, b, (((1,), (0,)), ((), ())), preferred_element_type=F32)


def _dot_nt(a, b):
    return lax.dot_general(a, b, (((1,), (1,)), ((), ())), preferred_element_type=F32)


def _dot_tn(a, b):
    return lax.dot_general(a, b, (((0,), (0,)), ((), ())), preferred_element_type=F32)


def _split_dot(x, tri):
    hi = x.astype(BF16)
    lo = (x - hi.astype(F32)).astype(BF16)
    return _dot(hi, tri) + _dot(lo, tri)


_HBM = pl.BlockSpec(memory_space=pltpu.HBM)


def _carry_parts(carry):
    src, gather = carry
    if gather == "sibling":
        shape = tuple(src.shape[1:])
    elif gather == "chips" or not gather:
        shape = tuple(src.shape)
    else:
        shape = (N_DEV,) + tuple(src.shape)
    return (src, _HBM, jax.ShapeDtypeStruct(shape, src.dtype), _HBM,
            [pltpu.SemaphoreType.DMA((N_DEV - 1,)), pltpu.SemaphoreType.DMA((N_DEV - 1,)), pltpu.SemaphoreType.DMA(())])


def _exchange_copies(src_ref, dst_ref, send_sems, recv_sems, local_sem, gather):
    x, y, c = lax.axis_index("x"), lax.axis_index("y"), lax.axis_index("c")
    if gather == "sibling":
        return [pltpu.make_async_remote_copy(
            src_ref=src_ref.at[1 - c], dst_ref=dst_ref, send_sem=send_sems.at[0], recv_sem=recv_sems.at[0],
            device_id=(x, y, 1 - c), device_id_type=pl.DeviceIdType.MESH)]
    if gather == "chips":
        chip = 2 * x + y
        copies = [pltpu.make_async_copy(src_ref.at[chip], dst_ref.at[chip], local_sem)]
        for k in range(1, N_DEV // 2):
            peer = ((1 - x) if (k >> 1) & 1 else x, (1 - y) if k & 1 else y, c)
            copies.append(pltpu.make_async_remote_copy(
                src_ref=src_ref.at[2 * peer[0] + peer[1]], dst_ref=dst_ref.at[chip],
                send_sem=send_sems.at[k - 1], recv_sem=recv_sems.at[k - 1],
                device_id=peer, device_id_type=pl.DeviceIdType.MESH))
        return copies
    me = 4 * x + 2 * y + c

    def slot_for(idx):
        return src_ref if gather else src_ref.at[idx]

    copies = [pltpu.make_async_copy(slot_for(me), dst_ref.at[me], local_sem)]
    for k in range(1, N_DEV):
        peer = ((1 - x) if (k >> 2) & 1 else x, (1 - y) if (k >> 1) & 1 else y, (1 - c) if k & 1 else c)
        peer_idx = 4 * peer[0] + 2 * peer[1] + peer[2]
        copies.append(pltpu.make_async_remote_copy(
            src_ref=slot_for(peer_idx), dst_ref=dst_ref.at[me],
            send_sem=send_sems.at[k - 1], recv_sem=recv_sems.at[k - 1],
            device_id=peer, device_id_type=pl.DeviceIdType.MESH))
    return copies


def _two_level_copies(src_ref, dst_ref, send_sems, recv_sems, local_sem):
    x, y, c = lax.axis_index("x"), lax.axis_index("y"), lax.axis_index("c")
    me, sibling = (x, y, c), (x, y, 1 - c)
    chips = [(1 - x, y), (x, 1 - y), (1 - x, 1 - y)]

    def slot(px, py, pc):
        return dst_ref.at[4 * px + 2 * py + pc]

    def copy(k, block, to, own=False):
        return pltpu.make_async_remote_copy(
            src_ref=src_ref if own else slot(*block), dst_ref=slot(*block),
            send_sem=send_sems.at[k], recv_sem=recv_sems.at[k],
            device_id=to, device_id_type=pl.DeviceIdType.MESH)

    mine = pltpu.make_async_copy(src_ref, slot(*me), local_sem)
    first = [copy(0, me, sibling, own=True)] + [copy(1 + j, me, (*chip, c), own=True) for j, chip in enumerate(chips)]
    relays = [(copy(1 + j, (*chip, c), me), copy(4 + j, (*chip, c), sibling)) for j, chip in enumerate(chips)]
    late = [copy(0, sibling, me)] + [copy(4 + j, (*chip, 1 - c), me) for j, chip in enumerate(chips)]
    return mine, first, relays, late


def _two_level_start(refs):
    mine, first, _, _ = _two_level_copies(*refs)
    mine.start()
    for cp in first:
        cp.start()


def _two_level_finish(refs):
    mine, first, relays, late = _two_level_copies(*refs)
    for arrival, onward in relays:
        arrival.wait_recv()
        onward.start()
    for cp in late:
        cp.wait_recv()
    for cp in first + [onward for _, onward in relays]:
        cp.wait_send()
    mine.wait()


def _carried_exchange(refs, gather, first, last):
    @pl.when(first)
    def _():
        if gather == "two_level":
            _two_level_start(refs)
        else:
            for cp in _exchange_copies(*refs, gather):
                cp.start()

    @pl.when(last)
    def _():
        if gather == "two_level":
            _two_level_finish(refs)
        else:
            for cp in _exchange_copies(*refs, gather):
                cp.wait()


def _exchange(src, *, gather, name):
    operand, in_spec, out_shape, out_spec, scratch = _carry_parts((src, gather))

    def body(src_ref, dst_ref, send_sems, recv_sems, local_sem):
        copies = _exchange_copies(src_ref, dst_ref, send_sems, recv_sems, local_sem, gather)
        for cp in copies:
            cp.start()
        for cp in copies:
            cp.wait()

    return pl.pallas_call(body, name=name, out_shape=out_shape, in_specs=[in_spec], out_specs=out_spec,
                          scratch_shapes=scratch)(operand)


def _gather_two_level(src, name):
    operand, in_spec, out_shape, out_spec, scratch = _carry_parts((src, True))

    def body(*refs):
        _two_level_start(refs)
        _two_level_finish(refs)

    return pl.pallas_call(body, name=name, out_shape=out_shape, in_specs=[in_spec], out_specs=out_spec,
                          scratch_shapes=scratch)(operand)


def _gather_rows(v, name):
    n = v.shape[1]
    padded = -(-n // (8 * LANES)) * (8 * LANES)
    tiles = jnp.pad(v, ((0, 0), (0, padded - n))).reshape(padded // LANES, LANES)
    return _exchange(tiles, gather=True, name=name).reshape(N_DEV, padded)[:, :n]


def _split_refs(refs, n_in, n_out, carry):
    if carry is None:
        return refs[:n_in], refs[n_in:n_in + n_out], refs[n_in + n_out:], None
    ins, src_ref = refs[:n_in], refs[n_in]
    outs, dst_ref = refs[n_in + 1:n_in + 1 + n_out], refs[n_in + 1 + n_out]
    rest = refs[n_in + n_out + 2:]
    return ins, outs, rest[:-3], (src_ref, dst_ref) + tuple(rest[-3:])


def _with_carry(carry, args, in_specs, out_shape, out_specs, scratch):
    if carry is not None:
        operand, c_in, c_shape, c_out, c_scratch = _carry_parts(carry)
        args.append(operand)
        in_specs.append(c_in)
        out_shape.append(c_shape)
        out_specs.append(c_out)
        scratch.extend(c_scratch)


def _slot_of_chunk(j):
    return j // 2 + (N_DEV // 2) * (j % 2)


def _mm(a, b, *, ta=False, tb=False, out_dtype=F32, name, carry=None, shards=None):
    kdim, m = a.shape if ta else a.shape[::-1]
    if shards == "b":
        chunk = b.shape[2]
        n, kdim_b = (b.shape[1], N_DEV * chunk) if tb else (N_DEV * chunk, b.shape[1])
    else:
        n, kdim_b = b.shape if tb else b.shape[::-1]
    assert kdim == kdim_b, (a.shape, b.shape, ta, tb)
    tm, tn, tk = _tile(m, 1024), _tile(n, 1024), _tile(kdim, MM_TK)
    if shards == "b":
        tn, tk = (tn, chunk) if tb else (chunk, tk)
    elif shards == "out":
        chunk = tn = n // N_DEV
    grid = (m // tm, n // tn, kdim // tk)
    nk = grid[2]
    dims = (((0 if ta else 1,), (1 if tb else 0,)), ((), ()))

    def body(*refs):
        (a_ref, b_ref), (o_ref,), scratch_refs, xrefs = _split_refs(refs, 2, 1, carry)
        i, j, k = pl.program_id(0), pl.program_id(1), pl.program_id(2)
        if carry is not None:
            _carried_exchange(xrefs, carry[1], (i == 0) & (j == 0) & (k == 0),
                              (i == grid[0] - 1) & (j == grid[1] - 1) & (k == nk - 1))
        prod = lax.dot_general(a_ref[...].astype(BF16), b_ref[...].astype(BF16), dims, preferred_element_type=F32)
        if nk == 1:
            o_ref[...] = prod.astype(o_ref.dtype)
        else:
            acc_ref = scratch_refs[0]

            @pl.when(k == 0)
            def _():
                acc_ref[...] = prod

            @pl.when(k > 0)
            def _():
                acc_ref[...] += prod

            @pl.when(k == nk - 1)
            def _():
                o_ref[...] = acc_ref[...].astype(o_ref.dtype)

    a_spec = (pl.BlockSpec((tk, tm), lambda i, j, k: (k, i)) if ta else pl.BlockSpec((tm, tk), lambda i, j, k: (i, k)))
    if shards == "b":
        b_spec = (pl.BlockSpec((None, tn, tk), lambda i, j, k: (_slot_of_chunk(k), j, 0)) if tb
                  else pl.BlockSpec((None, tk, tn), lambda i, j, k: (_slot_of_chunk(j), k, 0)))
    else:
        b_spec = (pl.BlockSpec((tn, tk), lambda i, j, k: (j, k)) if tb else pl.BlockSpec((tk, tn), lambda i, j, k: (k, j)))
    args, in_specs = [a, b], [a_spec, b_spec]
    if shards == "out":
        out_shape = [jax.ShapeDtypeStruct((N_DEV, m, tn), out_dtype)]
        out_specs = [pl.BlockSpec((None, tm, tn), lambda i, j, k: (_slot_of_chunk(j), i, 0))]
    else:
        out_shape, out_specs = [jax.ShapeDtypeStruct((m, n), out_dtype)], [pl.BlockSpec((tm, tn), lambda i, j, k: (i, j))]
    scratch = [] if nk == 1 else [pltpu.VMEM((tm, tn), F32)]
    _with_carry(carry, args, in_specs, out_shape, out_specs, scratch)
    out = pl.pallas_call(
        body, name=name, grid=grid, out_shape=tuple(out_shape), in_specs=in_specs, out_specs=tuple(out_specs),
        scratch_shapes=scratch, compiler_params=_params(),
    )(*args)
    return out if carry is not None else out[0]


def _ada_fwd(c_all, w_shard):
    d, n = w_shard.shape
    tn = _tile(n, 512)

    def body(c_ref, w_ref, o_ref):
        cv = c_ref[...]
        o_ref[...] = jnp.dot(cv * _sigmoid(cv), w_ref[...], precision=lax.Precision.HIGHEST,
                             preferred_element_type=F32)

    return pl.pallas_call(
        body, name="ada_fwd", grid=(n // tn,),
        out_shape=jax.ShapeDtypeStruct((N_DEV, n), F32),
        in_specs=[pl.BlockSpec((N_DEV, d), lambda j: (0, 0)), pl.BlockSpec((d, tn), lambda j: (0, j))],
        out_specs=pl.BlockSpec((N_DEV, tn), lambda j: (0, j)),
        compiler_params=_params(),
    )(c_all, w_shard)


def _ada_grad(c_all_t, d_rows):
    d, n = c_all_t.shape[0], d_rows.shape[1]
    tn = _tile(n, 512)

    def body(ct_ref, d_ref, o_ref):
        cv = ct_ref[...]
        s = cv * _sigmoid(cv)
        dv = d_ref[...]
        acc = s[:, 0:1] * dv[0:1, :]
        for b in range(1, N_DEV):
            acc = acc + s[:, b:b + 1] * dv[b:b + 1, :]
        o_ref[...] = acc

    return pl.pallas_call(
        body, name="ada_grad", grid=(n // tn,),
        out_shape=jax.ShapeDtypeStruct((d, n), F32),
        in_specs=[pl.BlockSpec((d, N_DEV), lambda j: (0, 0)), pl.BlockSpec((N_DEV, tn), lambda j: (0, j))],
        out_specs=pl.BlockSpec((d, tn), lambda j: (0, j)),
        compiler_params=_params(),
    )(c_all_t, d_rows)


def _row(ts, w, col=0):
    return pl.BlockSpec((ts, w), lambda i, col=col: (i, col))


def _vec(w):
    return pl.BlockSpec((1, w), lambda i: (0, 0))


def _norm_mod(x, g, sc, sh):
    s, d = x.shape
    ts = _tile(s, 512)

    def body(x_ref, g_ref, sc_ref, sh_ref, h_ref):
        xv = x_ref[...]
        r = lax.rsqrt(jnp.mean(xv * xv, axis=-1, keepdims=True) + EPS)
        h_ref[...] = ((xv * r) * g_ref[...] * (1.0 + sc_ref[...]) + sh_ref[...]).astype(BF16)

    return pl.pallas_call(
        body, name="norm_mod", grid=(s // ts,),
        out_shape=jax.ShapeDtypeStruct((s, d), BF16),
        in_specs=[_row(ts, d), _vec(d), _vec(d), _vec(d)],
        out_specs=_row(ts, d), compiler_params=_params(),
    )(x, g, sc, sh)


def _latent_norm(proj, off_q, off_kv, lat, g_q, g_kv):
    s = proj.shape[0]
    ts = _tile(s, 512)

    def body(cq_ref, ckv_ref, gq_ref, gkv_ref, oq_ref, okv_ref):
        for c_ref, g_ref, o_ref in ((cq_ref, gq_ref, oq_ref), (ckv_ref, gkv_ref, okv_ref)):
            v = c_ref[...]
            r = lax.rsqrt(jnp.mean(v * v, axis=-1, keepdims=True) + EPS)
            o_ref[...] = ((v * r) * g_ref[...]).astype(BF16)

    return pl.pallas_call(
        body, name="latent_norm", grid=(s // ts,),
        out_shape=(jax.ShapeDtypeStruct((s, lat), BF16), jax.ShapeDtypeStruct((s, lat), BF16)),
        in_specs=[_row(ts, lat, off_q // lat), _row(ts, lat, off_kv // lat), _vec(lat), _vec(lat)],
        out_specs=(_row(ts, lat), _row(ts, lat)), compiler_params=_params(),
    )(proj, proj, g_q, g_kv)


def _latent_norm_bwd(proj, off_q, off_kv, lat, g_q, g_kv, d_cqn, d_ckvn):
    s = proj.shape[0]
    ts = _tile(s, 512)

    def body(cq_ref, ckv_ref, gq_ref, gkv_ref, dq_ref, dkv_ref, oq_ref, okv_ref, sums_ref):
        @pl.when(pl.program_id(0) == 0)
        def _():
            sums_ref[...] = jnp.zeros_like(sums_ref)

        for row, (c_ref, g_ref, d_ref, o_ref) in enumerate(((cq_ref, gq_ref, dq_ref, oq_ref),
                                                             (ckv_ref, gkv_ref, dkv_ref, okv_ref))):
            v = c_ref[...]
            r = lax.rsqrt(jnp.mean(v * v, axis=-1, keepdims=True) + EPS)
            vn = v * r
            dn = d_ref[...]
            sums_ref[row:row + 1, :] += jnp.sum(dn * vn, axis=0, keepdims=True)
            dvn = dn * g_ref[...]
            o_ref[...] = (r * (dvn - vn * jnp.mean(dvn * vn, axis=-1, keepdims=True))).astype(BF16)

    return pl.pallas_call(
        body, name="latent_norm_bwd", grid=(s // ts,),
        out_shape=(jax.ShapeDtypeStruct((s, lat), BF16), jax.ShapeDtypeStruct((s, lat), BF16),
                   jax.ShapeDtypeStruct((8, lat), F32)),
        in_specs=[_row(ts, lat, off_q // lat), _row(ts, lat, off_kv // lat), _vec(lat), _vec(lat),
                  _row(ts, lat), _row(ts, lat)],
        out_specs=(_row(ts, lat), _row(ts, lat), pl.BlockSpec((8, lat), lambda i: (0, 0))),
        compiler_params=_params(),
    )(proj, proj, g_q, g_kv, d_cqn, d_ckvn)


def _merge(proj, off_a, off_b, ya_p, yb_p):
    s, d = ya_p.shape
    ts = _tile(s, 256)

    def body(ga_ref, gb_ref, ya_ref, yb_ref, o_ref):
        o_ref[...] = (_sigmoid(ga_ref[...]) * ya_ref[...] + _sigmoid(gb_ref[...]) * yb_ref[...]).astype(BF16)

    return pl.pallas_call(
        body, name="merge", grid=(s // ts,),
        out_shape=jax.ShapeDtypeStruct((s, d), BF16),
        in_specs=[_row(ts, d, off_a // d), _row(ts, d, off_b // d), _row(ts, d), _row(ts, d)],
        out_specs=_row(ts, d), compiler_params=_params(),
    )(proj, proj, ya_p, yb_p)


def _merge_bwd(proj, off_a, off_b, ya_p, yb_p, d_merged):
    s, d = ya_p.shape
    ts = _tile(s, 256)

    def body(ga_ref, gb_ref, ya_ref, yb_ref, dm_ref, dya_ref, dyb_ref, dga_ref, dgb_ref):
        dm = dm_ref[...]
        for g_ref, y_ref, dy_ref, dg_ref in ((ga_ref, ya_ref, dya_ref, dga_ref), (gb_ref, yb_ref, dyb_ref, dgb_ref)):
            sg = _sigmoid(g_ref[...])
            dy_ref[...] = (dm * sg).astype(BF16)
            dg_ref[...] = (dm * y_ref[...] * sg * (1.0 - sg)).astype(BF16)

    sd = jax.ShapeDtypeStruct((s, d), BF16)
    return pl.pallas_call(
        body, name="merge_bwd", grid=(s // ts,),
        out_shape=(sd, sd, sd, sd),
        in_specs=[_row(ts, d, off_a // d), _row(ts, d, off_b // d), _row(ts, d), _row(ts, d), _row(ts, d)],
        out_specs=(_row(ts, d),) * 4, compiler_params=_params(),
    )(proj, proj, ya_p, yb_p, d_merged)


def _resid_norm_mod(x, o, gt, g, sc, sh):
    s, d = x.shape
    ts = _tile(s, 256)

    def body(x_ref, o_ref, gt_ref, g_ref, sc_ref, sh_ref, x1_ref, h_ref):
        x1 = x_ref[...] + gt_ref[...] * o_ref[...]
        x1_ref[...] = x1
        r = lax.rsqrt(jnp.mean(x1 * x1, axis=-1, keepdims=True) + EPS)
        h_ref[...] = ((x1 * r) * g_ref[...] * (1.0 + sc_ref[...]) + sh_ref[...]).astype(BF16)

    return pl.pallas_call(
        body, name="resid_norm_mod", grid=(s // ts,),
        out_shape=(jax.ShapeDtypeStruct((s, d), F32), jax.ShapeDtypeStruct((s, d), BF16)),
        in_specs=[_row(ts, d), _row(ts, d), _vec(d), _vec(d), _vec(d), _vec(d)],
        out_specs=(_row(ts, d), _row(ts, d)), compiler_params=_params(),
    )(x, o, gt, g, sc, sh)


def _loss_head(x1, o2, gt2, target):
    s, d = x1.shape
    ts = _tile(s, 256)

    def body(x1_ref, o2_ref, gt_ref, t_ref, dy_ref, do_ref, sums_ref):
        @pl.when(pl.program_id(0) == 0)
        def _():
            sums_ref[...] = jnp.zeros_like(sums_ref)

        o2 = o2_ref[...]
        e = x1_ref[...] + gt_ref[...] * o2 - t_ref[...]
        dy = e / d
        dy_ref[...] = dy
        do_ref[...] = (dy * gt_ref[...]).astype(BF16)
        sums_ref[0:1, :] += jnp.sum(dy * o2, axis=0, keepdims=True)
        sums_ref[1:2, :] += jnp.sum(e * e, axis=0, keepdims=True)

    return pl.pallas_call(
        body, name="loss_head", grid=(s // ts,),
        out_shape=(jax.ShapeDtypeStruct((s, d), F32), jax.ShapeDtypeStruct((s, d), BF16),
                   jax.ShapeDtypeStruct((8, d), F32)),
        in_specs=[_row(ts, d), _row(ts, d), _vec(d), _row(ts, d)],
        out_specs=(_row(ts, d), _row(ts, d), pl.BlockSpec((8, d), lambda i: (0, 0))),
        compiler_params=_params(),
    )(x1, o2, gt2, target)


def _norm_mod_bwd(dh, xin, g, sc, dres, gate=None):
    s, d = xin.shape
    ts = _tile(s, 256)
    gated = gate is not None

    def body(*refs):
        if gated:
            dh_ref, x_ref, g_ref, sc_ref, dr_ref, o_ref, gt_ref, dx_ref, do_ref, sums_ref = refs
        else:
            dh_ref, x_ref, g_ref, sc_ref, dr_ref, dx_ref, sums_ref = refs

        @pl.when(pl.program_id(0) == 0)
        def _():
            sums_ref[...] = jnp.zeros_like(sums_ref)

        xv, dhv = x_ref[...], dh_ref[...]
        r = lax.rsqrt(jnp.mean(xv * xv, axis=-1, keepdims=True) + EPS)
        xn = xv * r
        one_sc = 1.0 + sc_ref[...]
        sums_ref[0:1, :] += jnp.sum(dhv, axis=0, keepdims=True)
        sums_ref[1:2, :] += jnp.sum(dhv * (xn * g_ref[...]), axis=0, keepdims=True)
        sums_ref[2:3, :] += jnp.sum(dhv * one_sc * xn, axis=0, keepdims=True)
        dxn = dhv * one_sc * g_ref[...]
        dx = dr_ref[...] + r * (dxn - xn * jnp.mean(dxn * xn, axis=-1, keepdims=True))
        dx_ref[...] = dx
        if gated:
            sums_ref[3:4, :] += jnp.sum(dx * o_ref[...], axis=0, keepdims=True)
            do_ref[...] = (dx * gt_ref[...]).astype(BF16)

    in_specs = [_row(ts, d), _row(ts, d), _vec(d), _vec(d), _row(ts, d)]
    args = [dh, xin, g, sc, dres]
    out_shape = [jax.ShapeDtypeStruct((s, d), F32)]
    out_specs = [_row(ts, d)]
    if gated:
        in_specs += [_row(ts, d), _vec(d)]
        args += list(gate)
        out_shape.append(jax.ShapeDtypeStruct((s, d), BF16))
        out_specs.append(_row(ts, d))
    out_shape.append(jax.ShapeDtypeStruct((8, d), F32))
    out_specs.append(pl.BlockSpec((8, d), lambda i: (0, 0)))
    return pl.pallas_call(
        body, name="norm_mod_bwd_gated" if gated else "norm_mod_bwd", grid=(s // ts,),
        out_shape=tuple(out_shape), in_specs=in_specs, out_specs=tuple(out_specs), compiler_params=_params(),
    )(*args)


def _ffn_tile(f):
    return 2 * f // N_DEV


def _swiglu(gu):
    s, f2 = gu.shape
    f = f2 // 2
    ts, tc = _tile(s, 512), _ffn_tile(f)

    def body(gu_ref, o_ref):
        gv, uv = gu_ref[:, :tc].astype(F32), gu_ref[:, tc:].astype(F32)
        o_ref[...] = (gv * _sigmoid(gv) * uv).astype(BF16)

    return pl.pallas_call(
        body, name="swiglu", grid=(s // ts, f // tc),
        out_shape=jax.ShapeDtypeStruct((s, f), BF16),
        in_specs=[pl.BlockSpec((ts, 2 * tc), lambda i, j: (i, j))],
        out_specs=pl.BlockSpec((ts, tc), lambda i, j: (i, j)), compiler_params=_params(),
    )(gu)


def _swiglu_bwd(gu, d_act):
    s, f2 = gu.shape
    f = f2 // 2
    ts, tc = _tile(s, 512), _ffn_tile(f)

    def body(gu_ref, da_ref, o_ref):
        gv, uv, da = gu_ref[:, :tc].astype(F32), gu_ref[:, tc:].astype(F32), da_ref[...]
        sg = _sigmoid(gv)
        o_ref[:, :tc] = (da * uv * (sg * (1.0 + gv * (1.0 - sg)))).astype(BF16)
        o_ref[:, tc:] = (da * (gv * sg)).astype(BF16)

    return pl.pallas_call(
        body, name="swiglu_bwd", grid=(s // ts, f // tc),
        out_shape=jax.ShapeDtypeStruct((s, f2), BF16),
        in_specs=[pl.BlockSpec((ts, 2 * tc), lambda i, j: (i, j)), pl.BlockSpec((ts, tc), lambda i, j: (i, j))],
        out_specs=pl.BlockSpec((ts, 2 * tc), lambda i, j: (i, j)), compiler_params=_params(),
    )(gu, d_act)


def _swap_halves(t):
    return pltpu.roll(t, ROPE // 2, 1) + pltpu.roll(t, LANES - ROPE // 2, 1)


def _head_norm(raw):
    r = lax.rsqrt(jnp.sum(raw * raw, axis=-1, keepdims=True) / QK_DIM + EPS)
    return raw * r, r


def _rope_fwd(v, cos, sin):
    rope_tile = v[:, NOPE:]
    return jnp.concatenate([v[:, :NOPE], rope_tile * cos + _swap_halves(rope_tile) * sin], axis=1)


def _rope_bwd(d, cos, sin, lane_ok):
    d_tile = d[:, NOPE:]
    return jnp.concatenate([d[:, :NOPE], d_tile * cos + _swap_halves(d_tile * sin) * lane_ok], axis=1)


def _qk_prep(q_raw, kv, proj, off_pe, cos, sin, g_q, g_k):
    s, hw = q_raw.shape
    heads = hw // QK_PAD
    ts = _tile(s, 512)

    def body(q_ref, kv_ref, pe_ref, cos_ref, sin_ref, gq_ref, gk_ref, qo_ref, ko_ref, vo_ref):
        cos_v, sin_v = cos_ref[...], sin_ref[...]
        qn, _ = _head_norm(q_ref[...])
        qo_ref[...] = _rope_fwd(qn * gq_ref[...], cos_v, sin_v).astype(BF16)
        kvv = kv_ref[...]
        kn, _ = _head_norm(jnp.concatenate([kvv[:, :NOPE], pe_ref[...]], axis=1))
        ko_ref[...] = _rope_fwd(kn * gk_ref[...], cos_v, sin_v).astype(BF16)
        vo_ref[...] = kvv[:, NOPE:].astype(BF16)

    blk = lambda w: pl.BlockSpec((ts, w), lambda i, h: (i, h))
    fixed = lambda w, col=0: pl.BlockSpec((ts, w), lambda i, h, col=col: (i, col))
    vec = pl.BlockSpec((1, QK_PAD), lambda i, h: (0, 0))
    return pl.pallas_call(
        body, name="qk_prep", grid=(s // ts, heads),
        out_shape=(jax.ShapeDtypeStruct((s, hw), BF16), jax.ShapeDtypeStruct((s, hw), BF16),
                   jax.ShapeDtypeStruct((s, heads * V_DIM), BF16)),
        in_specs=[blk(QK_PAD), blk(QK_PAD), fixed(LANES, off_pe // LANES), fixed(LANES), fixed(LANES), vec, vec],
        out_specs=(blk(QK_PAD), blk(QK_PAD), blk(V_DIM)), compiler_params=_params(),
    )(q_raw, kv, proj, cos, sin, g_q, g_k)


def _qk_prep_bwd(q_raw, kv, proj, off_pe, cos, sin, g_q, g_k, dq, dk, dv):
    s, hw = q_raw.shape
    heads = hw // QK_PAD
    ts = _tile(s, 512)

    def body(q_ref, kv_ref, pe_ref, cos_ref, sin_ref, gq_ref, gk_ref, dq_ref, dk_ref, dv_ref,
             dqr_ref, dkv_ref, dpe_ref, sums_ref):
        i, h = pl.program_id(0), pl.program_id(1)

        @pl.when((i == 0) & (h == 0))
        def _():
            sums_ref[...] = jnp.zeros_like(sums_ref)

        cos_v, sin_v = cos_ref[...], sin_ref[...]
        lane_ok = (lax.broadcasted_iota(jnp.int32, (ts, LANES), 1) < ROPE).astype(F32)

        def one(raw, g, d_post, row):
            vn, r = _head_norm(raw)
            d_pre = _rope_bwd(d_post, cos_v, sin_v, lane_ok)
            sums_ref[row:row + 1, :] += jnp.sum(d_pre * vn, axis=0, keepdims=True)
            dvn = d_pre * g
            return r * (dvn - vn * (jnp.sum(dvn * vn, axis=-1, keepdims=True) / QK_DIM))

        dqr_ref[...] = one(q_ref[...], gq_ref[...], dq_ref[...], 0).astype(BF16)
        kvv = kv_ref[...]
        d_kraw = one(jnp.concatenate([kvv[:, :NOPE], pe_ref[...]], axis=1), gk_ref[...], dk_ref[...], 1)
        dkv_ref[...] = jnp.concatenate([d_kraw[:, :NOPE], dv_ref[...]], axis=1).astype(BF16)

        @pl.when(h == 0)
        def _():
            dpe_ref[...] = jnp.zeros_like(dpe_ref)

        dpe_ref[...] += d_kraw[:, NOPE:]

    blk = lambda w: pl.BlockSpec((ts, w), lambda i, h: (i, h))
    fixed = lambda w, col=0: pl.BlockSpec((ts, w), lambda i, h, col=col: (i, col))
    vec = pl.BlockSpec((1, QK_PAD), lambda i, h: (0, 0))
    return pl.pallas_call(
        body, name="qk_prep_bwd", grid=(s // ts, heads),
        out_shape=(jax.ShapeDtypeStruct((s, hw), BF16), jax.ShapeDtypeStruct((s, hw), BF16),
                   jax.ShapeDtypeStruct((s, LANES), F32), jax.ShapeDtypeStruct((8, QK_PAD), F32)),
        in_specs=[blk(QK_PAD), blk(QK_PAD), fixed(LANES, off_pe // LANES), fixed(LANES), fixed(LANES), vec, vec,
                  blk(QK_PAD), blk(QK_PAD), blk(V_DIM)],
        out_specs=(blk(QK_PAD), blk(QK_PAD), fixed(LANES), pl.BlockSpec((8, QK_PAD), lambda i, h: (0, 0))),
        compiler_params=_params(),
    )(q_raw, kv, proj, cos, sin, g_q, g_k, dq, dk, dv)


def _att_blocks(s):
    tq = _tile(s, ATT_Q)
    tk = _tile(tq, ATT_K)
    return tq, tk, tq // tk


def _pairing(heads):
    return PAIR if heads % PAIR == 0 else 1


def _lanes(e, width):
    return slice(e * width, (e + 1) * width)


def _visible(qi, kb, tq, tk, strict):
    row = qi * tq + lax.broadcasted_iota(jnp.int32, (tq, tk), 0)
    col = kb * tk + lax.broadcasted_iota(jnp.int32, (tq, tk), 1)
    return (col < row) if strict else (col <= row)


def _first_last(heads, nq):
    h, qi = pl.program_id(0), pl.program_id(1)
    return (h == 0) & (qi == 0), (h == heads - 1) & (qi == nq - 1)


def _mla_fwd(q, k, v, carry=None):
    s = q.shape[0]
    heads = q.shape[1] // QK_PAD
    tq = tk = _tile(s, ATT_Q)
    ratio = 1
    nq = s // tq
    scale = QK_DIM ** -0.5

    hp = _pairing(heads)
    ev = range(hp)

    def body(*refs):
        (q_ref, k_ref, v_ref), (o_ref, lse_ref), (acc_ref,), xrefs = _split_refs(refs, 3, 2, carry)
        if carry is not None:
            _carried_exchange(xrefs, carry[1], *_first_last(heads // hp, nq))
        qi = pl.program_id(1)
        qs = [q_ref[:, _lanes(e, QK_PAD)] for e in ev]
        acc_ref[...] = jnp.zeros_like(acc_ref)

        def step(kb, state, masked):
            ms, ls = state
            off = pl.multiple_of(kb * tk, tk)
            sc = [_dot_nt(qs[e], k_ref[pl.ds(off, tk), _lanes(e, QK_PAD)]) * scale for e in ev]
            if masked:
                mask = _visible(qi, kb, tq, tk, False)
                sc = [jnp.where(mask, sc[e], -1e30) for e in ev]
            m_new = [jnp.maximum(ms[e], jnp.max(sc[e], axis=-1, keepdims=True)) for e in ev]
            alpha = [jnp.exp(ms[e] - m_new[e]) for e in ev]
            p = [jnp.exp(sc[e] - m_new[e]) for e in ev]
            for e in ev:
                lanes = _lanes(e, V_DIM)
                acc_ref[:, lanes] = alpha[e] * acc_ref[:, lanes] + _dot(p[e].astype(BF16), v_ref[pl.ds(off, tk), lanes])
            return tuple(m_new), tuple(alpha[e] * ls[e] + jnp.sum(p[e], axis=-1, keepdims=True) for e in ev)

        state = (tuple(jnp.full((tq, 1), -1e30, F32) for _ in ev), tuple(jnp.zeros((tq, 1), F32) for _ in ev))
        state = lax.fori_loop(0, qi * ratio, lambda kb, st: step(kb, st, False), state)
        for i in range(ratio):
            state = step(qi * ratio + i, state, True)
        ms, ls = state
        for e in ev:
            o_ref[:, _lanes(e, V_DIM)] = (acc_ref[:, _lanes(e, V_DIM)] / ls[e]).astype(BF16)
            lse_ref[e] = ms[e] + jnp.log(ls[e])

    args = [q, k, v]
    in_specs = [pl.BlockSpec((tq, hp * QK_PAD), lambda h, i: (i, h)), pl.BlockSpec((s, hp * QK_PAD), lambda h, i: (0, h)),
                pl.BlockSpec((s, hp * V_DIM), lambda h, i: (0, h))]
    out_shape = [jax.ShapeDtypeStruct((s, heads * V_DIM), BF16), jax.ShapeDtypeStruct((heads, s, 1), F32)]
    out_specs = [pl.BlockSpec((tq, hp * V_DIM), lambda h, i: (i, h)), pl.BlockSpec((hp, tq, 1), lambda h, i: (h, i, 0))]
    scratch = [pltpu.VMEM((tq, hp * V_DIM), F32)]
    _with_carry(carry, args, in_specs, out_shape, out_specs, scratch)
    return pl.pallas_call(
        body, name="mla_fwd", grid=(heads // hp, nq), out_shape=tuple(out_shape), in_specs=in_specs,
        out_specs=tuple(out_specs), scratch_shapes=scratch, compiler_params=_params(),
    )(*args)


def _mla_bwd(q, k, v, o, lse, do, carry=None):
    s = q.shape[0]
    heads = q.shape[1] // QK_PAD
    tq = tk = _tile(s, ATT_Q)
    ratio = 1
    nq = s // tq
    scale = QK_DIM ** -0.5

    hp = _pairing(heads)
    ev = range(hp)

    def body(*refs):
        (q_ref, k_ref, v_ref, o_ref, lse_ref, do_ref), (dq_ref, dk_ref, dv_ref), _, xrefs = _split_refs(refs, 6, 3, carry)
        if carry is not None:
            _carried_exchange(xrefs, carry[1], *_first_last(heads // hp, nq))
        qi = pl.program_id(1)

        @pl.when(qi == 0)
        def _():
            dk_ref[...] = jnp.zeros_like(dk_ref)
            dv_ref[...] = jnp.zeros_like(dv_ref)

        qs = [q_ref[:, _lanes(e, QK_PAD)] for e in ev]
        dos = [do_ref[:, _lanes(e, V_DIM)] for e in ev]
        do_b = [dos[e].astype(BF16) for e in ev]
        delta = [jnp.sum(dos[e] * o_ref[:, _lanes(e, V_DIM)].astype(F32), axis=-1, keepdims=True) for e in ev]
        lse_v = [lse_ref[e] for e in ev]
        dq_ref[...] = jnp.zeros_like(dq_ref)

        def step(kb, masked):
            off = pl.multiple_of(kb * tk, tk)
            ks = [k_ref[pl.ds(off, tk), _lanes(e, QK_PAD)] for e in ev]
            vs = [v_ref[pl.ds(off, tk), _lanes(e, V_DIM)] for e in ev]
            sc = [_dot_nt(qs[e], ks[e]) for e in ev]
            dp = [_dot_nt(do_b[e], vs[e]) for e in ev]
            p = [jnp.exp(sc[e] * scale - lse_v[e]) for e in ev]
            if masked:
                mask = _visible(qi, kb, tq, tk, False)
                p = [jnp.where(mask, p[e], 0.0) for e in ev]
            ds = [(p[e] * (dp[e] - delta[e]) * scale).astype(BF16) for e in ev]
            for e in ev:
                dv_ref[pl.ds(off, tk), _lanes(e, V_DIM)] += _dot_tn(p[e].astype(BF16), do_b[e])
            for e in ev:
                dk_ref[pl.ds(off, tk), _lanes(e, QK_PAD)] += _dot_tn(ds[e], qs[e])
            for e in ev:
                dq_ref[:, _lanes(e, QK_PAD)] += _dot(ds[e], ks[e])
            return 0

        lax.fori_loop(0, qi * ratio, lambda kb, _: step(kb, False), 0)
        for i in range(ratio):
            step(qi * ratio + i, True)

    args = [q, k, v, o, lse, do]
    wide, narrow = hp * QK_PAD, hp * V_DIM
    in_specs = [pl.BlockSpec((tq, wide), lambda h, i: (i, h)), pl.BlockSpec((s, wide), lambda h, i: (0, h)),
                pl.BlockSpec((s, narrow), lambda h, i: (0, h)), pl.BlockSpec((tq, narrow), lambda h, i: (i, h)),
                pl.BlockSpec((hp, tq, 1), lambda h, i: (h, i, 0)), pl.BlockSpec((tq, narrow), lambda h, i: (i, h))]
    out_shape = [jax.ShapeDtypeStruct(q.shape, F32), jax.ShapeDtypeStruct(k.shape, F32),
                 jax.ShapeDtypeStruct(v.shape, F32)]
    out_specs = [pl.BlockSpec((tq, wide), lambda h, i: (i, h)), pl.BlockSpec((s, wide), lambda h, i: (0, h)),
                 pl.BlockSpec((s, narrow), lambda h, i: (0, h))]
    scratch = []
    _with_carry(carry, args, in_specs, out_shape, out_specs, scratch)
    return pl.pallas_call(
        body, name="mla_bwd", grid=(heads // hp, nq), out_shape=tuple(out_shape), in_specs=in_specs,
        out_specs=tuple(out_specs), scratch_shapes=scratch, compiler_params=_params(),
    )(*args)


def _sb_terms(qv, k_blk, scale, mask):
    z = _dot_nt(qv, k_blk) * scale
    log_beta = jnp.minimum(z, 0.0) - jnp.log(1.0 + jnp.exp(-jnp.abs(z)))
    log_rest = log_beta - z
    if mask is not None:
        log_rest = jnp.where(mask, log_rest, 0.0)
    return log_beta, log_rest


def _head_lanes(e):
    return slice(e * SB_DIM, (e + 1) * SB_DIM)


def _sb_specs(s, tq, off_q, off_k, off_v, hp):
    w = hp * SB_DIM
    assert off_q % w == 0 and off_k % w == 0 and off_v % w == 0
    return [pl.BlockSpec((tq, w), lambda h, i: (i, off_q // w + h)),
            pl.BlockSpec((s, w), lambda h, i: (0, off_k // w + h)),
            pl.BlockSpec((s, w), lambda h, i: (0, off_v // w + h))]


def _sb_fwd(qkv, off_q, off_k, off_v, heads, carry=None):
    s = qkv.shape[0]
    tq, tk, ratio = _att_blocks(s)
    nq = s // tq
    hp = _pairing(heads)
    scale = SB_DIM ** -0.5

    def body(*refs):
        (q_ref, k_ref, v_ref), (o_ref, tot_ref), (acc_ref,), xrefs = _split_refs(refs, 3, 2, carry)
        if carry is not None:
            _carried_exchange(xrefs, carry[1], *_first_last(heads // hp, nq))
        qi = pl.program_id(1)
        qs = [q_ref[:, _head_lanes(e)] for e in range(hp)]
        after = (lax.broadcasted_iota(jnp.int32, (tk, tk), 0) > lax.broadcasted_iota(jnp.int32, (tk, tk), 1)).astype(BF16)
        acc_ref[...] = jnp.zeros_like(acc_ref)
        row_parts = [slice(r * (tq // ROW_PARTS), (r + 1) * (tq // ROW_PARTS)) for r in range(ROW_PARTS)]

        def step(kb, tails, masked):
            off = pl.multiple_of(kb * tk, tk)
            mask = _visible(qi, kb, tq, tk, True) if masked else None
            units = [(e, rows) for e in range(hp) for rows in row_parts]
            ks = [k_ref[pl.ds(off, tk), _head_lanes(e)] for e in range(hp)]
            vs = [v_ref[pl.ds(off, tk), _head_lanes(e)] for e in range(hp)]
            terms = [_sb_terms(qs[e][rows], ks[e], scale, None if mask is None else mask[rows]) for e, rows in units]
            sums = [_split_dot(t[1], after) for t in terms]
            a = [jnp.exp(t[0] + (sm + tails[e][rows])) for t, sm, (e, rows) in zip(terms, sums, units)]
            if masked:
                a = [jnp.where(mask[rows], a_, 0.0) for a_, (e, rows) in zip(a, units)]
            for a_, (e, rows) in zip(a, units):
                acc_ref[rows, _head_lanes(e)] += _dot(a_.astype(BF16), vs[e])
            rest = [jnp.sum(t[1], axis=-1, keepdims=True) for t in terms]
            return tuple(tails[e] + jnp.concatenate(rest[e * len(row_parts):(e + 1) * len(row_parts)], axis=0)
                         for e in range(hp))

        tails = tuple(jnp.zeros((tq, 1), F32) for _ in range(hp))
        for i in range(ratio):
            tails = step((qi + 1) * ratio - 1 - i, tails, True)
        tails = lax.fori_loop(0, qi * ratio, lambda i, t_: step(qi * ratio - 1 - i, t_, False), tails)
        o_ref[...] = acc_ref[...].astype(BF16)
        for e in range(hp):
            tot_ref[e] = tails[e]

    w = hp * SB_DIM
    args, in_specs = [qkv, qkv, qkv], _sb_specs(s, tq, off_q, off_k, off_v, hp)
    out_shape = [jax.ShapeDtypeStruct((s, heads * SB_DIM), BF16), jax.ShapeDtypeStruct((heads, s, 1), F32)]
    out_specs = [pl.BlockSpec((tq, w), lambda h, i: (i, h)), pl.BlockSpec((hp, tq, 1), lambda h, i: (h, i, 0))]
    scratch = [pltpu.VMEM((tq, w), F32)]
    _with_carry(carry, args, in_specs, out_shape, out_specs, scratch)
    return pl.pallas_call(
        body, name="sb_fwd", grid=(heads // hp, nq), out_shape=tuple(out_shape), in_specs=in_specs,
        out_specs=tuple(out_specs), scratch_shapes=scratch, compiler_params=_params(),
    )(*args)


def _sb_bwd(qkv, off_q, off_k, off_v, heads, dy, tot, carry=None):
    s = qkv.shape[0]
    tq = tk = _tile(s, ATT_K)
    ratio = 1
    nq = s // tq
    hp = _pairing(heads)
    ev = range(hp)
    scale = SB_DIM ** -0.5

    def body(*refs):
        (q_ref, k_ref, v_ref, dy_ref, tot_ref), (dq_ref, dk_ref, dv_ref), _, xrefs = _split_refs(refs, 5, 3, carry)
        if carry is not None:
            _carried_exchange(xrefs, carry[1], *_first_last(heads // hp, nq))
        qi = pl.program_id(1)

        @pl.when(qi == 0)
        def _():
            dk_ref[...] = jnp.zeros_like(dk_ref)
            dv_ref[...] = jnp.zeros_like(dv_ref)

        qs = [q_ref[:, _head_lanes(e)] for e in ev]
        dy_b = [dy_ref[:, _head_lanes(e)].astype(BF16) for e in ev]
        tots = [tot_ref[e] for e in ev]
        rows = lax.broadcasted_iota(jnp.int32, (tk, tk), 0)
        cols = lax.broadcasted_iota(jnp.int32, (tk, tk), 1)
        upto = (rows <= cols).astype(BF16)
        before = (rows < cols).astype(BF16)
        dq_ref[...] = jnp.zeros_like(dq_ref)

        def logits(kb):
            off = pl.multiple_of(kb * tk, tk)
            terms = [_sb_terms(qs[e], k_ref[pl.ds(off, tk), _head_lanes(e)], scale, None) for e in ev]
            da = [_dot_nt(dy_b[e], v_ref[pl.ds(off, tk), _head_lanes(e)]) for e in ev]
            return tuple(terms[e] + (da[e],) for e in ev)

        def step(kb, ahead, state, masked, prefetch):
            rest_left, g_left = state
            off = pl.multiple_of(kb * tk, tk)
            log_beta = [ahead[e][0] for e in ev]
            log_rest = [ahead[e][1] for e in ev]
            da = [ahead[e][2] for e in ev]
            if masked:
                mask = _visible(qi, kb, tq, tk, True)
                log_rest = [jnp.where(mask, log_rest[e], 0.0) for e in ev]
            rest_upto = [_split_dot(log_rest[e], upto) + rest_left[e] for e in ev]
            a = [jnp.exp(log_beta[e] + (tots[e] - rest_upto[e])) for e in ev]
            beta = [jnp.exp(log_beta[e]) for e in ev]
            if masked:
                a = [jnp.where(mask, a[e], 0.0) for e in ev]
                beta = [jnp.where(mask, beta[e], 0.0) for e in ev]
            for e in ev:
                dv_ref[pl.ds(off, tk), _head_lanes(e)] += _dot_tn(a[e].astype(BF16), dy_b[e])
            g = [a[e] * da[e] for e in ev]
            g_before = [_dot(g[e].astype(BF16), before) + g_left[e] for e in ev]
            nxt = logits(kb + 1) if prefetch else None
            dz = [((g[e] * (1.0 - beta[e]) - g_before[e] * beta[e]) * scale).astype(BF16) for e in ev]
            for e in ev:
                dk_ref[pl.ds(off, tk), _head_lanes(e)] += _dot_tn(dz[e], qs[e])
            for e in ev:
                dq_ref[:, _head_lanes(e)] += _dot(dz[e], k_ref[pl.ds(off, tk), _head_lanes(e)])
            state = (tuple(rest_left[e] + jnp.sum(log_rest[e], axis=-1, keepdims=True) for e in ev),
                     tuple(g_left[e] + jnp.sum(g[e], axis=-1, keepdims=True) for e in ev))
            return nxt, state

        zeros = tuple(jnp.zeros((tq, 1), F32) for _ in ev)
        ahead, state = lax.fori_loop(0, qi * ratio, lambda kb, c: step(kb, c[0], c[1], False, True),
                                     (logits(0), (zeros, zeros)))
        for i in range(ratio):
            ahead, state = step(qi * ratio + i, ahead, state, True, i < ratio - 1)

    w = hp * SB_DIM
    args = [qkv, qkv, qkv, dy, tot]
    in_specs = _sb_specs(s, tq, off_q, off_k, off_v, hp) + [pl.BlockSpec((tq, w), lambda h, i: (i, h)),
                                                            pl.BlockSpec((hp, tq, 1), lambda h, i: (h, i, 0))]
    out_shape = [jax.ShapeDtypeStruct((s, heads * SB_DIM), F32)] * 3
    out_specs = [pl.BlockSpec((tq, w), lambda h, i: (i, h)), pl.BlockSpec((s, w), lambda h, i: (0, h)),
                 pl.BlockSpec((s, w), lambda h, i: (0, h))]
    scratch = []
    _with_carry(carry, args, in_specs, out_shape, out_specs, scratch)
    return pl.pallas_call(
        body, name="sb_bwd", grid=(heads // hp, nq), out_shape=tuple(out_shape), in_specs=in_specs,
        out_specs=tuple(out_specs), scratch_shapes=scratch, compiler_params=_params(),
    )(*args)


def _slot_sum(recv, name):
    slots, r, w = recv.shape
    tr = _tile(r, 256)

    def body(r_ref, o_ref):
        acc = r_ref[0].astype(F32)
        for d in range(1, slots):
            acc = acc + r_ref[d].astype(F32)
        o_ref[...] = acc

    return pl.pallas_call(
        body, name=name, grid=(r // tr,),
        out_shape=jax.ShapeDtypeStruct((r, w), F32),
        in_specs=[pl.BlockSpec((slots, tr, w), lambda i: (0, i, 0))],
        out_specs=pl.BlockSpec((tr, w), lambda i: (i, 0)), compiler_params=_params(),
    )(recv)


def _pair_sum(by_core, theirs):
    _, slots, r, w = by_core.shape
    tr = _tile(r, 128)

    def body(a_ref, b_ref, o_ref):
        o_ref[...] = (a_ref[lax.axis_index("c")].astype(F32) + b_ref[...].astype(F32)).astype(BF16)

    spec = pl.BlockSpec((slots, tr, w), lambda i: (0, i, 0))
    return pl.pallas_call(
        body, name="pair_sum", grid=(r // tr,), out_shape=jax.ShapeDtypeStruct(theirs.shape, BF16),
        in_specs=[pl.BlockSpec((2, slots, tr, w), lambda i: (0, 0, i, 0)), spec], out_specs=spec,
        compiler_params=_params(),
    )(by_core, theirs)


def _small_sum(rows, loss_lo, loss_hi, d_model):
    n = rows.shape[1]

    def body(r_ref, o_ref, loss_ref):
        rv = r_ref[...]
        acc = rv[0:1, :]
        for d in range(1, N_DEV):
            acc = acc + rv[d:d + 1, :]
        o_ref[...] = acc
        total = jnp.sum(acc[:, loss_lo:loss_hi], axis=-1, keepdims=True) * (0.5 / d_model)
        loss_ref[...] = jnp.broadcast_to(total, (1, LANES))

    return pl.pallas_call(
        body, name="small_sum",
        out_shape=(jax.ShapeDtypeStruct((1, n), F32), jax.ShapeDtypeStruct((1, LANES), F32)),
        compiler_params=_params(),
    )(rows)


def _adamw(w, g, m, v, name):
    r, c = w.shape
    tr = _tile(r, max(8, (1 << 18) // c // 8 * 8))

    def body(w_ref, g_ref, m_ref, v_ref, d_ref, mo_ref, vo_ref):
        gv = g_ref[...]
        m_new = ADAM_B1 * m_ref[...] + (1.0 - ADAM_B1) * gv
        v_new = ADAM_B2 * v_ref[...] + (1.0 - ADAM_B2) * (gv * gv)
        m_hat = m_new / (1.0 - ADAM_B1 ** ADAM_STEP)
        v_hat = v_new / (1.0 - ADAM_B2 ** ADAM_STEP)
        d_ref[...] = -ADAM_LR * (m_hat / (jnp.sqrt(v_hat) + ADAM_EPS) + ADAM_WD * w_ref[...])
        mo_ref[...] = m_new
        vo_ref[...] = v_new

    spec = pl.BlockSpec((tr, c), lambda i: (i, 0))
    sd = jax.ShapeDtypeStruct((r, c), F32)
    return pl.pallas_call(
        body, name=name, grid=(r // tr,), out_shape=(sd, sd, sd),
        in_specs=[spec] * 4, out_specs=(spec,) * 3, compiler_params=_params(),
    )(w, g, m, v)


def _pack_rows(a):
    return a.reshape(-1, PACK_W)


def _unshard(slots, shape, col_sharded):
    r, c = shape
    if col_sharded:
        return slots.reshape(N_DEV, r, c).transpose(1, 0, 2).reshape(r, N_DEV * c)
    return slots.reshape(N_DEV * r, c)


def _to_shards(full, col_sharded, packed=True):
    r, c = full.shape
    shards = (full.reshape(r, N_DEV, c // N_DEV).transpose(1, 0, 2) if col_sharded
              else full.reshape(N_DEV, r // N_DEV, c))
    return shards.reshape(N_DEV, -1, PACK_W) if packed else shards


def kernel(x, c, positions, w_ada, b_ada, g_norm1, g_norm2, w_in, g_q_latent, g_kv_latent, w_uq, w_ukv, g_q_head, g_k_head, w_proj_mla, w_proj_sb, w_out, w_ffn_in, w_ffn_out, loss_target, m_w_ada, m_b_ada, m_g_norm1, m_g_norm2, m_w_in, m_g_q_latent, m_g_kv_latent, m_w_uq, m_w_ukv, m_g_q_head, m_g_k_head, m_w_proj_mla, m_w_proj_sb, m_w_out, m_w_ffn_in, m_w_ffn_out, v_w_ada, v_b_ada, v_g_norm1, v_g_norm2, v_w_in, v_g_q_latent, v_g_kv_latent, v_w_uq, v_w_ukv, v_g_q_head, v_g_k_head, v_w_proj_mla, v_w_proj_sb, v_w_out, v_w_ffn_in, v_w_ffn_out):
    env = dict(locals())
    drop = lambda a: a[0] if a.ndim == 3 else a
    wts = {n: drop(env[n]) for n in WEIGHT_NAMES}
    mom = {n: drop(env["m_" + n]) for n in WEIGHT_NAMES}
    var = {n: drop(env["v_" + n]) for n in WEIGHT_NAMES}
    xs, tgt, pos = x[0], loss_target[0], positions[0]
    s, d = xs.shape
    lat = wts["w_uq"].shape[0]
    assert wts["w_ukv"].shape[0] == lat
    h_mla = wts["w_uq"].shape[1] * N_DEV // QK_DIM
    sb_w = wts["w_proj_sb"].shape[0]
    h_sb = sb_w // SB_DIM
    d_ff = wts["w_ffn_out"].shape[0] * N_DEV
    me = 4 * lax.axis_index("x") + 2 * lax.axis_index("y") + lax.axis_index("c")

    ref_w = (("c_q", lat), ("c_kv", lat), ("k_pe", ROPE), ("q_sb", sb_w), ("k_sb", sb_w), ("v_sb", sb_w),
             ("gl_a", d), ("gl_b", d))
    ref_off, o = {}, 0
    for n_, w_ in ref_w:
        ref_off[n_] = (o, w_)
        o += w_
    order = ("gl_a", "gl_b", "q_sb", "k_sb", "v_sb", "c_q", "c_kv", "k_pe")
    off, o = {}, 0
    for n_ in order:
        w_ = LANES if n_ == "k_pe" else ref_off[n_][1]
        assert o % w_ == 0
        off[n_] = o
        o += w_

    used_w = o
    proj_w = -(-used_w // (2 * LANES)) * (2 * LANES)

    col_sharded = dict(BIG)
    rows_of = {n: wts[n].size // PACK_W for n, _ in BIG}
    full, grads = {}, {}

    def own_shape(names):
        return len(names) == 1

    def pack_weights(names):
        if own_shape(names):
            return wts[names[0]].astype(BF16)
        return jnp.concatenate([_pack_rows(wts[n].astype(BF16)) for n in names], axis=0)

    def unpack_weights(slots, names):
        r0 = 0
        for n in names:
            part = slots if own_shape(names) else slots[:, r0:r0 + rows_of[n]]
            full[n] = _unshard(part, wts[n].shape, col_sharded[n])
            r0 += rows_of[n]

    def pack_grads(names):
        if own_shape(names):
            return _to_shards(grads[names[0]].astype(BF16), col_sharded[names[0]], packed=False)
        return jnp.concatenate([_to_shards(grads[n].astype(BF16), col_sharded[n]) for n in names], axis=1)

    def unpack_grads(recv, names):
        if isinstance(recv, tuple):
            summed_rows = jnp.concatenate([_slot_sum(part, "slot_sum_%s_%d" % (names[0], i))
                                           for i, part in enumerate(recv)], axis=0)
        else:
            summed_rows = _slot_sum(recv, "slot_sum_" + names[0])
        r0 = 0
        for n in names:
            part = summed_rows if own_shape(names) else summed_rows[r0:r0 + rows_of[n]]
            grads[n] = part.reshape(wts[n].shape)
            r0 += rows_of[n]

    unpack_weights(_gather_two_level(pack_weights(("w_in",)), "gather_w_in"), ("w_in",))
    seg = lambda a, n_: a[:, ref_off[n_][0]:ref_off[n_][0] + ref_off[n_][1]]
    w_in_k = jnp.concatenate([seg(full["w_in"], n_) for n_ in order]
                             + [jnp.zeros((d, proj_w - used_w + LANES - ROPE), BF16)], axis=1)
    pad_gain = lambda g: jnp.pad(g, ((0, 0), (0, QK_PAD - QK_DIM)))
    g_qh, g_kh = pad_gain(wts["g_q_head"]), pad_gain(wts["g_k_head"])

    c_all = _gather_rows(c, "gather_c")
    ada_cols = _exchange(_ada_fwd(c_all, wts["w_ada"]), gather=True, name="gather_ada")
    ada = lax.dynamic_index_in_dim(ada_cols, me, axis=1, keepdims=False).reshape(1, 6 * d) + wts["b_ada"]
    sh1, sc1, gt1, sh2, sc2, gt2 = [ada[:, i * d:(i + 1) * d] for i in range(6)]

    half = ROPE // 2
    ang = pos.astype(F32)[:, None] * (ROPE_THETA ** (-jnp.arange(half, dtype=F32) / half))
    zeros = jnp.zeros((s, LANES - ROPE), F32)
    cos_t = jnp.concatenate([jnp.cos(ang), jnp.cos(ang), zeros], axis=1)
    sin_t = jnp.concatenate([-jnp.sin(ang), jnp.sin(ang), zeros], axis=1)

    h1 = _norm_mod(xs, wts["g_norm1"], sc1, sh1)
    mixer_w = ("w_uq", "w_ukv", "w_proj_mla", "w_proj_sb", "w_out")
    proj, slots = _mm(h1, w_in_k, name="mm_in", carry=(pack_weights(mixer_w), "two_level"))
    unpack_weights(slots, mixer_w)
    w_uq_k = jnp.pad(full["w_uq"].reshape(lat, h_mla, QK_DIM), ((0, 0), (0, 0), (0, QK_PAD - QK_DIM))
                     ).reshape(lat, h_mla * QK_PAD)
    cqn, ckvn = _latent_norm(proj, off["c_q"], off["c_kv"], lat, wts["g_q_latent"], wts["g_kv_latent"])
    q_raw = _mm(cqn, w_uq_k, name="mm_uq")
    kv = _mm(ckvn, full["w_ukv"], name="mm_ukv")
    q, k, v = _qk_prep(q_raw, kv, proj, off["k_pe"], cos_t, sin_t, g_qh, g_kh)
    y_a, lse, slots = _mla_fwd(q, k, v, carry=(pack_weights(("w_ffn_out",)), "two_level"))
    unpack_weights(slots, ("w_ffn_out",))
    assert off["k_sb"] == off["q_sb"] + sb_w and off["v_sb"] == off["k_sb"] + sb_w
    qkv_sb = proj[:, off["q_sb"]:off["q_sb"] + 3 * sb_w].astype(BF16)
    y_b, tot_sb, w_fi_slots = _sb_fwd(qkv_sb, 0, sb_w, 2 * sb_w, h_sb,
                              carry=(wts["w_ffn_in"].astype(BF16), "two_level"))
    ya_p = _mm(y_a, full["w_proj_mla"], name="mm_proj_mla")
    yb_p = _mm(y_b, full["w_proj_sb"], name="mm_proj_sb")
    merged = _merge(proj, off["gl_a"], off["gl_b"], ya_p, yb_p)
    o1 = _mm(merged, full["w_out"], name="mm_out")
    x1, h2 = _resid_norm_mod(xs, o1, gt1, wts["g_norm2"], sc2, sh2)
    gu = _mm(h2, w_fi_slots, shards="b", out_dtype=BF16, name="mm_ffn_in")
    act = _swiglu(gu)
    o2 = _mm(act, full["w_ffn_out"], name="mm_ffn_out")
    dy, d_o2, sums_l = _loss_head(x1, o2, gt2, tgt)

    recv = {}
    d_act = _mm(d_o2, full["w_ffn_out"], tb=True, name="mm_d_act")
    grads["w_ffn_out"] = _mm(act, d_o2, ta=True, out_dtype=BF16, name="mm_g_ffn_out")
    d_gu = _swiglu_bwd(gu, d_act)
    g_fo_send = pack_grads(("w_ffn_out",))
    half = g_fo_send.shape[1] // 2
    g_fi_slots, recv_lo = _mm(h2, d_gu, ta=True, shards="out", out_dtype=BF16, name="mm_g_ffn_in",
                              carry=(g_fo_send[:, :half], False))
    d_h2, recv_hi = _mm(d_gu, w_fi_slots, tb=True, shards="b", name="mm_d_h2", carry=(g_fo_send[:, half:], False))
    recv["w_ffn_out",] = (recv_lo, recv_hi)
    d_x1, d_o1, sums_2 = _norm_mod_bwd(d_h2, x1, wts["g_norm2"], sc2, dy, gate=(o1, gt1))
    grads["w_out"] = _mm(merged, d_o1, ta=True, out_dtype=BF16, name="mm_g_out")
    d_merged = _mm(d_o1, full["w_out"], tb=True, name="mm_d_merged")
    d_yap, d_ybp, d_gla, d_glb = _merge_bwd(proj, off["gl_a"], off["gl_b"], ya_p, yb_p, d_merged)
    grads["w_proj_mla"] = _mm(y_a, d_yap, ta=True, out_dtype=BF16, name="mm_g_proj_mla")
    grads["w_proj_sb"] = _mm(y_b, d_ybp, ta=True, out_dtype=BF16, name="mm_g_proj_sb")
    d_ya = _mm(d_yap, full["w_proj_mla"], tb=True, name="mm_d_ya")
    d_yb = _mm(d_ybp, full["w_proj_sb"], tb=True, name="mm_d_yb")
    dq_sb, dk_sb, dv_sb, recv["w_ffn_in",] = _sb_bwd(qkv_sb, 0, sb_w, 2 * sb_w, h_sb, d_yb, tot_sb,
                                                     carry=(g_fi_slots, False))
    merge_w = ("w_out", "w_proj_mla", "w_proj_sb")
    dq, dk, dv, recv[merge_w] = _mla_bwd(q, k, v, y_a, lse, d_ya, carry=(pack_grads(merge_w), False))
    d_qraw, d_kv, d_kpe, sums_h = _qk_prep_bwd(q_raw, kv, proj, off["k_pe"], cos_t, sin_t, g_qh, g_kh, dq, dk, dv)
    g_uq_k = _mm(cqn, d_qraw, ta=True, out_dtype=BF16, name="mm_g_uq")
    grads["w_uq"] = g_uq_k.reshape(lat, h_mla, QK_PAD)[:, :, :QK_DIM].reshape(lat, h_mla * QK_DIM)
    grads["w_ukv"] = _mm(ckvn, d_kv, ta=True, out_dtype=BF16, name="mm_g_ukv")
    d_cqn = _mm(d_qraw, w_uq_k, tb=True, name="mm_d_cqn")
    d_ckvn = _mm(d_kv, full["w_ukv"], tb=True, name="mm_d_ckvn")
    d_cq, d_ckv, sums_lat = _latent_norm_bwd(proj, off["c_q"], off["c_kv"], lat, wts["g_q_latent"],
                                             wts["g_kv_latent"], d_cqn, d_ckvn)
    d_parts = {"gl_a": d_gla, "gl_b": d_glb, "q_sb": dq_sb, "k_sb": dk_sb, "v_sb": dv_sb, "c_q": d_cq, "c_kv": d_ckv,
               "k_pe": d_kpe}
    d_proj = jnp.concatenate([d_parts[n_].astype(BF16) for n_ in order]
                             + ([jnp.zeros((s, proj_w - used_w), BF16)] if proj_w > used_w else []), axis=1)
    latent_w = ("w_uq", "w_ukv")
    g_in_k, recv[latent_w] = _mm(h1, d_proj, ta=True, out_dtype=BF16, name="mm_g_in",
                                 carry=(pack_grads(latent_w), False))
    grads["w_in"] = jnp.concatenate([g_in_k[:, off[n_]:off[n_] + w_] for n_, w_ in ref_w], axis=1)
    by_core = pack_grads(("w_in",))
    by_core = by_core.reshape((N_DEV // 2, 2) + by_core.shape[1:]).transpose(1, 0, 2, 3)
    from_sibling = _exchange(by_core, gather="sibling", name="swap_w_in")
    d_h1, recv_chips = _mm(d_proj, w_in_k, tb=True, name="mm_d_h1", carry=(_pair_sum(by_core, from_sibling), "chips"))
    grads["w_in"] = _slot_sum(recv_chips, "slot_sum_w_in").reshape(wts["w_in"].shape)
    grad_x, sums_1 = _norm_mod_bwd(d_h1, xs, wts["g_norm1"], sc1, d_x1)

    parts = [sums_1[0:1], sums_1[1:2], sums_2[3:4], sums_2[0:1], sums_2[1:2], sums_l[0:1],
             sums_1[2:3], sums_2[2:3], sums_lat[0:1], sums_lat[1:2], sums_h[0:1], sums_h[1:2], sums_l[1:2]]
    part_off, o = [], 0
    for p in parts:
        part_off.append(o)
        o += p.shape[1]
    all_rows = _gather_rows(jnp.concatenate(parts, axis=1), "gather_small")
    summed, loss_row = _small_sum(all_rows, part_off[12], part_off[12] + d, d)
    take = lambda i, w: summed[:, part_off[i]:part_off[i] + w]
    grads["b_ada"] = summed[:, :6 * d]
    grads["g_norm1"], grads["g_norm2"] = take(6, d), take(7, d)
    grads["g_q_latent"], grads["g_kv_latent"] = take(8, lat), take(9, lat)
    grads["g_q_head"], grads["g_k_head"] = take(10, QK_DIM), take(11, QK_DIM)
    n_ada = 6 * d // N_DEV
    d_ada_mine = lax.dynamic_slice_in_dim(all_rows[:, :6 * d], me * n_ada, n_ada, axis=1)
    grads["w_ada"] = _ada_grad(c_all.T, d_ada_mine)

    for names, slots in recv.items():
        unpack_grads(slots, names)

    delta, new_m, new_v = {}, {}, {}
    for n in ("w_ada",) + tuple(n for n, _ in BIG):
        delta[n], new_m[n], new_v[n] = _adamw(wts[n], grads[n], mom[n], var[n], "adamw_" + n)
    cat = lambda t: jnp.concatenate([t[n] for n in SMALL], axis=1)
    d_s, m_s, v_s = _adamw(cat(wts), cat(grads), cat(mom), cat(var), "adamw_small")
    o = 0
    for n in SMALL:
        w_ = wts[n].shape[1]
        delta[n], new_m[n], new_v[n] = d_s[:, o:o + w_], m_s[:, o:o + w_], v_s[:, o:o + w_]
        o += w_

    lead = lambda t: [t[n].reshape(env[n].shape) for n in WEIGHT_NAMES]
    return (loss_row[0, 0], grad_x[None], *lead(grads), *lead(delta), *lead(new_m), *lead(new_v))
```

```python
import jax
import jax.numpy as jnp
from jax import lax
from jax.experimental import pallas as pl
from jax.experimental.pallas import tpu as pltpu

F32 = jnp.float32
BF16 = jnp.bfloat16

N_DEV = 8
LANES = 128
PACK_W = 1024
VMEM_LIMIT = 52 * 1024 * 1024

EPS = 1e-6
ROPE_THETA = 10000.0
NOPE = 128
ROPE = 64
QK_DIM = NOPE + ROPE
QK_PAD = 2 * LANES
V_DIM = 128
SB_DIM = 128
ATT_Q = 512
ATT_K = 256
MM_TK = 2816
PAIR = 2
ROW_PARTS = 2

ADAM_LR = 0.001
ADAM_B1 = 0.9
ADAM_B2 = 0.999
ADAM_EPS = 1e-08
ADAM_WD = 0.01
ADAM_STEP = 10

WEIGHT_NAMES = ("w_ada", "b_ada", "g_norm1", "g_norm2", "w_in", "g_q_latent", "g_kv_latent", "w_uq", "w_ukv",
                "g_q_head", "g_k_head", "w_proj_mla", "w_proj_sb", "w_out", "w_ffn_in", "w_ffn_out")
BIG = (("w_in", True), ("w_uq", True), ("w_ukv", True), ("w_proj_mla", True), ("w_proj_sb", True),
       ("w_out", False), ("w_ffn_in", True), ("w_ffn_out", False))
SMALL = ("b_ada", "g_norm1", "g_norm2", "g_q_latent", "g_kv_latent", "g_q_head", "g_k_head")


def _tile(n, pref):
    if n <= pref:
        return n
    for step in (LANES, 8):
        for t in range(pref - pref % step, 0, -step):
            if n % t == 0:
                return t
    return n


def _params():
    return pltpu.CompilerParams(vmem_limit_bytes=VMEM_LIMIT)


def _sigmoid(x):
    return 1.0 / (1.0 + jnp.exp(-x))


def _dot(a, b):
    return lax.dot_general(a, b, (((1,), (0,)), ((), ())), preferred_element_type=F32)


def _dot_nt(a, b):
    return lax.dot_general(a, b, (((1,), (1,)), ((), ())), preferred_element_type=F32)


def _dot_tn(a, b):
    return lax.dot_general(a, b, (((0,), (0,)), ((), ())), preferred_element_type=F32)


def _split_dot(x, tri):
    hi = x.astype(BF16)
    lo = (x - hi.astype(F32)).astype(BF16)
    return _dot(hi, tri) + _dot(lo, tri)


_HBM = pl.BlockSpec(memory_space=pltpu.HBM)


def _carry_parts(carry):
    src, gather = carry
    if gather == "sibling":
        shape = tuple(src.shape[1:])
    elif gather == "chips" or not gather:
        shape = tuple(src.shape)
    else:
        shape = (N_DEV,) + tuple(src.shape)
    return (src, _HBM, jax.ShapeDtypeStruct(shape, src.dtype), _HBM,
            [pltpu.SemaphoreType.DMA((N_DEV - 1,)), pltpu.SemaphoreType.DMA((N_DEV - 1,)), pltpu.SemaphoreType.DMA(())])


def _exchange_copies(src_ref, dst_ref, send_sems, recv_sems, local_sem, gather):
    x, y, c = lax.axis_index("x"), lax.axis_index("y"), lax.axis_index("c")
    if gather == "sibling":
        return [pltpu.make_async_remote_copy(
            src_ref=src_ref.at[1 - c], dst_ref=dst_ref, send_sem=send_sems.at[0], recv_sem=recv_sems.at[0],
            device_id=(x, y, 1 - c), device_id_type=pl.DeviceIdType.MESH)]
    if gather == "chips":
        chip = 2 * x + y
        copies = [pltpu.make_async_copy(src_ref.at[chip], dst_ref.at[chip], local_sem)]
        for k in range(1, N_DEV // 2):
            peer = ((1 - x) if (k >> 1) & 1 else x, (1 - y) if k & 1 else y, c)
            copies.append(pltpu.make_async_remote_copy(
                src_ref=src_ref.at[2 * peer[0] + peer[1]], dst_ref=dst_ref.at[chip],
                send_sem=send_sems.at[k - 1], recv_sem=recv_sems.at[k - 1],
                device_id=peer, device_id_type=pl.DeviceIdType.MESH))
        return copies
    me = 4 * x + 2 * y + c

    def slot_for(idx):
        return src_ref if gather else src_ref.at[idx]

    copies = [pltpu.make_async_copy(slot_for(me), dst_ref.at[me], local_sem)]
    for k in range(1, N_DEV):
        peer = ((1 - x) if (k >> 2) & 1 else x, (1 - y) if (k >> 1) & 1 else y, (1 - c) if k & 1 else c)
        peer_idx = 4 * peer[0] + 2 * peer[1] + peer[2]
        copies.append(pltpu.make_async_remote_copy(
            src_ref=slot_for(peer_idx), dst_ref=dst_ref.at[me],
            send_sem=send_sems.at[k - 1], recv_sem=recv_sems.at[k - 1],
            device_id=peer, device_id_type=pl.DeviceIdType.MESH))
    return copies


def _two_level_copies(src_ref, dst_ref, send_sems, recv_sems, local_sem):
    x, y, c = lax.axis_index("x"), lax.axis_index("y"), lax.axis_index("c")
    me, sibling = (x, y, c), (x, y, 1 - c)
    chips = [(1 - x, y), (x, 1 - y), (1 - x, 1 - y)]

    def slot(px, py, pc):
        return dst_ref.at[4 * px + 2 * py + pc]

    def copy(k, block, to, own=False):
        return pltpu.make_async_remote_copy(
            src_ref=src_ref if own else slot(*block), dst_ref=slot(*block),
            send_sem=send_sems.at[k], recv_sem=recv_sems.at[k],
            device_id=to, device_id_type=pl.DeviceIdType.MESH)

    mine = pltpu.make_async_copy(src_ref, slot(*me), local_sem)
    first = [copy(0, me, sibling, own=True)] + [copy(1 + j, me, (*chip, c), own=True) for j, chip in enumerate(chips)]
    relays = [(copy(1 + j, (*chip, c), me), copy(4 + j, (*chip, c), sibling)) for j, chip in enumerate(chips)]
    late = [copy(0, sibling, me)] + [copy(4 + j, (*chip, 1 - c), me) for j, chip in enumerate(chips)]
    return mine, first, relays, late


def _two_level_start(refs):
    mine, first, _, _ = _two_level_copies(*refs)
    mine.start()
    for cp in first:
        cp.start()


def _two_level_finish(refs):
    mine, first, relays, late = _two_level_copies(*refs)
    for arrival, onward in relays:
        arrival.wait_recv()
        onward.start()
    for cp in late:
        cp.wait_recv()
    for cp in first + [onward for _, onward in relays]:
        cp.wait_send()
    mine.wait()


def _carried_exchange(refs, gather, first, last):
    @pl.when(first)
    def _():
        if gather == "two_level":
            _two_level_start(refs)
        else:
            for cp in _exchange_copies(*refs, gather):
                cp.start()

    @pl.when(last)
    def _():
        if gather == "two_level":
            _two_level_finish(refs)
        else:
            for cp in _exchange_copies(*refs, gather):
                cp.wait()


def _exchange(src, *, gather, name):
    operand, in_spec, out_shape, out_spec, scratch = _carry_parts((src, gather))

    def body(src_ref, dst_ref, send_sems, recv_sems, local_sem):
        copies = _exchange_copies(src_ref, dst_ref, send_sems, recv_sems, local_sem, gather)
        for cp in copies:
            cp.start()
        for cp in copies:
            cp.wait()

    return pl.pallas_call(body, name=name, out_shape=out_shape, in_specs=[in_spec], out_specs=out_spec,
                          scratch_shapes=scratch)(operand)


def _gather_two_level(src, name):
    operand, in_spec, out_shape, out_spec, scratch = _carry_parts((src, True))

    def body(*refs):
        _two_level_start(refs)
        _two_level_finish(refs)

    return pl.pallas_call(body, name=name, out_shape=out_shape, in_specs=[in_spec], out_specs=out_spec,
                          scratch_shapes=scratch)(operand)


def _gather_rows(v, name):
    n = v.shape[1]
    padded = -(-n // (8 * LANES)) * (8 * LANES)
    tiles = jnp.pad(v, ((0, 0), (0, padded - n))).reshape(padded // LANES, LANES)
    return _exchange(tiles, gather=True, name=name).reshape(N_DEV, padded)[:, :n]


def _split_refs(refs, n_in, n_out, carry):
    if carry is None:
        return refs[:n_in], refs[n_in:n_in + n_out], refs[n_in + n_out:], None
    ins, src_ref = refs[:n_in], refs[n_in]
    outs, dst_ref = refs[n_in + 1:n_in + 1 + n_out], refs[n_in + 1 + n_out]
    rest = refs[n_in + n_out + 2:]
    return ins, outs, rest[:-3], (src_ref, dst_ref) + tuple(rest[-3:])


def _with_carry(carry, args, in_specs, out_shape, out_specs, scratch):
    if carry is not None:
        operand, c_in, c_shape, c_out, c_scratch = _carry_parts(carry)
        args.append(operand)
        in_specs.append(c_in)
        out_shape.append(c_shape)
        out_specs.append(c_out)
        scratch.extend(c_scratch)


def _slot_of_chunk(j):
    return j // 2 + (N_DEV // 2) * (j % 2)


def _mm(a, b, *, ta=False, tb=False, out_dtype=F32, name, carry=None, shards=None):
    kdim, m = a.shape if ta else a.shape[::-1]
    if shards == "b":
        chunk = b.shape[2]
        n, kdim_b = (b.shape[1], N_DEV * chunk) if tb else (N_DEV * chunk, b.shape[1])
    else:
        n, kdim_b = b.shape if tb else b.shape[::-1]
    assert kdim == kdim_b, (a.shape, b.shape, ta, tb)
    tm, tn, tk = _tile(m, 1024), _tile(n, 1024), _tile(kdim, MM_TK)
    if shards == "b":
        tn, tk = (tn, chunk) if tb else (chunk, tk)
    elif shards == "out":
        chunk = tn = n // N_DEV
    grid = (m // tm, n // tn, kdim // tk)
    nk = grid[2]
    dims = (((0 if ta else 1,), (1 if tb else 0,)), ((), ()))

    def body(*refs):
        (a_ref, b_ref), (o_ref,), scratch_refs, xrefs = _split_refs(refs, 2, 1, carry)
        i, j, k = pl.program_id(0), pl.program_id(1), pl.program_id(2)
        if carry is not None:
            _carried_exchange(xrefs, carry[1], (i == 0) & (j == 0) & (k == 0),
                              (i == grid[0] - 1) & (j == grid[1] - 1) & (k == nk - 1))
        prod = lax.dot_general(a_ref[...].astype(BF16), b_ref[...].astype(BF16), dims, preferred_element_type=F32)
        if nk == 1:
            o_ref[...] = prod.astype(o_ref.dtype)
        else:
            acc_ref = scratch_refs[0]

            @pl.when(k == 0)
            def _():
                acc_ref[...] = prod

            @pl.when(k > 0)
            def _():
                acc_ref[...] += prod

            @pl.when(k == nk - 1)
            def _():
                o_ref[...] = acc_ref[...].astype(o_ref.dtype)

    a_spec = (pl.BlockSpec((tk, tm), lambda i, j, k: (k, i)) if ta else pl.BlockSpec((tm, tk), lambda i, j, k: (i, k)))
    if shards == "b":
        b_spec = (pl.BlockSpec((None, tn, tk), lambda i, j, k: (_slot_of_chunk(k), j, 0)) if tb
                  else pl.BlockSpec((None, tk, tn), lambda i, j, k: (_slot_of_chunk(j), k, 0)))
    else:
        b_spec = (pl.BlockSpec((tn, tk), lambda i, j, k: (j, k)) if tb else pl.BlockSpec((tk, tn), lambda i, j, k: (k, j)))
    args, in_specs = [a, b], [a_spec, b_spec]
    if shards == "out":
        out_shape = [jax.ShapeDtypeStruct((N_DEV, m, tn), out_dtype)]
        out_specs = [pl.BlockSpec((None, tm, tn), lambda i, j, k: (_slot_of_chunk(j), i, 0))]
    else:
        out_shape, out_specs = [jax.ShapeDtypeStruct((m, n), out_dtype)], [pl.BlockSpec((tm, tn), lambda i, j, k: (i, j))]
    scratch = [] if nk == 1 else [pltpu.VMEM((tm, tn), F32)]
    _with_carry(carry, args, in_specs, out_shape, out_specs, scratch)
    out = pl.pallas_call(
        body, name=name, grid=grid, out_shape=tuple(out_shape), in_specs=in_specs, out_specs=tuple(out_specs),
        scratch_shapes=scratch, compiler_params=_params(),
    )(*args)
    return out if carry is not None else out[0]


def _ada_fwd(c_all, w_shard):
    d, n = w_shard.shape
    tn = _tile(n, 512)

    def body(c_ref, w_ref, o_ref):
        cv = c_ref[...]
        o_ref[...] = jnp.dot(cv * _sigmoid(cv), w_ref[...], precision=lax.Precision.HIGHEST,
                             preferred_element_type=F32)

    return pl.pallas_call(
        body, name="ada_fwd", grid=(n // tn,),
        out_shape=jax.ShapeDtypeStruct((N_DEV, n), F32),
        in_specs=[pl.BlockSpec((N_DEV, d), lambda j: (0, 0)), pl.BlockSpec((d, tn), lambda j: (0, j))],
        out_specs=pl.BlockSpec((N_DEV, tn), lambda j: (0, j)),
        compiler_params=_params(),
    )(c_all, w_shard)


def _ada_grad(c_all_t, d_rows):
    d, n = c_all_t.shape[0], d_rows.shape[1]
    tn = _tile(n, 512)

    def body(ct_ref, d_ref, o_ref):
        cv = ct_ref[...]
        s = cv * _sigmoid(cv)
        dv = d_ref[...]
        acc = s[:, 0:1] * dv[0:1, :]
        for b in range(1, N_DEV):
            acc = acc + s[:, b:b + 1] * dv[b:b + 1, :]
        o_ref[...] = acc

    return pl.pallas_call(
        body, name="ada_grad", grid=(n // tn,),
        out_shape=jax.ShapeDtypeStruct((d, n), F32),
        in_specs=[pl.BlockSpec((d, N_DEV), lambda j: (0, 0)), pl.BlockSpec((N_DEV, tn), lambda j: (0, j))],
        out_specs=pl.BlockSpec((d, tn), lambda j: (0, j)),
        compiler_params=_params(),
    )(c_all_t, d_rows)


def _row(ts, w, col=0):
    return pl.BlockSpec((ts, w), lambda i, col=col: (i, col))


def _vec(w):
    return pl.BlockSpec((1, w), lambda i: (0, 0))


def _norm_mod(x, g, sc, sh):
    s, d = x.shape
    ts = _tile(s, 512)

    def body(x_ref, g_ref, sc_ref, sh_ref, h_ref):
        xv = x_ref[...]
        r = lax.rsqrt(jnp.mean(xv * xv, axis=-1, keepdims=True) + EPS)
        h_ref[...] = ((xv * r) * g_ref[...] * (1.0 + sc_ref[...]) + sh_ref[...]).astype(BF16)

    return pl.pallas_call(
        body, name="norm_mod", grid=(s // ts,),
        out_shape=jax.ShapeDtypeStruct((s, d), BF16),
        in_specs=[_row(ts, d), _vec(d), _vec(d), _vec(d)],
        out_specs=_row(ts, d), compiler_params=_params(),
    )(x, g, sc, sh)


def _latent_norm(proj, off_q, off_kv, lat, g_q, g_kv):
    s = proj.shape[0]
    ts = _tile(s, 512)

    def body(cq_ref, ckv_ref, gq_ref, gkv_ref, oq_ref, okv_ref):
        for c_ref, g_ref, o_ref in ((cq_ref, gq_ref, oq_ref), (ckv_ref, gkv_ref, okv_ref)):
            v = c_ref[...]
            r = lax.rsqrt(jnp.mean(v * v, axis=-1, keepdims=True) + EPS)
            o_ref[...] = ((v * r) * g_ref[...]).astype(BF16)

    return pl.pallas_call(
        body, name="latent_norm", grid=(s // ts,),
        out_shape=(jax.ShapeDtypeStruct((s, lat), BF16), jax.ShapeDtypeStruct((s, lat), BF16)),
        in_specs=[_row(ts, lat, off_q // lat), _row(ts, lat, off_kv // lat), _vec(lat), _vec(lat)],
        out_specs=(_row(ts, lat), _row(ts, lat)), compiler_params=_params(),
    )(proj, proj, g_q, g_kv)


def _latent_norm_bwd(proj, off_q, off_kv, lat, g_q, g_kv, d_cqn, d_ckvn):
    s = proj.shape[0]
    ts = _tile(s, 512)

    def body(cq_ref, ckv_ref, gq_ref, gkv_ref, dq_ref, dkv_ref, oq_ref, okv_ref, sums_ref):
        @pl.when(pl.program_id(0) == 0)
        def _():
            sums_ref[...] = jnp.zeros_like(sums_ref)

        for row, (c_ref, g_ref, d_ref, o_ref) in enumerate(((cq_ref, gq_ref, dq_ref, oq_ref),
                                                             (ckv_ref, gkv_ref, dkv_ref, okv_ref))):
            v = c_ref[...]
            r = lax.rsqrt(jnp.mean(v * v, axis=-1, keepdims=True) + EPS)
            vn = v * r
            dn = d_ref[...]
            sums_ref[row:row + 1, :] += jnp.sum(dn * vn, axis=0, keepdims=True)
            dvn = dn * g_ref[...]
            o_ref[...] = (r * (dvn - vn * jnp.mean(dvn * vn, axis=-1, keepdims=True))).astype(BF16)

    return pl.pallas_call(
        body, name="latent_norm_bwd", grid=(s // ts,),
        out_shape=(jax.ShapeDtypeStruct((s, lat), BF16), jax.ShapeDtypeStruct((s, lat), BF16),
                   jax.ShapeDtypeStruct((8, lat), F32)),
        in_specs=[_row(ts, lat, off_q // lat), _row(ts, lat, off_kv // lat), _vec(lat), _vec(lat),
                  _row(ts, lat), _row(ts, lat)],
        out_specs=(_row(ts, lat), _row(ts, lat), pl.BlockSpec((8, lat), lambda i: (0, 0))),
        compiler_params=_params(),
    )(proj, proj, g_q, g_kv, d_cqn, d_ckvn)


def _merge(proj, off_a, off_b, ya_p, yb_p):
    s, d = ya_p.shape
    ts = _tile(s, 256)

    def body(ga_ref, gb_ref, ya_ref, yb_ref, o_ref):
        o_ref[...] = (_sigmoid(ga_ref[...]) * ya_ref[...] + _sigmoid(gb_ref[...]) * yb_ref[...]).astype(BF16)

    return pl.pallas_call(
        body, name="merge", grid=(s // ts,),
        out_shape=jax.ShapeDtypeStruct((s, d), BF16),
        in_specs=[_row(ts, d, off_a // d), _row(ts, d, off_b // d), _row(ts, d), _row(ts, d)],
        out_specs=_row(ts, d), compiler_params=_params(),
    )(proj, proj, ya_p, yb_p)


def _merge_bwd(proj, off_a, off_b, ya_p, yb_p, d_merged):
    s, d = ya_p.shape
    ts = _tile(s, 256)

    def body(ga_ref, gb_ref, ya_ref, yb_ref, dm_ref, dya_ref, dyb_ref, dga_ref, dgb_ref):
        dm = dm_ref[...]
        for g_ref, y_ref, dy_ref, dg_ref in ((ga_ref, ya_ref, dya_ref, dga_ref), (gb_ref, yb_ref, dyb_ref, dgb_ref)):
            sg = _sigmoid(g_ref[...])
            dy_ref[...] = (dm * sg).astype(BF16)
            dg_ref[...] = (dm * y_ref[...] * sg * (1.0 - sg)).astype(BF16)

    sd = jax.ShapeDtypeStruct((s, d), BF16)
    return pl.pallas_call(
        body, name="merge_bwd", grid=(s // ts,),
        out_shape=(sd, sd, sd, sd),
        in_specs=[_row(ts, d, off_a // d), _row(ts, d, off_b // d), _row(ts, d), _row(ts, d), _row(ts, d)],
        out_specs=(_row(ts, d),) * 4, compiler_params=_params(),
    )(proj, proj, ya_p, yb_p, d_merged)


def _resid_norm_mod(x, o, gt, g, sc, sh):
    s, d = x.shape
    ts = _tile(s, 256)

    def body(x_ref, o_ref, gt_ref, g_ref, sc_ref, sh_ref, x1_ref, h_ref):
        x1 = x_ref[...] + gt_ref[...] * o_ref[...]
        x1_ref[...] = x1
        r = lax.rsqrt(jnp.mean(x1 * x1, axis=-1, keepdims=True) + EPS)
        h_ref[...] = ((x1 * r) * g_ref[...] * (1.0 + sc_ref[...]) + sh_ref[...]).astype(BF16)

    return pl.pallas_call(
        body, name="resid_norm_mod", grid=(s // ts,),
        out_shape=(jax.ShapeDtypeStruct((s, d), F32), jax.ShapeDtypeStruct((s, d), BF16)),
        in_specs=[_row(ts, d), _row(ts, d), _vec(d), _vec(d), _vec(d), _vec(d)],
        out_specs=(_row(ts, d), _row(ts, d)), compiler_params=_params(),
    )(x, o, gt, g, sc, sh)


def _loss_head(x1, o2, gt2, target):
    s, d = x1.shape
    ts = _tile(s, 256)

    def body(x1_ref, o2_ref, gt_ref, t_ref, dy_ref, do_ref, sums_ref):
        @pl.when(pl.program_id(0) == 0)
        def _():
            sums_ref[...] = jnp.zeros_like(sums_ref)

        o2 = o2_ref[...]
        e = x1_ref[...] + gt_ref[...] * o2 - t_ref[...]
        dy = e / d
        dy_ref[...] = dy
        do_ref[...] = (dy * gt_ref[...]).astype(BF16)
        sums_ref[0:1, :] += jnp.sum(dy * o2, axis=0, keepdims=True)
        sums_ref[1:2, :] += jnp.sum(e * e, axis=0, keepdims=True)

    return pl.pallas_call(
        body, name="loss_head", grid=(s // ts,),
        out_shape=(jax.ShapeDtypeStruct((s, d), F32), jax.ShapeDtypeStruct((s, d), BF16),
                   jax.ShapeDtypeStruct((8, d), F32)),
        in_specs=[_row(ts, d), _row(ts, d), _vec(d), _row(ts, d)],
        out_specs=(_row(ts, d), _row(ts, d), pl.BlockSpec((8, d), lambda i: (0, 0))),
        compiler_params=_params(),
    )(x1, o2, gt2, target)


def _norm_mod_bwd(dh, xin, g, sc, dres, gate=None):
    s, d = xin.shape
    ts = _tile(s, 256)
    gated = gate is not None

    def body(*refs):
        if gated:
            dh_ref, x_ref, g_ref, sc_ref, dr_ref, o_ref, gt_ref, dx_ref, do_ref, sums_ref = refs
        else:
            dh_ref, x_ref, g_ref, sc_ref, dr_ref, dx_ref, sums_ref = refs

        @pl.when(pl.program_id(0) == 0)
        def _():
            sums_ref[...] = jnp.zeros_like(sums_ref)

        xv, dhv = x_ref[...], dh_ref[...]
        r = lax.rsqrt(jnp.mean(xv * xv, axis=-1, keepdims=True) + EPS)
        xn = xv * r
        one_sc = 1.0 + sc_ref[...]
        sums_ref[0:1, :] += jnp.sum(dhv, axis=0, keepdims=True)
        sums_ref[1:2, :] += jnp.sum(dhv * (xn * g_ref[...]), axis=0, keepdims=True)
        sums_ref[2:3, :] += jnp.sum(dhv * one_sc * xn, axis=0, keepdims=True)
        dxn = dhv * one_sc * g_ref[...]
        dx = dr_ref[...] + r * (dxn - xn * jnp.mean(dxn * xn, axis=-1, keepdims=True))
        dx_ref[...] = dx
        if gated:
            sums_ref[3:4, :] += jnp.sum(dx * o_ref[...], axis=0, keepdims=True)
            do_ref[...] = (dx * gt_ref[...]).astype(BF16)

    in_specs = [_row(ts, d), _row(ts, d), _vec(d), _vec(d), _row(ts, d)]
    args = [dh, xin, g, sc, dres]
    out_shape = [jax.ShapeDtypeStruct((s, d), F32)]
    out_specs = [_row(ts, d)]
    if gated:
        in_specs += [_row(ts, d), _vec(d)]
        args += list(gate)
        out_shape.append(jax.ShapeDtypeStruct((s, d), BF16))
        out_specs.append(_row(ts, d))
    out_shape.append(jax.ShapeDtypeStruct((8, d), F32))
    out_specs.append(pl.BlockSpec((8, d), lambda i: (0, 0)))
    return pl.pallas_call(
        body, name="norm_mod_bwd_gated" if gated else "norm_mod_bwd", grid=(s // ts,),
        out_shape=tuple(out_shape), in_specs=in_specs, out_specs=tuple(out_specs), compiler_params=_params(),
    )(*args)


def _ffn_tile(f):
    return 2 * f // N_DEV


def _swiglu(gu):
    s, f2 = gu.shape
    f = f2 // 2
    ts, tc = _tile(s, 512), _ffn_tile(f)

    def body(gu_ref, o_ref):
        gv, uv = gu_ref[:, :tc].astype(F32), gu_ref[:, tc:].astype(F32)
        o_ref[...] = (gv * _sigmoid(gv) * uv).astype(BF16)

    return pl.pallas_call(
        body, name="swiglu", grid=(s // ts, f // tc),
        out_shape=jax.ShapeDtypeStruct((s, f), BF16),
        in_specs=[pl.BlockSpec((ts, 2 * tc), lambda i, j: (i, j))],
        out_specs=pl.BlockSpec((ts, tc), lambda i, j: (i, j)), compiler_params=_params(),
    )(gu)


def _mm_swiglu_bwd(d_o2, w_fo, gu):
    s, d = d_o2.shape
    f = w_fo.shape[0]
    tm, tc = _tile(s, 512), _ffn_tile(f)

    def body(a_ref, b_ref, gu_ref, o_ref):
        da = _dot_nt(a_ref[...], b_ref[...])
        gv, uv = gu_ref[:, :tc].astype(F32), gu_ref[:, tc:].astype(F32)
        sg = _sigmoid(gv)
        o_ref[:, :tc] = (da * uv * (sg * (1.0 + gv * (1.0 - sg)))).astype(BF16)
        o_ref[:, tc:] = (da * (gv * sg)).astype(BF16)

    return pl.pallas_call(
        body, name="mm_d_act_swiglu", grid=(s // tm, f // tc),
        out_shape=jax.ShapeDtypeStruct((s, 2 * f), BF16),
        in_specs=[pl.BlockSpec((tm, d), lambda i, j: (i, 0)), pl.BlockSpec((tc, d), lambda i, j: (j, 0)),
                  pl.BlockSpec((tm, 2 * tc), lambda i, j: (i, j))],
        out_specs=pl.BlockSpec((tm, 2 * tc), lambda i, j: (i, j)), compiler_params=_params(),
    )(d_o2, w_fo, gu)


def _swap_halves(t):
    return pltpu.roll(t, ROPE // 2, 1) + pltpu.roll(t, LANES - ROPE // 2, 1)


def _head_norm(raw):
    r = lax.rsqrt(jnp.sum(raw * raw, axis=-1, keepdims=True) / QK_DIM + EPS)
    return raw * r, r


def _rope_fwd(v, cos, sin):
    rope_tile = v[:, NOPE:]
    return jnp.concatenate([v[:, :NOPE], rope_tile * cos + _swap_halves(rope_tile) * sin], axis=1)


def _rope_bwd(d, cos, sin, lane_ok):
    d_tile = d[:, NOPE:]
    return jnp.concatenate([d[:, :NOPE], d_tile * cos + _swap_halves(d_tile * sin) * lane_ok], axis=1)


def _qk_prep(q_raw, kv, proj, off_pe, cos, sin, g_q, g_k):
    s, hw = q_raw.shape
    heads = hw // QK_PAD
    ts = _tile(s, 512)

    def body(q_ref, kv_ref, pe_ref, cos_ref, sin_ref, gq_ref, gk_ref, qo_ref, ko_ref, vo_ref):
        cos_v, sin_v = cos_ref[...], sin_ref[...]
        qn, _ = _head_norm(q_ref[...])
        qo_ref[...] = _rope_fwd(qn * gq_ref[...], cos_v, sin_v).astype(BF16)
        kvv = kv_ref[...]
        kn, _ = _head_norm(jnp.concatenate([kvv[:, :NOPE], pe_ref[...]], axis=1))
        ko_ref[...] = _rope_fwd(kn * gk_ref[...], cos_v, sin_v).astype(BF16)
        vo_ref[...] = kvv[:, NOPE:].astype(BF16)

    blk = lambda w: pl.BlockSpec((ts, w), lambda i, h: (i, h))
    fixed = lambda w, col=0: pl.BlockSpec((ts, w), lambda i, h, col=col: (i, col))
    vec = pl.BlockSpec((1, QK_PAD), lambda i, h: (0, 0))
    return pl.pallas_call(
        body, name="qk_prep", grid=(s // ts, heads),
        out_shape=(jax.ShapeDtypeStruct((s, hw), BF16), jax.ShapeDtypeStruct((s, hw), BF16),
                   jax.ShapeDtypeStruct((s, heads * V_DIM), BF16)),
        in_specs=[blk(QK_PAD), blk(QK_PAD), fixed(LANES, off_pe // LANES), fixed(LANES), fixed(LANES), vec, vec],
        out_specs=(blk(QK_PAD), blk(QK_PAD), blk(V_DIM)), compiler_params=_params(),
    )(q_raw, kv, proj, cos, sin, g_q, g_k)


def _qk_prep_bwd(q_raw, kv, proj, off_pe, cos, sin, g_q, g_k, dq, dk, dv):
    s, hw = q_raw.shape
    heads = hw // QK_PAD
    ts = _tile(s, 512)

    def body(q_ref, kv_ref, pe_ref, cos_ref, sin_ref, gq_ref, gk_ref, dq_ref, dk_ref, dv_ref,
             dqr_ref, dkv_ref, dpe_ref, sums_ref):
        i, h = pl.program_id(0), pl.program_id(1)

        @pl.when((i == 0) & (h == 0))
        def _():
            sums_ref[...] = jnp.zeros_like(sums_ref)

        cos_v, sin_v = cos_ref[...], sin_ref[...]
        lane_ok = (lax.broadcasted_iota(jnp.int32, (ts, LANES), 1) < ROPE).astype(F32)

        def one(raw, g, d_post, row):
            vn, r = _head_norm(raw)
            d_pre = _rope_bwd(d_post, cos_v, sin_v, lane_ok)
            sums_ref[row:row + 1, :] += jnp.sum(d_pre * vn, axis=0, keepdims=True)
            dvn = d_pre * g
            return r * (dvn - vn * (jnp.sum(dvn * vn, axis=-1, keepdims=True) / QK_DIM))

        dqr_ref[...] = one(q_ref[...], gq_ref[...], dq_ref[...], 0).astype(BF16)
        kvv = kv_ref[...]
        d_kraw = one(jnp.concatenate([kvv[:, :NOPE], pe_ref[...]], axis=1), gk_ref[...], dk_ref[...], 1)
        dkv_ref[...] = jnp.concatenate([d_kraw[:, :NOPE], dv_ref[...]], axis=1).astype(BF16)

        @pl.when(h == 0)
        def _():
            dpe_ref[...] = jnp.zeros_like(dpe_ref)

        dpe_ref[...] += d_kraw[:, NOPE:]

    blk = lambda w: pl.BlockSpec((ts, w), lambda i, h: (i, h))
    fixed = lambda w, col=0: pl.BlockSpec((ts, w), lambda i, h, col=col: (i, col))
    vec = pl.BlockSpec((1, QK_PAD), lambda i, h: (0, 0))
    return pl.pallas_call(
        body, name="qk_prep_bwd", grid=(s // ts, heads),
        out_shape=(jax.ShapeDtypeStruct((s, hw), BF16), jax.ShapeDtypeStruct((s, hw), BF16),
                   jax.ShapeDtypeStruct((s, LANES), F32), jax.ShapeDtypeStruct((8, QK_PAD), F32)),
        in_specs=[blk(QK_PAD), blk(QK_PAD), fixed(LANES, off_pe // LANES), fixed(LANES), fixed(LANES), vec, vec,
                  blk(QK_PAD), blk(QK_PAD), blk(V_DIM)],
        out_specs=(blk(QK_PAD), blk(QK_PAD), fixed(LANES), pl.BlockSpec((8, QK_PAD), lambda i, h: (0, 0))),
        compiler_params=_params(),
    )(q_raw, kv, proj, cos, sin, g_q, g_k, dq, dk, dv)


def _att_blocks(s):
    tq = _tile(s, ATT_Q)
    tk = _tile(tq, ATT_K)
    return tq, tk, tq // tk


def _pairing(heads):
    return PAIR if heads % PAIR == 0 else 1


def _lanes(e, width):
    return slice(e * width, (e + 1) * width)


def _visible(qi, kb, tq, tk, strict):
    row = qi * tq + lax.broadcasted_iota(jnp.int32, (tq, tk), 0)
    col = kb * tk + lax.broadcasted_iota(jnp.int32, (tq, tk), 1)
    return (col < row) if strict else (col <= row)


def _first_last(heads, nq):
    h, qi = pl.program_id(0), pl.program_id(1)
    return (h == 0) & (qi == 0), (h == heads - 1) & (qi == nq - 1)


def _mla_fwd(q, k, v, carry=None):
    s = q.shape[0]
    heads = q.shape[1] // QK_PAD
    tq = tk = _tile(s, ATT_Q)
    ratio = 1
    nq = s // tq
    scale = QK_DIM ** -0.5

    hp = _pairing(heads)
    ev = range(hp)

    def body(*refs):
        (q_ref, k_ref, v_ref), (o_ref, lse_ref), (acc_ref,), xrefs = _split_refs(refs, 3, 2, carry)
        if carry is not None:
            _carried_exchange(xrefs, carry[1], *_first_last(heads // hp, nq))
        qi = pl.program_id(1)
        qs = [q_ref[:, _lanes(e, QK_PAD)] for e in ev]
        acc_ref[...] = jnp.zeros_like(acc_ref)

        def step(kb, state, masked):
            ms, ls = state
            off = pl.multiple_of(kb * tk, tk)
            sc = [_dot_nt(qs[e], k_ref[pl.ds(off, tk), _lanes(e, QK_PAD)]) * scale for e in ev]
            if masked:
                mask = _visible(qi, kb, tq, tk, False)
                sc = [jnp.where(mask, sc[e], -1e30) for e in ev]
            m_new = [jnp.maximum(ms[e], jnp.max(sc[e], axis=-1, keepdims=True)) for e in ev]
            alpha = [jnp.exp(ms[e] - m_new[e]) for e in ev]
            p = [jnp.exp(sc[e] - m_new[e]) for e in ev]
            for e in ev:
                lanes = _lanes(e, V_DIM)
                acc_ref[:, lanes] = alpha[e] * acc_ref[:, lanes] + _dot(p[e].astype(BF16), v_ref[pl.ds(off, tk), lanes])
            return tuple(m_new), tuple(alpha[e] * ls[e] + jnp.sum(p[e], axis=-1, keepdims=True) for e in ev)

        state = (tuple(jnp.full((tq, 1), -1e30, F32) for _ in ev), tuple(jnp.zeros((tq, 1), F32) for _ in ev))
        state = lax.fori_loop(0, qi * ratio, lambda kb, st: step(kb, st, False), state)
        for i in range(ratio):
            state = step(qi * ratio + i, state, True)
        ms, ls = state
        for e in ev:
            o_ref[:, _lanes(e, V_DIM)] = (acc_ref[:, _lanes(e, V_DIM)] / ls[e]).astype(BF16)
            lse_ref[e] = ms[e] + jnp.log(ls[e])

    args = [q, k, v]
    in_specs = [pl.BlockSpec((tq, hp * QK_PAD), lambda h, i: (i, h)), pl.BlockSpec((s, hp * QK_PAD), lambda h, i: (0, h)),
                pl.BlockSpec((s, hp * V_DIM), lambda h, i: (0, h))]
    out_shape = [jax.ShapeDtypeStruct((s, heads * V_DIM), BF16), jax.ShapeDtypeStruct((heads, s, 1), F32)]
    out_specs = [pl.BlockSpec((tq, hp * V_DIM), lambda h, i: (i, h)), pl.BlockSpec((hp, tq, 1), lambda h, i: (h, i, 0))]
    scratch = [pltpu.VMEM((tq, hp * V_DIM), F32)]
    _with_carry(carry, args, in_specs, out_shape, out_specs, scratch)
    return pl.pallas_call(
        body, name="mla_fwd", grid=(heads // hp, nq), out_shape=tuple(out_shape), in_specs=in_specs,
        out_specs=tuple(out_specs), scratch_shapes=scratch, compiler_params=_params(),
    )(*args)


def _mla_bwd(q, k, v, o, lse, do, carry=None):
    s = q.shape[0]
    heads = q.shape[1] // QK_PAD
    tq = tk = _tile(s, ATT_Q)
    ratio = 1
    nq = s // tq
    scale = QK_DIM ** -0.5

    hp = _pairing(heads)
    ev = range(hp)

    def body(*refs):
        (q_ref, k_ref, v_ref, o_ref, lse_ref, do_ref), (dq_ref, dk_ref, dv_ref), _, xrefs = _split_refs(refs, 6, 3, carry)
        if carry is not None:
            _carried_exchange(xrefs, carry[1], *_first_last(heads // hp, nq))
        qi = pl.program_id(1)

        @pl.when(qi == 0)
        def _():
            dk_ref[...] = jnp.zeros_like(dk_ref)
            dv_ref[...] = jnp.zeros_like(dv_ref)

        qs = [q_ref[:, _lanes(e, QK_PAD)] for e in ev]
        dos = [do_ref[:, _lanes(e, V_DIM)] for e in ev]
        do_b = [dos[e].astype(BF16) for e in ev]
        delta = [jnp.sum(dos[e] * o_ref[:, _lanes(e, V_DIM)].astype(F32), axis=-1, keepdims=True) for e in ev]
        lse_v = [lse_ref[e] for e in ev]
        dq_ref[...] = jnp.zeros_like(dq_ref)

        def step(kb, masked):
            off = pl.multiple_of(kb * tk, tk)
            ks = [k_ref[pl.ds(off, tk), _lanes(e, QK_PAD)] for e in ev]
            vs = [v_ref[pl.ds(off, tk), _lanes(e, V_DIM)] for e in ev]
            sc = [_dot_nt(qs[e], ks[e]) for e in ev]
            dp = [_dot_nt(do_b[e], vs[e]) for e in ev]
            p = [jnp.exp(sc[e] * scale - lse_v[e]) for e in ev]
            if masked:
                mask = _visible(qi, kb, tq, tk, False)
                p = [jnp.where(mask, p[e], 0.0) for e in ev]
            ds = [(p[e] * (dp[e] - delta[e]) * scale).astype(BF16) for e in ev]
            for e in ev:
                dv_ref[pl.ds(off, tk), _lanes(e, V_DIM)] += _dot_tn(p[e].astype(BF16), do_b[e])
            for e in ev:
                dk_ref[pl.ds(off, tk), _lanes(e, QK_PAD)] += _dot_tn(ds[e], qs[e])
            for e in ev:
                dq_ref[:, _lanes(e, QK_PAD)] += _dot(ds[e], ks[e])
            return 0

        lax.fori_loop(0, qi * ratio, lambda kb, _: step(kb, False), 0)
        for i in range(ratio):
            step(qi * ratio + i, True)

    args = [q, k, v, o, lse, do]
    wide, narrow = hp * QK_PAD, hp * V_DIM
    in_specs = [pl.BlockSpec((tq, wide), lambda h, i: (i, h)), pl.BlockSpec((s, wide), lambda h, i: (0, h)),
                pl.BlockSpec((s, narrow), lambda h, i: (0, h)), pl.BlockSpec((tq, narrow), lambda h, i: (i, h)),
                pl.BlockSpec((hp, tq, 1), lambda h, i: (h, i, 0)), pl.BlockSpec((tq, narrow), lambda h, i: (i, h))]
    out_shape = [jax.ShapeDtypeStruct(q.shape, F32), jax.ShapeDtypeStruct(k.shape, F32),
                 jax.ShapeDtypeStruct(v.shape, F32)]
    out_specs = [pl.BlockSpec((tq, wide), lambda h, i: (i, h)), pl.BlockSpec((s, wide), lambda h, i: (0, h)),
                 pl.BlockSpec((s, narrow), lambda h, i: (0, h))]
    scratch = []
    _with_carry(carry, args, in_specs, out_shape, out_specs, scratch)
    return pl.pallas_call(
        body, name="mla_bwd", grid=(heads // hp, nq), out_shape=tuple(out_shape), in_specs=in_specs,
        out_specs=tuple(out_specs), scratch_shapes=scratch, compiler_params=_params(),
    )(*args)


def _sb_terms(qv, k_blk, scale, mask):
    z = _dot_nt(qv, k_blk) * scale
    log_beta = jnp.minimum(z, 0.0) - jnp.log(1.0 + jnp.exp(-jnp.abs(z)))
    log_rest = log_beta - z
    if mask is not None:
        log_rest = jnp.where(mask, log_rest, 0.0)
    return log_beta, log_rest


def _head_lanes(e):
    return slice(e * SB_DIM, (e + 1) * SB_DIM)


def _sb_specs(s, tq, off_q, off_k, off_v, hp):
    w = hp * SB_DIM
    assert off_q % w == 0 and off_k % w == 0 and off_v % w == 0
    return [pl.BlockSpec((tq, w), lambda h, i: (i, off_q // w + h)),
            pl.BlockSpec((s, w), lambda h, i: (0, off_k // w + h)),
            pl.BlockSpec((s, w), lambda h, i: (0, off_v // w + h))]


def _sb_fwd(qkv, off_q, off_k, off_v, heads, carry=None):
    s = qkv.shape[0]
    tq, tk, ratio = _att_blocks(s)
    nq = s // tq
    hp = _pairing(heads)
    scale = SB_DIM ** -0.5

    def body(*refs):
        (q_ref, k_ref, v_ref), (o_ref, tot_ref), (acc_ref,), xrefs = _split_refs(refs, 3, 2, carry)
        if carry is not None:
            _carried_exchange(xrefs, carry[1], *_first_last(heads // hp, nq))
        qi = pl.program_id(1)
        qs = [q_ref[:, _head_lanes(e)] for e in range(hp)]
        after = (lax.broadcasted_iota(jnp.int32, (tk, tk), 0) > lax.broadcasted_iota(jnp.int32, (tk, tk), 1)).astype(BF16)
        acc_ref[...] = jnp.zeros_like(acc_ref)
        row_parts = [slice(r * (tq // ROW_PARTS), (r + 1) * (tq // ROW_PARTS)) for r in range(ROW_PARTS)]

        def step(kb, tails, masked):
            off = pl.multiple_of(kb * tk, tk)
            mask = _visible(qi, kb, tq, tk, True) if masked else None
            units = [(e, rows) for e in range(hp) for rows in row_parts]
            ks = [k_ref[pl.ds(off, tk), _head_lanes(e)] for e in range(hp)]
            vs = [v_ref[pl.ds(off, tk), _head_lanes(e)] for e in range(hp)]
            terms = [_sb_terms(qs[e][rows], ks[e], scale, None if mask is None else mask[rows]) for e, rows in units]
            sums = [_split_dot(t[1], after) for t in terms]
            a = [jnp.exp(t[0] + (sm + tails[e][rows])) for t, sm, (e, rows) in zip(terms, sums, units)]
            if masked:
                a = [jnp.where(mask[rows], a_, 0.0) for a_, (e, rows) in zip(a, units)]
            for a_, (e, rows) in zip(a, units):
                acc_ref[rows, _head_lanes(e)] += _dot(a_.astype(BF16), vs[e])
            rest = [jnp.sum(t[1], axis=-1, keepdims=True) for t in terms]
            return tuple(tails[e] + jnp.concatenate(rest[e * len(row_parts):(e + 1) * len(row_parts)], axis=0)
                         for e in range(hp))

        tails = tuple(jnp.zeros((tq, 1), F32) for _ in range(hp))
        for i in range(ratio):
            tails = step((qi + 1) * ratio - 1 - i, tails, True)
        tails = lax.fori_loop(0, qi * ratio, lambda i, t_: step(qi * ratio - 1 - i, t_, False), tails)
        o_ref[...] = acc_ref[...].astype(BF16)
        for e in range(hp):
            tot_ref[e] = tails[e]

    w = hp * SB_DIM
    args, in_specs = [qkv, qkv, qkv], _sb_specs(s, tq, off_q, off_k, off_v, hp)
    out_shape = [jax.ShapeDtypeStruct((s, heads * SB_DIM), BF16), jax.ShapeDtypeStruct((heads, s, 1), F32)]
    out_specs = [pl.BlockSpec((tq, w), lambda h, i: (i, h)), pl.BlockSpec((hp, tq, 1), lambda h, i: (h, i, 0))]
    scratch = [pltpu.VMEM((tq, w), F32)]
    _with_carry(carry, args, in_specs, out_shape, out_specs, scratch)
    return pl.pallas_call(
        body, name="sb_fwd", grid=(heads // hp, nq), out_shape=tuple(out_shape), in_specs=in_specs,
        out_specs=tuple(out_specs), scratch_shapes=scratch, compiler_params=_params(),
    )(*args)


def _sb_bwd(qkv, off_q, off_k, off_v, heads, dy, tot, carry=None):
    s = qkv.shape[0]
    tq = tk = _tile(s, ATT_K)
    ratio = 1
    nq = s // tq
    hp = _pairing(heads)
    ev = range(hp)
    scale = SB_DIM ** -0.5

    def body(*refs):
        (q_ref, k_ref, v_ref, dy_ref, tot_ref), (dq_ref, dk_ref, dv_ref), _, xrefs = _split_refs(refs, 5, 3, carry)
        if carry is not None:
            _carried_exchange(xrefs, carry[1], *_first_last(heads // hp, nq))
        qi = pl.program_id(1)

        @pl.when(qi == 0)
        def _():
            dk_ref[...] = jnp.zeros_like(dk_ref)
            dv_ref[...] = jnp.zeros_like(dv_ref)

        qs = [q_ref[:, _head_lanes(e)] for e in ev]
        dy_b = [dy_ref[:, _head_lanes(e)].astype(BF16) for e in ev]
        tots = [tot_ref[e] for e in ev]
        rows = lax.broadcasted_iota(jnp.int32, (tk, tk), 0)
        cols = lax.broadcasted_iota(jnp.int32, (tk, tk), 1)
        upto = (rows <= cols).astype(BF16)
        before = (rows < cols).astype(BF16)
        dq_ref[...] = jnp.zeros_like(dq_ref)

        def logits(kb):
            off = pl.multiple_of(kb * tk, tk)
            terms = [_sb_terms(qs[e], k_ref[pl.ds(off, tk), _head_lanes(e)], scale, None) for e in ev]
            da = [_dot_nt(dy_b[e], v_ref[pl.ds(off, tk), _head_lanes(e)]) for e in ev]
            return tuple(terms[e] + (da[e],) for e in ev)

        def step(kb, ahead, state, masked, prefetch):
            rest_left, g_left = state
            off = pl.multiple_of(kb * tk, tk)
            log_beta = [ahead[e][0] for e in ev]
            log_rest = [ahead[e][1] for e in ev]
            da = [ahead[e][2] for e in ev]
            if masked:
                mask = _visible(qi, kb, tq, tk, True)
                log_rest = [jnp.where(mask, log_rest[e], 0.0) for e in ev]
            rest_upto = [_split_dot(log_rest[e], upto) + rest_left[e] for e in ev]
            a = [jnp.exp(log_beta[e] + (tots[e] - rest_upto[e])) for e in ev]
            beta = [jnp.exp(log_beta[e]) for e in ev]
            if masked:
                a = [jnp.where(mask, a[e], 0.0) for e in ev]
                beta = [jnp.where(mask, beta[e], 0.0) for e in ev]
            for e in ev:
                dv_ref[pl.ds(off, tk), _head_lanes(e)] += _dot_tn(a[e].astype(BF16), dy_b[e])
            g = [a[e] * da[e] for e in ev]
            g_before = [_dot(g[e].astype(BF16), before) + g_left[e] for e in ev]
            nxt = logits(kb + 1) if prefetch else None
            dz = [((g[e] * (1.0 - beta[e]) - g_before[e] * beta[e]) * scale).astype(BF16) for e in ev]
            for e in ev:
                dk_ref[pl.ds(off, tk), _head_lanes(e)] += _dot_tn(dz[e], qs[e])
            for e in ev:
                dq_ref[:, _head_lanes(e)] += _dot(dz[e], k_ref[pl.ds(off, tk), _head_lanes(e)])
            state = (tuple(rest_left[e] + jnp.sum(log_rest[e], axis=-1, keepdims=True) for e in ev),
                     tuple(g_left[e] + jnp.sum(g[e], axis=-1, keepdims=True) for e in ev))
            return nxt, state

        zeros = tuple(jnp.zeros((tq, 1), F32) for _ in ev)
        ahead, state = lax.fori_loop(0, qi * ratio, lambda kb, c: step(kb, c[0], c[1], False, True),
                                     (logits(0), (zeros, zeros)))
        for i in range(ratio):
            ahead, state = step(qi * ratio + i, ahead, state, True, i < ratio - 1)

    w = hp * SB_DIM
    args = [qkv, qkv, qkv, dy, tot]
    in_specs = _sb_specs(s, tq, off_q, off_k, off_v, hp) + [pl.BlockSpec((tq, w), lambda h, i: (i, h)),
                                                            pl.BlockSpec((hp, tq, 1), lambda h, i: (h, i, 0))]
    out_shape = [jax.ShapeDtypeStruct((s, heads * SB_DIM), F32)] * 3
    out_specs = [pl.BlockSpec((tq, w), lambda h, i: (i, h)), pl.BlockSpec((s, w), lambda h, i: (0, h)),
                 pl.BlockSpec((s, w), lambda h, i: (0, h))]
    scratch = []
    _with_carry(carry, args, in_specs, out_shape, out_specs, scratch)
    return pl.pallas_call(
        body, name="sb_bwd", grid=(heads // hp, nq), out_shape=tuple(out_shape), in_specs=in_specs,
        out_specs=tuple(out_specs), scratch_shapes=scratch, compiler_params=_params(),
    )(*args)


def _slot_sum(recv, name):
    slots, r, w = recv.shape
    tr = _tile(r, 256)

    def body(r_ref, o_ref):
        acc = r_ref[0].astype(F32)
        for d in range(1, slots):
            acc = acc + r_ref[d].astype(F32)
        o_ref[...] = acc

    return pl.pallas_call(
        body, name=name, grid=(r // tr,),
        out_shape=jax.ShapeDtypeStruct((r, w), F32),
        in_specs=[pl.BlockSpec((slots, tr, w), lambda i: (0, i, 0))],
        out_specs=pl.BlockSpec((tr, w), lambda i: (i, 0)), compiler_params=_params(),
    )(recv)


def _pair_sum(by_core, theirs):
    _, slots, r, w = by_core.shape
    tr = _tile(r, 128)

    def body(a_ref, b_ref, o_ref):
        o_ref[...] = (a_ref[lax.axis_index("c")].astype(F32) + b_ref[...].astype(F32)).astype(BF16)

    spec = pl.BlockSpec((slots, tr, w), lambda i: (0, i, 0))
    return pl.pallas_call(
        body, name="pair_sum", grid=(r // tr,), out_shape=jax.ShapeDtypeStruct(theirs.shape, BF16),
        in_specs=[pl.BlockSpec((2, slots, tr, w), lambda i: (0, 0, i, 0)), spec], out_specs=spec,
        compiler_params=_params(),
    )(by_core, theirs)


def _small_sum(rows, loss_lo, loss_hi, d_model):
    n = rows.shape[1]

    def body(r_ref, o_ref, loss_ref):
        rv = r_ref[...]
        acc = rv[0:1, :]
        for d in range(1, N_DEV):
            acc = acc + rv[d:d + 1, :]
        o_ref[...] = acc
        total = jnp.sum(acc[:, loss_lo:loss_hi], axis=-1, keepdims=True) * (0.5 / d_model)
        loss_ref[...] = jnp.broadcast_to(total, (1, LANES))

    return pl.pallas_call(
        body, name="small_sum",
        out_shape=(jax.ShapeDtypeStruct((1, n), F32), jax.ShapeDtypeStruct((1, LANES), F32)),
        compiler_params=_params(),
    )(rows)


def _adamw(w, g, m, v, name):
    r, c = w.shape
    tr = _tile(r, max(8, (1 << 18) // c // 8 * 8))

    def body(w_ref, g_ref, m_ref, v_ref, d_ref, mo_ref, vo_ref):
        gv = g_ref[...]
        m_new = ADAM_B1 * m_ref[...] + (1.0 - ADAM_B1) * gv
        v_new = ADAM_B2 * v_ref[...] + (1.0 - ADAM_B2) * (gv * gv)
        m_hat = m_new / (1.0 - ADAM_B1 ** ADAM_STEP)
        v_hat = v_new / (1.0 - ADAM_B2 ** ADAM_STEP)
        d_ref[...] = -ADAM_LR * (m_hat / (jnp.sqrt(v_hat) + ADAM_EPS) + ADAM_WD * w_ref[...])
        mo_ref[...] = m_new
        vo_ref[...] = v_new

    spec = pl.BlockSpec((tr, c), lambda i: (i, 0))
    sd = jax.ShapeDtypeStruct((r, c), F32)
    return pl.pallas_call(
        body, name=name, grid=(r // tr,), out_shape=(sd, sd, sd),
        in_specs=[spec] * 4, out_specs=(spec,) * 3, compiler_params=_params(),
    )(w, g, m, v)


def _pack_rows(a):
    return a.reshape(-1, PACK_W)


def _unshard(slots, shape, col_sharded):
    r, c = shape
    if col_sharded:
        return slots.reshape(N_DEV, r, c).transpose(1, 0, 2).reshape(r, N_DEV * c)
    return slots.reshape(N_DEV * r, c)


def _to_shards(full, col_sharded, packed=True):
    r, c = full.shape
    shards = (full.reshape(r, N_DEV, c // N_DEV).transpose(1, 0, 2) if col_sharded
              else full.reshape(N_DEV, r // N_DEV, c))
    return shards.reshape(N_DEV, -1, PACK_W) if packed else shards


def kernel(x, c, positions, w_ada, b_ada, g_norm1, g_norm2, w_in, g_q_latent, g_kv_latent, w_uq, w_ukv, g_q_head, g_k_head, w_proj_mla, w_proj_sb, w_out, w_ffn_in, w_ffn_out, loss_target, m_w_ada, m_b_ada, m_g_norm1, m_g_norm2, m_w_in, m_g_q_latent, m_g_kv_latent, m_w_uq, m_w_ukv, m_g_q_head, m_g_k_head, m_w_proj_mla, m_w_proj_sb, m_w_out, m_w_ffn_in, m_w_ffn_out, v_w_ada, v_b_ada, v_g_norm1, v_g_norm2, v_w_in, v_g_q_latent, v_g_kv_latent, v_w_uq, v_w_ukv, v_g_q_head, v_g_k_head, v_w_proj_mla, v_w_proj_sb, v_w_out, v_w_ffn_in, v_w_ffn_out):
    env = dict(locals())
    drop = lambda a: a[0] if a.ndim == 3 else a
    wts = {n: drop(env[n]) for n in WEIGHT_NAMES}
    mom = {n: drop(env["m_" + n]) for n in WEIGHT_NAMES}
    var = {n: drop(env["v_" + n]) for n in WEIGHT_NAMES}
    xs, tgt, pos = x[0], loss_target[0], positions[0]
    s, d = xs.shape
    lat = wts["w_uq"].shape[0]
    assert wts["w_ukv"].shape[0] == lat
    h_mla = wts["w_uq"].shape[1] * N_DEV // QK_DIM
    sb_w = wts["w_proj_sb"].shape[0]
    h_sb = sb_w // SB_DIM
    d_ff = wts["w_ffn_out"].shape[0] * N_DEV
    me = 4 * lax.axis_index("x") + 2 * lax.axis_index("y") + lax.axis_index("c")

    ref_w = (("c_q", lat), ("c_kv", lat), ("k_pe", ROPE), ("q_sb", sb_w), ("k_sb", sb_w), ("v_sb", sb_w),
             ("gl_a", d), ("gl_b", d))
    ref_off, o = {}, 0
    for n_, w_ in ref_w:
        ref_off[n_] = (o, w_)
        o += w_
    order = ("gl_a", "gl_b", "q_sb", "k_sb", "v_sb", "c_q", "c_kv", "k_pe")
    off, o = {}, 0
    for n_ in order:
        w_ = LANES if n_ == "k_pe" else ref_off[n_][1]
        assert o % w_ == 0
        off[n_] = o
        o += w_

    used_w = o
    proj_w = -(-used_w // (2 * LANES)) * (2 * LANES)

    col_sharded = dict(BIG)
    rows_of = {n: wts[n].size // PACK_W for n, _ in BIG}
    full, grads = {}, {}

    def own_shape(names):
        return len(names) == 1

    def pack_weights(names):
        if own_shape(names):
            return wts[names[0]].astype(BF16)
        return jnp.concatenate([_pack_rows(wts[n].astype(BF16)) for n in names], axis=0)

    def unpack_weights(slots, names):
        r0 = 0
        for n in names:
            part = slots if own_shape(names) else slots[:, r0:r0 + rows_of[n]]
            full[n] = _unshard(part, wts[n].shape, col_sharded[n])
            r0 += rows_of[n]

    def pack_grads(names):
        if own_shape(names):
            return _to_shards(grads[names[0]].astype(BF16), col_sharded[names[0]], packed=False)
        return jnp.concatenate([_to_shards(grads[n].astype(BF16), col_sharded[n]) for n in names], axis=1)

    def unpack_grads(recv, names):
        if isinstance(recv, tuple):
            summed_rows = jnp.concatenate([_slot_sum(part, "slot_sum_%s_%d" % (names[0], i))
                                           for i, part in enumerate(recv)], axis=0)
        else:
            summed_rows = _slot_sum(recv, "slot_sum_" + names[0])
        r0 = 0
        for n in names:
            part = summed_rows if own_shape(names) else summed_rows[r0:r0 + rows_of[n]]
            grads[n] = part.reshape(wts[n].shape)
            r0 += rows_of[n]

    unpack_weights(_gather_two_level(pack_weights(("w_in",)), "gather_w_in"), ("w_in",))
    seg = lambda a, n_: a[:, ref_off[n_][0]:ref_off[n_][0] + ref_off[n_][1]]
    w_in_k = jnp.concatenate([seg(full["w_in"], n_) for n_ in order]
                             + [jnp.zeros((d, proj_w - used_w + LANES - ROPE), BF16)], axis=1)
    pad_gain = lambda g: jnp.pad(g, ((0, 0), (0, QK_PAD - QK_DIM)))
    g_qh, g_kh = pad_gain(wts["g_q_head"]), pad_gain(wts["g_k_head"])

    c_all = _gather_rows(c, "gather_c")
    ada_cols = _exchange(_ada_fwd(c_all, wts["w_ada"]), gather=True, name="gather_ada")
    ada = lax.dynamic_index_in_dim(ada_cols, me, axis=1, keepdims=False).reshape(1, 6 * d) + wts["b_ada"]
    sh1, sc1, gt1, sh2, sc2, gt2 = [ada[:, i * d:(i + 1) * d] for i in range(6)]

    half = ROPE // 2
    ang = pos.astype(F32)[:, None] * (ROPE_THETA ** (-jnp.arange(half, dtype=F32) / half))
    zeros = jnp.zeros((s, LANES - ROPE), F32)
    cos_t = jnp.concatenate([jnp.cos(ang), jnp.cos(ang), zeros], axis=1)
    sin_t = jnp.concatenate([-jnp.sin(ang), jnp.sin(ang), zeros], axis=1)

    h1 = _norm_mod(xs, wts["g_norm1"], sc1, sh1)
    mixer_w = ("w_uq", "w_ukv", "w_proj_mla", "w_proj_sb", "w_out")
    proj, slots = _mm(h1, w_in_k, name="mm_in", carry=(pack_weights(mixer_w), "two_level"))
    unpack_weights(slots, mixer_w)
    w_uq_k = jnp.pad(full["w_uq"].reshape(lat, h_mla, QK_DIM), ((0, 0), (0, 0), (0, QK_PAD - QK_DIM))
                     ).reshape(lat, h_mla * QK_PAD)
    cqn, ckvn = _latent_norm(proj, off["c_q"], off["c_kv"], lat, wts["g_q_latent"], wts["g_kv_latent"])
    q_raw = _mm(cqn, w_uq_k, name="mm_uq")
    kv = _mm(ckvn, full["w_ukv"], name="mm_ukv")
    q, k, v = _qk_prep(q_raw, kv, proj, off["k_pe"], cos_t, sin_t, g_qh, g_kh)
    y_a, lse, slots = _mla_fwd(q, k, v, carry=(pack_weights(("w_ffn_out",)), "two_level"))
    unpack_weights(slots, ("w_ffn_out",))
    assert off["k_sb"] == off["q_sb"] + sb_w and off["v_sb"] == off["k_sb"] + sb_w
    qkv_sb = proj[:, off["q_sb"]:off["q_sb"] + 3 * sb_w].astype(BF16)
    y_b, tot_sb, w_fi_slots = _sb_fwd(qkv_sb, 0, sb_w, 2 * sb_w, h_sb,
                              carry=(wts["w_ffn_in"].astype(BF16), "two_level"))
    ya_p = _mm(y_a, full["w_proj_mla"], name="mm_proj_mla")
    yb_p = _mm(y_b, full["w_proj_sb"], name="mm_proj_sb")
    merged = _merge(proj, off["gl_a"], off["gl_b"], ya_p, yb_p)
    o1 = _mm(merged, full["w_out"], name="mm_out")
    x1, h2 = _resid_norm_mod(xs, o1, gt1, wts["g_norm2"], sc2, sh2)
    gu = _mm(h2, w_fi_slots, shards="b", out_dtype=BF16, name="mm_ffn_in")
    act = _swiglu(gu)
    o2 = _mm(act, full["w_ffn_out"], name="mm_ffn_out")
    dy, d_o2, sums_l = _loss_head(x1, o2, gt2, tgt)

    recv = {}
    grads["w_ffn_out"] = _mm(act, d_o2, ta=True, out_dtype=BF16, name="mm_g_ffn_out")
    d_gu = _mm_swiglu_bwd(d_o2, full["w_ffn_out"], gu)
    g_fo_send = pack_grads(("w_ffn_out",))
    half = g_fo_send.shape[1] // 2
    g_fi_slots, recv_lo = _mm(h2, d_gu, ta=True, shards="out", out_dtype=BF16, name="mm_g_ffn_in",
                              carry=(g_fo_send[:, :half], False))
    d_h2, recv_hi = _mm(d_gu, w_fi_slots, tb=True, shards="b", name="mm_d_h2", carry=(g_fo_send[:, half:], False))
    recv["w_ffn_out",] = (recv_lo, recv_hi)
    d_x1, d_o1, sums_2 = _norm_mod_bwd(d_h2, x1, wts["g_norm2"], sc2, dy, gate=(o1, gt1))
    grads["w_out"] = _mm(merged, d_o1, ta=True, out_dtype=BF16, name="mm_g_out")
    d_merged = _mm(d_o1, full["w_out"], tb=True, name="mm_d_merged")
    d_yap, d_ybp, d_gla, d_glb = _merge_bwd(proj, off["gl_a"], off["gl_b"], ya_p, yb_p, d_merged)
    grads["w_proj_mla"] = _mm(y_a, d_yap, ta=True, out_dtype=BF16, name="mm_g_proj_mla")
    grads["w_proj_sb"] = _mm(y_b, d_ybp, ta=True, out_dtype=BF16, name="mm_g_proj_sb")
    d_ya = _mm(d_yap, full["w_proj_mla"], tb=True, name="mm_d_ya")
    d_yb = _mm(d_ybp, full["w_proj_sb"], tb=True, name="mm_d_yb")
    dq_sb, dk_sb, dv_sb, recv["w_ffn_in",] = _sb_bwd(qkv_sb, 0, sb_w, 2 * sb_w, h_sb, d_yb, tot_sb,
                                                     carry=(g_fi_slots, False))
    merge_w = ("w_out", "w_proj_mla", "w_proj_sb")
    dq, dk, dv, recv[merge_w] = _mla_bwd(q, k, v, y_a, lse, d_ya, carry=(pack_grads(merge_w), False))
    d_qraw, d_kv, d_kpe, sums_h = _qk_prep_bwd(q_raw, kv, proj, off["k_pe"], cos_t, sin_t, g_qh, g_kh, dq, dk, dv)
    g_uq_k = _mm(cqn, d_qraw, ta=True, out_dtype=BF16, name="mm_g_uq")
    grads["w_uq"] = g_uq_k.reshape(lat, h_mla, QK_PAD)[:, :, :QK_DIM].reshape(lat, h_mla * QK_DIM)
    grads["w_ukv"] = _mm(ckvn, d_kv, ta=True, out_dtype=BF16, name="mm_g_ukv")
    d_cqn = _mm(d_qraw, w_uq_k, tb=True, name="mm_d_cqn")
    d_ckvn = _mm(d_kv, full["w_ukv"], tb=True, name="mm_d_ckvn")
    d_cq, d_ckv, sums_lat = _latent_norm_bwd(proj, off["c_q"], off["c_kv"], lat, wts["g_q_latent"],
                                             wts["g_kv_latent"], d_cqn, d_ckvn)
    d_parts = {"gl_a": d_gla, "gl_b": d_glb, "q_sb": dq_sb, "k_sb": dk_sb, "v_sb": dv_sb, "c_q": d_cq, "c_kv": d_ckv,
               "k_pe": d_kpe}
    d_proj = jnp.concatenate([d_parts[n_].astype(BF16) for n_ in order]
                             + ([jnp.zeros((s, proj_w - used_w), BF16)] if proj_w > used_w else []), axis=1)
    latent_w = ("w_uq", "w_ukv")
    g_in_k, recv[latent_w] = _mm(h1, d_proj, ta=True, out_dtype=BF16, name="mm_g_in",
                                 carry=(pack_grads(latent_w), False))
    grads["w_in"] = jnp.concatenate([g_in_k[:, off[n_]:off[n_] + w_] for n_, w_ in ref_w], axis=1)
    by_core = pack_grads(("w_in",))
    by_core = by_core.reshape((N_DEV // 2, 2) + by_core.shape[1:]).transpose(1, 0, 2, 3)
    from_sibling = _exchange(by_core, gather="sibling", name="swap_w_in")
    d_h1, recv_chips = _mm(d_proj, w_in_k, tb=True, name="mm_d_h1", carry=(_pair_sum(by_core, from_sibling), "chips"))
    grads["w_in"] = _slot_sum(recv_chips, "slot_sum_w_in").reshape(wts["w_in"].shape)
    grad_x, sums_1 = _norm_mod_bwd(d_h1, xs, wts["g_norm1"], sc1, d_x1)

    parts = [sums_1[0:1], sums_1[1:2], sums_2[3:4], sums_2[0:1], sums_2[1:2], sums_l[0:1],
             sums_1[2:3], sums_2[2:3], sums_lat[0:1], sums_lat[1:2], sums_h[0:1], sums_h[1:2], sums_l[1:2]]
    part_off, o = [], 0
    for p in parts:
        part_off.append(o)
        o += p.shape[1]
    all_rows = _gather_rows(jnp.concatenate(parts, axis=1), "gather_small")
    summed, loss_row = _small_sum(all_rows, part_off[12], part_off[12] + d, d)
    take = lambda i, w: summed[:, part_off[i]:part_off[i] + w]
    grads["b_ada"] = summed[:, :6 * d]
    grads["g_norm1"], grads["g_norm2"] = take(6, d), take(7, d)
    grads["g_q_latent"], grads["g_kv_latent"] = take(8, lat), take(9, lat)
    grads["g_q_head"], grads["g_k_head"] = take(10, QK_DIM), take(11, QK_DIM)
    n_ada = 6 * d // N_DEV
    d_ada_mine = lax.dynamic_slice_in_dim(all_rows[:, :6 * d], me * n_ada, n_ada, axis=1)
    grads["w_ada"] = _ada_grad(c_all.T, d_ada_mine)

    for names, slots in recv.items():
        unpack_grads(slots, names)

    delta, new_m, new_v = {}, {}, {}
    for n in ("w_ada",) + tuple(n for n, _ in BIG):
        delta[n], new_m[n], new_v[n] = _adamw(wts[n], grads[n], mom[n], var[n], "adamw_" + n)
    cat = lambda t: jnp.concatenate([t[n] for n in SMALL], axis=1)
    d_s, m_s, v_s = _adamw(cat(wts), cat(grads), cat(mom), cat(var), "adamw_small")
    o = 0
    for n in SMALL:
        w_ = wts[n].shape[1]
        delta[n], new_m[n], new_v[n] = d_s[:, o:o + w_], m_s[:, o:o + w_], v_s[:, o:o + w_]
        o += w_

    lead = lambda t: [t[n].reshape(env[n].shape) for n in WEIGHT_NAMES]
    return (loss_row[0, 0], grad_x[None], *lead(grads), *lead(delta), *lead(new_m), *lead(new_v))
```
